```python
import math
import jax, jax.numpy as jnp
from jax import lax
import numpy as np

D_MODEL = 1024
BATCH = 1
SEQ = 16384
DEPTH = 2

GRID_W = 64
CTX_LEN = 256
N_EVEN = (DEPTH + 1) // 2
N_ODD = DEPTH // 2
D_FF = -(-8 * D_MODEL // (3 * 256)) * 256
CHUNK = 64
EPS = 1e-6
SSD_WIDTH = D_MODEL
SSD_HEAD_DIM = 64
SSD_HEADS = SSD_WIDTH // SSD_HEAD_DIM
SSD_STATE = 128
SSD_GROUPS = 2
SSD_CONV = 5
SSD_XBC = SSD_WIDTH + 2 * SSD_GROUPS * SSD_STATE
HG_WIDTH = D_MODEL
HG_HEAD_DIM = 128
HG_HEADS = HG_WIDTH // HG_HEAD_DIM
EVEN_SIZES = (SSD_WIDTH, SSD_XBC, 2 * SSD_HEADS, HG_WIDTH, 2 * HG_WIDTH, HG_WIDTH, HG_WIDTH)
EVEN_OFFSETS = tuple(sum(EVEN_SIZES[:i + 1]) for i in range(len(EVEN_SIZES) - 1))
EVEN_IN = sum(EVEN_SIZES)
MIX_EVEN = SSD_WIDTH + HG_WIDTH
S5_WIDTH = D_MODEL // 2
S5_GROUP = 16
S5_GROUPS = S5_WIDTH // S5_GROUP
S5_STATE = 64

kernel_name = "hybrid_ssd_hgrn2_s5_context_prefix"

F32 = jnp.float32


def rms(x):
    xf = x.astype(F32)
    return xf * lax.rsqrt(jnp.mean(xf * xf, axis=-1, keepdims=True) + EPS)


def modulate(h, gain, shift, scale):
    return rms(h) * gain * (1.0 + scale) + shift


def ada(cvec, w, b):
    m = jax.nn.silu(cvec.astype(F32)) @ w + b
    return jnp.split(m, 6, axis=-1)


def swiglu(h, wg, wu, wd):
    return (jax.nn.silu(h @ wg) * (h @ wu)) @ wd


def identity(t):
    return t


def flip_seq(t):
    return jnp.flip(t, axis=1)


def centred_dwconv(u, w, b):
    k = w.shape[0]
    out = lax.conv_general_dilated(u, w[:, None, :].astype(u.dtype), window_strides=(1,),
                                   padding=((k // 2, k // 2),), dimension_numbers=('NWC', 'WIO', 'NWC'),
                                   feature_group_count=u.shape[-1])
    return out + b


def to_chunks(t):
    n_chunks = t.shape[1] // CHUNK
    return jnp.moveaxis(t.reshape(t.shape[0], n_chunks, CHUNK, *t.shape[2:]), 1, 0)


def from_chunks(t):
    t = jnp.moveaxis(t, 0, 1)
    return t.reshape(t.shape[0], t.shape[1] * CHUNK, *t.shape[3:])


def in_chunk_mask():
    return jnp.tril(jnp.ones((CHUNK, CHUNK), dtype=bool))[None, :, :, None, None]


def ssd_scan(xdt, log_a, bm, cm, h0):
    b, l, h, p = xdt.shape
    g = bm.shape[2]
    r = h // g
    mask = in_chunk_mask()
    xs = (to_chunks(xdt.reshape(b, l, g, r, p)), to_chunks(log_a.reshape(b, l, g, r)),
          to_chunks(bm), to_chunks(cm))

    def step(state, inp):
        xc, ac, bc, cc = inp
        acum = jnp.cumsum(ac, axis=1)
        seg = acum[:, :, None] - acum[:, None, :]
        decay = jnp.exp(jnp.where(mask, seg, -jnp.inf))
        cb = jnp.einsum('btgn,bsgn->btsg', cc, bc)
        y = jnp.einsum('btsg,btsgr,bsgrp->btgrp', cb, decay, xc)
        y = y + jnp.einsum('btgn,bgrpn->btgrp', cc, state) * jnp.exp(acum)[..., None]
        last = acum[:, -1]
        to_end = jnp.exp(last[:, None] - acum)
        state = state * jnp.exp(last)[..., None, None] + jnp.einsum('bsgn,bsgr,bsgrp->bgrpn', bc, to_end, xc)
        return state, y

    state, ys = lax.scan(step, h0, xs)
    return from_chunks(ys).reshape(b, l, h, p), state


def hgrn2_scan(q, k, v, log_f, h0):
    mask = in_chunk_mask()
    xs = (to_chunks(q), to_chunks(k), to_chunks(v), to_chunks(log_f))

    def step(state, inp):
        qc, kc, vc, gc = inp
        bcum = jnp.cumsum(gc, axis=1)
        seg = bcum[:, :, None] - bcum[:, None, :]
        decay = jnp.exp(jnp.where(mask, seg, -jnp.inf))
        att = jnp.einsum('bthk,bshk,btshk->bhts', qc, kc, decay)
        o = jnp.einsum('bhts,bshv->bthv', att, vc) + jnp.einsum('bthk,bhkv->bthv', qc * jnp.exp(bcum), state)
        last = bcum[:, -1]
        state = state * jnp.exp(last)[..., None] + jnp.einsum('bshk,bshv->bhkv', kc * jnp.exp(last[:, None] - bcum), vc)
        return state, o

    state, os_ = lax.scan(step, h0, xs)
    return from_chunks(os_), state


def run_bidir(scan_fn, ctx_dirs, lat_dirs, h0):
    y_ctx = 0.0
    y_lat = 0.0
    for d in range(2):
        rev = flip_seq if d == 1 else identity
        yc, state = scan_fn(*[rev(t) for t in ctx_dirs[d]], h0)
        yl, _ = scan_fn(*[rev(t) for t in lat_dirs[d]], state)
        y_ctx = y_ctx + rev(yc)
        y_lat = y_lat + rev(yl)
    return y_ctx, y_lat


def even_mixer(h_lat, h_ctx, lb, w_in, w_out, conv_w, conv_b, dt_bias, a_log, d_skip, ssd_norm_g, hg_norm_g, ctx_out):
    bsz = h_lat.shape[0]

    def prep(h):
        zs = (h @ w_in).astype(F32)
        b, l = zs.shape[:2]
        z, xbc, dt_raw, q, f_raw, v, g = jnp.split(zs, EVEN_OFFSETS, axis=-1)
        xbc = jax.nn.silu(centred_dwconv(xbc, conv_w, conv_b))
        xs, bm, cm = jnp.split(xbc, [SSD_WIDTH, SSD_WIDTH + SSD_GROUPS * SSD_STATE], axis=-1)
        xs = xs.reshape(b, l, SSD_HEADS, SSD_HEAD_DIM)
        bm = bm.reshape(b, l, SSD_GROUPS, SSD_STATE)
        cm = cm.reshape(b, l, SSD_GROUPS, SSD_STATE)
        dt = jax.nn.softplus(dt_raw.reshape(b, l, 2, SSD_HEADS) + dt_bias)
        log_a = -dt * jnp.exp(a_log)
        ssd_dirs = [(xs * dt[:, :, d, :, None], log_a[:, :, d], bm, cm) for d in range(2)]
        q = jax.nn.silu(q).reshape(b, l, HG_HEADS, HG_HEAD_DIM)
        v = v.reshape(b, l, HG_HEADS, HG_HEAD_DIM)
        f = lb + (1.0 - lb) * jax.nn.sigmoid(f_raw.reshape(b, l, 2, HG_WIDTH))
        f = f.reshape(b, l, 2, HG_HEADS, HG_HEAD_DIM)
        hg_dirs = [(q, 1.0 - f[:, :, d], v, jnp.log(f[:, :, d])) for d in range(2)]
        return ssd_dirs, hg_dirs, xs, z, g

    c_ssd, c_hg, c_x, c_z, c_g = prep(h_ctx)
    l_ssd, l_hg, l_x, l_z, l_g = prep(h_lat)
    ssd_h0 = jnp.zeros((bsz, SSD_GROUPS, SSD_HEADS // SSD_GROUPS, SSD_HEAD_DIM, SSD_STATE), F32)
    hg_h0 = jnp.zeros((bsz, HG_HEADS, HG_HEAD_DIM, HG_HEAD_DIM), F32)
    ssd_c, ssd_l = run_bidir(ssd_scan, c_ssd, l_ssd, ssd_h0)
    hg_c, hg_l = run_bidir(hgrn2_scan, c_hg, l_hg, hg_h0)

    def finish(y_ssd, xs, z, o_hg, g):
        b, l = z.shape[:2]
        y = (y_ssd + d_skip[:, None] * xs).reshape(b, l, SSD_GROUPS, -1) * jax.nn.silu(z).reshape(b, l, SSD_GROUPS, -1)
        y = (rms(y) * ssd_norm_g.reshape(SSD_GROUPS, -1)).reshape(b, l, SSD_WIDTH)
        o = (rms(o_hg) * hg_norm_g).reshape(b, l, HG_WIDTH) * jax.nn.silu(g)
        return jnp.concatenate([y, o], axis=-1) @ w_out

    y_lat = finish(ssd_l, l_x, l_z, hg_l, l_g)
    y_ctx = finish(ssd_c, c_x, c_z, hg_c, c_g) if ctx_out else None
    return y_lat, y_ctx


def to_colmajor(t, rows):
    b, l, w = t.shape
    return t.reshape(b, rows, GRID_W, w).transpose(0, 2, 1, 3).reshape(b, l, w)


def to_rowmajor(t, rows):
    b, l, w = t.shape
    return t.reshape(b, GRID_W, rows, w).transpose(0, 2, 1, 3).reshape(b, l, w)


def zoh(a_re, a_im, log_dt, b_re, b_im):
    a_re, a_im, b_re, b_im = (t.astype(F32) for t in (a_re, a_im, b_re, b_im))
    dt = jnp.exp(log_dt.astype(F32))[:, None]
    mag = jnp.exp(a_re * dt)
    lr, li = mag * jnp.cos(a_im * dt), mag * jnp.sin(a_im * dt)
    den = a_re * a_re + a_im * a_im
    cr = ((lr - 1.0) * a_re + li * a_im) / den
    ci = (li * a_re - (lr - 1.0) * a_im) / den
    cr, ci = cr[..., None], ci[..., None]
    return lr, li, cr * b_re - ci * b_im, cr * b_im + ci * b_re


def s5_drive(u, bb_re, bb_im):
    b, l, _ = u.shape
    ug = u.reshape(b, l, S5_GROUPS, S5_GROUP)
    return jnp.einsum('blgc,gpc->blgp', ug, bb_re), jnp.einsum('blgc,gpc->blgp', ug, bb_im)


def s5_readout(c_re, c_im, s_re, s_im):
    b, l = s_re.shape[:2]
    y = jnp.einsum('gcp,blgp->blgc', c_re, s_re) - jnp.einsum('gcp,blgp->blgc', c_im, s_im)
    return y.reshape(b, l, S5_WIDTH)


def complex_linear_combine(e1, e2):
    a1r, a1i, b1r, b1i = e1
    a2r, a2i, b2r, b2i = e2
    return (a2r * a1r - a2i * a1i, a2r * a1i + a2i * a1r,
            a2r * b1r - a2i * b1i + b2r, a2r * b1i + a2i * b1r + b2i)


def diag_scan(lam_re, lam_im, b_re, b_im, h0=None):
    if h0 is not None:
        h0_re, h0_im = h0
        b_re = b_re.at[:, 0].add(lam_re * h0_re - lam_im * h0_im)
        b_im = b_im.at[:, 0].add(lam_re * h0_im + lam_im * h0_re)
    a_re = jnp.broadcast_to(lam_re, b_re.shape)
    a_im = jnp.broadcast_to(lam_im, b_im.shape)
    _, _, s_re, s_im = lax.associative_scan(complex_linear_combine, (a_re, a_im, b_re, b_im), axis=1)
    return s_re, s_im


def s5_glu(y, w_val, w_gate):
    a = jax.nn.gelu(y)
    return (a @ w_val) * jax.nn.sigmoid(a @ w_gate)


def odd_mixer(h_lat, h_ctx, rows, w_in, a_re, a_im, log_dt, b_re, b_im, c_re, c_im, d_skip, w_val, w_gate, ctx_out):
    u_lat = to_colmajor((h_lat @ w_in).astype(F32), rows)
    u_ctx = (h_ctx @ w_in).astype(F32)
    y_lat = d_skip * u_lat
    y_ctx = d_skip * u_ctx if ctx_out else None
    for d in range(2):
        rev = flip_seq if d == 1 else identity
        lr, li, bb_re, bb_im = zoh(a_re[d], a_im[d], log_dt[d], b_re[d], b_im[d])
        cr, ci = c_re[d].astype(F32), c_im[d].astype(F32)
        sc_re, sc_im = diag_scan(lr, li, *s5_drive(rev(u_ctx), bb_re, bb_im))
        sl_re, sl_im = diag_scan(lr, li, *s5_drive(rev(u_lat), bb_re, bb_im), h0=(sc_re[:, -1], sc_im[:, -1]))
        y_lat = y_lat + rev(s5_readout(cr, ci, sl_re, sl_im))
        if ctx_out:
            y_ctx = y_ctx + rev(s5_readout(cr, ci, sc_re, sc_im))
    out_lat = s5_glu(to_rowmajor(y_lat, rows), w_val, w_gate)
    out_ctx = s5_glu(y_ctx, w_val, w_gate) if ctx_out else None
    return out_lat, out_ctx


def setup_inputs(seed: int = 0) -> dict:
    key = jax.random.key(seed)
    ks = iter(jax.random.split(key, 48))

    def nrm(shape, scale):
        return jax.random.normal(next(ks), shape, F32) * scale

    def unif(shape, lo, hi):
        return jax.random.uniform(next(ks), shape, F32, lo, hi)

    dt0 = jnp.exp(unif((N_EVEN, 2, SSD_HEADS), math.log(1e-3), math.log(1e-1)))
    return {
        "x": nrm((BATCH, SEQ, D_MODEL), 1.0),
        "c": nrm((BATCH, D_MODEL), 1.0),
        "ctx": nrm((BATCH, CTX_LEN, D_MODEL), 1.0),
        "c_ctx": nrm((D_MODEL,), 1.0),
        "ada_w": nrm((DEPTH, D_MODEL, 6 * D_MODEL), 0.5 * D_MODEL ** -0.5),
        "ada_b": nrm((DEPTH, 6 * D_MODEL), 0.02),
        "norm_mix_g": 1.0 + nrm((DEPTH, D_MODEL), 0.02),
        "norm_ffn_g": 1.0 + nrm((DEPTH, D_MODEL), 0.02),
        "ffn_w_gate": nrm((DEPTH, D_MODEL, D_FF), D_MODEL ** -0.5),
        "ffn_w_up": nrm((DEPTH, D_MODEL, D_FF), D_MODEL ** -0.5),
        "ffn_w_down": nrm((DEPTH, D_FF, D_MODEL), D_FF ** -0.5),
        "final_norm_g": 1.0 + nrm((D_MODEL,), 0.02),
        "ev_w_in": nrm((N_EVEN, D_MODEL, EVEN_IN), D_MODEL ** -0.5),
        "ev_w_out": nrm((N_EVEN, MIX_EVEN, D_MODEL), MIX_EVEN ** -0.5),
        "ssd_conv_w": nrm((N_EVEN, SSD_CONV, SSD_XBC), SSD_CONV ** -0.5),
        "ssd_conv_b": nrm((N_EVEN, SSD_XBC), 0.02),
        "ssd_dt_bias": dt0 + jnp.log(-jnp.expm1(-dt0)),
        "ssd_a_log": jnp.log(unif((N_EVEN, 2, SSD_HEADS), 1.0, 16.0)),
        "ssd_d": 1.0 + nrm((N_EVEN, SSD_HEADS), 0.02),
        "ssd_norm_g": 1.0 + nrm((N_EVEN, SSD_WIDTH), 0.02),
        "hg_lb_logits": nrm((N_EVEN + 1, HG_WIDTH), 0.1),
        "hg_norm_g": 1.0 + nrm((N_EVEN, HG_HEAD_DIM), 0.02),
        "od_w_in": nrm((N_ODD, D_MODEL, S5_WIDTH), D_MODEL ** -0.5),
        "s5_a_re": -0.5 + nrm((N_ODD, 2, S5_GROUPS, S5_STATE), 0.01),
        "s5_a_im": math.pi * jnp.arange(S5_STATE, dtype=F32) + nrm((N_ODD, 2, S5_GROUPS, S5_STATE), 0.01),
        "s5_log_dt": unif((N_ODD, 2, S5_GROUPS), math.log(1e-3), math.log(1e-1)),
        "s5_b_re": nrm((N_ODD, 2, S5_GROUPS, S5_STATE, S5_GROUP), S5_GROUP ** -0.5),
        "s5_b_im": nrm((N_ODD, 2, S5_GROUPS, S5_STATE, S5_GROUP), S5_GROUP ** -0.5),
        "s5_c_re": nrm((N_ODD, 2, S5_GROUPS, S5_GROUP, S5_STATE), S5_STATE ** -0.5),
        "s5_c_im": nrm((N_ODD, 2, S5_GROUPS, S5_GROUP, S5_STATE), S5_STATE ** -0.5),
        "s5_d": 1.0 + nrm((N_ODD, S5_WIDTH), 0.1),
        "od_w_val": nrm((N_ODD, S5_WIDTH, D_MODEL), S5_WIDTH ** -0.5),
        "od_w_gate": nrm((N_ODD, S5_WIDTH, D_MODEL), S5_WIDTH ** -0.5),
    }


def reference(x, c, ctx, c_ctx, ada_w, ada_b, norm_mix_g, norm_ffn_g, ffn_w_gate, ffn_w_up, ffn_w_down,
              final_norm_g, ev_w_in, ev_w_out, ssd_conv_w, ssd_conv_b, ssd_dt_bias, ssd_a_log, ssd_d, ssd_norm_g,
              hg_lb_logits, hg_norm_g, od_w_in, s5_a_re, s5_a_im, s5_log_dt, s5_b_re, s5_b_im, s5_c_re, s5_c_im,
              s5_d, od_w_val, od_w_gate):
    rows = x.shape[1] // GRID_W
    lower_bounds = jnp.cumsum(jax.nn.softmax(hg_lb_logits.astype(F32), axis=0), axis=0)
    h = x.astype(F32)
    hc = ctx.astype(F32)
    for layer in range(DEPTH):
        ctx_out = layer < DEPTH - 1
        j = layer // 2
        sm, scm, gm, sf, scf, gf = ada(c, ada_w[layer], ada_b[layer])
        csm, cscm, cgm, csf, cscf, cgf = ada(c_ctx, ada_w[layer], ada_b[layer])
        a_lat = modulate(h, norm_mix_g[layer], sm[:, None], scm[:, None])
        a_ctx = modulate(hc, norm_mix_g[layer], csm, cscm)
        if layer % 2 == 0:
            y_lat, y_ctx = even_mixer(a_lat, a_ctx, lower_bounds[j], ev_w_in[j], ev_w_out[j], ssd_conv_w[j],
                                      ssd_conv_b[j], ssd_dt_bias[j], ssd_a_log[j], ssd_d[j], ssd_norm_g[j],
                                      hg_norm_g[j], ctx_out)
        else:
            y_lat, y_ctx = odd_mixer(a_lat, a_ctx, rows, od_w_in[j], s5_a_re[j], s5_a_im[j], s5_log_dt[j],
                                     s5_b_re[j], s5_b_im[j], s5_c_re[j], s5_c_im[j], s5_d[j], od_w_val[j],
                                     od_w_gate[j], ctx_out)
        h = h + gm[:, None] * y_lat
        h = h + gf[:, None] * swiglu(modulate(h, norm_ffn_g[layer], sf[:, None], scf[:, None]),
                                     ffn_w_gate[layer], ffn_w_up[layer], ffn_w_down[layer])
        if ctx_out:
            hc = hc + cgm * y_ctx
            hc = hc + cgf * swiglu(modulate(hc, norm_ffn_g[layer], csf, cscf),
                                   ffn_w_gate[layer], ffn_w_up[layer], ffn_w_down[layer])
    return (rms(h) * final_norm_g).astype(x.dtype)
```

```python
import functools
import math

import jax
import jax.numpy as jnp
from jax import lax
from jax.experimental import pallas as pl
from jax.experimental.pallas import tpu as pltpu

F32 = jnp.float32
BF16 = jnp.bfloat16
EPS = 1e-6

LANES = 128
SUBLANES = 8
GRID_W = 64
SCAN_CHUNK = 64
SSD_HEADS = 16
SSD_HEAD_DIM = 64
SSD_STATE = 128
SSD_GROUPS = 2
SSD_CONV = 5
HG_HEADS = 8
HG_HEAD_DIM = 128
S5_GROUP = 16
S5_STATE = 64
S5_SUB = 16
S5_GPB = LANES // S5_GROUP

ZS_XBC = 0
ZS_DT = 1536
ZS_Z = 2048
ZS_Q = 3072
ZS_V = 4096
ZS_G = 5120
ZS_F = 6144
ZS_WIDTH = 8192


def _cparams(semantics, vmem_mb):
    return pltpu.CompilerParams(dimension_semantics=semantics, vmem_limit_bytes=vmem_mb * 1024 * 1024)


def _silu(x):
    return x * jax.nn.sigmoid(x)


def _modulate(h, mod_ref):
    ms = jnp.mean(h * h, axis=-1, keepdims=True)
    return h * lax.rsqrt(ms + EPS) * (mod_ref[0:1, :] * (1.0 + mod_ref[1:2, :])) + mod_ref[2:3, :]


def _split_dot(x, w, passes):
    acc, rem = None, x
    for _ in range(passes):
        hi = rem.astype(BF16)
        d = jnp.dot(hi, w, preferred_element_type=F32)
        acc = d if acc is None else acc + d
        rem = rem - hi.astype(F32)
    return acc


def _tri_cumsum(tri, x, passes):
    acc, rem = None, x
    for _ in range(passes):
        hi = rem.astype(BF16)
        d = jnp.dot(tri, hi, preferred_element_type=F32)
        acc = d if acc is None else acc + d
        rem = rem - hi.astype(F32)
    return acc


def _ada_kernel(c_ref, w_ref, b_ref, o_ref):
    o_ref[...] = jnp.dot(_silu(c_ref[...]), w_ref[...], precision=lax.Precision.HIGHEST,
                         preferred_element_type=F32) + b_ref[...]


def _ada(cvecs, ada_w, ada_b):
    depth, d, n = ada_w.shape
    tn = n // 4
    return pl.pallas_call(
        _ada_kernel,
        grid=(depth, n // tn),
        in_specs=[pl.BlockSpec((SUBLANES, d), lambda l, j: (0, 0)),
                  pl.BlockSpec((None, d, tn), lambda l, j: (l, 0, j)),
                  pl.BlockSpec((None, 1, tn), lambda l, j: (l, 0, j))],
        out_specs=pl.BlockSpec((None, SUBLANES, tn), lambda l, j: (l, 0, j)),
        out_shape=jax.ShapeDtypeStruct((depth, SUBLANES, n), F32),
        compiler_params=_cparams(("arbitrary", "arbitrary"), 40),
        name="ada",
    )(cvecs, ada_w, ada_b.reshape(depth, 1, n))


def _mod_matmul_kernel(h_ref, mod_ref, w_ref, o_ref, a_ref):
    @pl.when(pl.program_id(1) == 0)
    def _():
        a_ref[...] = _modulate(h_ref[...], mod_ref).astype(BF16)

    o_ref[...] = jnp.dot(a_ref[...], w_ref[...], preferred_element_type=F32)


def _mod_matmul(h, mod, w):
    t, d = h.shape
    n = w.shape[1]
    tm = min(t, 1024)
    tn = 1024
    return pl.pallas_call(
        _mod_matmul_kernel,
        grid=(t // tm, n // tn),
        in_specs=[pl.BlockSpec((tm, d), lambda i, j: (i, 0)),
                  pl.BlockSpec((SUBLANES, d), lambda i, j: (0, 0)),
                  pl.BlockSpec((d, tn), lambda i, j: (0, j))],
        out_specs=pl.BlockSpec((tm, tn), lambda i, j: (i, j)),
        out_shape=jax.ShapeDtypeStruct((t, n), F32),
        scratch_shapes=[pltpu.VMEM((tm, d), BF16)],
        compiler_params=_cparams(("arbitrary", "arbitrary"), 48),
        name="even_in",
    )(h, mod, w)


def _even_scan_kernel(xbc_ref, hp_ref, hn_ref, dt_ref, q_ref, v_ref, f_ref,
                      cw_ref, cb_ref, dtc_ref, dsk_ref, lb_ref, e_ref, h0s_ref, h0g_ref,
                      y_ref, o_ref, st_ref, gt_ref, *, reverse, nch):
    Q = SCAN_CHUNK
    c = pl.program_id(0)
    blk = (nch - 1 - c) if reverse else c

    @pl.when(c == 0)
    def _():
        st_ref[...] = h0s_ref[...]
        gt_ref[...] = h0g_ref[...]

    ti = lax.broadcasted_iota(jnp.int32, (Q, Q), 0)
    si = lax.broadcasted_iota(jnp.int32, (Q, Q), 1)
    mask = (ti <= si) if reverse else (ti >= si)
    tri = jnp.where(mask, 1.0, 0.0).astype(BF16)
    edge = 0 if reverse else Q - 1

    prev = jnp.where(blk == 0, 0.0, hp_ref[...])
    nxt = jnp.where(blk == nch - 1, 0.0, hn_ref[...])
    ext = jnp.concatenate([prev, xbc_ref[...], nxt], axis=0)
    conv = cb_ref[...]
    for j in range(SSD_CONV):
        off = SUBLANES - SSD_CONV // 2 + j
        conv = conv + cw_ref[j:j + 1, :] * ext[off:off + Q]
    xbc = _silu(conv)
    width = SSD_HEADS * SSD_HEAD_DIM
    gw = width // SSD_GROUPS
    xs = xbc[:, :width]

    dtv = jax.nn.softplus(dt_ref[...] + dtc_ref[0:1, :])
    la = -dtv * jnp.exp(dtc_ref[1:2, :])
    acum = _tri_cumsum(tri, la, 3)
    last = acum[edge:edge + 1, :]
    acum_t = acum.T
    e_a = jnp.exp(acum)
    w_end = jnp.exp(last - acum)
    e_last = jnp.broadcast_to(jnp.exp(last), (SUBLANES, LANES))
    xe = _split_dot(jnp.concatenate([dtv, e_a, w_end, e_last], axis=0), e_ref[...], 2)
    dt_x, ea_x, wend_x, elast_x = xe[:Q], xe[Q:2 * Q], xe[2 * Q:3 * Q], xe[3 * Q:3 * Q + 1]
    xdt = xs * dt_x
    xw = (xdt * wend_x).astype(BF16)
    xdt_b = xdt.astype(BF16)
    lane = lax.broadcasted_iota(jnp.int32, (Q, LANES), 1)
    lo = lane < SSD_HEAD_DIM
    zero_b = jnp.zeros((Q, LANES), BF16)
    ys = []
    for g in range(SSD_GROUPS):
        bm = xbc[:, width + g * SSD_STATE:width + (g + 1) * SSD_STATE].astype(BF16)
        cm = xbc[:, width + (SSD_GROUPS + g) * SSD_STATE:width + (SSD_GROUPS + g + 1) * SSD_STATE].astype(BF16)
        cb = lax.dot_general(cm, bm, (((1,), (1,)), ((), ())), preferred_element_type=F32)
        y_state = jnp.dot(cm, st_ref[g].astype(BF16), preferred_element_type=F32)
        pairs = []
        for hp in range(gw // LANES):
            xp = xdt_b[:, g * gw + hp * LANES:g * gw + (hp + 1) * LANES]
            acc = None
            for half in range(2):
                h = (g * gw + hp * LANES) // SSD_HEAD_DIM + half
                seg = acum[:, h:h + 1] - acum_t[h:h + 1, :]
                m = jnp.where(mask, cb * jnp.exp(seg), 0.0).astype(BF16)
                xh = jnp.where(lo, xp, zero_b) if half == 0 else jnp.where(lo, zero_b, xp)
                d = jnp.dot(m, xh, preferred_element_type=F32)
                acc = d if acc is None else acc + d
            pairs.append(acc)
        y_g = jnp.concatenate(pairs, axis=1) + y_state * ea_x[:, g * gw:(g + 1) * gw]
        ys.append(y_g)
        upd = lax.dot_general(bm, xw[:, g * gw:(g + 1) * gw], (((0,), (0,)), ((), ())),
                              preferred_element_type=F32)
        st_ref[g] = st_ref[g] * elast_x[:, g * gw:(g + 1) * gw] + upd
    y = jnp.concatenate(ys, axis=1)
    if not reverse:
        y = y + dsk_ref[...] * xs
    y_ref[...] = y

    lb = lb_ref[...]
    f = lb + (1.0 - lb) * jax.nn.sigmoid(f_ref[...])
    bcum = _tri_cumsum(tri, jnp.log(f), 3)
    blast = bcum[edge:edge + 1, :]
    qe = (_silu(q_ref[...]) * jnp.exp(bcum)).astype(BF16)
    k_end = (1.0 - f) * jnp.exp(blast - bcum)
    k_til = (k_end * jnp.exp(-blast)).astype(BF16)
    k_end = k_end.astype(BF16)
    e_blast = jnp.exp(blast)
    vb = v_ref[...].astype(BF16)
    outs = []
    for h in range(HG_HEADS):
        sl = slice(h * HG_HEAD_DIM, (h + 1) * HG_HEAD_DIM)
        att = lax.dot_general(qe[:, sl], k_til[:, sl], (((1,), (1,)), ((), ())), preferred_element_type=F32)
        att = jnp.where(mask, att, 0.0).astype(BF16)
        gt = gt_ref[h]
        o_h = jnp.dot(att, vb[:, sl], preferred_element_type=F32)
        o_h = o_h + lax.dot_general(qe[:, sl], gt.astype(BF16), (((1,), (1,)), ((), ())),
                                    preferred_element_type=F32)
        outs.append(o_h)
        upd = lax.dot_general(vb[:, sl], k_end[:, sl], (((0,), (0,)), ((), ())), preferred_element_type=F32)
        gt_ref[h] = gt * e_blast[:, sl] + upd
    o_ref[...] = jnp.concatenate(outs, axis=1)


def _even_scan(zs, consts, h0s, h0g, reverse):
    t = zs.shape[0]
    Q = SCAN_CHUNK
    nch = t // Q
    d = 1 if reverse else 0
    cw, cb, dtc, dsk, lb, e16 = consts
    hb = Q // SUBLANES

    def cidx(c):
        return (nch - 1 - c) if reverse else c

    def col(width, start):
        return lambda c: (cidx(c), start // width)

    in_specs = [
        pl.BlockSpec((Q, 1536), col(1536, ZS_XBC)),
        pl.BlockSpec((SUBLANES, 1536), lambda c: (jnp.maximum(cidx(c) * hb - 1, 0), 0)),
        pl.BlockSpec((SUBLANES, 1536), lambda c: (jnp.minimum((cidx(c) + 1) * hb, t // SUBLANES - 1), 0)),
        pl.BlockSpec((Q, LANES), col(LANES, ZS_DT + d * LANES)),
        pl.BlockSpec((Q, 1024), col(1024, ZS_Q)),
        pl.BlockSpec((Q, 1024), col(1024, ZS_V)),
        pl.BlockSpec((Q, 1024), col(1024, ZS_F + d * 1024)),
        pl.BlockSpec(cw.shape, lambda c: (0, 0)),
        pl.BlockSpec(cb.shape, lambda c: (0, 0)),
        pl.BlockSpec((None,) + dtc.shape[1:], lambda c: (d, 0, 0)),
        pl.BlockSpec(dsk.shape, lambda c: (0, 0)),
        pl.BlockSpec(lb.shape, lambda c: (0, 0)),
        pl.BlockSpec(e16.shape, lambda c: (0, 0)),
        pl.BlockSpec(h0s.shape, lambda c: (0, 0, 0)),
        pl.BlockSpec(h0g.shape, lambda c: (0, 0, 0)),
    ]
    out_specs = [
        pl.BlockSpec((Q, 1024), lambda c: (cidx(c), 0)),
        pl.BlockSpec((Q, 1024), lambda c: (cidx(c), 0)),
        pl.BlockSpec(h0s.shape, lambda c: (0, 0, 0)),
        pl.BlockSpec(h0g.shape, lambda c: (0, 0, 0)),
    ]
    out_shape = [jax.ShapeDtypeStruct((t, 1024), F32), jax.ShapeDtypeStruct((t, 1024), F32),
                 jax.ShapeDtypeStruct(h0s.shape, F32), jax.ShapeDtypeStruct(h0g.shape, F32)]
    return pl.pallas_call(
        functools.partial(_even_scan_kernel, reverse=reverse, nch=nch),
        grid=(nch,),
        in_specs=in_specs,
        out_specs=out_specs,
        out_shape=out_shape,
        compiler_params=_cparams(("arbitrary",), 40),
        name="even_scan_bwd" if reverse else "even_scan_fwd",
    )(zs, zs, zs, zs, zs, zs, zs, cw, cb, dtc, dsk, lb, e16, h0s, h0g)


def _even_out_kernel(h_ref, z_ref, g_ref, yf_ref, yb_ref, of_ref, ob_ref, w_ref, vec_ref, o_ref):
    y = (yf_ref[...] + yb_ref[...]) * _silu(z_ref[...])
    o = of_ref[...] + ob_ref[...]
    gw = y.shape[1] // SSD_GROUPS
    parts = []
    for g in range(SSD_GROUPS):
        yg = y[:, g * gw:(g + 1) * gw]
        ms = jnp.mean(yg * yg, axis=-1, keepdims=True)
        parts.append(yg * lax.rsqrt(ms + EPS))
    yn = (jnp.concatenate(parts, axis=1) * vec_ref[0:1, :]).astype(BF16)
    parts = []
    for hh in range(HG_HEADS):
        oh = o[:, hh * HG_HEAD_DIM:(hh + 1) * HG_HEAD_DIM]
        ms = jnp.mean(oh * oh, axis=-1, keepdims=True)
        parts.append(oh * lax.rsqrt(ms + EPS))
    on = (jnp.concatenate(parts, axis=1) * vec_ref[1:2, :] * _silu(g_ref[...])).astype(BF16)
    half = yn.shape[1]
    mix = jnp.dot(yn, w_ref[:half, :], preferred_element_type=F32)
    mix = mix + jnp.dot(on, w_ref[half:, :], preferred_element_type=F32)
    o_ref[...] = h_ref[...] + vec_ref[2:3, :] * mix


def _even_out(h, zs, yf, yb, of, ob, w_out, vec):
    t, d = h.shape
    tm = min(t, 512)
    row = lambda i: (i, 0)
    return pl.pallas_call(
        _even_out_kernel,
        grid=(t // tm,),
        in_specs=[pl.BlockSpec((tm, d), row),
                  pl.BlockSpec((tm, 1024), lambda i: (i, ZS_Z // 1024)),
                  pl.BlockSpec((tm, 1024), lambda i: (i, ZS_G // 1024)),
                  pl.BlockSpec((tm, 1024), row), pl.BlockSpec((tm, 1024), row),
                  pl.BlockSpec((tm, 1024), row), pl.BlockSpec((tm, 1024), row),
                  pl.BlockSpec(w_out.shape, lambda i: (0, 0)),
                  pl.BlockSpec(vec.shape, lambda i: (0, 0))],
        out_specs=pl.BlockSpec((tm, d), row),
        out_shape=jax.ShapeDtypeStruct((t, d), F32),
        compiler_params=_cparams(("arbitrary",), 56),
        name="even_out",
    )(h, zs, zs, yf, yb, of, ob, w_out, vec)


def _ffn_kernel(h_ref, mod_ref, wg_ref, wu_ref, wd_ref, o_ref, a_ref, acc_ref, *, final):
    j = pl.program_id(1)

    @pl.when(j == 0)
    def _():
        a_ref[...] = _modulate(h_ref[...], mod_ref).astype(BF16)
        acc_ref[...] = jnp.zeros_like(acc_ref)

    a = a_ref[...]
    gate = jnp.dot(a, wg_ref[...], preferred_element_type=F32)
    up = jnp.dot(a, wu_ref[...], preferred_element_type=F32)
    act = (_silu(gate) * up).astype(BF16)
    acc_ref[...] += jnp.dot(act, wd_ref[...], preferred_element_type=F32)

    @pl.when(j == pl.num_programs(1) - 1)
    def _():
        out = h_ref[...] + mod_ref[3:4, :] * acc_ref[...]
        if final:
            ms = jnp.mean(out * out, axis=-1, keepdims=True)
            out = out * lax.rsqrt(ms + EPS) * mod_ref[4:5, :]
        o_ref[...] = out


def _ffn(h, mod, wg, wu, wd, final):
    t, d = h.shape
    f = wg.shape[1]
    tm = min(t, 512)
    tf = f // 2
    return pl.pallas_call(
        functools.partial(_ffn_kernel, final=final),
        grid=(t // tm, f // tf),
        in_specs=[pl.BlockSpec((tm, d), lambda i, j: (i, 0)),
                  pl.BlockSpec((SUBLANES, d), lambda i, j: (0, 0)),
                  pl.BlockSpec((d, tf), lambda i, j: (0, j)),
                  pl.BlockSpec((d, tf), lambda i, j: (0, j)),
                  pl.BlockSpec((tf, d), lambda i, j: (j, 0))],
        out_specs=pl.BlockSpec((tm, d), lambda i, j: (i, 0)),
        out_shape=jax.ShapeDtypeStruct((t, d), F32),
        scratch_shapes=[pltpu.VMEM((tm, d), BF16), pltpu.VMEM((tm, d), F32)],
        compiler_params=_cparams(("arbitrary", "arbitrary"), 56),
        name="ffn_final" if final else "ffn",
    )(h, mod, wg, wu, wd)


def _odd_in_kernel(h_ref, mod_ref, w_ref, o_ref):
    a = _modulate(h_ref[...], mod_ref).astype(BF16)
    u = jnp.dot(a, w_ref[...], preferred_element_type=F32)
    for j in range(o_ref.shape[0]):
        o_ref[j] = u[:, j * LANES:(j + 1) * LANES]


def _odd_in(h, mod, w, grid_w):
    t, d = h.shape
    rows = t // grid_w
    n = w.shape[1]
    nblk = n // LANES
    return pl.pallas_call(
        _odd_in_kernel,
        grid=(grid_w,),
        in_specs=[pl.BlockSpec((rows, d), lambda w_: (0, w_)),
                  pl.BlockSpec((SUBLANES, d), lambda w_: (0, 0)),
                  pl.BlockSpec(w.shape, lambda w_: (0, 0))],
        out_specs=pl.BlockSpec((nblk, rows, LANES), lambda w_: (0, w_, 0)),
        out_shape=jax.ShapeDtypeStruct((nblk, t, LANES), F32),
        compiler_params=_cparams(("arbitrary",), 32),
        name="odd_in",
    )(h.reshape(rows, grid_w * d), mod, w)


def _s5_scan_kernel(u_ref, win_ref, wout_ref, tt_ref, lam_ref, h0_ref, y_ref, hf_ref, s_ref, c_ref, *, reverse, nb):
    b = pl.program_id(1)
    half = c_ref.shape[1] // 2

    @pl.when(b == 0)
    def _():
        c_ref[...] = h0_ref[0:1, :]

    xb = u_ref[...].astype(BF16)
    s_ref[...] = jnp.dot(xb, win_ref[...], preferred_element_type=F32)
    lr = lam_ref[0:1, :half]
    li = lam_ref[0:1, half:]

    def step(i, carry):
        k = (nb - 1 - i) if reverse else i
        pr = c_ref[:, :half]
        pi = c_ref[:, half:]
        row = s_ref[pl.ds(k, 1), :]
        c_ref[:, :half] = lr * pr - li * pi + row[:, :half]
        c_ref[:, half:] = lr * pi + li * pr + row[:, half:]
        s_ref[pl.ds(k, 1), :half] = pr
        s_ref[pl.ds(k, 1), half:] = pi
        return carry

    lax.fori_loop(0, nb, step, 0)
    y = jnp.dot(s_ref[...].astype(BF16), wout_ref[...], preferred_element_type=F32)
    y_ref[...] = y + jnp.dot(xb, tt_ref[...], preferred_element_type=F32)

    @pl.when(b == pl.num_programs(1) - 1)
    def _():
        hf_ref[...] = jnp.broadcast_to(c_ref[...], hf_ref.shape)


def _s5_scan(u, win, wout, tt, lam, h0, reverse):
    nj, n, width = u.shape
    nb = min(n, 256)
    nblk = n // nb
    ns = win.shape[2]

    def bidx(b):
        return (nblk - 1 - b) if reverse else b

    return pl.pallas_call(
        functools.partial(_s5_scan_kernel, reverse=reverse, nb=nb),
        grid=(nj, nblk),
        in_specs=[pl.BlockSpec((None, nb, width), lambda j, b: (j, bidx(b), 0)),
                  pl.BlockSpec((None, width, ns), lambda j, b: (j, 0, 0)),
                  pl.BlockSpec((None, ns, width), lambda j, b: (j, 0, 0)),
                  pl.BlockSpec((None, width, width), lambda j, b: (j, 0, 0)),
                  pl.BlockSpec((None, SUBLANES, ns), lambda j, b: (j, 0, 0)),
                  pl.BlockSpec((None, SUBLANES, ns), lambda j, b: (j, 0, 0))],
        out_specs=[pl.BlockSpec((None, nb, width), lambda j, b: (j, bidx(b), 0)),
                   pl.BlockSpec((None, SUBLANES, ns), lambda j, b: (j, 0, 0))],
        out_shape=[jax.ShapeDtypeStruct((nj, n, width), F32), jax.ShapeDtypeStruct((nj, SUBLANES, ns), F32)],
        scratch_shapes=[pltpu.VMEM((nb, ns), F32), pltpu.VMEM((1, ns), F32)],
        compiler_params=_cparams(("arbitrary", "arbitrary"), 56),
        name="s5_scan_bwd" if reverse else "s5_scan_fwd",
    )(u, win, wout, tt, lam, h0)


def _odd_out_kernel(h_ref, u_ref, yf_ref, yb_ref, wv_ref, wg_ref, vec_ref, dsk_ref, o_ref):
    nblk = u_ref.shape[0]
    y = jnp.concatenate([u_ref[j] * dsk_ref[j] + yf_ref[j] + yb_ref[j] for j in range(nblk)], axis=1)
    a = jax.nn.gelu(y).astype(BF16)
    val = jnp.dot(a, wv_ref[...], preferred_element_type=F32)
    gate = jnp.dot(a, wg_ref[...], preferred_element_type=F32)
    o_ref[...] = h_ref[...] + vec_ref[0:1, :] * (val * jax.nn.sigmoid(gate))


def _odd_out(h, u, yf, yb, w_val, w_gate, vec, dsk, grid_w):
    t, d = h.shape
    rows = t // grid_w
    nblk = u.shape[0]
    plane = pl.BlockSpec((nblk, rows, LANES), lambda w_: (0, w_, 0))
    out = pl.pallas_call(
        _odd_out_kernel,
        grid=(grid_w,),
        in_specs=[pl.BlockSpec((rows, d), lambda w_: (0, w_)), plane, plane, plane,
                  pl.BlockSpec(w_val.shape, lambda w_: (0, 0)),
                  pl.BlockSpec(w_gate.shape, lambda w_: (0, 0)),
                  pl.BlockSpec(vec.shape, lambda w_: (0, 0)),
                  pl.BlockSpec(dsk.shape, lambda w_: (0, 0, 0))],
        out_specs=pl.BlockSpec((rows, d), lambda w_: (0, w_)),
        out_shape=jax.ShapeDtypeStruct((rows, grid_w * d), F32),
        compiler_params=_cparams(("arbitrary",), 32),
        name="odd_out",
    )(h.reshape(rows, grid_w * d), u, yf, yb, w_val, w_gate, vec, dsk)
    return out.reshape(t, d)


def _s5_matrices(a_re, a_im, log_dt, b_re, b_im, c_re, c_im, reverse):
    g, p = a_re.shape
    cdim = b_re.shape[2]
    T = S5_SUB
    nj = g // S5_GPB
    hi = lax.Precision.HIGHEST
    dt = jnp.exp(log_dt)[:, None]
    tau = jnp.arange(T + 1, dtype=F32)[:, None, None]
    mag = jnp.exp(a_re * dt * tau)
    pr = mag * jnp.cos(a_im * dt * tau)
    pi = mag * jnp.sin(a_im * dt * tau)
    lr, li = pr[1], pi[1]
    den = a_re * a_re + a_im * a_im
    cr = ((lr - 1.0) * a_re + li * a_im) / den
    ci = (li * a_re - (lr - 1.0) * a_im) / den
    bbr = cr[..., None] * b_re - ci[..., None] * b_im
    bbi = cr[..., None] * b_im + ci[..., None] * b_re
    steps = jnp.arange(T)
    e_in = steps if reverse else T - 1 - steps
    e_out = (T - steps) if reverse else steps + 1
    pin_r, pin_i = pr[e_in], pi[e_in]
    win_r = pin_r[..., None] * bbr - pin_i[..., None] * bbi
    win_i = pin_r[..., None] * bbi + pin_i[..., None] * bbr
    win = jnp.stack([win_r, win_i], axis=0).transpose(1, 2, 4, 0, 3)
    po_r, po_i = pr[e_out], pi[e_out]
    wo_r = c_re[None] * po_r[:, :, None, :] - c_im[None] * po_i[:, :, None, :]
    wo_i = -(c_re[None] * po_i[:, :, None, :] + c_im[None] * po_r[:, :, None, :])
    wout = jnp.stack([wo_r, wo_i], axis=0).transpose(0, 2, 4, 1, 3)
    kr = pr[:T, :, None, :] * c_re[None] - pi[:T, :, None, :] * c_im[None]
    ki = pr[:T, :, None, :] * c_im[None] + pi[:T, :, None, :] * c_re[None]
    kt = (jnp.einsum('tgop,gpc->tgco', kr, bbr, precision=hi)
          - jnp.einsum('tgop,gpc->tgco', ki, bbi, precision=hi))
    lag = (steps[:, None] - steps[None, :]) if reverse else (steps[None, :] - steps[:, None])
    toe = jnp.where((lag >= 0)[:, :, None, None, None], kt[jnp.clip(lag, 0, T - 1)], 0.0)
    eye = jnp.eye(S5_GPB, dtype=F32)
    win = win.reshape(T, nj, S5_GPB, cdim, 2, p)
    win = jnp.einsum('sjgcrp,gh->jsgcrhp', win, eye).reshape(nj, T * LANES, 2 * S5_GPB * p)
    wout = wout.reshape(2, nj, S5_GPB, p, T, cdim)
    wout = jnp.einsum('rjgptc,gh->jrgpthc', wout, eye).reshape(nj, 2 * S5_GPB * p, T * LANES)
    toe = toe.reshape(T, T, nj, S5_GPB, cdim, cdim)
    toe = jnp.einsum('stjgio,gh->jsgitho', toe, eye).reshape(nj, T * LANES, T * LANES)
    lam_t = jnp.concatenate([pr[T].reshape(nj, 1, S5_GPB * p), pi[T].reshape(nj, 1, S5_GPB * p)], axis=2)
    lam_t = jnp.broadcast_to(lam_t, (nj, SUBLANES, 2 * S5_GPB * p))
    return win.astype(BF16), wout.astype(BF16), toe.astype(BF16), lam_t


def _rows(*vs, width):
    out = jnp.zeros((SUBLANES, width), F32)
    for i, v in enumerate(vs):
        out = out.at[i].set(v.astype(F32))
    return out


def kernel(x, c, ctx, c_ctx, ada_w, ada_b, norm_mix_g, norm_ffn_g, ffn_w_gate, ffn_w_up, ffn_w_down, final_norm_g, ev_w_in, ev_w_out, ssd_conv_w, ssd_conv_b, ssd_dt_bias, ssd_a_log, ssd_d, ssd_norm_g, hg_lb_logits, hg_norm_g, od_w_in, s5_a_re, s5_a_im, s5_log_dt, s5_b_re, s5_b_im, s5_c_re, s5_c_im, s5_d, od_w_val, od_w_gate):
    d = x.shape[-1]
    lat = x[0].astype(F32)
    hc = ctx[0].astype(F32)

    m = _ada(_rows(c[0], c_ctx, width=d), ada_w, ada_b)

    def ada_vecs(layer, stream):
        return [m[layer, stream, i * d:(i + 1) * d] for i in range(6)]

    w = ev_w_in[0]
    n_x = SSD_HEADS * SSD_HEAD_DIM
    n_xbc = n_x + 2 * SSD_GROUPS * SSD_STATE
    n_hg = HG_HEADS * HG_HEAD_DIM
    o_z, o_xbc, o_dt = 0, n_x, n_x + n_xbc
    o_q = o_dt + 2 * SSD_HEADS
    o_f = o_q + n_hg
    o_v = o_f + 2 * n_hg
    o_g = o_v + n_hg
    pad = lambda n: jnp.zeros((d, n), F32)
    w_in = jnp.concatenate([
        w[:, o_xbc:o_xbc + n_xbc],
        w[:, o_dt:o_dt + SSD_HEADS], pad(LANES - SSD_HEADS),
        w[:, o_dt + SSD_HEADS:o_dt + 2 * SSD_HEADS], pad(LANES - SSD_HEADS),
        pad(ZS_Z - ZS_DT - 2 * LANES),
        w[:, o_z:o_z + n_x], w[:, o_q:o_q + n_hg], w[:, o_v:o_v + n_hg], w[:, o_g:o_g + n_hg],
        w[:, o_f:o_f + 2 * n_hg]], axis=1).astype(BF16)
    cw = jnp.zeros((SUBLANES, n_xbc), F32).at[:SSD_CONV].set(ssd_conv_w[0])
    cb = ssd_conv_b[0].reshape(1, n_xbc)
    dtc = jnp.zeros((2, SUBLANES, LANES), F32)
    dtc = dtc.at[:, 0, :SSD_HEADS].set(ssd_dt_bias[0]).at[:, 1, :SSD_HEADS].set(ssd_a_log[0])
    dsk = jnp.repeat(ssd_d[0], SSD_HEAD_DIM).reshape(1, n_x)
    lower = jnp.cumsum(jax.nn.softmax(hg_lb_logits.astype(F32), axis=0), axis=0)[0].reshape(1, n_hg)
    head_of_lane = jnp.arange(n_x) // SSD_HEAD_DIM
    e16 = (jnp.arange(LANES)[:, None] == head_of_lane[None, :]).astype(BF16)
    consts = (cw, cb, dtc, dsk, lower, e16)
    w_out = ev_w_out[0].astype(BF16)
    out_vec = lambda gate: _rows(ssd_norm_g[0], jnp.tile(hg_norm_g[0], HG_HEADS), gate, width=d)
    wg0, wu0, wd0 = (t_[0].astype(BF16) for t_ in (ffn_w_gate, ffn_w_up, ffn_w_down))

    zero_s = jnp.zeros((SSD_GROUPS, SSD_STATE, n_x // SSD_GROUPS), F32)
    zero_g = jnp.zeros((HG_HEADS, HG_HEAD_DIM, HG_HEAD_DIM), F32)
    streams = {}
    for stream, h in ((1, hc), (0, lat)):
        sm, scm, gm, sf, scf, gf = ada_vecs(0, stream)
        zs = _mod_matmul(h, _rows(norm_mix_g[0], scm, sm, width=d), w_in)
        streams[stream] = (h, zs, gm, sf, scf, gf)
    scans = {}
    for reverse in (False, True):
        yc, oc, s_c, g_c = _even_scan(streams[1][1], consts, zero_s, zero_g, reverse)
        yl, ol, _, _ = _even_scan(streams[0][1], consts, s_c, g_c, reverse)
        scans[reverse] = {1: (yc, oc), 0: (yl, ol)}
    layer0 = {}
    for stream in (1, 0):
        h, zs, gm, sf, scf, gf = streams[stream]
        yf, of = scans[False][stream]
        yb, ob = scans[True][stream]
        h1 = _even_out(h, zs, yf, yb, of, ob, w_out, out_vec(gm))
        layer0[stream] = _ffn(h1, _rows(norm_ffn_g[0], scf, sf, gf, width=d), wg0, wu0, wd0, final=False)

    lat, hc = layer0[0], layer0[1]
    w_s5 = od_w_in[0].astype(BF16)
    u = {}
    for stream, h, gw in ((1, hc, 1), (0, lat, GRID_W)):
        sm, scm = ada_vecs(1, stream)[:2]
        planes = _odd_in(h, _rows(norm_mix_g[1], scm, sm, width=d), w_s5, gw)
        u[stream] = planes.reshape(planes.shape[0], planes.shape[1] // S5_SUB, S5_SUB * LANES)
    ys = {}
    for reverse in (False, True):
        di = 1 if reverse else 0
        win, wout, toe, lam_t = _s5_matrices(s5_a_re[0, di], s5_a_im[0, di], s5_log_dt[0, di], s5_b_re[0, di],
                                             s5_b_im[0, di], s5_c_re[0, di], s5_c_im[0, di], reverse)
        _, h_ctx = _s5_scan(u[1], win, wout, toe, lam_t, jnp.zeros_like(lam_t), reverse)
        ys[reverse], _ = _s5_scan(u[0], win, wout, toe, lam_t, h_ctx, reverse)
    _, _, gm, sf, scf, gf = ada_vecs(1, 0)
    nblk = u[0].shape[0]
    plane = lambda a: a.reshape(nblk, -1, LANES)
    h3 = _odd_out(lat, plane(u[0]), plane(ys[False]), plane(ys[True]), od_w_val[0].astype(BF16),
                  od_w_gate[0].astype(BF16), _rows(gm, width=d), s5_d[0].reshape(nblk, 1, LANES), GRID_W)
    wg1, wu1, wd1 = (t_[1].astype(BF16) for t_ in (ffn_w_gate, ffn_w_up, ffn_w_down))
    out = _ffn(h3, _rows(norm_ffn_g[1], scf, sf, gf, final_norm_g, width=d), wg1, wu1, wd1, final=True)
    return out[None].astype(x.dtype)
```

```python
import functools
import math

import jax
import jax.numpy as jnp
from jax import lax
from jax.experimental import pallas as pl
from jax.experimental.pallas import tpu as pltpu

F32 = jnp.float32
BF16 = jnp.bfloat16
EPS = 1e-6

LANES = 128
SUBLANES = 8
GRID_W = 64
SCAN_CHUNK = 64
SSD_HEADS = 16
SSD_HEAD_DIM = 64
SSD_STATE = 128
SSD_GROUPS = 2
SSD_CONV = 5
HG_HEADS = 8
HG_HEAD_DIM = 128
S5_GROUP = 16
S5_STATE = 64
S5_SUB = 16
S5_GPB = LANES // S5_GROUP

ZS_XBC = 0
ZS_DT = 1536
ZS_Z = 2048
ZS_Q = 3072
ZS_V = 4096
ZS_G = 5120
ZS_F = 6144
ZS_WIDTH = 8192


def _cparams(semantics, vmem_mb):
    return pltpu.CompilerParams(dimension_semantics=semantics, vmem_limit_bytes=vmem_mb * 1024 * 1024)


def _silu(x):
    return x * jax.nn.sigmoid(x)


def _modulate(h, mod_ref):
    ms = jnp.mean(h * h, axis=-1, keepdims=True)
    return h * lax.rsqrt(ms + EPS) * (mod_ref[0:1, :] * (1.0 + mod_ref[1:2, :])) + mod_ref[2:3, :]


def _split_dot(x, w, passes):
    acc, rem = None, x
    for _ in range(passes):
        hi = rem.astype(BF16)
        d = jnp.dot(hi, w, preferred_element_type=F32)
        acc = d if acc is None else acc + d
        rem = rem - hi.astype(F32)
    return acc


def _tri_cumsum(tri, x, passes):
    acc, rem = None, x
    for _ in range(passes):
        hi = rem.astype(BF16)
        d = jnp.dot(tri, hi, preferred_element_type=F32)
        acc = d if acc is None else acc + d
        rem = rem - hi.astype(F32)
    return acc


def _ada_kernel(c_ref, w_ref, b_ref, o_ref):
    o_ref[...] = jnp.dot(_silu(c_ref[...]), w_ref[...], precision=lax.Precision.HIGHEST,
                         preferred_element_type=F32) + b_ref[...]


def _ada(cvecs, ada_w, ada_b):
    depth, d, n = ada_w.shape
    tn = n // 4
    return pl.pallas_call(
        _ada_kernel,
        grid=(depth, n // tn),
        in_specs=[pl.BlockSpec((SUBLANES, d), lambda l, j: (0, 0)),
                  pl.BlockSpec((None, d, tn), lambda l, j: (l, 0, j)),
                  pl.BlockSpec((None, 1, tn), lambda l, j: (l, 0, j))],
        out_specs=pl.BlockSpec((None, SUBLANES, tn), lambda l, j: (l, 0, j)),
        out_shape=jax.ShapeDtypeStruct((depth, SUBLANES, n), F32),
        compiler_params=_cparams(("arbitrary", "arbitrary"), 40),
        name="ada",
    )(cvecs, ada_w, ada_b.reshape(depth, 1, n))


def _mod_matmul_kernel(h_ref, mod_ref, w_ref, o_ref, a_ref):
    @pl.when(pl.program_id(1) == 0)
    def _():
        a_ref[...] = _modulate(h_ref[...], mod_ref).astype(BF16)

    o_ref[...] = jnp.dot(a_ref[...], w_ref[...], preferred_element_type=F32)


def _mod_matmul(h, mod, w):
    t, d = h.shape
    n = w.shape[1]
    tm = min(t, 1024)
    tn = 1024
    return pl.pallas_call(
        _mod_matmul_kernel,
        grid=(t // tm, n // tn),
        in_specs=[pl.BlockSpec((tm, d), lambda i, j: (i, 0)),
                  pl.BlockSpec((SUBLANES, d), lambda i, j: (0, 0)),
                  pl.BlockSpec((d, tn), lambda i, j: (0, j))],
        out_specs=pl.BlockSpec((tm, tn), lambda i, j: (i, j)),
        out_shape=jax.ShapeDtypeStruct((t, n), F32),
        scratch_shapes=[pltpu.VMEM((tm, d), BF16)],
        compiler_params=_cparams(("arbitrary", "arbitrary"), 48),
        name="even_in",
    )(h, mod, w)


def _even_scan_kernel(xbc_ref, hp_ref, hn_ref, dt_ref, q_ref, v_ref, f_ref,
                      cw_ref, cb_ref, dtc_ref, dsk_ref, lb_ref, e_ref, h0s_ref, h0g_ref,
                      y_ref, o_ref, st_ref, gt_ref, *, reverse, nch):
    Q = SCAN_CHUNK
    c = pl.program_id(0)
    blk = (nch - 1 - c) if reverse else c

    @pl.when(c == 0)
    def _():
        st_ref[...] = h0s_ref[...]
        gt_ref[...] = h0g_ref[...]

    ti = lax.broadcasted_iota(jnp.int32, (Q, Q), 0)
    si = lax.broadcasted_iota(jnp.int32, (Q, Q), 1)
    mask = (ti <= si) if reverse else (ti >= si)
    tri = jnp.where(mask, 1.0, 0.0).astype(BF16)
    edge = 0 if reverse else Q - 1

    prev = jnp.where(blk == 0, 0.0, hp_ref[...])
    nxt = jnp.where(blk == nch - 1, 0.0, hn_ref[...])
    ext = jnp.concatenate([prev, xbc_ref[...], nxt], axis=0)
    conv = cb_ref[...]
    for j in range(SSD_CONV):
        off = SUBLANES - SSD_CONV // 2 + j
        conv = conv + cw_ref[j:j + 1, :] * ext[off:off + Q]
    xbc = _silu(conv)
    width = SSD_HEADS * SSD_HEAD_DIM
    gw = width // SSD_GROUPS
    xs = xbc[:, :width]

    dtv = jax.nn.softplus(dt_ref[...] + dtc_ref[0:1, :])
    la = -dtv * jnp.exp(dtc_ref[1:2, :])
    acum = _tri_cumsum(tri, la, 3)
    last = acum[edge:edge + 1, :]
    acum_t = acum.T
    e_a = jnp.exp(acum)
    w_end = jnp.exp(last - acum)
    e_last = jnp.broadcast_to(jnp.exp(last), (SUBLANES, LANES))
    xe = _split_dot(jnp.concatenate([dtv, e_a, w_end], axis=0), e_ref[...], 1)
    dt_x, ea_x, wend_x = xe[:Q], xe[Q:2 * Q], xe[2 * Q:3 * Q]
    elast_x = _split_dot(e_last, e_ref[...], 3)[0:1]
    xdt = xs * dt_x
    xw = (xdt * wend_x).astype(BF16)
    xdt_b = xdt.astype(BF16)
    lane = lax.broadcasted_iota(jnp.int32, (Q, LANES), 1)
    lo = lane < SSD_HEAD_DIM
    zero_b = jnp.zeros((Q, LANES), BF16)
    ys = []
    for g in range(SSD_GROUPS):
        bm = xbc[:, width + g * SSD_STATE:width + (g + 1) * SSD_STATE].astype(BF16)
        cm = xbc[:, width + (SSD_GROUPS + g) * SSD_STATE:width + (SSD_GROUPS + g + 1) * SSD_STATE].astype(BF16)
        cb = lax.dot_general(cm, bm, (((1,), (1,)), ((), ())), preferred_element_type=F32)
        y_state = jnp.dot(cm, st_ref[g].astype(BF16), preferred_element_type=F32)
        pairs = []
        for hp in range(gw // LANES):
            xp = xdt_b[:, g * gw + hp * LANES:g * gw + (hp + 1) * LANES]
            acc = None
            for half in range(2):
                h = (g * gw + hp * LANES) // SSD_HEAD_DIM + half
                seg = acum[:, h:h + 1] - acum_t[h:h + 1, :]
                m = jnp.where(mask, cb * jnp.exp(seg), 0.0).astype(BF16)
                xh = jnp.where(lo, xp, zero_b) if half == 0 else jnp.where(lo, zero_b, xp)
                d = jnp.dot(m, xh, preferred_element_type=F32)
                acc = d if acc is None else acc + d
            pairs.append(acc)
        y_g = jnp.concatenate(pairs, axis=1) + y_state * ea_x[:, g * gw:(g + 1) * gw]
        ys.append(y_g)
        upd = lax.dot_general(bm, xw[:, g * gw:(g + 1) * gw], (((0,), (0,)), ((), ())),
                              preferred_element_type=F32)
        st_ref[g] = st_ref[g] * elast_x[:, g * gw:(g + 1) * gw] + upd
    y = jnp.concatenate(ys, axis=1)
    if not reverse:
        y = y + dsk_ref[...] * xs
    y_ref[...] = y

    lb = lb_ref[...]
    f = lb + (1.0 - lb) * jax.nn.sigmoid(f_ref[...])
    bcum = _tri_cumsum(tri, jnp.log(f), 2)
    blast = bcum[edge:edge + 1, :]
    qe = (_silu(q_ref[...]) * jnp.exp(bcum)).astype(BF16)
    k_end = (1.0 - f) * jnp.exp(blast - bcum)
    k_til = (k_end * jnp.exp(-blast)).astype(BF16)
    k_end = k_end.astype(BF16)
    e_blast = jnp.exp(blast)
    vb = v_ref[...].astype(BF16)
    outs = []
    for h in range(HG_HEADS):
        sl = slice(h * HG_HEAD_DIM, (h + 1) * HG_HEAD_DIM)
        att = lax.dot_general(qe[:, sl], k_til[:, sl], (((1,), (1,)), ((), ())), preferred_element_type=F32)
        att = jnp.where(mask, att, 0.0).astype(BF16)
        gt = gt_ref[h]
        o_h = jnp.dot(att, vb[:, sl], preferred_element_type=F32)
        o_h = o_h + lax.dot_general(qe[:, sl], gt.astype(BF16), (((1,), (1,)), ((), ())),
                                    preferred_element_type=F32)
        outs.append(o_h)
        upd = lax.dot_general(vb[:, sl], k_end[:, sl], (((0,), (0,)), ((), ())), preferred_element_type=F32)
        gt_ref[h] = gt * e_blast[:, sl] + upd
    o_ref[...] = jnp.concatenate(outs, axis=1)


def _even_scan(zs, consts, h0s, h0g, reverse):
    t = zs.shape[0]
    Q = SCAN_CHUNK
    nch = t // Q
    d = 1 if reverse else 0
    cw, cb, dtc, dsk, lb, e16 = consts
    hb = Q // SUBLANES

    def cidx(c):
        return (nch - 1 - c) if reverse else c

    def col(width, start):
        return lambda c: (cidx(c), start // width)

    in_specs = [
        pl.BlockSpec((Q, 1536), col(1536, ZS_XBC)),
        pl.BlockSpec((SUBLANES, 1536), lambda c: (jnp.maximum(cidx(c) * hb - 1, 0), 0)),
        pl.BlockSpec((SUBLANES, 1536), lambda c: (jnp.minimum((cidx(c) + 1) * hb, t // SUBLANES - 1), 0)),
        pl.BlockSpec((Q, LANES), col(LANES, ZS_DT + d * LANES)),
        pl.BlockSpec((Q, 1024), col(1024, ZS_Q)),
        pl.BlockSpec((Q, 1024), col(1024, ZS_V)),
        pl.BlockSpec((Q, 1024), col(1024, ZS_F + d * 1024)),
        pl.BlockSpec(cw.shape, lambda c: (0, 0)),
        pl.BlockSpec(cb.shape, lambda c: (0, 0)),
        pl.BlockSpec((None,) + dtc.shape[1:], lambda c: (d, 0, 0)),
        pl.BlockSpec(dsk.shape, lambda c: (0, 0)),
        pl.BlockSpec(lb.shape, lambda c: (0, 0)),
        pl.BlockSpec(e16.shape, lambda c: (0, 0)),
        pl.BlockSpec(h0s.shape, lambda c: (0, 0, 0)),
        pl.BlockSpec(h0g.shape, lambda c: (0, 0, 0)),
    ]
    out_specs = [
        pl.BlockSpec((Q, 1024), lambda c: (cidx(c), 0)),
        pl.BlockSpec((Q, 1024), lambda c: (cidx(c), 0)),
        pl.BlockSpec(h0s.shape, lambda c: (0, 0, 0)),
        pl.BlockSpec(h0g.shape, lambda c: (0, 0, 0)),
    ]
    out_shape = [jax.ShapeDtypeStruct((t, 1024), F32), jax.ShapeDtypeStruct((t, 1024), F32),
                 jax.ShapeDtypeStruct(h0s.shape, F32), jax.ShapeDtypeStruct(h0g.shape, F32)]
    return pl.pallas_call(
        functools.partial(_even_scan_kernel, reverse=reverse, nch=nch),
        grid=(nch,),
        in_specs=in_specs,
        out_specs=out_specs,
        out_shape=out_shape,
        compiler_params=_cparams(("arbitrary",), 40),
        name="even_scan_bwd" if reverse else "even_scan_fwd",
    )(zs, zs, zs, zs, zs, zs, zs, cw, cb, dtc, dsk, lb, e16, h0s, h0g)


def _even_out_kernel(h_ref, z_ref, g_ref, yf_ref, yb_ref, of_ref, ob_ref, w_ref, vec_ref, o_ref):
    y = (yf_ref[...] + yb_ref[...]) * _silu(z_ref[...])
    o = of_ref[...] + ob_ref[...]
    gw = y.shape[1] // SSD_GROUPS
    parts = []
    for g in range(SSD_GROUPS):
        yg = y[:, g * gw:(g + 1) * gw]
        ms = jnp.mean(yg * yg, axis=-1, keepdims=True)
        parts.append(yg * lax.rsqrt(ms + EPS))
    yn = (jnp.concatenate(parts, axis=1) * vec_ref[0:1, :]).astype(BF16)
    parts = []
    for hh in range(HG_HEADS):
        oh = o[:, hh * HG_HEAD_DIM:(hh + 1) * HG_HEAD_DIM]
        ms = jnp.mean(oh * oh, axis=-1, keepdims=True)
        parts.append(oh * lax.rsqrt(ms + EPS))
    on = (jnp.concatenate(parts, axis=1) * vec_ref[1:2, :] * _silu(g_ref[...])).astype(BF16)
    half = yn.shape[1]
    mix = jnp.dot(yn, w_ref[:half, :], preferred_element_type=F32)
    mix = mix + jnp.dot(on, w_ref[half:, :], preferred_element_type=F32)
    o_ref[...] = h_ref[...] + vec_ref[2:3, :] * mix


def _even_out(h, zs, yf, yb, of, ob, w_out, vec):
    t, d = h.shape
    tm = min(t, 512)
    row = lambda i: (i, 0)
    return pl.pallas_call(
        _even_out_kernel,
        grid=(t // tm,),
        in_specs=[pl.BlockSpec((tm, d), row),
                  pl.BlockSpec((tm, 1024), lambda i: (i, ZS_Z // 1024)),
                  pl.BlockSpec((tm, 1024), lambda i: (i, ZS_G // 1024)),
                  pl.BlockSpec((tm, 1024), row), pl.BlockSpec((tm, 1024), row),
                  pl.BlockSpec((tm, 1024), row), pl.BlockSpec((tm, 1024), row),
                  pl.BlockSpec(w_out.shape, lambda i: (0, 0)),
                  pl.BlockSpec(vec.shape, lambda i: (0, 0))],
        out_specs=pl.BlockSpec((tm, d), row),
        out_shape=jax.ShapeDtypeStruct((t, d), F32),
        compiler_params=_cparams(("arbitrary",), 56),
        name="even_out",
    )(h, zs, zs, yf, yb, of, ob, w_out, vec)


def _ffn_kernel(h_ref, mod_ref, wg_ref, wu_ref, wd_ref, o_ref, a_ref, acc_ref, *, final):
    j = pl.program_id(1)

    @pl.when(j == 0)
    def _():
        a_ref[...] = _modulate(h_ref[...], mod_ref).astype(BF16)
        acc_ref[...] = jnp.zeros_like(acc_ref)

    a = a_ref[...]
    gate = jnp.dot(a, wg_ref[...], preferred_element_type=F32)
    up = jnp.dot(a, wu_ref[...], preferred_element_type=F32)
    act = (_silu(gate) * up).astype(BF16)
    acc_ref[...] += jnp.dot(act, wd_ref[...], preferred_element_type=F32)

    @pl.when(j == pl.num_programs(1) - 1)
    def _():
        out = h_ref[...] + mod_ref[3:4, :] * acc_ref[...]
        if final:
            ms = jnp.mean(out * out, axis=-1, keepdims=True)
            out = out * lax.rsqrt(ms + EPS) * mod_ref[4:5, :]
        o_ref[...] = out


def _ffn(h, mod, wg, wu, wd, final):
    t, d = h.shape
    f = wg.shape[1]
    tm = min(t, 512)
    tf = f // 2
    return pl.pallas_call(
        functools.partial(_ffn_kernel, final=final),
        grid=(t // tm, f // tf),
        in_specs=[pl.BlockSpec((tm, d), lambda i, j: (i, 0)),
                  pl.BlockSpec((SUBLANES, d), lambda i, j: (0, 0)),
                  pl.BlockSpec((d, tf), lambda i, j: (0, j)),
                  pl.BlockSpec((d, tf), lambda i, j: (0, j)),
                  pl.BlockSpec((tf, d), lambda i, j: (j, 0))],
        out_specs=pl.BlockSpec((tm, d), lambda i, j: (i, 0)),
        out_shape=jax.ShapeDtypeStruct((t, d), F32),
        scratch_shapes=[pltpu.VMEM((tm, d), BF16), pltpu.VMEM((tm, d), F32)],
        compiler_params=_cparams(("arbitrary", "arbitrary"), 56),
        name="ffn_final" if final else "ffn",
    )(h, mod, wg, wu, wd)


def _odd_in_kernel(h_ref, mod_ref, w_ref, o_ref):
    a = _modulate(h_ref[...], mod_ref).astype(BF16)
    u = jnp.dot(a, w_ref[...], preferred_element_type=F32)
    for j in range(o_ref.shape[0]):
        o_ref[j] = u[:, j * LANES:(j + 1) * LANES]


def _odd_in(h, mod, w, grid_w):
    t, d = h.shape
    rows = t // grid_w
    n = w.shape[1]
    nblk = n // LANES
    return pl.pallas_call(
        _odd_in_kernel,
        grid=(grid_w,),
        in_specs=[pl.BlockSpec((rows, d), lambda w_: (0, w_)),
                  pl.BlockSpec((SUBLANES, d), lambda w_: (0, 0)),
                  pl.BlockSpec(w.shape, lambda w_: (0, 0))],
        out_specs=pl.BlockSpec((nblk, rows, LANES), lambda w_: (0, w_, 0)),
        out_shape=jax.ShapeDtypeStruct((nblk, t, LANES), F32),
        compiler_params=_cparams(("arbitrary",), 32),
        name="odd_in",
    )(h.reshape(rows, grid_w * d), mod, w)


def _s5_scan_kernel(u_ref, win_ref, wout_ref, tt_ref, lam_ref, h0_ref, y_ref, hf_ref, s_ref, c_ref, *, reverse, nb):
    b = pl.program_id(1)
    half = c_ref.shape[1] // 2

    @pl.when(b == 0)
    def _():
        c_ref[...] = h0_ref[0:1, :]

    xb = u_ref[...].astype(BF16)
    s_ref[...] = jnp.dot(xb, win_ref[...], preferred_element_type=F32)
    lr = lam_ref[0:1, :half]
    li = lam_ref[0:1, half:]

    def step(i, carry):
        k = (nb - 1 - i) if reverse else i
        pr = c_ref[:, :half]
        pi = c_ref[:, half:]
        row = s_ref[pl.ds(k, 1), :]
        c_ref[:, :half] = lr * pr - li * pi + row[:, :half]
        c_ref[:, half:] = lr * pi + li * pr + row[:, half:]
        s_ref[pl.ds(k, 1), :half] = pr
        s_ref[pl.ds(k, 1), half:] = pi
        return carry

    lax.fori_loop(0, nb, step, 0)
    sb = s_ref[...].astype(BF16)
    width = xb.shape[1]
    tile = 2 * LANES
    for m in range(width // tile):
        cols = slice(m * tile, (m + 1) * tile)
        rows = slice(m * tile, width) if reverse else slice(0, (m + 1) * tile)
        y = jnp.dot(sb, wout_ref[:, cols], preferred_element_type=F32)
        y_ref[:, cols] = y + jnp.dot(xb[:, rows], tt_ref[rows, cols], preferred_element_type=F32)

    @pl.when(b == pl.num_programs(1) - 1)
    def _():
        hf_ref[...] = jnp.broadcast_to(c_ref[...], hf_ref.shape)


def _s5_scan(u, win, wout, tt, lam, h0, reverse):
    nj, n, width = u.shape
    nb = min(n, 256)
    nblk = n // nb
    ns = win.shape[2]

    def bidx(b):
        return (nblk - 1 - b) if reverse else b

    return pl.pallas_call(
        functools.partial(_s5_scan_kernel, reverse=reverse, nb=nb),
        grid=(nj, nblk),
        in_specs=[pl.BlockSpec((None, nb, width), lambda j, b: (j, bidx(b), 0)),
                  pl.BlockSpec((None, width, ns), lambda j, b: (j, 0, 0)),
                  pl.BlockSpec((None, ns, width), lambda j, b: (j, 0, 0)),
                  pl.BlockSpec((None, width, width), lambda j, b: (j, 0, 0)),
                  pl.BlockSpec((None, SUBLANES, ns), lambda j, b: (j, 0, 0)),
                  pl.BlockSpec((None, SUBLANES, ns), lambda j, b: (j, 0, 0))],
        out_specs=[pl.BlockSpec((None, nb, width), lambda j, b: (j, bidx(b), 0)),
                   pl.BlockSpec((None, SUBLANES, ns), lambda j, b: (j, 0, 0))],
        out_shape=[jax.ShapeDtypeStruct((nj, n, width), F32), jax.ShapeDtypeStruct((nj, SUBLANES, ns), F32)],
        scratch_shapes=[pltpu.VMEM((nb, ns), F32), pltpu.VMEM((1, ns), F32)],
        compiler_params=_cparams(("arbitrary", "arbitrary"), 56),
        name="s5_scan_bwd" if reverse else "s5_scan_fwd",
    )(u, win, wout, tt, lam, h0)


def _odd_out_kernel(h_ref, u_ref, yf_ref, yb_ref, wv_ref, wg_ref, vec_ref, dsk_ref, o_ref):
    nblk = u_ref.shape[0]
    y = jnp.concatenate([u_ref[j] * dsk_ref[j] + yf_ref[j] + yb_ref[j] for j in range(nblk)], axis=1)
    a = jax.nn.gelu(y).astype(BF16)
    val = jnp.dot(a, wv_ref[...], preferred_element_type=F32)
    gate = jnp.dot(a, wg_ref[...], preferred_element_type=F32)
    o_ref[...] = h_ref[...] + vec_ref[0:1, :] * (val * jax.nn.sigmoid(gate))


def _odd_out(h, u, yf, yb, w_val, w_gate, vec, dsk, grid_w):
    t, d = h.shape
    rows = t // grid_w
    nblk = u.shape[0]
    plane = pl.BlockSpec((nblk, rows, LANES), lambda w_: (0, w_, 0))
    out = pl.pallas_call(
        _odd_out_kernel,
        grid=(grid_w,),
        in_specs=[pl.BlockSpec((rows, d), lambda w_: (0, w_)), plane, plane, plane,
                  pl.BlockSpec(w_val.shape, lambda w_: (0, 0)),
                  pl.BlockSpec(w_gate.shape, lambda w_: (0, 0)),
                  pl.BlockSpec(vec.shape, lambda w_: (0, 0)),
                  pl.BlockSpec(dsk.shape, lambda w_: (0, 0, 0))],
        out_specs=pl.BlockSpec((rows, d), lambda w_: (0, w_)),
        out_shape=jax.ShapeDtypeStruct((rows, grid_w * d), F32),
        compiler_params=_cparams(("arbitrary",), 32),
        name="odd_out",
    )(h.reshape(rows, grid_w * d), u, yf, yb, w_val, w_gate, vec, dsk)
    return out.reshape(t, d)


def _s5_prep_kernel(prm_ref, bre_ref, bim_ref, colp_ref, cre_ref, cim_ref, win_ref, wout_ref, tt_ref, lam_ref, *,
                    reverse):
    T = S5_SUB
    ns = prm_ref.shape[1]
    hi = lax.Precision.HIGHEST
    a_re, a_im, dt = prm_ref[0:1, :], prm_ref[1:2, :], jnp.exp(prm_ref[2:3, :])
    tau = lax.broadcasted_iota(jnp.int32, (3 * SUBLANES, ns), 0).astype(F32)
    mag = jnp.exp(a_re * dt * tau)
    pr = mag * jnp.cos(a_im * dt * tau)
    pi = mag * jnp.sin(a_im * dt * tau)
    lr, li = pr[1:2], pi[1:2]
    den = a_re * a_re + a_im * a_im
    cr = ((lr - 1.0) * a_re + li * a_im) / den
    ci = (li * a_re - (lr - 1.0) * a_im) / den
    row_g = lax.broadcasted_iota(jnp.int32, (LANES, ns), 0) // S5_GROUP
    col_g = lax.broadcasted_iota(jnp.int32, (LANES, ns), 1) // S5_STATE
    same = row_g == col_g
    bbr = jnp.where(same, cr * bre_ref[...] - ci * bim_ref[...], 0.0)
    bbi = jnp.where(same, cr * bim_ref[...] + ci * bre_ref[...], 0.0)
    for s in range(T):
        e = s if reverse else T - 1 - s
        rows = slice(s * LANES, (s + 1) * LANES)
        win_ref[rows, :ns] = (pr[e:e + 1] * bbr - pi[e:e + 1] * bbi).astype(BF16)
        win_ref[rows, ns:] = (pr[e:e + 1] * bbi + pi[e:e + 1] * bbr).astype(BF16)
    lam_ref[...] = jnp.broadcast_to(jnp.concatenate([pr[T:T + 1], pi[T:T + 1]], axis=1), lam_ref.shape)
    dt_c = jnp.exp(colp_ref[2])
    mag_c = jnp.exp(colp_ref[0] * dt_c)
    lr_c = mag_c * jnp.cos(colp_ref[1] * dt_c)
    li_c = mag_c * jnp.sin(colp_ref[1] * dt_c)
    row_gc = lax.broadcasted_iota(jnp.int32, (ns, LANES), 0) // S5_STATE
    col_gc = lax.broadcasted_iota(jnp.int32, (ns, LANES), 1) // S5_GROUP
    same_c = row_gc == col_gc
    c_re = jnp.where(same_c, cre_ref[...], 0.0)
    c_im = jnp.where(same_c, cim_ref[...], 0.0)
    bb = jnp.concatenate([bbr, bbi], axis=1)
    cur_r = jnp.ones((ns, LANES), F32)
    cur_i = jnp.zeros((ns, LANES), F32)
    kblk = []
    for e in range(T + 1):
        wo_r = c_re * cur_r - c_im * cur_i
        wo_i = -(c_re * cur_i + c_im * cur_r)
        if e >= 1:
            t = (T - e) if reverse else e - 1
            wout_ref[:ns, t * LANES:(t + 1) * LANES] = wo_r.astype(BF16)
            wout_ref[ns:, t * LANES:(t + 1) * LANES] = wo_i.astype(BF16)
        if e < T:
            k = jnp.dot(bb, jnp.concatenate([wo_r, wo_i], axis=0), precision=hi, preferred_element_type=F32)
            kblk.append(k.astype(BF16))
        cur_r, cur_i = cur_r * lr_c - cur_i * li_c, cur_r * li_c + cur_i * lr_c
    zeros = jnp.zeros((LANES, LANES), BF16)
    for s in range(T):
        for t in range(T):
            lag = (s - t) if reverse else (t - s)
            tt_ref[s * LANES:(s + 1) * LANES, t * LANES:(t + 1) * LANES] = kblk[lag] if lag >= 0 else zeros


def _s5_prep(a_re, a_im, log_dt, b_re, b_im, c_re, c_im, reverse):
    g, p = a_re.shape
    cdim = b_re.shape[2]
    nj = g // S5_GPB
    ns = S5_GPB * p
    width = S5_SUB * LANES
    prm = jnp.zeros((nj, SUBLANES, ns), F32)
    prm = prm.at[:, 0].set(a_re.reshape(nj, ns)).at[:, 1].set(a_im.reshape(nj, ns))
    prm = prm.at[:, 2].set(jnp.repeat(log_dt, p).reshape(nj, ns))
    colp = jnp.broadcast_to(prm[:, :3, :, None], (nj, 3, ns, LANES))
    row_tile = lambda b: jnp.tile(b.reshape(nj, S5_GPB, p, cdim).transpose(0, 3, 1, 2).reshape(nj, 1, cdim, ns),
                                  (1, S5_GPB, 1, 1)).reshape(nj, LANES, ns)
    col_tile = lambda c_: jnp.tile(c_.reshape(nj, S5_GPB, cdim, p).transpose(0, 1, 3, 2).reshape(nj, ns, 1, cdim),
                                   (1, 1, S5_GPB, 1)).reshape(nj, ns, LANES)
    blk = lambda *shape: pl.BlockSpec((None,) + shape, lambda j: (j,) + (0,) * len(shape))
    return pl.pallas_call(
        functools.partial(_s5_prep_kernel, reverse=reverse),
        grid=(nj,),
        in_specs=[blk(SUBLANES, ns), blk(LANES, ns), blk(LANES, ns), blk(3, ns, LANES), blk(ns, LANES),
                  blk(ns, LANES)],
        out_specs=[blk(width, 2 * ns), blk(2 * ns, width), blk(width, width), blk(SUBLANES, 2 * ns)],
        out_shape=[jax.ShapeDtypeStruct((nj, width, 2 * ns), BF16), jax.ShapeDtypeStruct((nj, 2 * ns, width), BF16),
                   jax.ShapeDtypeStruct((nj, width, width), BF16), jax.ShapeDtypeStruct((nj, SUBLANES, 2 * ns), F32)],
        compiler_params=_cparams(("arbitrary",), 56),
        name="s5_prep_bwd" if reverse else "s5_prep_fwd",
    )(prm, row_tile(b_re), row_tile(b_im), colp, col_tile(c_re), col_tile(c_im))


def _rows(*vs, width):
    out = jnp.zeros((SUBLANES, width), F32)
    for i, v in enumerate(vs):
        out = out.at[i].set(v.astype(F32))
    return out


def kernel(x, c, ctx, c_ctx, ada_w, ada_b, norm_mix_g, norm_ffn_g, ffn_w_gate, ffn_w_up, ffn_w_down, final_norm_g, ev_w_in, ev_w_out, ssd_conv_w, ssd_conv_b, ssd_dt_bias, ssd_a_log, ssd_d, ssd_norm_g, hg_lb_logits, hg_norm_g, od_w_in, s5_a_re, s5_a_im, s5_log_dt, s5_b_re, s5_b_im, s5_c_re, s5_c_im, s5_d, od_w_val, od_w_gate):
    d = x.shape[-1]
    lat = x[0].astype(F32)
    hc = ctx[0].astype(F32)

    m = _ada(_rows(c[0], c_ctx, width=d), ada_w, ada_b)

    def ada_vecs(layer, stream):
        return [m[layer, stream, i * d:(i + 1) * d] for i in range(6)]

    w = ev_w_in[0]
    n_x = SSD_HEADS * SSD_HEAD_DIM
    n_xbc = n_x + 2 * SSD_GROUPS * SSD_STATE
    n_hg = HG_HEADS * HG_HEAD_DIM
    o_z, o_xbc, o_dt = 0, n_x, n_x + n_xbc
    o_q = o_dt + 2 * SSD_HEADS
    o_f = o_q + n_hg
    o_v = o_f + 2 * n_hg
    o_g = o_v + n_hg
    pad = lambda n: jnp.zeros((d, n), F32)
    w_in = jnp.concatenate([
        w[:, o_xbc:o_xbc + n_xbc],
        w[:, o_dt:o_dt + SSD_HEADS], pad(LANES - SSD_HEADS),
        w[:, o_dt + SSD_HEADS:o_dt + 2 * SSD_HEADS], pad(LANES - SSD_HEADS),
        pad(ZS_Z - ZS_DT - 2 * LANES),
        w[:, o_z:o_z + n_x], w[:, o_q:o_q + n_hg], w[:, o_v:o_v + n_hg], w[:, o_g:o_g + n_hg],
        w[:, o_f:o_f + 2 * n_hg]], axis=1).astype(BF16)
    cw = jnp.zeros((SUBLANES, n_xbc), F32).at[:SSD_CONV].set(ssd_conv_w[0])
    cb = ssd_conv_b[0].reshape(1, n_xbc)
    dtc = jnp.zeros((2, SUBLANES, LANES), F32)
    dtc = dtc.at[:, 0, :SSD_HEADS].set(ssd_dt_bias[0]).at[:, 1, :SSD_HEADS].set(ssd_a_log[0])
    dsk = jnp.repeat(ssd_d[0], SSD_HEAD_DIM).reshape(1, n_x)
    lower = jnp.cumsum(jax.nn.softmax(hg_lb_logits.astype(F32), axis=0), axis=0)[0].reshape(1, n_hg)
    head_of_lane = jnp.arange(n_x) // SSD_HEAD_DIM
    e16 = (jnp.arange(LANES)[:, None] == head_of_lane[None, :]).astype(BF16)
    consts = (cw, cb, dtc, dsk, lower, e16)
    w_out = ev_w_out[0].astype(BF16)
    out_vec = lambda gate: _rows(ssd_norm_g[0], jnp.tile(hg_norm_g[0], HG_HEADS), gate, width=d)
    wg0, wu0, wd0 = (t_[0].astype(BF16) for t_ in (ffn_w_gate, ffn_w_up, ffn_w_down))

    zero_s = jnp.zeros((SSD_GROUPS, SSD_STATE, n_x // SSD_GROUPS), F32)
    zero_g = jnp.zeros((HG_HEADS, HG_HEAD_DIM, HG_HEAD_DIM), F32)
    streams = {}
    for stream, h in ((1, hc), (0, lat)):
        sm, scm, gm, sf, scf, gf = ada_vecs(0, stream)
        zs = _mod_matmul(h, _rows(norm_mix_g[0], scm, sm, width=d), w_in)
        streams[stream] = (h, zs, gm, sf, scf, gf)
    scans = {}
    for reverse in (False, True):
        yc, oc, s_c, g_c = _even_scan(streams[1][1], consts, zero_s, zero_g, reverse)
        yl, ol, _, _ = _even_scan(streams[0][1], consts, s_c, g_c, reverse)
        scans[reverse] = {1: (yc, oc), 0: (yl, ol)}
    layer0 = {}
    for stream in (1, 0):
        h, zs, gm, sf, scf, gf = streams[stream]
        yf, of = scans[False][stream]
        yb, ob = scans[True][stream]
        h1 = _even_out(h, zs, yf, yb, of, ob, w_out, out_vec(gm))
        layer0[stream] = _ffn(h1, _rows(norm_ffn_g[0], scf, sf, gf, width=d), wg0, wu0, wd0, final=False)

    lat, hc = layer0[0], layer0[1]
    w_s5 = od_w_in[0].astype(BF16)
    u = {}
    for stream, h, gw in ((1, hc, 1), (0, lat, GRID_W)):
        sm, scm = ada_vecs(1, stream)[:2]
        planes = _odd_in(h, _rows(norm_mix_g[1], scm, sm, width=d), w_s5, gw)
        u[stream] = planes.reshape(planes.shape[0], planes.shape[1] // S5_SUB, S5_SUB * LANES)
    ys = {}
    for reverse in (False, True):
        di = 1 if reverse else 0
        win, wout, toe, lam_t = _s5_prep(s5_a_re[0, di], s5_a_im[0, di], s5_log_dt[0, di], s5_b_re[0, di],
                                             s5_b_im[0, di], s5_c_re[0, di], s5_c_im[0, di], reverse)
        _, h_ctx = _s5_scan(u[1], win, wout, toe, lam_t, jnp.zeros_like(lam_t), reverse)
        ys[reverse], _ = _s5_scan(u[0], win, wout, toe, lam_t, h_ctx, reverse)
    _, _, gm, sf, scf, gf = ada_vecs(1, 0)
    nblk = u[0].shape[0]
    plane = lambda a: a.reshape(nblk, -1, LANES)
    h3 = _odd_out(lat, plane(u[0]), plane(ys[False]), plane(ys[True]), od_w_val[0].astype(BF16),
                  od_w_gate[0].astype(BF16), _rows(gm, width=d), s5_d[0].reshape(nblk, 1, LANES), GRID_W)
    wg1, wu1, wd1 = (t_[1].astype(BF16) for t_ in (ffn_w_gate, ffn_w_up, ffn_w_down))
    out = _ffn(h3, _rows(norm_ffn_g[1], scf, sf, gf, final_norm_g, width=d), wg1, wu1, wd1, final=True)
    return out[None].astype(x.dtype)
```

```python
import functools
import math

import jax
import jax.numpy as jnp
from jax import lax
from jax.experimental import pallas as pl
from jax.experimental.pallas import tpu as pltpu

F32 = jnp.float32
BF16 = jnp.bfloat16
EPS = 1e-6

LANES = 128
SUBLANES = 8
GRID_W = 64
SCAN_CHUNK = 64
SSD_HEADS = 16
SSD_HEAD_DIM = 64
SSD_STATE = 128
SSD_GROUPS = 2
SSD_CONV = 5
HG_HEADS = 8
HG_HEAD_DIM = 128
S5_GROUP = 16
S5_STATE = 64
S5_SUB = 16
S5_GPB = LANES // S5_GROUP

ZS_XBC = 0
ZS_DT = 1536
ZS_Z = 2048
ZS_Q = 3072
ZS_V = 4096
ZS_G = 5120
ZS_F = 6144
ZS_WIDTH = 8192


def _cparams(semantics, vmem_mb):
    return pltpu.CompilerParams(dimension_semantics=semantics, vmem_limit_bytes=vmem_mb * 1024 * 1024)


def _silu(x):
    return x * jax.nn.sigmoid(x)


def _modulate(h, mod_ref):
    ms = jnp.mean(h * h, axis=-1, keepdims=True)
    return h * lax.rsqrt(ms + EPS) * (mod_ref[0:1, :] * (1.0 + mod_ref[1:2, :])) + mod_ref[2:3, :]


def _split_dot(x, w, passes):
    acc, rem = None, x
    for _ in range(passes):
        hi = rem.astype(BF16)
        d = jnp.dot(hi, w, preferred_element_type=F32)
        acc = d if acc is None else acc + d
        rem = rem - hi.astype(F32)
    return acc


def _tri_cumsum(tri, x, passes):
    acc, rem = None, x
    for _ in range(passes):
        hi = rem.astype(BF16)
        d = jnp.dot(tri, hi, preferred_element_type=F32)
        acc = d if acc is None else acc + d
        rem = rem - hi.astype(F32)
    return acc


def _ada_kernel(c_ref, w_ref, b_ref, o_ref):
    o_ref[...] = jnp.dot(_silu(c_ref[...]), w_ref[...], precision=lax.Precision.HIGHEST,
                         preferred_element_type=F32) + b_ref[...]


def _ada(cvecs, ada_w, ada_b):
    depth, d, n = ada_w.shape
    tn = n // 4
    return pl.pallas_call(
        _ada_kernel,
        grid=(depth, n // tn),
        in_specs=[pl.BlockSpec((SUBLANES, d), lambda l, j: (0, 0)),
                  pl.BlockSpec((None, d, tn), lambda l, j: (l, 0, j)),
                  pl.BlockSpec((None, 1, tn), lambda l, j: (l, 0, j))],
        out_specs=pl.BlockSpec((None, SUBLANES, tn), lambda l, j: (l, 0, j)),
        out_shape=jax.ShapeDtypeStruct((depth, SUBLANES, n), F32),
        compiler_params=_cparams(("arbitrary", "arbitrary"), 40),
        name="ada",
    )(cvecs, ada_w, ada_b.reshape(depth, 1, n))


def _mod_matmul_kernel(h_ref, mod_ref, w_ref, o_ref, a_ref):
    @pl.when(pl.program_id(1) == 0)
    def _():
        a_ref[...] = _modulate(h_ref[...], mod_ref).astype(BF16)

    o_ref[...] = jnp.dot(a_ref[...], w_ref[...], preferred_element_type=F32)


def _mod_matmul(h, mod, w):
    t, d = h.shape
    n = w.shape[1]
    tm = min(t, 1024)
    tn = 1024
    return pl.pallas_call(
        _mod_matmul_kernel,
        grid=(t // tm, n // tn),
        in_specs=[pl.BlockSpec((tm, d), lambda i, j: (i, 0)),
                  pl.BlockSpec((SUBLANES, d), lambda i, j: (0, 0)),
                  pl.BlockSpec((d, tn), lambda i, j: (0, j))],
        out_specs=pl.BlockSpec((tm, tn), lambda i, j: (i, j)),
        out_shape=jax.ShapeDtypeStruct((t, n), F32),
        scratch_shapes=[pltpu.VMEM((tm, d), BF16)],
        compiler_params=_cparams(("arbitrary", "arbitrary"), 48),
        name="even_in",
    )(h, mod, w)


def _scan_block_prep(x_ref, hp_ref, hn_ref, dt_ref, q_ref, v_ref, f_ref, cw_ref, cb_ref, dtc_ref, lb_ref, blk, nblk, d):
    rows = x_ref.shape[0]
    prev = jnp.where(blk == 0, 0.0, hp_ref[...])
    nxt = jnp.where(blk == nblk - 1, 0.0, hn_ref[...])
    ext = jnp.concatenate([prev, x_ref[...], nxt], axis=0)
    conv = cb_ref[...]
    for j in range(SSD_CONV):
        off = SUBLANES - SSD_CONV // 2 + j
        conv = conv + cw_ref[j:j + 1, :] * ext[off:off + rows]
    xbc = _silu(conv)
    dtv = jax.nn.softplus(dt_ref[...] + dtc_ref[d, 0:1, :])
    la = -dtv * jnp.exp(dtc_ref[d, 1:2, :])
    lb = lb_ref[...]
    f = lb + (1.0 - lb) * jax.nn.sigmoid(f_ref[...])
    return xbc, dtv, la, 1.0 - f, jnp.log(f), _silu(q_ref[...]), v_ref[...].astype(BF16)


def _scan_chunk(prep, rows, dsk_ref, e_ref, st_ref, gt_ref, y_ref, o_ref, d, reverse):
    Q = SCAN_CHUNK
    xbc, dtv, la, kk, logf, qs, vb = (a[rows] for a in prep)
    ti = lax.broadcasted_iota(jnp.int32, (Q, Q), 0)
    si = lax.broadcasted_iota(jnp.int32, (Q, Q), 1)
    mask = (ti <= si) if reverse else (ti >= si)
    tri = jnp.where(mask, 1.0, 0.0).astype(BF16)
    edge = 0 if reverse else Q - 1
    width = SSD_HEADS * SSD_HEAD_DIM
    gw = width // SSD_GROUPS
    xs = xbc[:, :width]

    acum = _tri_cumsum(tri, la, 3)
    last = acum[edge:edge + 1, :]
    acum_t = acum.T
    e_a = jnp.exp(acum)
    w_end = jnp.exp(last - acum)
    e_last = jnp.broadcast_to(jnp.exp(last), (SUBLANES, LANES))
    xe = _split_dot(jnp.concatenate([dtv, e_a, w_end], axis=0), e_ref[...], 1)
    dt_x, ea_x, wend_x = xe[:Q], xe[Q:2 * Q], xe[2 * Q:3 * Q]
    elast_x = _split_dot(e_last, e_ref[...], 3)[0:1]
    xdt = xs * dt_x
    xw = (xdt * wend_x).astype(BF16)
    xdt_b = xdt.astype(BF16)
    lane = lax.broadcasted_iota(jnp.int32, (Q, LANES), 1)
    lo = lane < SSD_HEAD_DIM
    zero_b = jnp.zeros((Q, LANES), BF16)
    ys = []
    for g in range(SSD_GROUPS):
        bm = xbc[:, width + g * SSD_STATE:width + (g + 1) * SSD_STATE].astype(BF16)
        cm = xbc[:, width + (SSD_GROUPS + g) * SSD_STATE:width + (SSD_GROUPS + g + 1) * SSD_STATE].astype(BF16)
        cb = lax.dot_general(cm, bm, (((1,), (1,)), ((), ())), preferred_element_type=F32)
        y_state = jnp.dot(cm, st_ref[d, g].astype(BF16), preferred_element_type=F32)
        pairs = []
        for hp in range(gw // LANES):
            xp = xdt_b[:, g * gw + hp * LANES:g * gw + (hp + 1) * LANES]
            acc = None
            for half in range(2):
                h = (g * gw + hp * LANES) // SSD_HEAD_DIM + half
                seg = acum[:, h:h + 1] - acum_t[h:h + 1, :]
                m = jnp.where(mask, cb * jnp.exp(seg), 0.0).astype(BF16)
                xh = jnp.where(lo, xp, zero_b) if half == 0 else jnp.where(lo, zero_b, xp)
                part = jnp.dot(m, xh, preferred_element_type=F32)
                acc = part if acc is None else acc + part
            pairs.append(acc)
        y_g = jnp.concatenate(pairs, axis=1) + y_state * ea_x[:, g * gw:(g + 1) * gw]
        ys.append(y_g)
        upd = lax.dot_general(bm, xw[:, g * gw:(g + 1) * gw], (((0,), (0,)), ((), ())),
                              preferred_element_type=F32)
        st_ref[d, g] = st_ref[d, g] * elast_x[:, g * gw:(g + 1) * gw] + upd
    y = jnp.concatenate(ys, axis=1)
    if not reverse:
        y = y + dsk_ref[...] * xs
    y_ref[rows, :] = y.astype(y_ref.dtype)

    bcum = _tri_cumsum(tri, logf, 2)
    blast = bcum[edge:edge + 1, :]
    qe = (qs * jnp.exp(bcum)).astype(BF16)
    k_end = kk * jnp.exp(blast - bcum)
    k_til = (k_end * jnp.exp(-blast)).astype(BF16)
    k_end = k_end.astype(BF16)
    e_blast = jnp.exp(blast)
    outs = []
    for h in range(HG_HEADS):
        sl = slice(h * HG_HEAD_DIM, (h + 1) * HG_HEAD_DIM)
        att = lax.dot_general(qe[:, sl], k_til[:, sl], (((1,), (1,)), ((), ())), preferred_element_type=F32)
        att = jnp.where(mask, att, 0.0).astype(BF16)
        gt = gt_ref[d, h]
        o_h = jnp.dot(att, vb[:, sl], preferred_element_type=F32)
        o_h = o_h + lax.dot_general(qe[:, sl], gt.astype(BF16), (((1,), (1,)), ((), ())),
                                    preferred_element_type=F32)
        outs.append(o_h)
        upd = lax.dot_general(vb[:, sl], k_end[:, sl], (((0,), (0,)), ((), ())), preferred_element_type=F32)
        gt_ref[d, h] = gt * e_blast[:, sl] + upd
    o_ref[rows, :] = jnp.concatenate(outs, axis=1).astype(o_ref.dtype)


def _even_scan_kernel(*refs, nblk, cpb):
    fwd_in, bwd_in = refs[0:7], refs[7:14]
    cw_ref, cb_ref, dtc_ref, dsk_ref, lb_ref, e_ref, h0s_ref, h0g_ref = refs[14:22]
    yf_ref, of_ref, yb_ref, ob_ref, st_ref, gt_ref = refs[22:28]
    c = pl.program_id(0)

    @pl.when(c == 0)
    def _():
        st_ref[...] = h0s_ref[...]
        gt_ref[...] = h0g_ref[...]

    streams = ((False, fwd_in, c, yf_ref, of_ref), (True, bwd_in, nblk - 1 - c, yb_ref, ob_ref))
    preps = [_scan_block_prep(*ins, cw_ref, cb_ref, dtc_ref, lb_ref, blk, nblk, d)
             for d, (_, ins, blk, _, _) in enumerate(streams)]
    for step in range(cpb):
        for d, (reverse, _, _, y_ref, o_ref) in enumerate(streams):
            ci = (cpb - 1 - step) if reverse else step
            rows = slice(ci * SCAN_CHUNK, (ci + 1) * SCAN_CHUNK)
            _scan_chunk(preps[d], rows, dsk_ref, e_ref, st_ref, gt_ref, y_ref, o_ref, d, reverse)


def _even_scan(zs, consts, h0s, h0g):
    t = zs.shape[0]
    cpb = 2
    rows = cpb * SCAN_CHUNK
    nblk = t // rows
    cw, cb, dtc, dsk, lb, e16 = consts
    n_x = SSD_HEADS * SSD_HEAD_DIM
    n_xbc = n_x + 2 * SSD_GROUPS * SSD_STATE
    n_hg = HG_HEADS * HG_HEAD_DIM
    hb = rows // SUBLANES

    def stream_specs(bidx, d):
        col = lambda width, start: (lambda c: (bidx(c), start // width))
        return [
            pl.BlockSpec((rows, n_xbc), col(n_xbc, ZS_XBC)),
            pl.BlockSpec((SUBLANES, n_xbc), lambda c: (jnp.maximum(bidx(c) * hb - 1, 0), 0)),
            pl.BlockSpec((SUBLANES, n_xbc), lambda c: (jnp.minimum((bidx(c) + 1) * hb, t // SUBLANES - 1), 0)),
            pl.BlockSpec((rows, LANES), col(LANES, ZS_DT + d * LANES)),
            pl.BlockSpec((rows, n_hg), col(n_hg, ZS_Q)),
            pl.BlockSpec((rows, n_hg), col(n_hg, ZS_V)),
            pl.BlockSpec((rows, n_hg), col(n_hg, ZS_F + d * n_hg)),
        ]

    fwd = lambda c: c
    bwd = lambda c: nblk - 1 - c
    whole = lambda a: pl.BlockSpec(a.shape, lambda c: (0,) * a.ndim)
    in_specs = stream_specs(fwd, 0) + stream_specs(bwd, 1) + [whole(a) for a in (cw, cb, dtc, dsk, lb, e16, h0s, h0g)]
    out_specs = [pl.BlockSpec((rows, n_x), lambda c: (c, 0)), pl.BlockSpec((rows, n_hg), lambda c: (c, 0)),
                 pl.BlockSpec((rows, n_x), lambda c: (bwd(c), 0)), pl.BlockSpec((rows, n_hg), lambda c: (bwd(c), 0)),
                 whole(h0s), whole(h0g)]
    out_shape = [jax.ShapeDtypeStruct((t, n_x), BF16), jax.ShapeDtypeStruct((t, n_hg), BF16),
                 jax.ShapeDtypeStruct((t, n_x), BF16), jax.ShapeDtypeStruct((t, n_hg), BF16),
                 jax.ShapeDtypeStruct(h0s.shape, F32), jax.ShapeDtypeStruct(h0g.shape, F32)]
    return pl.pallas_call(
        functools.partial(_even_scan_kernel, nblk=nblk, cpb=cpb),
        grid=(nblk,),
        in_specs=in_specs,
        out_specs=out_specs,
        out_shape=out_shape,
        compiler_params=_cparams(("arbitrary",), 48),
        name="even_scan",
    )(*([zs] * 14), cw, cb, dtc, dsk, lb, e16, h0s, h0g)


def _even_out_kernel(h_ref, z_ref, g_ref, yf_ref, yb_ref, of_ref, ob_ref, w_ref, vec_ref, o_ref):
    y = (yf_ref[...].astype(F32) + yb_ref[...].astype(F32)) * _silu(z_ref[...])
    o = of_ref[...].astype(F32) + ob_ref[...].astype(F32)
    gw = y.shape[1] // SSD_GROUPS
    parts = []
    for g in range(SSD_GROUPS):
        yg = y[:, g * gw:(g + 1) * gw]
        ms = jnp.mean(yg * yg, axis=-1, keepdims=True)
        parts.append(yg * lax.rsqrt(ms + EPS))
    yn = (jnp.concatenate(parts, axis=1) * vec_ref[0:1, :]).astype(BF16)
    parts = []
    for hh in range(HG_HEADS):
        oh = o[:, hh * HG_HEAD_DIM:(hh + 1) * HG_HEAD_DIM]
        ms = jnp.mean(oh * oh, axis=-1, keepdims=True)
        parts.append(oh * lax.rsqrt(ms + EPS))
    on = (jnp.concatenate(parts, axis=1) * vec_ref[1:2, :] * _silu(g_ref[...])).astype(BF16)
    half = yn.shape[1]
    mix = jnp.dot(yn, w_ref[:half, :], preferred_element_type=F32)
    mix = mix + jnp.dot(on, w_ref[half:, :], preferred_element_type=F32)
    o_ref[...] = h_ref[...] + vec_ref[2:3, :] * mix


def _even_out(h, zs, yf, yb, of, ob, w_out, vec):
    t, d = h.shape
    tm = min(t, 512)
    row = lambda i: (i, 0)
    return pl.pallas_call(
        _even_out_kernel,
        grid=(t // tm,),
        in_specs=[pl.BlockSpec((tm, d), row),
                  pl.BlockSpec((tm, 1024), lambda i: (i, ZS_Z // 1024)),
                  pl.BlockSpec((tm, 1024), lambda i: (i, ZS_G // 1024)),
                  pl.BlockSpec((tm, 1024), row), pl.BlockSpec((tm, 1024), row),
                  pl.BlockSpec((tm, 1024), row), pl.BlockSpec((tm, 1024), row),
                  pl.BlockSpec(w_out.shape, lambda i: (0, 0)),
                  pl.BlockSpec(vec.shape, lambda i: (0, 0))],
        out_specs=pl.BlockSpec((tm, d), row),
        out_shape=jax.ShapeDtypeStruct((t, d), F32),
        compiler_params=_cparams(("arbitrary",), 56),
        name="even_out",
    )(h, zs, zs, yf, yb, of, ob, w_out, vec)


def _ffn_kernel(h_ref, mod_ref, wg_ref, wu_ref, wd_ref, o_ref, a_ref, acc_ref, *, final):
    j = pl.program_id(1)

    @pl.when(j == 0)
    def _():
        a_ref[...] = _modulate(h_ref[...], mod_ref).astype(BF16)
        acc_ref[...] = jnp.zeros_like(acc_ref)

    a = a_ref[...]
    gate = jnp.dot(a, wg_ref[...], preferred_element_type=F32)
    up = jnp.dot(a, wu_ref[...], preferred_element_type=F32)
    act = (_silu(gate) * up).astype(BF16)
    acc_ref[...] += jnp.dot(act, wd_ref[...], preferred_element_type=F32)

    @pl.when(j == pl.num_programs(1) - 1)
    def _():
        out = h_ref[...] + mod_ref[3:4, :] * acc_ref[...]
        if final:
            ms = jnp.mean(out * out, axis=-1, keepdims=True)
            out = out * lax.rsqrt(ms + EPS) * mod_ref[4:5, :]
        o_ref[...] = out


def _ffn(h, mod, wg, wu, wd, final):
    t, d = h.shape
    f = wg.shape[1]
    tm = min(t, 512)
    tf = f // 2
    return pl.pallas_call(
        functools.partial(_ffn_kernel, final=final),
        grid=(t // tm, f // tf),
        in_specs=[pl.BlockSpec((tm, d), lambda i, j: (i, 0)),
                  pl.BlockSpec((SUBLANES, d), lambda i, j: (0, 0)),
                  pl.BlockSpec((d, tf), lambda i, j: (0, j)),
                  pl.BlockSpec((d, tf), lambda i, j: (0, j)),
                  pl.BlockSpec((tf, d), lambda i, j: (j, 0))],
        out_specs=pl.BlockSpec((tm, d), lambda i, j: (i, 0)),
        out_shape=jax.ShapeDtypeStruct((t, d), F32),
        scratch_shapes=[pltpu.VMEM((tm, d), BF16), pltpu.VMEM((tm, d), F32)],
        compiler_params=_cparams(("arbitrary", "arbitrary"), 56),
        name="ffn_final" if final else "ffn",
    )(h, mod, wg, wu, wd)


def _odd_in_kernel(h_ref, mod_ref, w_ref, o_ref):
    a = _modulate(h_ref[...], mod_ref).astype(BF16)
    u = jnp.dot(a, w_ref[...], preferred_element_type=F32)
    for j in range(o_ref.shape[0]):
        o_ref[j] = u[:, j * LANES:(j + 1) * LANES]


def _odd_in(h, mod, w, grid_w):
    t, d = h.shape
    rows = t // grid_w
    n = w.shape[1]
    nblk = n // LANES
    return pl.pallas_call(
        _odd_in_kernel,
        grid=(grid_w,),
        in_specs=[pl.BlockSpec((rows, d), lambda w_: (0, w_)),
                  pl.BlockSpec((SUBLANES, d), lambda w_: (0, 0)),
                  pl.BlockSpec(w.shape, lambda w_: (0, 0))],
        out_specs=pl.BlockSpec((nblk, rows, LANES), lambda w_: (0, w_, 0)),
        out_shape=jax.ShapeDtypeStruct((nblk, t, LANES), F32),
        compiler_params=_cparams(("arbitrary",), 32),
        name="odd_in",
    )(h.reshape(rows, grid_w * d), mod, w)


def _s5_scan_kernel(u_ref, win_ref, wout_ref, tt_ref, lam_ref, h0_ref, y_ref, hf_ref, s_ref, c_ref, *, reverse, nb):
    b = pl.program_id(1)
    half = c_ref.shape[1] // 2

    @pl.when(b == 0)
    def _():
        c_ref[...] = h0_ref[0:1, :]

    xb = u_ref[...].astype(BF16)
    s_ref[...] = jnp.dot(xb, win_ref[...], preferred_element_type=F32)
    lr = lam_ref[0:1, :half]
    li = lam_ref[0:1, half:]

    def step(i, carry):
        k = (nb - 1 - i) if reverse else i
        pr = c_ref[:, :half]
        pi = c_ref[:, half:]
        row = s_ref[pl.ds(k, 1), :]
        c_ref[:, :half] = lr * pr - li * pi + row[:, :half]
        c_ref[:, half:] = lr * pi + li * pr + row[:, half:]
        s_ref[pl.ds(k, 1), :half] = pr
        s_ref[pl.ds(k, 1), half:] = pi
        return carry

    lax.fori_loop(0, nb, step, 0)
    sb = s_ref[...].astype(BF16)
    width = xb.shape[1]
    tile = 2 * LANES
    for m in range(width // tile):
        cols = slice(m * tile, (m + 1) * tile)
        rows = slice(m * tile, width) if reverse else slice(0, (m + 1) * tile)
        y = jnp.dot(sb, wout_ref[:, cols], preferred_element_type=F32)
        y_ref[:, cols] = y + jnp.dot(xb[:, rows], tt_ref[rows, cols], preferred_element_type=F32)

    @pl.when(b == pl.num_programs(1) - 1)
    def _():
        hf_ref[...] = jnp.broadcast_to(c_ref[...], hf_ref.shape)


def _s5_scan(u, win, wout, tt, lam, h0, reverse):
    nj, n, width = u.shape
    nb = min(n, 256)
    nblk = n // nb
    ns = win.shape[2]

    def bidx(b):
        return (nblk - 1 - b) if reverse else b

    return pl.pallas_call(
        functools.partial(_s5_scan_kernel, reverse=reverse, nb=nb),
        grid=(nj, nblk),
        in_specs=[pl.BlockSpec((None, nb, width), lambda j, b: (j, bidx(b), 0)),
                  pl.BlockSpec((None, width, ns), lambda j, b: (j, 0, 0)),
                  pl.BlockSpec((None, ns, width), lambda j, b: (j, 0, 0)),
                  pl.BlockSpec((None, width, width), lambda j, b: (j, 0, 0)),
                  pl.BlockSpec((None, SUBLANES, ns), lambda j, b: (j, 0, 0)),
                  pl.BlockSpec((None, SUBLANES, ns), lambda j, b: (j, 0, 0))],
        out_specs=[pl.BlockSpec((None, nb, width), lambda j, b: (j, bidx(b), 0)),
                   pl.BlockSpec((None, SUBLANES, ns), lambda j, b: (j, 0, 0))],
        out_shape=[jax.ShapeDtypeStruct((nj, n, width), F32), jax.ShapeDtypeStruct((nj, SUBLANES, ns), F32)],
        scratch_shapes=[pltpu.VMEM((nb, ns), F32), pltpu.VMEM((1, ns), F32)],
        compiler_params=_cparams(("arbitrary", "arbitrary"), 56),
        name="s5_scan_bwd" if reverse else "s5_scan_fwd",
    )(u, win, wout, tt, lam, h0)


def _odd_out_kernel(h_ref, u_ref, yf_ref, yb_ref, wv_ref, wg_ref, vec_ref, dsk_ref, o_ref):
    nblk = u_ref.shape[0]
    y = jnp.concatenate([u_ref[j] * dsk_ref[j] + yf_ref[j] + yb_ref[j] for j in range(nblk)], axis=1)
    a = jax.nn.gelu(y).astype(BF16)
    val = jnp.dot(a, wv_ref[...], preferred_element_type=F32)
    gate = jnp.dot(a, wg_ref[...], preferred_element_type=F32)
    o_ref[...] = h_ref[...] + vec_ref[0:1, :] * (val * jax.nn.sigmoid(gate))


def _odd_out(h, u, yf, yb, w_val, w_gate, vec, dsk, grid_w):
    t, d = h.shape
    rows = t // grid_w
    nblk = u.shape[0]
    plane = pl.BlockSpec((nblk, rows, LANES), lambda w_: (0, w_, 0))
    out = pl.pallas_call(
        _odd_out_kernel,
        grid=(grid_w,),
        in_specs=[pl.BlockSpec((rows, d), lambda w_: (0, w_)), plane, plane, plane,
                  pl.BlockSpec(w_val.shape, lambda w_: (0, 0)),
                  pl.BlockSpec(w_gate.shape, lambda w_: (0, 0)),
                  pl.BlockSpec(vec.shape, lambda w_: (0, 0)),
                  pl.BlockSpec(dsk.shape, lambda w_: (0, 0, 0))],
        out_specs=pl.BlockSpec((rows, d), lambda w_: (0, w_)),
        out_shape=jax.ShapeDtypeStruct((rows, grid_w * d), F32),
        compiler_params=_cparams(("arbitrary",), 32),
        name="odd_out",
    )(h.reshape(rows, grid_w * d), u, yf, yb, w_val, w_gate, vec, dsk)
    return out.reshape(t, d)


def _s5_prep_kernel(prm_ref, bre_ref, bim_ref, colp_ref, cre_ref, cim_ref, win_ref, wout_ref, tt_ref, lam_ref, *,
                    reverse):
    T = S5_SUB
    ns = prm_ref.shape[1]
    hi = lax.Precision.HIGHEST
    a_re, a_im, dt = prm_ref[0:1, :], prm_ref[1:2, :], jnp.exp(prm_ref[2:3, :])
    tau = lax.broadcasted_iota(jnp.int32, (3 * SUBLANES, ns), 0).astype(F32)
    mag = jnp.exp(a_re * dt * tau)
    pr = mag * jnp.cos(a_im * dt * tau)
    pi = mag * jnp.sin(a_im * dt * tau)
    lr, li = pr[1:2], pi[1:2]
    den = a_re * a_re + a_im * a_im
    cr = ((lr - 1.0) * a_re + li * a_im) / den
    ci = (li * a_re - (lr - 1.0) * a_im) / den
    row_g = lax.broadcasted_iota(jnp.int32, (LANES, ns), 0) // S5_GROUP
    col_g = lax.broadcasted_iota(jnp.int32, (LANES, ns), 1) // S5_STATE
    same = row_g == col_g
    bbr = jnp.where(same, cr * bre_ref[...] - ci * bim_ref[...], 0.0)
    bbi = jnp.where(same, cr * bim_ref[...] + ci * bre_ref[...], 0.0)
    for s in range(T):
        e = s if reverse else T - 1 - s
        rows = slice(s * LANES, (s + 1) * LANES)
        win_ref[rows, :ns] = (pr[e:e + 1] * bbr - pi[e:e + 1] * bbi).astype(BF16)
        win_ref[rows, ns:] = (pr[e:e + 1] * bbi + pi[e:e + 1] * bbr).astype(BF16)
    lam_ref[...] = jnp.broadcast_to(jnp.concatenate([pr[T:T + 1], pi[T:T + 1]], axis=1), lam_ref.shape)
    dt_c = jnp.exp(colp_ref[2])
    mag_c = jnp.exp(colp_ref[0] * dt_c)
    lr_c = mag_c * jnp.cos(colp_ref[1] * dt_c)
    li_c = mag_c * jnp.sin(colp_ref[1] * dt_c)
    row_gc = lax.broadcasted_iota(jnp.int32, (ns, LANES), 0) // S5_STATE
    col_gc = lax.broadcasted_iota(jnp.int32, (ns, LANES), 1) // S5_GROUP
    same_c = row_gc == col_gc
    c_re = jnp.where(same_c, cre_ref[...], 0.0)
    c_im = jnp.where(same_c, cim_ref[...], 0.0)
    bb = jnp.concatenate([bbr, bbi], axis=1)
    cur_r = jnp.ones((ns, LANES), F32)
    cur_i = jnp.zeros((ns, LANES), F32)
    kblk = []
    for e in range(T + 1):
        wo_r = c_re * cur_r - c_im * cur_i
        wo_i = -(c_re * cur_i + c_im * cur_r)
        if e >= 1:
            t = (T - e) if reverse else e - 1
            wout_ref[:ns, t * LANES:(t + 1) * LANES] = wo_r.astype(BF16)
            wout_ref[ns:, t * LANES:(t + 1) * LANES] = wo_i.astype(BF16)
        if e < T:
            k = jnp.dot(bb, jnp.concatenate([wo_r, wo_i], axis=0), precision=hi, preferred_element_type=F32)
            kblk.append(k.astype(BF16))
        cur_r, cur_i = cur_r * lr_c - cur_i * li_c, cur_r * li_c + cur_i * lr_c
    zeros = jnp.zeros((LANES, LANES), BF16)
    for s in range(T):
        for t in range(T):
            lag = (s - t) if reverse else (t - s)
            tt_ref[s * LANES:(s + 1) * LANES, t * LANES:(t + 1) * LANES] = kblk[lag] if lag >= 0 else zeros


def _s5_prep(a_re, a_im, log_dt, b_re, b_im, c_re, c_im, reverse):
    g, p = a_re.shape
    cdim = b_re.shape[2]
    nj = g // S5_GPB
    ns = S5_GPB * p
    width = S5_SUB * LANES
    prm = jnp.zeros((nj, SUBLANES, ns), F32)
    prm = prm.at[:, 0].set(a_re.reshape(nj, ns)).at[:, 1].set(a_im.reshape(nj, ns))
    prm = prm.at[:, 2].set(jnp.repeat(log_dt, p).reshape(nj, ns))
    colp = jnp.broadcast_to(prm[:, :3, :, None], (nj, 3, ns, LANES))
    row_tile = lambda b: jnp.tile(b.reshape(nj, S5_GPB, p, cdim).transpose(0, 3, 1, 2).reshape(nj, 1, cdim, ns),
                                  (1, S5_GPB, 1, 1)).reshape(nj, LANES, ns)
    col_tile = lambda c_: jnp.tile(c_.reshape(nj, S5_GPB, cdim, p).transpose(0, 1, 3, 2).reshape(nj, ns, 1, cdim),
                                   (1, 1, S5_GPB, 1)).reshape(nj, ns, LANES)
    blk = lambda *shape: pl.BlockSpec((None,) + shape, lambda j: (j,) + (0,) * len(shape))
    return pl.pallas_call(
        functools.partial(_s5_prep_kernel, reverse=reverse),
        grid=(nj,),
        in_specs=[blk(SUBLANES, ns), blk(LANES, ns), blk(LANES, ns), blk(3, ns, LANES), blk(ns, LANES),
                  blk(ns, LANES)],
        out_specs=[blk(width, 2 * ns), blk(2 * ns, width), blk(width, width), blk(SUBLANES, 2 * ns)],
        out_shape=[jax.ShapeDtypeStruct((nj, width, 2 * ns), BF16), jax.ShapeDtypeStruct((nj, 2 * ns, width), BF16),
                   jax.ShapeDtypeStruct((nj, width, width), BF16), jax.ShapeDtypeStruct((nj, SUBLANES, 2 * ns), F32)],
        compiler_params=_cparams(("arbitrary",), 56),
        name="s5_prep_bwd" if reverse else "s5_prep_fwd",
    )(prm, row_tile(b_re), row_tile(b_im), colp, col_tile(c_re), col_tile(c_im))


def _rows(*vs, width):
    out = jnp.zeros((SUBLANES, width), F32)
    for i, v in enumerate(vs):
        out = out.at[i].set(v.astype(F32))
    return out


def kernel(x, c, ctx, c_ctx, ada_w, ada_b, norm_mix_g, norm_ffn_g, ffn_w_gate, ffn_w_up, ffn_w_down, final_norm_g, ev_w_in, ev_w_out, ssd_conv_w, ssd_conv_b, ssd_dt_bias, ssd_a_log, ssd_d, ssd_norm_g, hg_lb_logits, hg_norm_g, od_w_in, s5_a_re, s5_a_im, s5_log_dt, s5_b_re, s5_b_im, s5_c_re, s5_c_im, s5_d, od_w_val, od_w_gate):
    d = x.shape[-1]
    lat = x[0].astype(F32)
    hc = ctx[0].astype(F32)

    m = _ada(_rows(c[0], c_ctx, width=d), ada_w, ada_b)

    def ada_vecs(layer, stream):
        return [m[layer, stream, i * d:(i + 1) * d] for i in range(6)]

    w = ev_w_in[0]
    n_x = SSD_HEADS * SSD_HEAD_DIM
    n_xbc = n_x + 2 * SSD_GROUPS * SSD_STATE
    n_hg = HG_HEADS * HG_HEAD_DIM
    o_z, o_xbc, o_dt = 0, n_x, n_x + n_xbc
    o_q = o_dt + 2 * SSD_HEADS
    o_f = o_q + n_hg
    o_v = o_f + 2 * n_hg
    o_g = o_v + n_hg
    pad = lambda n: jnp.zeros((d, n), F32)
    w_in = jnp.concatenate([
        w[:, o_xbc:o_xbc + n_xbc],
        w[:, o_dt:o_dt + SSD_HEADS], pad(LANES - SSD_HEADS),
        w[:, o_dt + SSD_HEADS:o_dt + 2 * SSD_HEADS], pad(LANES - SSD_HEADS),
        pad(ZS_Z - ZS_DT - 2 * LANES),
        w[:, o_z:o_z + n_x], w[:, o_q:o_q + n_hg], w[:, o_v:o_v + n_hg], w[:, o_g:o_g + n_hg],
        w[:, o_f:o_f + 2 * n_hg]], axis=1).astype(BF16)
    cw = jnp.zeros((SUBLANES, n_xbc), F32).at[:SSD_CONV].set(ssd_conv_w[0])
    cb = ssd_conv_b[0].reshape(1, n_xbc)
    dtc = jnp.zeros((2, SUBLANES, LANES), F32)
    dtc = dtc.at[:, 0, :SSD_HEADS].set(ssd_dt_bias[0]).at[:, 1, :SSD_HEADS].set(ssd_a_log[0])
    dsk = jnp.repeat(ssd_d[0], SSD_HEAD_DIM).reshape(1, n_x)
    lower = jnp.cumsum(jax.nn.softmax(hg_lb_logits.astype(F32), axis=0), axis=0)[0].reshape(1, n_hg)
    head_of_lane = jnp.arange(n_x) // SSD_HEAD_DIM
    e16 = (jnp.arange(LANES)[:, None] == head_of_lane[None, :]).astype(BF16)
    consts = (cw, cb, dtc, dsk, lower, e16)
    w_out = ev_w_out[0].astype(BF16)
    out_vec = lambda gate: _rows(ssd_norm_g[0], jnp.tile(hg_norm_g[0], HG_HEADS), gate, width=d)
    wg0, wu0, wd0 = (t_[0].astype(BF16) for t_ in (ffn_w_gate, ffn_w_up, ffn_w_down))

    s_state = jnp.zeros((2, SSD_GROUPS, SSD_STATE, n_x // SSD_GROUPS), F32)
    g_state = jnp.zeros((2, HG_HEADS, HG_HEAD_DIM, HG_HEAD_DIM), F32)
    layer0 = {}
    for stream, h in ((1, hc), (0, lat)):
        sm, scm, gm, sf, scf, gf = ada_vecs(0, stream)
        zs = _mod_matmul(h, _rows(norm_mix_g[0], scm, sm, width=d), w_in)
        yf, of, yb, ob, s_state, g_state = _even_scan(zs, consts, s_state, g_state)
        h1 = _even_out(h, zs, yf, yb, of, ob, w_out, out_vec(gm))
        layer0[stream] = _ffn(h1, _rows(norm_ffn_g[0], scf, sf, gf, width=d), wg0, wu0, wd0, final=False)

    lat, hc = layer0[0], layer0[1]
    w_s5 = od_w_in[0].astype(BF16)
    u = {}
    for stream, h, gw in ((1, hc, 1), (0, lat, GRID_W)):
        sm, scm = ada_vecs(1, stream)[:2]
        planes = _odd_in(h, _rows(norm_mix_g[1], scm, sm, width=d), w_s5, gw)
        u[stream] = planes.reshape(planes.shape[0], planes.shape[1] // S5_SUB, S5_SUB * LANES)
    ys = {}
    for reverse in (False, True):
        di = 1 if reverse else 0
        win, wout, toe, lam_t = _s5_prep(s5_a_re[0, di], s5_a_im[0, di], s5_log_dt[0, di], s5_b_re[0, di],
                                             s5_b_im[0, di], s5_c_re[0, di], s5_c_im[0, di], reverse)
        _, h_ctx = _s5_scan(u[1], win, wout, toe, lam_t, jnp.zeros_like(lam_t), reverse)
        ys[reverse], _ = _s5_scan(u[0], win, wout, toe, lam_t, h_ctx, reverse)
    _, _, gm, sf, scf, gf = ada_vecs(1, 0)
    nblk = u[0].shape[0]
    plane = lambda a: a.reshape(nblk, -1, LANES)
    h3 = _odd_out(lat, plane(u[0]), plane(ys[False]), plane(ys[True]), od_w_val[0].astype(BF16),
                  od_w_gate[0].astype(BF16), _rows(gm, width=d), s5_d[0].reshape(nblk, 1, LANES), GRID_W)
    wg1, wu1, wd1 = (t_[1].astype(BF16) for t_ in (ffn_w_gate, ffn_w_up, ffn_w_down))
    out = _ffn(h3, _rows(norm_ffn_g[1], scf, sf, gf, final_norm_g, width=d), wg1, wu1, wd1, final=True)
    return out[None].astype(x.dtype)
```

```python
import functools
import math

import jax
import jax.numpy as jnp
from jax import lax
from jax.experimental import pallas as pl
from jax.experimental.pallas import tpu as pltpu

F32 = jnp.float32
BF16 = jnp.bfloat16
EPS = 1e-6

LANES = 128
SUBLANES = 8
GRID_W = 64
SCAN_CHUNK = 64
SSD_HEADS = 16
SSD_HEAD_DIM = 64
SSD_STATE = 128
SSD_GROUPS = 2
SSD_CONV = 5
HG_HEADS = 8
HG_HEAD_DIM = 128
S5_GROUP = 16
S5_STATE = 64
S5_SUB = 16
S5_GPB = LANES // S5_GROUP

ZS_XBC = 0
ZS_DT = 1536
ZS_Z = 2048
ZS_Q = 3072
ZS_V = 4096
ZS_G = 5120
ZS_F = 6144
ZS_WIDTH = 8192


def _cparams(semantics, vmem_mb):
    return pltpu.CompilerParams(dimension_semantics=semantics, vmem_limit_bytes=vmem_mb * 1024 * 1024)


def _silu(x):
    return x * jax.nn.sigmoid(x)


def _modulate(h, mod_ref):
    ms = jnp.mean(h * h, axis=-1, keepdims=True)
    return h * lax.rsqrt(ms + EPS) * (mod_ref[0:1, :] * (1.0 + mod_ref[1:2, :])) + mod_ref[2:3, :]


def _split_dot(x, w, passes):
    acc, rem = None, x
    for _ in range(passes):
        hi = rem.astype(BF16)
        d = jnp.dot(hi, w, preferred_element_type=F32)
        acc = d if acc is None else acc + d
        rem = rem - hi.astype(F32)
    return acc


def _tri_cumsum(tri, x, passes):
    acc, rem = None, x
    for _ in range(passes):
        hi = rem.astype(BF16)
        d = jnp.dot(tri, hi, preferred_element_type=F32)
        acc = d if acc is None else acc + d
        rem = rem - hi.astype(F32)
    return acc


def _ada_kernel(c_ref, w_ref, b_ref, o_ref):
    o_ref[...] = jnp.dot(_silu(c_ref[...]), w_ref[...], precision=lax.Precision.HIGHEST,
                         preferred_element_type=F32) + b_ref[...]


def _ada(cvecs, ada_w, ada_b):
    depth, d, n = ada_w.shape
    tn = n // 4
    return pl.pallas_call(
        _ada_kernel,
        grid=(depth, n // tn),
        in_specs=[pl.BlockSpec((SUBLANES, d), lambda l, j: (0, 0)),
                  pl.BlockSpec((None, d, tn), lambda l, j: (l, 0, j)),
                  pl.BlockSpec((None, 1, tn), lambda l, j: (l, 0, j))],
        out_specs=pl.BlockSpec((None, SUBLANES, tn), lambda l, j: (l, 0, j)),
        out_shape=jax.ShapeDtypeStruct((depth, SUBLANES, n), F32),
        compiler_params=_cparams(("arbitrary", "arbitrary"), 40),
        name="ada",
    )(cvecs, ada_w, ada_b.reshape(depth, 1, n))


def _mod_matmul_kernel(h_ref, mod_ref, w_ref, o_ref, a_ref):
    @pl.when(pl.program_id(1) == 0)
    def _():
        a_ref[...] = _modulate(h_ref[...], mod_ref).astype(BF16)

    o_ref[...] = jnp.dot(a_ref[...], w_ref[...], preferred_element_type=F32)


def _mod_matmul(h, mod, w):
    t, d = h.shape
    n = w.shape[1]
    tm = min(t, 1024)
    tn = 1024
    return pl.pallas_call(
        _mod_matmul_kernel,
        grid=(t // tm, n // tn),
        in_specs=[pl.BlockSpec((tm, d), lambda i, j: (i, 0)),
                  pl.BlockSpec((SUBLANES, d), lambda i, j: (0, 0)),
                  pl.BlockSpec((d, tn), lambda i, j: (0, j))],
        out_specs=pl.BlockSpec((tm, tn), lambda i, j: (i, j)),
        out_shape=jax.ShapeDtypeStruct((t, n), F32),
        scratch_shapes=[pltpu.VMEM((tm, d), BF16)],
        compiler_params=_cparams(("arbitrary", "arbitrary"), 48),
        name="even_in",
    )(h, mod, w)


def _scan_block_prep(x_ref, hp_ref, hn_ref, dt_ref, q_ref, v_ref, f_ref, cw_ref, cb_ref, dtc_ref, lb_ref, blk, nblk, d):
    rows = x_ref.shape[0]
    prev = jnp.where(blk == 0, 0.0, hp_ref[...])
    nxt = jnp.where(blk == nblk - 1, 0.0, hn_ref[...])
    ext = jnp.concatenate([prev, x_ref[...], nxt], axis=0)
    conv = cb_ref[...]
    for j in range(SSD_CONV):
        off = SUBLANES - SSD_CONV // 2 + j
        conv = conv + cw_ref[j:j + 1, :] * ext[off:off + rows]
    xbc = _silu(conv)
    dtv = jax.nn.softplus(dt_ref[...] + dtc_ref[d, 0:1, :])
    la = -dtv * jnp.exp(dtc_ref[d, 1:2, :])
    lb = lb_ref[...]
    f = lb + (1.0 - lb) * jax.nn.sigmoid(f_ref[...])
    return xbc, dtv, la, 1.0 - f, jnp.log(f), _silu(q_ref[...]), v_ref[...].astype(BF16)


def _scan_chunk(prep, rows, dsk_ref, e_ref, st_ref, gt_ref, y_ref, o_ref, d, reverse):
    Q = SCAN_CHUNK
    xbc, dtv, la, kk, logf, qs, vb = (a[rows] for a in prep)
    ti = lax.broadcasted_iota(jnp.int32, (Q, Q), 0)
    si = lax.broadcasted_iota(jnp.int32, (Q, Q), 1)
    mask = (ti <= si) if reverse else (ti >= si)
    tri = jnp.where(mask, 1.0, 0.0).astype(BF16)
    edge = 0 if reverse else Q - 1
    width = SSD_HEADS * SSD_HEAD_DIM
    gw = width // SSD_GROUPS
    xs = xbc[:, :width]

    acum = _tri_cumsum(tri, la, 3)
    last = acum[edge:edge + 1, :]
    acum_t = acum.T
    e_a = jnp.exp(acum)
    w_end = jnp.exp(last - acum)
    e_last = jnp.broadcast_to(jnp.exp(last), (SUBLANES, LANES))
    xe = _split_dot(jnp.concatenate([dtv, e_a, w_end], axis=0), e_ref[...], 1)
    dt_x, ea_x, wend_x = xe[:Q], xe[Q:2 * Q], xe[2 * Q:3 * Q]
    elast_x = _split_dot(e_last, e_ref[...], 3)[0:1]
    xdt = xs * dt_x
    xw = (xdt * wend_x).astype(BF16)
    xdt_b = xdt.astype(BF16)
    lane = lax.broadcasted_iota(jnp.int32, (Q, LANES), 1)
    lo = lane < SSD_HEAD_DIM
    zero_b = jnp.zeros((Q, LANES), BF16)
    ys = []
    for g in range(SSD_GROUPS):
        bm = xbc[:, width + g * SSD_STATE:width + (g + 1) * SSD_STATE].astype(BF16)
        cm = xbc[:, width + (SSD_GROUPS + g) * SSD_STATE:width + (SSD_GROUPS + g + 1) * SSD_STATE].astype(BF16)
        cb = lax.dot_general(cm, bm, (((1,), (1,)), ((), ())), preferred_element_type=F32)
        y_state = jnp.dot(cm, st_ref[d, g].astype(BF16), preferred_element_type=F32)
        pairs = []
        for hp in range(gw // LANES):
            xp = xdt_b[:, g * gw + hp * LANES:g * gw + (hp + 1) * LANES]
            acc = None
            for half in range(2):
                h = (g * gw + hp * LANES) // SSD_HEAD_DIM + half
                seg = acum[:, h:h + 1] - acum_t[h:h + 1, :]
                m = jnp.where(mask, cb * jnp.exp(seg), 0.0).astype(BF16)
                xh = jnp.where(lo, xp, zero_b) if half == 0 else jnp.where(lo, zero_b, xp)
                part = jnp.dot(m, xh, preferred_element_type=F32)
                acc = part if acc is None else acc + part
            pairs.append(acc)
        y_g = jnp.concatenate(pairs, axis=1) + y_state * ea_x[:, g * gw:(g + 1) * gw]
        ys.append(y_g)
        upd = lax.dot_general(bm, xw[:, g * gw:(g + 1) * gw], (((0,), (0,)), ((), ())),
                              preferred_element_type=F32)
        st_ref[d, g] = st_ref[d, g] * elast_x[:, g * gw:(g + 1) * gw] + upd
    y = jnp.concatenate(ys, axis=1)
    if not reverse:
        y = y + dsk_ref[...] * xs
    y_ref[rows, :] = y.astype(y_ref.dtype)

    bcum = _tri_cumsum(tri, logf, 2)
    blast = bcum[edge:edge + 1, :]
    qe = (qs * jnp.exp(bcum)).astype(BF16)
    k_end = kk * jnp.exp(blast - bcum)
    k_til = (k_end * jnp.exp(-blast)).astype(BF16)
    k_end = k_end.astype(BF16)
    e_blast = jnp.exp(blast)
    outs = []
    for h in range(HG_HEADS):
        sl = slice(h * HG_HEAD_DIM, (h + 1) * HG_HEAD_DIM)
        att = lax.dot_general(qe[:, sl], k_til[:, sl], (((1,), (1,)), ((), ())), preferred_element_type=F32)
        att = jnp.where(mask, att, 0.0).astype(BF16)
        gt = gt_ref[d, h]
        o_h = jnp.dot(att, vb[:, sl], preferred_element_type=F32)
        o_h = o_h + lax.dot_general(qe[:, sl], gt.astype(BF16), (((1,), (1,)), ((), ())),
                                    preferred_element_type=F32)
        outs.append(o_h)
        upd = lax.dot_general(vb[:, sl], k_end[:, sl], (((0,), (0,)), ((), ())), preferred_element_type=F32)
        gt_ref[d, h] = gt * e_blast[:, sl] + upd
    o_ref[rows, :] = jnp.concatenate(outs, axis=1).astype(o_ref.dtype)


def _even_scan_kernel(*refs, nblk, cpb):
    fwd_in, bwd_in = refs[0:7], refs[7:14]
    cw_ref, cb_ref, dtc_ref, dsk_ref, lb_ref, e_ref, h0s_ref, h0g_ref = refs[14:22]
    yf_ref, of_ref, yb_ref, ob_ref, st_ref, gt_ref = refs[22:28]
    c = pl.program_id(0)

    @pl.when(c == 0)
    def _():
        st_ref[...] = h0s_ref[...]
        gt_ref[...] = h0g_ref[...]

    streams = ((False, fwd_in, c, yf_ref, of_ref), (True, bwd_in, nblk - 1 - c, yb_ref, ob_ref))
    preps = [_scan_block_prep(*ins, cw_ref, cb_ref, dtc_ref, lb_ref, blk, nblk, d)
             for d, (_, ins, blk, _, _) in enumerate(streams)]
    for step in range(cpb):
        for d, (reverse, _, _, y_ref, o_ref) in enumerate(streams):
            ci = (cpb - 1 - step) if reverse else step
            rows = slice(ci * SCAN_CHUNK, (ci + 1) * SCAN_CHUNK)
            _scan_chunk(preps[d], rows, dsk_ref, e_ref, st_ref, gt_ref, y_ref, o_ref, d, reverse)


def _even_scan(zs, consts, h0s, h0g):
    t = zs.shape[0]
    cpb = 2
    rows = cpb * SCAN_CHUNK
    nblk = t // rows
    cw, cb, dtc, dsk, lb, e16 = consts
    n_x = SSD_HEADS * SSD_HEAD_DIM
    n_xbc = n_x + 2 * SSD_GROUPS * SSD_STATE
    n_hg = HG_HEADS * HG_HEAD_DIM
    hb = rows // SUBLANES

    def stream_specs(bidx, d):
        col = lambda width, start: (lambda c: (bidx(c), start // width))
        return [
            pl.BlockSpec((rows, n_xbc), col(n_xbc, ZS_XBC)),
            pl.BlockSpec((SUBLANES, n_xbc), lambda c: (jnp.maximum(bidx(c) * hb - 1, 0), 0)),
            pl.BlockSpec((SUBLANES, n_xbc), lambda c: (jnp.minimum((bidx(c) + 1) * hb, t // SUBLANES - 1), 0)),
            pl.BlockSpec((rows, LANES), col(LANES, ZS_DT + d * LANES)),
            pl.BlockSpec((rows, n_hg), col(n_hg, ZS_Q)),
            pl.BlockSpec((rows, n_hg), col(n_hg, ZS_V)),
            pl.BlockSpec((rows, n_hg), col(n_hg, ZS_F + d * n_hg)),
        ]

    fwd = lambda c: c
    bwd = lambda c: nblk - 1 - c
    whole = lambda a: pl.BlockSpec(a.shape, lambda c: (0,) * a.ndim)
    in_specs = stream_specs(fwd, 0) + stream_specs(bwd, 1) + [whole(a) for a in (cw, cb, dtc, dsk, lb, e16, h0s, h0g)]
    out_specs = [pl.BlockSpec((rows, n_x), lambda c: (c, 0)), pl.BlockSpec((rows, n_hg), lambda c: (c, 0)),
                 pl.BlockSpec((rows, n_x), lambda c: (bwd(c), 0)), pl.BlockSpec((rows, n_hg), lambda c: (bwd(c), 0)),
                 whole(h0s), whole(h0g)]
    out_shape = [jax.ShapeDtypeStruct((t, n_x), BF16), jax.ShapeDtypeStruct((t, n_hg), BF16),
                 jax.ShapeDtypeStruct((t, n_x), BF16), jax.ShapeDtypeStruct((t, n_hg), BF16),
                 jax.ShapeDtypeStruct(h0s.shape, F32), jax.ShapeDtypeStruct(h0g.shape, F32)]
    return pl.pallas_call(
        functools.partial(_even_scan_kernel, nblk=nblk, cpb=cpb),
        grid=(nblk,),
        in_specs=in_specs,
        out_specs=out_specs,
        out_shape=out_shape,
        compiler_params=_cparams(("arbitrary",), 48),
        name="even_scan",
    )(*([zs] * 14), cw, cb, dtc, dsk, lb, e16, h0s, h0g)


def _even_out_kernel(h_ref, z_ref, g_ref, yf_ref, yb_ref, of_ref, ob_ref, w_ref, vec_ref, o_ref):
    y = (yf_ref[...].astype(F32) + yb_ref[...].astype(F32)) * _silu(z_ref[...])
    o = of_ref[...].astype(F32) + ob_ref[...].astype(F32)
    gw = y.shape[1] // SSD_GROUPS
    parts = []
    for g in range(SSD_GROUPS):
        yg = y[:, g * gw:(g + 1) * gw]
        ms = jnp.mean(yg * yg, axis=-1, keepdims=True)
        parts.append(yg * lax.rsqrt(ms + EPS))
    yn = (jnp.concatenate(parts, axis=1) * vec_ref[0:1, :]).astype(BF16)
    parts = []
    for hh in range(HG_HEADS):
        oh = o[:, hh * HG_HEAD_DIM:(hh + 1) * HG_HEAD_DIM]
        ms = jnp.mean(oh * oh, axis=-1, keepdims=True)
        parts.append(oh * lax.rsqrt(ms + EPS))
    on = (jnp.concatenate(parts, axis=1) * vec_ref[1:2, :] * _silu(g_ref[...])).astype(BF16)
    half = yn.shape[1]
    mix = jnp.dot(yn, w_ref[:half, :], preferred_element_type=F32)
    mix = mix + jnp.dot(on, w_ref[half:, :], preferred_element_type=F32)
    o_ref[...] = h_ref[...] + vec_ref[2:3, :] * mix


def _even_out(h, zs, yf, yb, of, ob, w_out, vec):
    t, d = h.shape
    tm = min(t, 512)
    n_x = SSD_HEADS * SSD_HEAD_DIM
    n_hg = HG_HEADS * HG_HEAD_DIM
    row = lambda i: (i, 0)
    return pl.pallas_call(
        _even_out_kernel,
        grid=(t // tm,),
        in_specs=[pl.BlockSpec((tm, d), row),
                  pl.BlockSpec((tm, n_x), lambda i: (i, ZS_Z // n_x)),
                  pl.BlockSpec((tm, n_hg), lambda i: (i, ZS_G // n_hg)),
                  pl.BlockSpec((tm, n_x), row), pl.BlockSpec((tm, n_x), row),
                  pl.BlockSpec((tm, n_hg), row), pl.BlockSpec((tm, n_hg), row),
                  pl.BlockSpec(w_out.shape, lambda i: (0, 0)),
                  pl.BlockSpec(vec.shape, lambda i: (0, 0))],
        out_specs=pl.BlockSpec((tm, d), row),
        out_shape=jax.ShapeDtypeStruct((t, d), F32),
        compiler_params=_cparams(("arbitrary",), 56),
        name="even_out",
    )(h, zs, zs, yf, yb, of, ob, w_out, vec)


def _ffn_kernel(h_ref, mod_ref, wg_ref, wu_ref, wd_ref, o_ref, a_ref, acc_ref, *, final):
    j = pl.program_id(1)

    @pl.when(j == 0)
    def _():
        a_ref[...] = _modulate(h_ref[...], mod_ref).astype(BF16)
        acc_ref[...] = jnp.zeros_like(acc_ref)

    a = a_ref[...]
    gate = jnp.dot(a, wg_ref[...], preferred_element_type=F32)
    up = jnp.dot(a, wu_ref[...], preferred_element_type=F32)
    act = (_silu(gate) * up).astype(BF16)
    acc_ref[...] += jnp.dot(act, wd_ref[...], preferred_element_type=F32)

    @pl.when(j == pl.num_programs(1) - 1)
    def _():
        out = h_ref[...] + mod_ref[3:4, :] * acc_ref[...]
        if final:
            ms = jnp.mean(out * out, axis=-1, keepdims=True)
            out = out * lax.rsqrt(ms + EPS) * mod_ref[4:5, :]
        o_ref[...] = out


def _ffn(h, mod, wg, wu, wd, final):
    t, d = h.shape
    f = wg.shape[1]
    tm = min(t, 512)
    tf = f // 2
    return pl.pallas_call(
        functools.partial(_ffn_kernel, final=final),
        grid=(t // tm, f // tf),
        in_specs=[pl.BlockSpec((tm, d), lambda i, j: (i, 0)),
                  pl.BlockSpec((SUBLANES, d), lambda i, j: (0, 0)),
                  pl.BlockSpec((d, tf), lambda i, j: (0, j)),
                  pl.BlockSpec((d, tf), lambda i, j: (0, j)),
                  pl.BlockSpec((tf, d), lambda i, j: (j, 0))],
        out_specs=pl.BlockSpec((tm, d), lambda i, j: (i, 0)),
        out_shape=jax.ShapeDtypeStruct((t, d), F32),
        scratch_shapes=[pltpu.VMEM((tm, d), BF16), pltpu.VMEM((tm, d), F32)],
        compiler_params=_cparams(("arbitrary", "arbitrary"), 56),
        name="ffn_final" if final else "ffn",
    )(h, mod, wg, wu, wd)


def _odd_in_kernel(h_ref, mod_ref, w_ref, o_ref):
    rows = h_ref.shape[0]
    for wi in range(h_ref.shape[1]):
        a = _modulate(h_ref[:, wi, :], mod_ref).astype(BF16)
        u = jnp.dot(a, w_ref[...], preferred_element_type=F32)
        for j in range(o_ref.shape[0]):
            o_ref[j, wi * rows:(wi + 1) * rows, :] = u[:, j * LANES:(j + 1) * LANES]


def _odd_in(h, mod, w, grid_w):
    t, d = h.shape
    rows = t // grid_w
    wb = min(grid_w, SUBLANES)
    nj = w.shape[1] // LANES
    return pl.pallas_call(
        _odd_in_kernel,
        grid=(grid_w // wb,),
        in_specs=[pl.BlockSpec((rows, wb, d), lambda i: (0, i, 0)),
                  pl.BlockSpec((SUBLANES, d), lambda i: (0, 0)),
                  pl.BlockSpec(w.shape, lambda i: (0, 0))],
        out_specs=pl.BlockSpec((nj, wb * rows, LANES), lambda i: (0, i, 0)),
        out_shape=jax.ShapeDtypeStruct((nj, t, LANES), F32),
        compiler_params=_cparams(("arbitrary",), 40),
        name="odd_in",
    )(h.reshape(rows, grid_w, d), mod, w)


def _sub_chunk_rows(u_ref, nb):
    return [u_ref[pl.ds(s, nb, stride=S5_SUB), :] for s in range(S5_SUB)]


def _s5_inject_kernel(u_ref, win_ref, s_ref):
    x = jnp.concatenate(_sub_chunk_rows(u_ref, s_ref.shape[0]), axis=1).astype(BF16)
    s_ref[...] = jnp.dot(x, win_ref[...], preferred_element_type=F32)


def _s5_inject(u, win):
    nj, t, _ = u.shape
    n = t // S5_SUB
    nb = min(n, 512)
    width, ns = win.shape[1:]
    return pl.pallas_call(
        _s5_inject_kernel,
        grid=(nj, n // nb),
        in_specs=[pl.BlockSpec((None, nb * S5_SUB, LANES), lambda j, b: (j, b, 0)),
                  pl.BlockSpec((None, width, ns), lambda j, b: (j, 0, 0))],
        out_specs=pl.BlockSpec((None, nb, ns), lambda j, b: (j, b, 0)),
        out_shape=jax.ShapeDtypeStruct((nj, n, ns), F32),
        compiler_params=_cparams(("arbitrary", "arbitrary"), 40),
        name="s5_inject",
    )(u, win)


def _s5_carry_kernel(sf_ref, sb_ref, lam_ref, h0_ref, pf_ref, pb_ref, hf_ref, tf_ref, tb_ref, c_ref):
    b = pl.program_id(0)
    nj, kb, ns = sf_ref.shape
    half = ns // 2

    @pl.when(b == 0)
    def _():
        c_ref[...] = h0_ref[...]

    for j in range(nj):
        tf_ref[:, j, :] = sf_ref[j]
        tb_ref[:, j, :] = sb_ref[j]

    def advance(d, c, s):
        lr, li = lam_ref[d, :, :half], lam_ref[d, :, half:]
        pr, pi = c[:, :half], c[:, half:]
        return jnp.concatenate([lr * pr - li * pi + s[:, :half], lr * pi + li * pr + s[:, half:]], axis=1)

    def step(i, carry):
        cf, cb = carry
        kr = kb - 1 - i
        sf = tf_ref[i]
        sb = tb_ref[kr]
        tf_ref[i] = cf
        tb_ref[kr] = cb
        return advance(0, cf, sf), advance(1, cb, sb)

    cf, cb = lax.fori_loop(0, kb, step, (c_ref[0], c_ref[1]))
    c_ref[0] = cf
    c_ref[1] = cb
    for j in range(nj):
        pf_ref[j] = tf_ref[:, j, :]
        pb_ref[j] = tb_ref[:, j, :]

    @pl.when(b == pl.num_programs(0) - 1)
    def _():
        hf_ref[...] = c_ref[...]


def _s5_carry(s_f, s_b, lam, h0):
    nj, n, ns = s_f.shape
    kb = min(n, 256)
    nblk = n // kb
    fwd = pl.BlockSpec((nj, kb, ns), lambda b: (0, b, 0))
    bwd = pl.BlockSpec((nj, kb, ns), lambda b: (0, nblk - 1 - b, 0))
    whole = pl.BlockSpec(lam.shape, lambda b: (0, 0, 0))
    return pl.pallas_call(
        _s5_carry_kernel,
        grid=(nblk,),
        in_specs=[fwd, bwd, whole, whole],
        out_specs=[fwd, bwd, whole],
        out_shape=[jax.ShapeDtypeStruct(s_f.shape, F32), jax.ShapeDtypeStruct(s_b.shape, F32),
                   jax.ShapeDtypeStruct(lam.shape, F32)],
        scratch_shapes=[pltpu.VMEM((kb, nj, ns), F32), pltpu.VMEM((kb, nj, ns), F32), pltpu.VMEM(lam.shape, F32)],
        compiler_params=_cparams(("arbitrary",), 48),
        name="s5_carry",
    )(s_f, s_b, lam, h0)


def _s5_readout_kernel(u_ref, p_ref, wout_ref, tt_ref, add_ref, y_ref, *, reverse):
    nb = p_ref.shape[0]
    us = _sub_chunk_rows(u_ref, nb)
    xb = jnp.concatenate(us, axis=1).astype(BF16)
    pb = p_ref[...].astype(BF16)
    width = xb.shape[1]
    tile = 2 * LANES
    for m in range(width // tile):
        cols = slice(m * tile, (m + 1) * tile)
        rows = slice(m * tile, width) if reverse else slice(0, (m + 1) * tile)
        y = jnp.dot(pb, wout_ref[:, cols], preferred_element_type=F32)
        y = y + jnp.dot(xb[:, rows], tt_ref[rows, cols], preferred_element_type=F32)
        for q in range(tile // LANES):
            s = m * (tile // LANES) + q
            other = add_ref[pl.ds(s, nb, stride=S5_SUB), :] if reverse else add_ref[...] * us[s]
            y_ref[pl.ds(s, nb, stride=S5_SUB), :] = y[:, q * LANES:(q + 1) * LANES] + other


def _s5_readout(u, p, wout, tt, add, reverse):
    nj, t, _ = u.shape
    n = t // S5_SUB
    nb = min(n, 256)
    ns, width = wout.shape[1:]
    tokens = pl.BlockSpec((None, nb * S5_SUB, LANES), lambda j, b: (j, b, 0))
    add_spec = tokens if reverse else pl.BlockSpec((None, 1, LANES), lambda j, b: (j, 0, 0))
    return pl.pallas_call(
        functools.partial(_s5_readout_kernel, reverse=reverse),
        grid=(nj, n // nb),
        in_specs=[tokens,
                  pl.BlockSpec((None, nb, ns), lambda j, b: (j, b, 0)),
                  pl.BlockSpec((None, ns, width), lambda j, b: (j, 0, 0)),
                  pl.BlockSpec((None, width, width), lambda j, b: (j, 0, 0)),
                  add_spec],
        out_specs=tokens,
        out_shape=jax.ShapeDtypeStruct(u.shape, F32),
        compiler_params=_cparams(("arbitrary", "arbitrary"), 48),
        name="s5_readout_bwd" if reverse else "s5_readout_fwd",
    )(u, p, wout, tt, add)


def _odd_out_kernel(h_ref, y_ref, wv_ref, wg_ref, vec_ref, o_ref):
    rows = h_ref.shape[0]
    for wi in range(h_ref.shape[1]):
        y = jnp.concatenate([y_ref[j, wi * rows:(wi + 1) * rows, :] for j in range(y_ref.shape[0])], axis=1)
        a = jax.nn.gelu(y).astype(BF16)
        val = jnp.dot(a, wv_ref[...], preferred_element_type=F32)
        gate = jnp.dot(a, wg_ref[...], preferred_element_type=F32)
        o_ref[:, wi, :] = h_ref[:, wi, :] + vec_ref[0:1, :] * (val * jax.nn.sigmoid(gate))


def _odd_out(h, y, w_val, w_gate, vec, grid_w):
    t, d = h.shape
    rows = t // grid_w
    wb = min(grid_w, SUBLANES)
    nj = y.shape[0]
    tok = pl.BlockSpec((rows, wb, d), lambda i: (0, i, 0))
    out = pl.pallas_call(
        _odd_out_kernel,
        grid=(grid_w // wb,),
        in_specs=[tok, pl.BlockSpec((nj, wb * rows, LANES), lambda i: (0, i, 0)),
                  pl.BlockSpec(w_val.shape, lambda i: (0, 0)),
                  pl.BlockSpec(w_gate.shape, lambda i: (0, 0)),
                  pl.BlockSpec(vec.shape, lambda i: (0, 0))],
        out_specs=tok,
        out_shape=jax.ShapeDtypeStruct((rows, grid_w, d), F32),
        compiler_params=_cparams(("arbitrary",), 56),
        name="odd_out",
    )(h.reshape(rows, grid_w, d), y, w_val, w_gate, vec)
    return out.reshape(t, d)


def _s5_prep_kernel(prm_ref, bre_ref, bim_ref, colp_ref, cre_ref, cim_ref, win_ref, wout_ref, tt_ref, lam_ref, *,
                    reverse):
    T = S5_SUB
    ns = prm_ref.shape[1]
    hi = lax.Precision.HIGHEST
    a_re, a_im, dt = prm_ref[0:1, :], prm_ref[1:2, :], jnp.exp(prm_ref[2:3, :])
    tau = lax.broadcasted_iota(jnp.int32, (3 * SUBLANES, ns), 0).astype(F32)
    mag = jnp.exp(a_re * dt * tau)
    pr = mag * jnp.cos(a_im * dt * tau)
    pi = mag * jnp.sin(a_im * dt * tau)
    lr, li = pr[1:2], pi[1:2]
    den = a_re * a_re + a_im * a_im
    cr = ((lr - 1.0) * a_re + li * a_im) / den
    ci = (li * a_re - (lr - 1.0) * a_im) / den
    row_g = lax.broadcasted_iota(jnp.int32, (LANES, ns), 0) // S5_GROUP
    col_g = lax.broadcasted_iota(jnp.int32, (LANES, ns), 1) // S5_STATE
    same = row_g == col_g
    bbr = jnp.where(same, cr * bre_ref[...] - ci * bim_ref[...], 0.0)
    bbi = jnp.where(same, cr * bim_ref[...] + ci * bre_ref[...], 0.0)
    for s in range(T):
        e = s if reverse else T - 1 - s
        rows = slice(s * LANES, (s + 1) * LANES)
        win_ref[rows, :ns] = (pr[e:e + 1] * bbr - pi[e:e + 1] * bbi).astype(BF16)
        win_ref[rows, ns:] = (pr[e:e + 1] * bbi + pi[e:e + 1] * bbr).astype(BF16)
    lam_ref[...] = jnp.broadcast_to(jnp.concatenate([pr[T:T + 1], pi[T:T + 1]], axis=1), lam_ref.shape)
    dt_c = jnp.exp(colp_ref[2])
    mag_c = jnp.exp(colp_ref[0] * dt_c)
    lr_c = mag_c * jnp.cos(colp_ref[1] * dt_c)
    li_c = mag_c * jnp.sin(colp_ref[1] * dt_c)
    row_gc = lax.broadcasted_iota(jnp.int32, (ns, LANES), 0) // S5_STATE
    col_gc = lax.broadcasted_iota(jnp.int32, (ns, LANES), 1) // S5_GROUP
    same_c = row_gc == col_gc
    c_re = jnp.where(same_c, cre_ref[...], 0.0)
    c_im = jnp.where(same_c, cim_ref[...], 0.0)
    bb = jnp.concatenate([bbr, bbi], axis=1)
    cur_r = jnp.ones((ns, LANES), F32)
    cur_i = jnp.zeros((ns, LANES), F32)
    kblk = []
    for e in range(T + 1):
        wo_r = c_re * cur_r - c_im * cur_i
        wo_i = -(c_re * cur_i + c_im * cur_r)
        if e >= 1:
            t = (T - e) if reverse else e - 1
            wout_ref[:ns, t * LANES:(t + 1) * LANES] = wo_r.astype(BF16)
            wout_ref[ns:, t * LANES:(t + 1) * LANES] = wo_i.astype(BF16)
        if e < T:
            k = jnp.dot(bb, jnp.concatenate([wo_r, wo_i], axis=0), precision=hi, preferred_element_type=F32)
            kblk.append(k.astype(BF16))
        cur_r, cur_i = cur_r * lr_c - cur_i * li_c, cur_r * li_c + cur_i * lr_c
    zeros = jnp.zeros((LANES, LANES), BF16)
    for s in range(T):
        for t in range(T):
            lag = (s - t) if reverse else (t - s)
            tt_ref[s * LANES:(s + 1) * LANES, t * LANES:(t + 1) * LANES] = kblk[lag] if lag >= 0 else zeros


def _s5_prep(a_re, a_im, log_dt, b_re, b_im, c_re, c_im, reverse):
    g, p = a_re.shape
    cdim = b_re.shape[2]
    nj = g // S5_GPB
    ns = S5_GPB * p
    width = S5_SUB * LANES
    prm = jnp.zeros((nj, SUBLANES, ns), F32)
    prm = prm.at[:, 0].set(a_re.reshape(nj, ns)).at[:, 1].set(a_im.reshape(nj, ns))
    prm = prm.at[:, 2].set(jnp.repeat(log_dt, p).reshape(nj, ns))
    colp = jnp.broadcast_to(prm[:, :3, :, None], (nj, 3, ns, LANES))
    row_tile = lambda b: jnp.tile(b.reshape(nj, S5_GPB, p, cdim).transpose(0, 3, 1, 2).reshape(nj, 1, cdim, ns),
                                  (1, S5_GPB, 1, 1)).reshape(nj, LANES, ns)
    col_tile = lambda c_: jnp.tile(c_.reshape(nj, S5_GPB, cdim, p).transpose(0, 1, 3, 2).reshape(nj, ns, 1, cdim),
                                   (1, 1, S5_GPB, 1)).reshape(nj, ns, LANES)
    blk = lambda *shape: pl.BlockSpec((None,) + shape, lambda j: (j,) + (0,) * len(shape))
    return pl.pallas_call(
        functools.partial(_s5_prep_kernel, reverse=reverse),
        grid=(nj,),
        in_specs=[blk(SUBLANES, ns), blk(LANES, ns), blk(LANES, ns), blk(3, ns, LANES), blk(ns, LANES),
                  blk(ns, LANES)],
        out_specs=[blk(width, 2 * ns), blk(2 * ns, width), blk(width, width), blk(SUBLANES, 2 * ns)],
        out_shape=[jax.ShapeDtypeStruct((nj, width, 2 * ns), BF16), jax.ShapeDtypeStruct((nj, 2 * ns, width), BF16),
                   jax.ShapeDtypeStruct((nj, width, width), BF16), jax.ShapeDtypeStruct((nj, SUBLANES, 2 * ns), F32)],
        compiler_params=_cparams(("arbitrary",), 56),
        name="s5_prep_bwd" if reverse else "s5_prep_fwd",
    )(prm, row_tile(b_re), row_tile(b_im), colp, col_tile(c_re), col_tile(c_im))


def _rows(*vs, width):
    out = jnp.zeros((SUBLANES, width), F32)
    for i, v in enumerate(vs):
        out = out.at[i].set(v.astype(F32))
    return out


def kernel(x, c, ctx, c_ctx, ada_w, ada_b, norm_mix_g, norm_ffn_g, ffn_w_gate, ffn_w_up, ffn_w_down, final_norm_g, ev_w_in, ev_w_out, ssd_conv_w, ssd_conv_b, ssd_dt_bias, ssd_a_log, ssd_d, ssd_norm_g, hg_lb_logits, hg_norm_g, od_w_in, s5_a_re, s5_a_im, s5_log_dt, s5_b_re, s5_b_im, s5_c_re, s5_c_im, s5_d, od_w_val, od_w_gate):
    d = x.shape[-1]
    lat = x[0].astype(F32)
    hc = ctx[0].astype(F32)

    m = _ada(_rows(c[0], c_ctx, width=d), ada_w, ada_b)

    def ada_vecs(layer, stream):
        return [m[layer, stream, i * d:(i + 1) * d] for i in range(6)]

    w = ev_w_in[0]
    n_x = SSD_HEADS * SSD_HEAD_DIM
    n_xbc = n_x + 2 * SSD_GROUPS * SSD_STATE
    n_hg = HG_HEADS * HG_HEAD_DIM
    o_z, o_xbc, o_dt = 0, n_x, n_x + n_xbc
    o_q = o_dt + 2 * SSD_HEADS
    o_f = o_q + n_hg
    o_v = o_f + 2 * n_hg
    o_g = o_v + n_hg
    pad = lambda n: jnp.zeros((d, n), F32)
    w_in = jnp.concatenate([
        w[:, o_xbc:o_xbc + n_xbc],
        w[:, o_dt:o_dt + SSD_HEADS], pad(LANES - SSD_HEADS),
        w[:, o_dt + SSD_HEADS:o_dt + 2 * SSD_HEADS], pad(LANES - SSD_HEADS),
        pad(ZS_Z - ZS_DT - 2 * LANES),
        w[:, o_z:o_z + n_x], w[:, o_q:o_q + n_hg], w[:, o_v:o_v + n_hg], w[:, o_g:o_g + n_hg],
        w[:, o_f:o_f + 2 * n_hg]], axis=1).astype(BF16)
    cw = jnp.zeros((SUBLANES, n_xbc), F32).at[:SSD_CONV].set(ssd_conv_w[0])
    cb = ssd_conv_b[0].reshape(1, n_xbc)
    dtc = jnp.zeros((2, SUBLANES, LANES), F32)
    dtc = dtc.at[:, 0, :SSD_HEADS].set(ssd_dt_bias[0]).at[:, 1, :SSD_HEADS].set(ssd_a_log[0])
    dsk = jnp.repeat(ssd_d[0], SSD_HEAD_DIM).reshape(1, n_x)
    lower = jnp.cumsum(jax.nn.softmax(hg_lb_logits.astype(F32), axis=0), axis=0)[0].reshape(1, n_hg)
    head_of_lane = jnp.arange(n_x) // SSD_HEAD_DIM
    e16 = (jnp.arange(LANES)[:, None] == head_of_lane[None, :]).astype(BF16)
    consts = (cw, cb, dtc, dsk, lower, e16)
    w_out = ev_w_out[0].astype(BF16)
    out_vec = lambda gate: _rows(ssd_norm_g[0], jnp.tile(hg_norm_g[0], HG_HEADS), gate, width=d)
    wg0, wu0, wd0 = (t_[0].astype(BF16) for t_ in (ffn_w_gate, ffn_w_up, ffn_w_down))

    s_state = jnp.zeros((2, SSD_GROUPS, SSD_STATE, n_x // SSD_GROUPS), F32)
    g_state = jnp.zeros((2, HG_HEADS, HG_HEAD_DIM, HG_HEAD_DIM), F32)
    layer0 = {}
    for stream, h in ((1, hc), (0, lat)):
        sm, scm, gm, sf, scf, gf = ada_vecs(0, stream)
        zs = _mod_matmul(h, _rows(norm_mix_g[0], scm, sm, width=d), w_in)
        yf, of, yb, ob, s_state, g_state = _even_scan(zs, consts, s_state, g_state)
        h1 = _even_out(h, zs, yf, yb, of, ob, w_out, out_vec(gm))
        layer0[stream] = _ffn(h1, _rows(norm_ffn_g[0], scf, sf, gf, width=d), wg0, wu0, wd0, final=False)

    lat, hc = layer0[0], layer0[1]
    w_s5 = od_w_in[0].astype(BF16)
    u = {}
    for stream, h, gw in ((1, hc, 1), (0, lat, GRID_W)):
        sm, scm = ada_vecs(1, stream)[:2]
        u[stream] = _odd_in(h, _rows(norm_mix_g[1], scm, sm, width=d), w_s5, gw)
    mats = [_s5_prep(s5_a_re[0, di], s5_a_im[0, di], s5_log_dt[0, di], s5_b_re[0, di], s5_b_im[0, di],
                     s5_c_re[0, di], s5_c_im[0, di], reverse=bool(di)) for di in range(2)]
    lam = jnp.stack([mats[0][3][:, 0], mats[1][3][:, 0]])
    carry = jnp.zeros_like(lam)
    for stream in (1, 0):
        s_f, s_b = (_s5_inject(u[stream], mats[di][0]) for di in range(2))
        p_f, p_b, carry = _s5_carry(s_f, s_b, lam, carry)
    nj = u[0].shape[0]
    y = _s5_readout(u[0], p_f, mats[0][1], mats[0][2], s5_d[0].reshape(nj, 1, LANES), reverse=False)
    y = _s5_readout(u[0], p_b, mats[1][1], mats[1][2], y, reverse=True)
    _, _, gm, sf, scf, gf = ada_vecs(1, 0)
    h3 = _odd_out(lat, y, od_w_val[0].astype(BF16), od_w_gate[0].astype(BF16), _rows(gm, width=d), GRID_W)
    wg1, wu1, wd1 = (t_[1].astype(BF16) for t_ in (ffn_w_gate, ffn_w_up, ffn_w_down))
    out = _ffn(h3, _rows(norm_ffn_g[1], scf, sf, gf, final_norm_g, width=d), wg1, wu1, wd1, final=True)
    return out[None].astype(x.dtype)
```

```python
import functools
import math

import jax
import jax.numpy as jnp
from jax import lax
from jax.experimental import pallas as pl
from jax.experimental.pallas import tpu as pltpu

F32 = jnp.float32
BF16 = jnp.bfloat16
EPS = 1e-6

LANES = 128
SUBLANES = 8
GRID_W = 64
SCAN_CHUNK = 64
SSD_HEADS = 16
SSD_HEAD_DIM = 64
SSD_STATE = 128
SSD_GROUPS = 2
SSD_CONV = 5
HG_HEADS = 8
HG_HEAD_DIM = 128
S5_GROUP = 16
S5_STATE = 64
S5_SUB = 16
S5_GPB = LANES // S5_GROUP

ZS_XBC = 0
ZS_DT = 1536
ZS_XD = 1792
ZS_Q = 2048
ZS_V = 3072
ZS_F = 4096
ZS_Z = 6144
ZS_G = 7168
ZS_WIDTH = 8192


def _cparams(semantics, vmem_mb):
    return pltpu.CompilerParams(dimension_semantics=semantics, vmem_limit_bytes=vmem_mb * 1024 * 1024)


def _silu(x):
    return x * jax.nn.sigmoid(x)


def _modulate(h, mod_ref):
    ms = jnp.mean(h * h, axis=-1, keepdims=True)
    return h * lax.rsqrt(ms + EPS) * (mod_ref[0:1, :] * (1.0 + mod_ref[1:2, :])) + mod_ref[2:3, :]


def _split_dot(x, w, passes):
    acc, rem = None, x
    for _ in range(passes):
        hi = rem.astype(BF16)
        d = jnp.dot(hi, w, preferred_element_type=F32)
        acc = d if acc is None else acc + d
        rem = rem - hi.astype(F32)
    return acc


def _tri_cumsum(tri, x, passes):
    acc, rem = None, x
    for _ in range(passes):
        hi = rem.astype(BF16)
        d = jnp.dot(tri, hi, preferred_element_type=F32)
        acc = d if acc is None else acc + d
        rem = rem - hi.astype(F32)
    return acc


def _ada_kernel(c_ref, w_ref, b_ref, o_ref):
    o_ref[...] = jnp.dot(_silu(c_ref[...]), w_ref[...], precision=lax.Precision.HIGHEST,
                         preferred_element_type=F32) + b_ref[...]


def _ada(cvecs, ada_w, ada_b):
    depth, d, n = ada_w.shape
    tn = n // 4
    return pl.pallas_call(
        _ada_kernel,
        grid=(depth, n // tn),
        in_specs=[pl.BlockSpec((SUBLANES, d), lambda l, j: (0, 0)),
                  pl.BlockSpec((None, d, tn), lambda l, j: (l, 0, j)),
                  pl.BlockSpec((None, 1, tn), lambda l, j: (l, 0, j))],
        out_specs=pl.BlockSpec((None, SUBLANES, tn), lambda l, j: (l, 0, j)),
        out_shape=jax.ShapeDtypeStruct((depth, SUBLANES, n), F32),
        compiler_params=_cparams(("arbitrary", "arbitrary"), 40),
        name="ada",
    )(cvecs, ada_w, ada_b.reshape(depth, 1, n))


def _mod_matmul_kernel(h_ref, mod_ref, w_ref, o_ref, a_ref):
    @pl.when(pl.program_id(1) == 0)
    def _():
        a_ref[...] = _modulate(h_ref[...], mod_ref).astype(BF16)

    o_ref[...] = jnp.dot(a_ref[...], w_ref[...], preferred_element_type=F32)


def _mod_matmul(h, mod, w):
    t, d = h.shape
    n = w.shape[1]
    tm = min(t, 1024)
    tn = 1024
    return pl.pallas_call(
        _mod_matmul_kernel,
        grid=(t // tm, n // tn),
        in_specs=[pl.BlockSpec((tm, d), lambda i, j: (i, 0)),
                  pl.BlockSpec((SUBLANES, d), lambda i, j: (0, 0)),
                  pl.BlockSpec((d, tn), lambda i, j: (0, j))],
        out_specs=pl.BlockSpec((tm, tn), lambda i, j: (i, j)),
        out_shape=jax.ShapeDtypeStruct((t, n), F32),
        scratch_shapes=[pltpu.VMEM((tm, d), BF16)],
        compiler_params=_cparams(("arbitrary", "arbitrary"), 48),
        name="even_in",
    )(h, mod, w)


def _scan_block_prep(xd_ref, hp_ref, hn_ref, qv_ref, f_ref, cw_ref, cb_ref, dtc_ref, lb_ref, blk, nblk, d):
    rows = xd_ref.shape[0]
    n_xbc = hp_ref.shape[1]
    n_hg = f_ref.shape[1]
    prev = jnp.where(blk == 0, 0.0, hp_ref[...])
    nxt = jnp.where(blk == nblk - 1, 0.0, hn_ref[...])
    ext = jnp.concatenate([prev, xd_ref[:, :n_xbc], nxt], axis=0)
    conv = cb_ref[...]
    for j in range(SSD_CONV):
        off = SUBLANES - SSD_CONV // 2 + j
        conv = conv + cw_ref[j:j + 1, :] * ext[off:off + rows]
    xbc = _silu(conv)
    dt_raw = xd_ref[:, ZS_DT + d * LANES:ZS_DT + (d + 1) * LANES]
    dtv = jax.nn.softplus(dt_raw + dtc_ref[d, 0:1, :])
    la = -dtv * jnp.exp(dtc_ref[d, 1:2, :])
    lb = lb_ref[...]
    f = lb + (1.0 - lb) * jax.nn.sigmoid(f_ref[...])
    return xbc, dtv, la, 1.0 - f, jnp.log(f), _silu(qv_ref[:, :n_hg]), qv_ref[:, n_hg:].astype(BF16)


def _scan_chunk(prep, rows, dsk_ref, e_ref, st_ref, gt_ref, yo_ref, d, reverse):
    Q = SCAN_CHUNK
    xbc, dtv, la, kk, logf, qs, vb = (a[rows] for a in prep)
    ti = lax.broadcasted_iota(jnp.int32, (Q, Q), 0)
    si = lax.broadcasted_iota(jnp.int32, (Q, Q), 1)
    mask = (ti <= si) if reverse else (ti >= si)
    tri = jnp.where(mask, 1.0, 0.0).astype(BF16)
    edge = 0 if reverse else Q - 1
    width = SSD_HEADS * SSD_HEAD_DIM
    gw = width // SSD_GROUPS
    xs = xbc[:, :width]

    acum = _tri_cumsum(tri, la, 3)
    last = acum[edge:edge + 1, :]
    acum_t = acum.T
    e_a = jnp.exp(acum)
    w_end = jnp.exp(last - acum)
    e_last = jnp.broadcast_to(jnp.exp(last), (SUBLANES, LANES))
    xe = _split_dot(jnp.concatenate([dtv, e_a, w_end], axis=0), e_ref[...], 1)
    dt_x, ea_x, wend_x = xe[:Q], xe[Q:2 * Q], xe[2 * Q:3 * Q]
    elast_x = _split_dot(e_last, e_ref[...], 3)[0:1]
    xdt = xs * dt_x
    xw = (xdt * wend_x).astype(BF16)
    xdt_b = xdt.astype(BF16)
    lane = lax.broadcasted_iota(jnp.int32, (Q, LANES), 1)
    lo = lane < SSD_HEAD_DIM
    zero_b = jnp.zeros((Q, LANES), BF16)
    ys = []
    for g in range(SSD_GROUPS):
        bm = xbc[:, width + g * SSD_STATE:width + (g + 1) * SSD_STATE].astype(BF16)
        cm = xbc[:, width + (SSD_GROUPS + g) * SSD_STATE:width + (SSD_GROUPS + g + 1) * SSD_STATE].astype(BF16)
        cb = lax.dot_general(cm, bm, (((1,), (1,)), ((), ())), preferred_element_type=F32)
        y_state = jnp.dot(cm, st_ref[d, g].astype(BF16), preferred_element_type=F32)
        pairs = []
        for hp in range(gw // LANES):
            xp = xdt_b[:, g * gw + hp * LANES:g * gw + (hp + 1) * LANES]
            acc = None
            for half in range(2):
                h = (g * gw + hp * LANES) // SSD_HEAD_DIM + half
                seg = acum[:, h:h + 1] - acum_t[h:h + 1, :]
                m = jnp.where(mask, cb * jnp.exp(seg), 0.0).astype(BF16)
                xh = jnp.where(lo, xp, zero_b) if half == 0 else jnp.where(lo, zero_b, xp)
                part = jnp.dot(m, xh, preferred_element_type=F32)
                acc = part if acc is None else acc + part
            pairs.append(acc)
        y_g = jnp.concatenate(pairs, axis=1) + y_state * ea_x[:, g * gw:(g + 1) * gw]
        ys.append(y_g)
        upd = lax.dot_general(bm, xw[:, g * gw:(g + 1) * gw], (((0,), (0,)), ((), ())),
                              preferred_element_type=F32)
        st_ref[d, g] = st_ref[d, g] * elast_x[:, g * gw:(g + 1) * gw] + upd
    y = jnp.concatenate(ys, axis=1)
    if not reverse:
        y = y + dsk_ref[...] * xs
    yo_ref[rows, :width] = y.astype(yo_ref.dtype)

    bcum = _tri_cumsum(tri, logf, 2)
    blast = bcum[edge:edge + 1, :]
    qe = (qs * jnp.exp(bcum)).astype(BF16)
    k_end = kk * jnp.exp(blast - bcum)
    k_til = (k_end * jnp.exp(-blast)).astype(BF16)
    k_end = k_end.astype(BF16)
    e_blast = jnp.exp(blast)
    outs = []
    for h in range(HG_HEADS):
        sl = slice(h * HG_HEAD_DIM, (h + 1) * HG_HEAD_DIM)
        att = lax.dot_general(qe[:, sl], k_til[:, sl], (((1,), (1,)), ((), ())), preferred_element_type=F32)
        att = jnp.where(mask, att, 0.0).astype(BF16)
        gt = gt_ref[d, h]
        o_h = jnp.dot(att, vb[:, sl], preferred_element_type=F32)
        o_h = o_h + lax.dot_general(qe[:, sl], gt.astype(BF16), (((1,), (1,)), ((), ())),
                                    preferred_element_type=F32)
        outs.append(o_h)
        upd = lax.dot_general(vb[:, sl], k_end[:, sl], (((0,), (0,)), ((), ())), preferred_element_type=F32)
        gt_ref[d, h] = gt * e_blast[:, sl] + upd
    yo_ref[rows, width:] = jnp.concatenate(outs, axis=1).astype(yo_ref.dtype)


def _even_scan_kernel(*refs, nblk, cpb):
    fwd_in, bwd_in = refs[0:5], refs[5:10]
    cw_ref, cb_ref, dtc_ref, dsk_ref, lb_ref, e_ref, h0s_ref, h0g_ref = refs[10:18]
    yof_ref, yob_ref, st_ref, gt_ref = refs[18:22]
    c = pl.program_id(0)

    @pl.when(c == 0)
    def _():
        st_ref[...] = h0s_ref[...]
        gt_ref[...] = h0g_ref[...]

    streams = ((False, fwd_in, c, yof_ref), (True, bwd_in, nblk - 1 - c, yob_ref))
    preps = [_scan_block_prep(*ins, cw_ref, cb_ref, dtc_ref, lb_ref, blk, nblk, d)
             for d, (_, ins, blk, _) in enumerate(streams)]
    for step in range(cpb):
        for d, (reverse, _, _, yo_ref) in enumerate(streams):
            ci = (cpb - 1 - step) if reverse else step
            rows = slice(ci * SCAN_CHUNK, (ci + 1) * SCAN_CHUNK)
            _scan_chunk(preps[d], rows, dsk_ref, e_ref, st_ref, gt_ref, yo_ref, d, reverse)


def _even_scan(zs, consts, h0s, h0g):
    t = zs.shape[0]
    cpb = 4
    rows = cpb * SCAN_CHUNK
    nblk = t // rows
    cw, cb, dtc, dsk, lb, e16 = consts
    n_x = SSD_HEADS * SSD_HEAD_DIM
    n_xbc = n_x + 2 * SSD_GROUPS * SSD_STATE
    n_hg = HG_HEADS * HG_HEAD_DIM
    hb = rows // SUBLANES

    def stream_specs(bidx, d):
        col = lambda width, start: (lambda c: (bidx(c), start // width))
        return [
            pl.BlockSpec((rows, ZS_XD), col(ZS_XD, ZS_XBC)),
            pl.BlockSpec((SUBLANES, n_xbc), lambda c: (jnp.maximum(bidx(c) * hb - 1, 0), 0)),
            pl.BlockSpec((SUBLANES, n_xbc), lambda c: (jnp.minimum((bidx(c) + 1) * hb, t // SUBLANES - 1), 0)),
            pl.BlockSpec((rows, 2 * n_hg), col(2 * n_hg, ZS_Q)),
            pl.BlockSpec((rows, n_hg), col(n_hg, ZS_F + d * n_hg)),
        ]

    fwd = lambda c: c
    bwd = lambda c: nblk - 1 - c
    whole = lambda a: pl.BlockSpec(a.shape, lambda c: (0,) * a.ndim)
    in_specs = stream_specs(fwd, 0) + stream_specs(bwd, 1) + [whole(a) for a in (cw, cb, dtc, dsk, lb, e16, h0s, h0g)]
    out_specs = [pl.BlockSpec((rows, n_x + n_hg), lambda c: (c, 0)),
                 pl.BlockSpec((rows, n_x + n_hg), lambda c: (bwd(c), 0)), whole(h0s), whole(h0g)]
    out_shape = [jax.ShapeDtypeStruct((t, n_x + n_hg), BF16), jax.ShapeDtypeStruct((t, n_x + n_hg), BF16),
                 jax.ShapeDtypeStruct(h0s.shape, F32), jax.ShapeDtypeStruct(h0g.shape, F32)]
    return pl.pallas_call(
        functools.partial(_even_scan_kernel, nblk=nblk, cpb=cpb),
        grid=(nblk,),
        in_specs=in_specs,
        out_specs=out_specs,
        out_shape=out_shape,
        compiler_params=_cparams(("arbitrary",), 48),
        name="even_scan",
    )(*([zs] * 10), cw, cb, dtc, dsk, lb, e16, h0s, h0g)


def _even_out_kernel(h_ref, z_ref, g_ref, yof_ref, yob_ref, w_ref, vec_ref, o_ref):
    n_x = z_ref.shape[1]
    yo = yof_ref[...].astype(F32) + yob_ref[...].astype(F32)
    y = yo[:, :n_x] * _silu(z_ref[...])
    o = yo[:, n_x:]
    gw = n_x // SSD_GROUPS
    parts = []
    for g in range(SSD_GROUPS):
        yg = y[:, g * gw:(g + 1) * gw]
        ms = jnp.mean(yg * yg, axis=-1, keepdims=True)
        parts.append(yg * lax.rsqrt(ms + EPS))
    yn = (jnp.concatenate(parts, axis=1) * vec_ref[0:1, :]).astype(BF16)
    parts = []
    for hh in range(HG_HEADS):
        oh = o[:, hh * HG_HEAD_DIM:(hh + 1) * HG_HEAD_DIM]
        ms = jnp.mean(oh * oh, axis=-1, keepdims=True)
        parts.append(oh * lax.rsqrt(ms + EPS))
    on = (jnp.concatenate(parts, axis=1) * vec_ref[1:2, :] * _silu(g_ref[...])).astype(BF16)
    half = yn.shape[1]
    mix = jnp.dot(yn, w_ref[:half, :], preferred_element_type=F32)
    mix = mix + jnp.dot(on, w_ref[half:, :], preferred_element_type=F32)
    o_ref[...] = h_ref[...] + vec_ref[2:3, :] * mix


def _even_out(h, zs, yo_f, yo_b, w_out, vec):
    t, d = h.shape
    tm = min(t, 512)
    n_x = SSD_HEADS * SSD_HEAD_DIM
    n_hg = HG_HEADS * HG_HEAD_DIM
    row = lambda i: (i, 0)
    return pl.pallas_call(
        _even_out_kernel,
        grid=(t // tm,),
        in_specs=[pl.BlockSpec((tm, d), row),
                  pl.BlockSpec((tm, n_x), lambda i: (i, ZS_Z // n_x)),
                  pl.BlockSpec((tm, n_hg), lambda i: (i, ZS_G // n_hg)),
                  pl.BlockSpec((tm, n_x + n_hg), row), pl.BlockSpec((tm, n_x + n_hg), row),
                  pl.BlockSpec(w_out.shape, lambda i: (0, 0)),
                  pl.BlockSpec(vec.shape, lambda i: (0, 0))],
        out_specs=pl.BlockSpec((tm, d), row),
        out_shape=jax.ShapeDtypeStruct((t, d), F32),
        compiler_params=_cparams(("arbitrary",), 56),
        name="even_out",
    )(h, zs, zs, yo_f, yo_b, w_out, vec)


def _ffn_kernel(h_ref, mod_ref, wg_ref, wu_ref, wd_ref, o_ref, a_ref, acc_ref, *, final):
    j = pl.program_id(1)

    @pl.when(j == 0)
    def _():
        a_ref[...] = _modulate(h_ref[...], mod_ref).astype(BF16)
        acc_ref[...] = jnp.zeros_like(acc_ref)

    a = a_ref[...]
    gate = jnp.dot(a, wg_ref[...], preferred_element_type=F32)
    up = jnp.dot(a, wu_ref[...], preferred_element_type=F32)
    act = (_silu(gate) * up).astype(BF16)
    acc_ref[...] += jnp.dot(act, wd_ref[...], preferred_element_type=F32)

    @pl.when(j == pl.num_programs(1) - 1)
    def _():
        out = h_ref[...] + mod_ref[3:4, :] * acc_ref[...]
        if final:
            ms = jnp.mean(out * out, axis=-1, keepdims=True)
            out = out * lax.rsqrt(ms + EPS) * mod_ref[4:5, :]
        o_ref[...] = out


def _ffn(h, mod, wg, wu, wd, final):
    t, d = h.shape
    f = wg.shape[1]
    tm = min(t, 512)
    tf = f // 2
    return pl.pallas_call(
        functools.partial(_ffn_kernel, final=final),
        grid=(t // tm, f // tf),
        in_specs=[pl.BlockSpec((tm, d), lambda i, j: (i, 0)),
                  pl.BlockSpec((SUBLANES, d), lambda i, j: (0, 0)),
                  pl.BlockSpec((d, tf), lambda i, j: (0, j)),
                  pl.BlockSpec((d, tf), lambda i, j: (0, j)),
                  pl.BlockSpec((tf, d), lambda i, j: (j, 0))],
        out_specs=pl.BlockSpec((tm, d), lambda i, j: (i, 0)),
        out_shape=jax.ShapeDtypeStruct((t, d), F32),
        scratch_shapes=[pltpu.VMEM((tm, d), BF16), pltpu.VMEM((tm, d), F32)],
        compiler_params=_cparams(("arbitrary", "arbitrary"), 56),
        name="ffn_final" if final else "ffn",
    )(h, mod, wg, wu, wd)


def _odd_in_kernel(h_ref, mod_ref, w_ref, o_ref, u_ref):
    a = _modulate(h_ref[...], mod_ref).astype(BF16)
    u = jnp.dot(a, w_ref[...], preferred_element_type=F32)
    nj, grid_w, rb, _ = o_ref.shape
    for j in range(nj):
        u_ref[j] = u[:, j * LANES:(j + 1) * LANES]
    for wi in range(grid_w):
        for j in range(nj):
            o_ref[j, wi] = u_ref[j, pl.ds(wi, rb, stride=grid_w), :]


def _odd_in(h, mod, w, grid_w):
    t, d = h.shape
    rows = t // grid_w
    rb = min(rows, 4 * SUBLANES)
    n = w.shape[1]
    nj = n // LANES
    out = pl.pallas_call(
        _odd_in_kernel,
        grid=(rows // rb,),
        in_specs=[pl.BlockSpec((rb * grid_w, d), lambda i: (i, 0)),
                  pl.BlockSpec((SUBLANES, d), lambda i: (0, 0)),
                  pl.BlockSpec(w.shape, lambda i: (0, 0))],
        out_specs=pl.BlockSpec((nj, grid_w, rb, LANES), lambda i: (0, 0, i, 0)),
        out_shape=jax.ShapeDtypeStruct((nj, grid_w, rows, LANES), F32),
        scratch_shapes=[pltpu.VMEM((nj, rb * grid_w, LANES), F32)],
        compiler_params=_cparams(("arbitrary",), 48),
        name="odd_in",
    )(h, mod, w)
    return out.reshape(nj, t, LANES)


def _sub_chunk_rows(u_ref, nb):
    return [u_ref[pl.ds(s, nb, stride=S5_SUB), :] for s in range(S5_SUB)]


def _s5_inject_kernel(u_ref, win_ref, s_ref):
    x = jnp.concatenate(_sub_chunk_rows(u_ref, s_ref.shape[0]), axis=1).astype(BF16)
    s_ref[...] = jnp.dot(x, win_ref[...], preferred_element_type=F32)


def _s5_inject(u, win):
    nj, t, _ = u.shape
    n = t // S5_SUB
    nb = min(n, 512)
    width, ns = win.shape[1:]
    return pl.pallas_call(
        _s5_inject_kernel,
        grid=(nj, n // nb),
        in_specs=[pl.BlockSpec((None, nb * S5_SUB, LANES), lambda j, b: (j, b, 0)),
                  pl.BlockSpec((None, width, ns), lambda j, b: (j, 0, 0))],
        out_specs=pl.BlockSpec((None, nb, ns), lambda j, b: (j, b, 0)),
        out_shape=jax.ShapeDtypeStruct((nj, n, ns), F32),
        compiler_params=_cparams(("arbitrary", "arbitrary"), 40),
        name="s5_inject",
    )(u, win)


def _s5_carry_kernel(sf_ref, sb_ref, lam_ref, h0_ref, pf_ref, pb_ref, hf_ref, tf_ref, tb_ref, c_ref):
    b = pl.program_id(0)
    nj, kb, ns = sf_ref.shape
    half = ns // 2

    @pl.when(b == 0)
    def _():
        c_ref[...] = h0_ref[...]

    for j in range(nj):
        tf_ref[:, j, :] = sf_ref[j]
        tb_ref[:, j, :] = sb_ref[j]

    def advance(d, c, s):
        lr, li = lam_ref[d, :, :half], lam_ref[d, :, half:]
        pr, pi = c[:, :half], c[:, half:]
        return jnp.concatenate([lr * pr - li * pi + s[:, :half], lr * pi + li * pr + s[:, half:]], axis=1)

    def step(i, carry):
        cf, cb = carry
        kr = kb - 1 - i
        sf = tf_ref[i]
        sb = tb_ref[kr]
        tf_ref[i] = cf
        tb_ref[kr] = cb
        return advance(0, cf, sf), advance(1, cb, sb)

    cf, cb = lax.fori_loop(0, kb, step, (c_ref[0], c_ref[1]), unroll=4)
    c_ref[0] = cf
    c_ref[1] = cb
    for j in range(nj):
        pf_ref[j] = tf_ref[:, j, :]
        pb_ref[j] = tb_ref[:, j, :]

    @pl.when(b == pl.num_programs(0) - 1)
    def _():
        hf_ref[...] = c_ref[...]


def _s5_carry(s_f, s_b, lam, h0):
    nj, n, ns = s_f.shape
    kb = min(n, 256)
    nblk = n // kb
    fwd = pl.BlockSpec((nj, kb, ns), lambda b: (0, b, 0))
    bwd = pl.BlockSpec((nj, kb, ns), lambda b: (0, nblk - 1 - b, 0))
    whole = pl.BlockSpec(lam.shape, lambda b: (0, 0, 0))
    return pl.pallas_call(
        _s5_carry_kernel,
        grid=(nblk,),
        in_specs=[fwd, bwd, whole, whole],
        out_specs=[fwd, bwd, whole],
        out_shape=[jax.ShapeDtypeStruct(s_f.shape, F32), jax.ShapeDtypeStruct(s_b.shape, F32),
                   jax.ShapeDtypeStruct(lam.shape, F32)],
        scratch_shapes=[pltpu.VMEM((kb, nj, ns), F32), pltpu.VMEM((kb, nj, ns), F32), pltpu.VMEM(lam.shape, F32)],
        compiler_params=_cparams(("arbitrary",), 48),
        name="s5_carry",
    )(s_f, s_b, lam, h0)


def _s5_readout_kernel(u_ref, p_ref, wout_ref, tt_ref, add_ref, y_ref, *, reverse):
    nb = p_ref.shape[0]
    us = _sub_chunk_rows(u_ref, nb)
    xb = jnp.concatenate(us, axis=1).astype(BF16)
    pb = p_ref[...].astype(BF16)
    width = xb.shape[1]
    tile = 2 * LANES
    for m in range(width // tile):
        cols = slice(m * tile, (m + 1) * tile)
        rows = slice(m * tile, width) if reverse else slice(0, (m + 1) * tile)
        y = jnp.dot(pb, wout_ref[:, cols], preferred_element_type=F32)
        y = y + jnp.dot(xb[:, rows], tt_ref[rows, cols], preferred_element_type=F32)
        for q in range(tile // LANES):
            s = m * (tile // LANES) + q
            other = add_ref[pl.ds(s, nb, stride=S5_SUB), :] if reverse else add_ref[...] * us[s]
            y_ref[pl.ds(s, nb, stride=S5_SUB), :] = y[:, q * LANES:(q + 1) * LANES] + other


def _s5_readout(u, p, wout, tt, add, reverse):
    nj, t, _ = u.shape
    n = t // S5_SUB
    nb = min(n, 256)
    ns, width = wout.shape[1:]
    tokens = pl.BlockSpec((None, nb * S5_SUB, LANES), lambda j, b: (j, b, 0))
    add_spec = tokens if reverse else pl.BlockSpec((None, 1, LANES), lambda j, b: (j, 0, 0))
    return pl.pallas_call(
        functools.partial(_s5_readout_kernel, reverse=reverse),
        grid=(nj, n // nb),
        in_specs=[tokens,
                  pl.BlockSpec((None, nb, ns), lambda j, b: (j, b, 0)),
                  pl.BlockSpec((None, ns, width), lambda j, b: (j, 0, 0)),
                  pl.BlockSpec((None, width, width), lambda j, b: (j, 0, 0)),
                  add_spec],
        out_specs=tokens,
        out_shape=jax.ShapeDtypeStruct(u.shape, F32),
        compiler_params=_cparams(("arbitrary", "arbitrary"), 48),
        name="s5_readout_bwd" if reverse else "s5_readout_fwd",
    )(u, p, wout, tt, add)


def _odd_out_kernel(h_ref, y_ref, wv_ref, wg_ref, vec_ref, o_ref, a_ref):
    nj, grid_w, rb, _ = y_ref.shape
    for wi in range(grid_w):
        for j in range(nj):
            a_ref[j, pl.ds(wi, rb, stride=grid_w), :] = y_ref[j, wi]
    a = jax.nn.gelu(jnp.concatenate([a_ref[j] for j in range(nj)], axis=1)).astype(BF16)
    val = jnp.dot(a, wv_ref[...], preferred_element_type=F32)
    gate = jnp.dot(a, wg_ref[...], preferred_element_type=F32)
    o_ref[...] = h_ref[...] + vec_ref[0:1, :] * (val * jax.nn.sigmoid(gate))


def _odd_out(h, y, w_val, w_gate, vec, grid_w):
    t, d = h.shape
    rows = t // grid_w
    rb = min(rows, 2 * SUBLANES)
    nj = y.shape[0]
    tok = pl.BlockSpec((rb * grid_w, d), lambda i: (i, 0))
    return pl.pallas_call(
        _odd_out_kernel,
        grid=(rows // rb,),
        in_specs=[tok, pl.BlockSpec((nj, grid_w, rb, LANES), lambda i: (0, 0, i, 0)),
                  pl.BlockSpec(w_val.shape, lambda i: (0, 0)),
                  pl.BlockSpec(w_gate.shape, lambda i: (0, 0)),
                  pl.BlockSpec(vec.shape, lambda i: (0, 0))],
        out_specs=tok,
        out_shape=jax.ShapeDtypeStruct((t, d), F32),
        scratch_shapes=[pltpu.VMEM((nj, rb * grid_w, LANES), F32)],
        compiler_params=_cparams(("arbitrary",), 48),
        name="odd_out",
    )(h, y.reshape(nj, grid_w, rows, LANES), w_val, w_gate, vec)


def _s5_prep_kernel(prm_ref, bre_ref, bim_ref, colp_ref, cre_ref, cim_ref, win_ref, wout_ref, tt_ref, lam_ref, *,
                    reverse):
    T = S5_SUB
    ns = prm_ref.shape[1]
    hi = lax.Precision.HIGHEST
    a_re, a_im, dt = prm_ref[0:1, :], prm_ref[1:2, :], jnp.exp(prm_ref[2:3, :])
    tau = lax.broadcasted_iota(jnp.int32, (3 * SUBLANES, ns), 0).astype(F32)
    mag = jnp.exp(a_re * dt * tau)
    pr = mag * jnp.cos(a_im * dt * tau)
    pi = mag * jnp.sin(a_im * dt * tau)
    lr, li = pr[1:2], pi[1:2]
    den = a_re * a_re + a_im * a_im
    cr = ((lr - 1.0) * a_re + li * a_im) / den
    ci = (li * a_re - (lr - 1.0) * a_im) / den
    row_g = lax.broadcasted_iota(jnp.int32, (LANES, ns), 0) // S5_GROUP
    col_g = lax.broadcasted_iota(jnp.int32, (LANES, ns), 1) // S5_STATE
    same = row_g == col_g
    bbr = jnp.where(same, cr * bre_ref[...] - ci * bim_ref[...], 0.0)
    bbi = jnp.where(same, cr * bim_ref[...] + ci * bre_ref[...], 0.0)
    for s in range(T):
        e = s if reverse else T - 1 - s
        rows = slice(s * LANES, (s + 1) * LANES)
        win_ref[rows, :ns] = (pr[e:e + 1] * bbr - pi[e:e + 1] * bbi).astype(BF16)
        win_ref[rows, ns:] = (pr[e:e + 1] * bbi + pi[e:e + 1] * bbr).astype(BF16)
    lam_ref[...] = jnp.broadcast_to(jnp.concatenate([pr[T:T + 1], pi[T:T + 1]], axis=1), lam_ref.shape)
    dt_c = jnp.exp(colp_ref[2])
    mag_c = jnp.exp(colp_ref[0] * dt_c)
    lr_c = mag_c * jnp.cos(colp_ref[1] * dt_c)
    li_c = mag_c * jnp.sin(colp_ref[1] * dt_c)
    row_gc = lax.broadcasted_iota(jnp.int32, (ns, LANES), 0) // S5_STATE
    col_gc = lax.broadcasted_iota(jnp.int32, (ns, LANES), 1) // S5_GROUP
    same_c = row_gc == col_gc
    c_re = jnp.where(same_c, cre_ref[...], 0.0)
    c_im = jnp.where(same_c, cim_ref[...], 0.0)
    bb = jnp.concatenate([bbr, bbi], axis=1)
    cur_r = jnp.ones((ns, LANES), F32)
    cur_i = jnp.zeros((ns, LANES), F32)
    kblk = []
    for e in range(T + 1):
        wo_r = c_re * cur_r - c_im * cur_i
        wo_i = -(c_re * cur_i + c_im * cur_r)
        if e >= 1:
            t = (T - e) if reverse else e - 1
            wout_ref[:ns, t * LANES:(t + 1) * LANES] = wo_r.astype(BF16)
            wout_ref[ns:, t * LANES:(t + 1) * LANES] = wo_i.astype(BF16)
        if e < T:
            k = jnp.dot(bb, jnp.concatenate([wo_r, wo_i], axis=0), precision=hi, preferred_element_type=F32)
            kblk.append(k.astype(BF16))
        cur_r, cur_i = cur_r * lr_c - cur_i * li_c, cur_r * li_c + cur_i * lr_c
    zeros = jnp.zeros((LANES, LANES), BF16)
    for s in range(T):
        for t in range(T):
            lag = (s - t) if reverse else (t - s)
            tt_ref[s * LANES:(s + 1) * LANES, t * LANES:(t + 1) * LANES] = kblk[lag] if lag >= 0 else zeros


def _s5_prep(a_re, a_im, log_dt, b_re, b_im, c_re, c_im, reverse):
    g, p = a_re.shape
    cdim = b_re.shape[2]
    nj = g // S5_GPB
    ns = S5_GPB * p
    width = S5_SUB * LANES
    prm = jnp.zeros((nj, SUBLANES, ns), F32)
    prm = prm.at[:, 0].set(a_re.reshape(nj, ns)).at[:, 1].set(a_im.reshape(nj, ns))
    prm = prm.at[:, 2].set(jnp.repeat(log_dt, p).reshape(nj, ns))
    colp = jnp.broadcast_to(prm[:, :3, :, None], (nj, 3, ns, LANES))
    row_tile = lambda b: jnp.tile(b.reshape(nj, S5_GPB, p, cdim).transpose(0, 3, 1, 2).reshape(nj, 1, cdim, ns),
                                  (1, S5_GPB, 1, 1)).reshape(nj, LANES, ns)
    col_tile = lambda c_: jnp.tile(c_.reshape(nj, S5_GPB, cdim, p).transpose(0, 1, 3, 2).reshape(nj, ns, 1, cdim),
                                   (1, 1, S5_GPB, 1)).reshape(nj, ns, LANES)
    blk = lambda *shape: pl.BlockSpec((None,) + shape, lambda j: (j,) + (0,) * len(shape))
    return pl.pallas_call(
        functools.partial(_s5_prep_kernel, reverse=reverse),
        grid=(nj,),
        in_specs=[blk(SUBLANES, ns), blk(LANES, ns), blk(LANES, ns), blk(3, ns, LANES), blk(ns, LANES),
                  blk(ns, LANES)],
        out_specs=[blk(width, 2 * ns), blk(2 * ns, width), blk(width, width), blk(SUBLANES, 2 * ns)],
        out_shape=[jax.ShapeDtypeStruct((nj, width, 2 * ns), BF16), jax.ShapeDtypeStruct((nj, 2 * ns, width), BF16),
                   jax.ShapeDtypeStruct((nj, width, width), BF16), jax.ShapeDtypeStruct((nj, SUBLANES, 2 * ns), F32)],
        compiler_params=_cparams(("arbitrary",), 56),
        name="s5_prep_bwd" if reverse else "s5_prep_fwd",
    )(prm, row_tile(b_re), row_tile(b_im), colp, col_tile(c_re), col_tile(c_im))


def _rows(*vs, width):
    out = jnp.zeros((SUBLANES, width), F32)
    for i, v in enumerate(vs):
        out = out.at[i].set(v.astype(F32))
    return out


def kernel(x, c, ctx, c_ctx, ada_w, ada_b, norm_mix_g, norm_ffn_g, ffn_w_gate, ffn_w_up, ffn_w_down, final_norm_g, ev_w_in, ev_w_out, ssd_conv_w, ssd_conv_b, ssd_dt_bias, ssd_a_log, ssd_d, ssd_norm_g, hg_lb_logits, hg_norm_g, od_w_in, s5_a_re, s5_a_im, s5_log_dt, s5_b_re, s5_b_im, s5_c_re, s5_c_im, s5_d, od_w_val, od_w_gate):
    d = x.shape[-1]
    lat = x[0].astype(F32)
    hc = ctx[0].astype(F32)

    m = _ada(_rows(c[0], c_ctx, width=d), ada_w, ada_b)

    def ada_vecs(layer, stream):
        return [m[layer, stream, i * d:(i + 1) * d] for i in range(6)]

    w = ev_w_in[0]
    n_x = SSD_HEADS * SSD_HEAD_DIM
    n_xbc = n_x + 2 * SSD_GROUPS * SSD_STATE
    n_hg = HG_HEADS * HG_HEAD_DIM
    o_z, o_xbc, o_dt = 0, n_x, n_x + n_xbc
    o_q = o_dt + 2 * SSD_HEADS
    o_f = o_q + n_hg
    o_v = o_f + 2 * n_hg
    o_g = o_v + n_hg
    pad = lambda n: jnp.zeros((d, n), F32)
    w_in = jnp.concatenate([
        w[:, o_xbc:o_xbc + n_xbc],
        w[:, o_dt:o_dt + SSD_HEADS], pad(LANES - SSD_HEADS),
        w[:, o_dt + SSD_HEADS:o_dt + 2 * SSD_HEADS], pad(LANES - SSD_HEADS),
        pad(ZS_Q - ZS_XD),
        w[:, o_q:o_q + n_hg], w[:, o_v:o_v + n_hg], w[:, o_f:o_f + 2 * n_hg],
        w[:, o_z:o_z + n_x], w[:, o_g:o_g + n_hg]], axis=1).astype(BF16)
    cw = jnp.zeros((SUBLANES, n_xbc), F32).at[:SSD_CONV].set(ssd_conv_w[0])
    cb = ssd_conv_b[0].reshape(1, n_xbc)
    dtc = jnp.zeros((2, SUBLANES, LANES), F32)
    dtc = dtc.at[:, 0, :SSD_HEADS].set(ssd_dt_bias[0]).at[:, 1, :SSD_HEADS].set(ssd_a_log[0])
    dsk = jnp.repeat(ssd_d[0], SSD_HEAD_DIM).reshape(1, n_x)
    lower = jnp.cumsum(jax.nn.softmax(hg_lb_logits.astype(F32), axis=0), axis=0)[0].reshape(1, n_hg)
    head_of_lane = jnp.arange(n_x) // SSD_HEAD_DIM
    e16 = (jnp.arange(LANES)[:, None] == head_of_lane[None, :]).astype(BF16)
    consts = (cw, cb, dtc, dsk, lower, e16)
    w_out = ev_w_out[0].astype(BF16)
    out_vec = lambda gate: _rows(ssd_norm_g[0], jnp.tile(hg_norm_g[0], HG_HEADS), gate, width=d)
    wg0, wu0, wd0 = (t_[0].astype(BF16) for t_ in (ffn_w_gate, ffn_w_up, ffn_w_down))

    s_state = jnp.zeros((2, SSD_GROUPS, SSD_STATE, n_x // SSD_GROUPS), F32)
    g_state = jnp.zeros((2, HG_HEADS, HG_HEAD_DIM, HG_HEAD_DIM), F32)
    layer0 = {}
    for stream, h in ((1, hc), (0, lat)):
        sm, scm, gm, sf, scf, gf = ada_vecs(0, stream)
        zs = _mod_matmul(h, _rows(norm_mix_g[0], scm, sm, width=d), w_in)
        yo_f, yo_b, s_state, g_state = _even_scan(zs, consts, s_state, g_state)
        h1 = _even_out(h, zs, yo_f, yo_b, w_out, out_vec(gm))
        layer0[stream] = _ffn(h1, _rows(norm_ffn_g[0], scf, sf, gf, width=d), wg0, wu0, wd0, final=False)

    lat, hc = layer0[0], layer0[1]
    w_s5 = od_w_in[0].astype(BF16)
    u = {}
    for stream, h, gw in ((1, hc, 1), (0, lat, GRID_W)):
        sm, scm = ada_vecs(1, stream)[:2]
        u[stream] = _odd_in(h, _rows(norm_mix_g[1], scm, sm, width=d), w_s5, gw)
    mats = [_s5_prep(s5_a_re[0, di], s5_a_im[0, di], s5_log_dt[0, di], s5_b_re[0, di], s5_b_im[0, di],
                     s5_c_re[0, di], s5_c_im[0, di], reverse=bool(di)) for di in range(2)]
    lam = jnp.stack([mats[0][3][:, 0], mats[1][3][:, 0]])
    carry = jnp.zeros_like(lam)
    for stream in (1, 0):
        s_f, s_b = (_s5_inject(u[stream], mats[di][0]) for di in range(2))
        p_f, p_b, carry = _s5_carry(s_f, s_b, lam, carry)
    nj = u[0].shape[0]
    y = _s5_readout(u[0], p_f, mats[0][1], mats[0][2], s5_d[0].reshape(nj, 1, LANES), reverse=False)
    y = _s5_readout(u[0], p_b, mats[1][1], mats[1][2], y, reverse=True)
    _, _, gm, sf, scf, gf = ada_vecs(1, 0)
    h3 = _odd_out(lat, y, od_w_val[0].astype(BF16), od_w_gate[0].astype(BF16), _rows(gm, width=d), GRID_W)
    wg1, wu1, wd1 = (t_[1].astype(BF16) for t_ in (ffn_w_gate, ffn_w_up, ffn_w_down))
    out = _ffn(h3, _rows(norm_ffn_g[1], scf, sf, gf, final_norm_g, width=d), wg1, wu1, wd1, final=True)
    return out[None].astype(x.dtype)
```

```python
import functools
import math

import jax
import jax.numpy as jnp
from jax import lax
from jax.experimental import pallas as pl
from jax.experimental.pallas import tpu as pltpu

F32 = jnp.float32
BF16 = jnp.bfloat16
EPS = 1e-6

LANES = 128
SUBLANES = 8
GRID_W = 64
SCAN_CHUNK = 64
SSD_HEADS = 16
SSD_HEAD_DIM = 64
SSD_STATE = 128
SSD_GROUPS = 2
SSD_CONV = 5
HG_HEADS = 8
HG_HEAD_DIM = 128
S5_GROUP = 16
S5_STATE = 64
S5_SUB = 16
S5_GPB = LANES // S5_GROUP

ZS_XBC = 0
ZS_DT = 1536
ZS_XD = 1792
ZS_Q = 2048
ZS_V = 3072
ZS_F = 4096
ZS_Z = 6144
ZS_G = 7168
ZS_WIDTH = 8192


def _cparams(semantics, vmem_mb):
    return pltpu.CompilerParams(dimension_semantics=semantics, vmem_limit_bytes=vmem_mb * 1024 * 1024)


def _silu(x):
    return x * jax.nn.sigmoid(x)


def _modulate(h, mod_ref):
    ms = jnp.mean(h * h, axis=-1, keepdims=True)
    return h * lax.rsqrt(ms + EPS) * (mod_ref[0:1, :] * (1.0 + mod_ref[1:2, :])) + mod_ref[2:3, :]


def _split_dot(x, w, passes):
    acc, rem = None, x
    for _ in range(passes):
        hi = rem.astype(BF16)
        d = jnp.dot(hi, w, preferred_element_type=F32)
        acc = d if acc is None else acc + d
        rem = rem - hi.astype(F32)
    return acc


def _tri_cumsum(tri, x, passes):
    acc, rem = None, x
    for _ in range(passes):
        hi = rem.astype(BF16)
        d = jnp.dot(tri, hi, preferred_element_type=F32)
        acc = d if acc is None else acc + d
        rem = rem - hi.astype(F32)
    return acc


def _ada_kernel(c_ref, w_ref, b_ref, o_ref):
    o_ref[...] = jnp.dot(_silu(c_ref[...]), w_ref[...], precision=lax.Precision.HIGHEST,
                         preferred_element_type=F32) + b_ref[...]


def _ada(cvecs, ada_w, ada_b):
    depth, d, n = ada_w.shape
    tn = n // 4
    return pl.pallas_call(
        _ada_kernel,
        grid=(depth, n // tn),
        in_specs=[pl.BlockSpec((SUBLANES, d), lambda l, j: (0, 0)),
                  pl.BlockSpec((None, d, tn), lambda l, j: (l, 0, j)),
                  pl.BlockSpec((None, 1, tn), lambda l, j: (l, 0, j))],
        out_specs=pl.BlockSpec((None, SUBLANES, tn), lambda l, j: (l, 0, j)),
        out_shape=jax.ShapeDtypeStruct((depth, SUBLANES, n), F32),
        compiler_params=_cparams(("arbitrary", "arbitrary"), 40),
        name="ada",
    )(cvecs, ada_w, ada_b.reshape(depth, 1, n))


EVEN_TN = 1024
HALO = 2 * SUBLANES
EVEN_TILES = (("conv", 0, None), ("conv_dt", 1, None), ("silu", None, 0), ("copy", None, 1),
              ("logf", 2, None), ("logf", 3, None), ("silu", None, 2), ("silu", None, 3))


def _even_in_kernel(h_ref, hp_ref, hn_ref, mod_ref, w_ref, cw_ref, cb_ref, dtc_ref, lb_ref, za_ref, zb_ref, a_ref):
    i = pl.program_id(0)
    tm = h_ref.shape[0]
    n_lb = lb_ref.shape[1]
    a_ref[:HALO] = _modulate(hp_ref[...], mod_ref).astype(BF16)
    a_ref[HALO:HALO + tm] = _modulate(h_ref[...], mod_ref).astype(BF16)
    a_ref[HALO + tm:] = _modulate(hn_ref[...], mod_ref).astype(BF16)
    row = lax.broadcasted_iota(jnp.int32, (tm + 2 * HALO, 1), 0)
    outside = ((row < HALO) & (i == 0)) | ((row >= HALO + tm) & (i == pl.num_programs(0) - 1))

    def conv_silu(col0, lanes):
        acc = jnp.dot(a_ref[...], w_ref[:, col0:col0 + lanes], preferred_element_type=F32)
        acc = jnp.where(outside, 0.0, acc)
        conv = cb_ref[:, col0:col0 + lanes]
        for k in range(SSD_CONV):
            off = HALO - SSD_CONV // 2 + k
            conv = conv + cw_ref[k:k + 1, col0:col0 + lanes] * acc[off:off + tm]
        return _silu(conv)

    def main_dot(col0, lanes=EVEN_TN):
        return jnp.dot(a_ref[HALO:HALO + tm], w_ref[:, col0:col0 + lanes], preferred_element_type=F32)

    for tile, (kind, fa, fb) in enumerate(EVEN_TILES):
        col0 = tile * EVEN_TN
        if kind == "conv":
            za_ref[:, fa * EVEN_TN:(fa + 1) * EVEN_TN] = conv_silu(col0, EVEN_TN)
        elif kind == "conv_dt":
            n_c = ZS_DT - col0
            za_ref[:, col0:ZS_DT] = conv_silu(col0, n_c)
            dt = main_dot(ZS_DT, 2 * LANES)
            for d in range(2):
                za_ref[:, ZS_DT + d * LANES:ZS_DT + (d + 1) * LANES] = jax.nn.softplus(
                    dt[:, d * LANES:(d + 1) * LANES] + dtc_ref[d, 0:1, :])
            za_ref[:, ZS_XD:(fa + 1) * EVEN_TN] = jnp.zeros((tm, (fa + 1) * EVEN_TN - ZS_XD), F32)
        elif kind == "logf":
            lb = lb_ref[:, (col0 - ZS_F) % n_lb:(col0 - ZS_F) % n_lb + EVEN_TN]
            za_ref[:, fa * EVEN_TN:(fa + 1) * EVEN_TN] = jnp.log(lb + (1.0 - lb) * jax.nn.sigmoid(main_dot(col0)))
        elif kind == "silu":
            zb_ref[:, fb * EVEN_TN:(fb + 1) * EVEN_TN] = _silu(main_dot(col0)).astype(BF16)
        else:
            zb_ref[:, fb * EVEN_TN:(fb + 1) * EVEN_TN] = main_dot(col0).astype(BF16)


def _even_in(h, mod, w, cw, cb, dtc, lb):
    t, d = h.shape
    tm = min(t, 512)
    hb = tm // HALO
    n_a = (1 + max(fa for _, fa, _ in EVEN_TILES if fa is not None)) * EVEN_TN
    n_b = (1 + max(fb for _, _, fb in EVEN_TILES if fb is not None)) * EVEN_TN
    whole = lambda a, **kw: pl.BlockSpec(a.shape, lambda i: (0,) * a.ndim, **kw)
    return pl.pallas_call(
        _even_in_kernel,
        grid=(t // tm,),
        in_specs=[pl.BlockSpec((tm, d), lambda i: (i, 0)),
                  pl.BlockSpec((HALO, d), lambda i: (jnp.maximum(i * hb - 1, 0), 0)),
                  pl.BlockSpec((HALO, d), lambda i: (jnp.minimum((i + 1) * hb, t // HALO - 1), 0)),
                  whole(mod), whole(w, pipeline_mode=pl.Buffered(1)),
                  whole(cw), whole(cb), whole(dtc), whole(lb)],
        out_specs=[pl.BlockSpec((tm, n_a), lambda i: (i, 0)), pl.BlockSpec((tm, n_b), lambda i: (i, 0))],
        out_shape=[jax.ShapeDtypeStruct((t, n_a), F32), jax.ShapeDtypeStruct((t, n_b), BF16)],
        scratch_shapes=[pltpu.VMEM((tm + 2 * HALO, d), BF16)],
        compiler_params=_cparams(("arbitrary",), 56),
        name="even_in",
    )(h, h, h, mod, w, cw, cb, dtc, lb)


def _scan_block_prep(xd_ref, qv_ref, logf_ref, dtc_ref, d):
    n_hg = logf_ref.shape[1]
    xbc = xd_ref[:, :ZS_DT]
    dtv = xd_ref[:, ZS_DT + d * LANES:ZS_DT + (d + 1) * LANES]
    la = -dtv * jnp.exp(dtc_ref[d, 1:2, :])
    logf = logf_ref[...]
    return xbc, dtv, la, 1.0 - jnp.exp(logf), logf, qv_ref[:, :n_hg].astype(F32), qv_ref[:, n_hg:]


def _scan_chunk(prep, rows, dsk_ref, e_ref, st_ref, gt_ref, yo_ref, d, reverse):
    Q = SCAN_CHUNK
    xbc, dtv, la, kk, logf, qs, vb = (a[rows] for a in prep)
    ti = lax.broadcasted_iota(jnp.int32, (Q, Q), 0)
    si = lax.broadcasted_iota(jnp.int32, (Q, Q), 1)
    mask = (ti <= si) if reverse else (ti >= si)
    tri = jnp.where(mask, 1.0, 0.0).astype(BF16)
    edge = 0 if reverse else Q - 1
    width = SSD_HEADS * SSD_HEAD_DIM
    gw = width // SSD_GROUPS
    xs = xbc[:, :width]

    acum = _tri_cumsum(tri, la, 3)
    last = acum[edge:edge + 1, :]
    acum_t = acum.T
    e_a = jnp.exp(acum)
    w_end = jnp.exp(last - acum)
    e_last = jnp.broadcast_to(jnp.exp(last), (SUBLANES, LANES))
    xe = _split_dot(jnp.concatenate([dtv, e_a, w_end], axis=0), e_ref[...], 1)
    dt_x, ea_x, wend_x = xe[:Q], xe[Q:2 * Q], xe[2 * Q:3 * Q]
    elast_x = _split_dot(e_last, e_ref[...], 3)[0:1]
    xdt = xs * dt_x
    xw = (xdt * wend_x).astype(BF16)
    xdt_b = xdt.astype(BF16)
    lane = lax.broadcasted_iota(jnp.int32, (Q, LANES), 1)
    lo = lane < SSD_HEAD_DIM
    zero_b = jnp.zeros((Q, LANES), BF16)
    ys = []
    for g in range(SSD_GROUPS):
        bm = xbc[:, width + g * SSD_STATE:width + (g + 1) * SSD_STATE].astype(BF16)
        cm = xbc[:, width + (SSD_GROUPS + g) * SSD_STATE:width + (SSD_GROUPS + g + 1) * SSD_STATE].astype(BF16)
        cb = lax.dot_general(cm, bm, (((1,), (1,)), ((), ())), preferred_element_type=F32)
        y_state = jnp.dot(cm, st_ref[d, g].astype(BF16), preferred_element_type=F32)
        pairs = []
        for hp in range(gw // LANES):
            xp = xdt_b[:, g * gw + hp * LANES:g * gw + (hp + 1) * LANES]
            acc = None
            for half in range(2):
                h = (g * gw + hp * LANES) // SSD_HEAD_DIM + half
                seg = acum[:, h:h + 1] - acum_t[h:h + 1, :]
                m = jnp.where(mask, cb * jnp.exp(seg), 0.0).astype(BF16)
                xh = jnp.where(lo, xp, zero_b) if half == 0 else jnp.where(lo, zero_b, xp)
                part = jnp.dot(m, xh, preferred_element_type=F32)
                acc = part if acc is None else acc + part
            pairs.append(acc)
        y_g = jnp.concatenate(pairs, axis=1) + y_state * ea_x[:, g * gw:(g + 1) * gw]
        ys.append(y_g)
        upd = lax.dot_general(bm, xw[:, g * gw:(g + 1) * gw], (((0,), (0,)), ((), ())),
                              preferred_element_type=F32)
        st_ref[d, g] = st_ref[d, g] * elast_x[:, g * gw:(g + 1) * gw] + upd
    y = jnp.concatenate(ys, axis=1)
    if not reverse:
        y = y + dsk_ref[...] * xs
    yo_ref[rows, :width] = y.astype(yo_ref.dtype)

    bcum = _tri_cumsum(tri, logf, 2)
    blast = bcum[edge:edge + 1, :]
    qe = (qs * jnp.exp(bcum)).astype(BF16)
    k_end = kk * jnp.exp(blast - bcum)
    k_til = (k_end * jnp.exp(-blast)).astype(BF16)
    k_end = k_end.astype(BF16)
    e_blast = jnp.exp(blast)
    outs = []
    for h in range(HG_HEADS):
        sl = slice(h * HG_HEAD_DIM, (h + 1) * HG_HEAD_DIM)
        att = lax.dot_general(qe[:, sl], k_til[:, sl], (((1,), (1,)), ((), ())), preferred_element_type=F32)
        att = jnp.where(mask, att, 0.0).astype(BF16)
        gt = gt_ref[d, h]
        o_h = jnp.dot(att, vb[:, sl], preferred_element_type=F32)
        o_h = o_h + lax.dot_general(qe[:, sl], gt.astype(BF16), (((1,), (1,)), ((), ())),
                                    preferred_element_type=F32)
        outs.append(o_h)
        upd = lax.dot_general(vb[:, sl], k_end[:, sl], (((0,), (0,)), ((), ())), preferred_element_type=F32)
        gt_ref[d, h] = gt * e_blast[:, sl] + upd
    yo_ref[rows, width:] = jnp.concatenate(outs, axis=1).astype(yo_ref.dtype)


def _even_scan_kernel(*refs, nblk, cpb):
    fwd_in, bwd_in = refs[0:3], refs[3:6]
    dtc_ref, dsk_ref, e_ref, h0s_ref, h0g_ref = refs[6:11]
    yof_ref, yob_ref, st_ref, gt_ref = refs[11:15]
    c = pl.program_id(0)

    @pl.when(c == 0)
    def _():
        st_ref[...] = h0s_ref[...]
        gt_ref[...] = h0g_ref[...]

    streams = ((False, fwd_in, yof_ref), (True, bwd_in, yob_ref))
    preps = [_scan_block_prep(*ins, dtc_ref, d) for d, (_, ins, _) in enumerate(streams)]
    for step in range(cpb):
        for d, (reverse, _, yo_ref) in enumerate(streams):
            ci = (cpb - 1 - step) if reverse else step
            rows = slice(ci * SCAN_CHUNK, (ci + 1) * SCAN_CHUNK)
            _scan_chunk(preps[d], rows, dsk_ref, e_ref, st_ref, gt_ref, yo_ref, d, reverse)


def _even_scan(za, zb, consts, h0s, h0g):
    t = za.shape[0]
    nch = t // SCAN_CHUNK
    cpb = max(k for k in (4, 2, 1) if nch % k == 0)
    rows = cpb * SCAN_CHUNK
    nblk = nch // cpb
    dtc, dsk, e16 = consts
    n_x = SSD_HEADS * SSD_HEAD_DIM
    n_hg = HG_HEADS * HG_HEAD_DIM
    a_logf = 2 * EVEN_TN

    def stream_specs(bidx, d):
        return [
            pl.BlockSpec((rows, ZS_XD), lambda c: (bidx(c), 0)),
            pl.BlockSpec((rows, 2 * n_hg), lambda c: (bidx(c), 0)),
            pl.BlockSpec((rows, n_hg), lambda c: (bidx(c), a_logf // n_hg + d)),
        ]

    fwd = lambda c: c
    bwd = lambda c: nblk - 1 - c
    whole = lambda a: pl.BlockSpec(a.shape, lambda c: (0,) * a.ndim)
    in_specs = stream_specs(fwd, 0) + stream_specs(bwd, 1) + [whole(a) for a in (dtc, dsk, e16, h0s, h0g)]
    out_specs = [pl.BlockSpec((rows, n_x + n_hg), lambda c: (c, 0)),
                 pl.BlockSpec((rows, n_x + n_hg), lambda c: (bwd(c), 0)), whole(h0s), whole(h0g)]
    out_shape = [jax.ShapeDtypeStruct((t, n_x + n_hg), BF16), jax.ShapeDtypeStruct((t, n_x + n_hg), BF16),
                 jax.ShapeDtypeStruct(h0s.shape, F32), jax.ShapeDtypeStruct(h0g.shape, F32)]
    return pl.pallas_call(
        functools.partial(_even_scan_kernel, nblk=nblk, cpb=cpb),
        grid=(nblk,),
        in_specs=in_specs,
        out_specs=out_specs,
        out_shape=out_shape,
        compiler_params=_cparams(("arbitrary",), 48),
        name="even_scan",
    )(za, zb, za, za, zb, za, dtc, dsk, e16, h0s, h0g)


def _even_out_kernel(h_ref, z_ref, g_ref, yof_ref, yob_ref, w_ref, vec_ref, o_ref):
    n_x = z_ref.shape[1]
    yo = yof_ref[...].astype(F32) + yob_ref[...].astype(F32)
    y = yo[:, :n_x] * z_ref[...].astype(F32)
    o = yo[:, n_x:]
    gw = n_x // SSD_GROUPS
    parts = []
    for g in range(SSD_GROUPS):
        yg = y[:, g * gw:(g + 1) * gw]
        ms = jnp.mean(yg * yg, axis=-1, keepdims=True)
        parts.append(yg * lax.rsqrt(ms + EPS))
    yn = (jnp.concatenate(parts, axis=1) * vec_ref[0:1, :]).astype(BF16)
    parts = []
    for hh in range(HG_HEADS):
        oh = o[:, hh * HG_HEAD_DIM:(hh + 1) * HG_HEAD_DIM]
        ms = jnp.mean(oh * oh, axis=-1, keepdims=True)
        parts.append(oh * lax.rsqrt(ms + EPS))
    on = (jnp.concatenate(parts, axis=1) * vec_ref[1:2, :] * g_ref[...].astype(F32)).astype(BF16)
    half = yn.shape[1]
    mix = jnp.dot(yn, w_ref[:half, :], preferred_element_type=F32)
    mix = mix + jnp.dot(on, w_ref[half:, :], preferred_element_type=F32)
    o_ref[...] = h_ref[...] + vec_ref[2:3, :] * mix


def _even_out(h, zb, yo_f, yo_b, w_out, vec):
    t, d = h.shape
    tm = min(t, 512)
    n_x = SSD_HEADS * SSD_HEAD_DIM
    n_hg = HG_HEADS * HG_HEAD_DIM
    row = lambda i: (i, 0)
    z_tile, g_tile = (next(fb for k, (_, _, fb) in enumerate(EVEN_TILES) if k * EVEN_TN == start)
                      for start in (ZS_Z, ZS_G))
    return pl.pallas_call(
        _even_out_kernel,
        grid=(t // tm,),
        in_specs=[pl.BlockSpec((tm, d), row),
                  pl.BlockSpec((tm, n_x), lambda i: (i, z_tile)),
                  pl.BlockSpec((tm, n_hg), lambda i: (i, g_tile)),
                  pl.BlockSpec((tm, n_x + n_hg), row), pl.BlockSpec((tm, n_x + n_hg), row),
                  pl.BlockSpec(w_out.shape, lambda i: (0, 0)),
                  pl.BlockSpec(vec.shape, lambda i: (0, 0))],
        out_specs=pl.BlockSpec((tm, d), row),
        out_shape=jax.ShapeDtypeStruct((t, d), F32),
        compiler_params=_cparams(("arbitrary",), 56),
        name="even_out",
    )(h, zb, zb, yo_f, yo_b, w_out, vec)


def _ffn_kernel(h_ref, mod_ref, wg_ref, wu_ref, wd_ref, o_ref, *, final):
    h = h_ref[...]
    a = _modulate(h, mod_ref).astype(BF16)
    gate = jnp.dot(a, wg_ref[...], preferred_element_type=F32)
    up = jnp.dot(a, wu_ref[...], preferred_element_type=F32)
    act = (_silu(gate) * up).astype(BF16)
    out = h + mod_ref[3:4, :] * jnp.dot(act, wd_ref[...], preferred_element_type=F32)
    if final:
        ms = jnp.mean(out * out, axis=-1, keepdims=True)
        out = out * lax.rsqrt(ms + EPS) * mod_ref[4:5, :]
    o_ref[...] = out


def _ffn(h, mod, wg, wu, wd, final):
    t, d = h.shape
    tm = min(t, 512)
    resident = lambda a: pl.BlockSpec(a.shape, lambda i: (0,) * a.ndim, pipeline_mode=pl.Buffered(1))
    return pl.pallas_call(
        functools.partial(_ffn_kernel, final=final),
        grid=(t // tm,),
        in_specs=[pl.BlockSpec((tm, d), lambda i: (i, 0)),
                  pl.BlockSpec((SUBLANES, d), lambda i: (0, 0)),
                  resident(wg), resident(wu), resident(wd)],
        out_specs=pl.BlockSpec((tm, d), lambda i: (i, 0)),
        out_shape=jax.ShapeDtypeStruct((t, d), F32),
        compiler_params=_cparams(("arbitrary",), 56),
        name="ffn_final" if final else "ffn",
    )(h, mod, wg, wu, wd)


def _odd_in_kernel(h_ref, mod_ref, w_ref, o_ref, u_ref):
    a = _modulate(h_ref[...], mod_ref).astype(BF16)
    u = jnp.dot(a, w_ref[...], preferred_element_type=F32)
    nj, grid_w, rb, _ = o_ref.shape
    for j in range(nj):
        u_ref[j] = u[:, j * LANES:(j + 1) * LANES]
    for wi in range(grid_w):
        for j in range(nj):
            o_ref[j, wi] = u_ref[j, pl.ds(wi, rb, stride=grid_w), :]


def _odd_in(h, mod, w, grid_w):
    t, d = h.shape
    rows = t // grid_w
    rb = min(rows, 4 * SUBLANES)
    n = w.shape[1]
    nj = n // LANES
    out = pl.pallas_call(
        _odd_in_kernel,
        grid=(rows // rb,),
        in_specs=[pl.BlockSpec((rb * grid_w, d), lambda i: (i, 0)),
                  pl.BlockSpec((SUBLANES, d), lambda i: (0, 0)),
                  pl.BlockSpec(w.shape, lambda i: (0, 0))],
        out_specs=pl.BlockSpec((nj, grid_w, rb, LANES), lambda i: (0, 0, i, 0)),
        out_shape=jax.ShapeDtypeStruct((nj, grid_w, rows, LANES), F32),
        scratch_shapes=[pltpu.VMEM((nj, rb * grid_w, LANES), F32)],
        compiler_params=_cparams(("arbitrary",), 48),
        name="odd_in",
    )(h, mod, w)
    return out.reshape(nj, t, LANES)


def _sub_chunk_rows(u_ref, nb):
    return [u_ref[pl.ds(s, nb, stride=S5_SUB), :] for s in range(S5_SUB)]


def _s5_inject_kernel(u_ref, win_ref, s_ref):
    x = jnp.concatenate(_sub_chunk_rows(u_ref, s_ref.shape[0]), axis=1).astype(BF16)
    s_ref[...] = jnp.dot(x, win_ref[...], preferred_element_type=F32)


def _s5_inject(u, win):
    nj, t, _ = u.shape
    n = t // S5_SUB
    nb = min(n, 512)
    width, ns = win.shape[1:]
    return pl.pallas_call(
        _s5_inject_kernel,
        grid=(nj, n // nb),
        in_specs=[pl.BlockSpec((None, nb * S5_SUB, LANES), lambda j, b: (j, b, 0)),
                  pl.BlockSpec((None, width, ns), lambda j, b: (j, 0, 0))],
        out_specs=pl.BlockSpec((None, nb, ns), lambda j, b: (j, b, 0)),
        out_shape=jax.ShapeDtypeStruct((nj, n, ns), F32),
        compiler_params=_cparams(("arbitrary", "arbitrary"), 40),
        name="s5_inject",
    )(u, win)


def _s5_carry_kernel(sf_ref, sb_ref, lam_ref, h0_ref, pf_ref, pb_ref, hf_ref, tf_ref, tb_ref, c_ref):
    b = pl.program_id(0)
    nj, kb, ns = sf_ref.shape
    half = ns // 2

    @pl.when(b == 0)
    def _():
        c_ref[...] = h0_ref[...]

    for j in range(nj):
        tf_ref[:, j, :] = sf_ref[j]
        tb_ref[:, j, :] = sb_ref[j]

    def advance(d, c, s):
        lr, li = lam_ref[d, :, :half], lam_ref[d, :, half:]
        pr, pi = c[:, :half], c[:, half:]
        return jnp.concatenate([lr * pr - li * pi + s[:, :half], lr * pi + li * pr + s[:, half:]], axis=1)

    def step(i, carry):
        cf, cb = carry
        kr = kb - 1 - i
        sf = tf_ref[i]
        sb = tb_ref[kr]
        tf_ref[i] = cf
        tb_ref[kr] = cb
        return advance(0, cf, sf), advance(1, cb, sb)

    cf, cb = lax.fori_loop(0, kb, step, (c_ref[0], c_ref[1]), unroll=4)
    c_ref[0] = cf
    c_ref[1] = cb
    for j in range(nj):
        pf_ref[j] = tf_ref[:, j, :]
        pb_ref[j] = tb_ref[:, j, :]

    @pl.when(b == pl.num_programs(0) - 1)
    def _():
        hf_ref[...] = c_ref[...]


def _s5_carry(s_f, s_b, lam, h0):
    nj, n, ns = s_f.shape
    kb = min(n, 256)
    nblk = n // kb
    fwd = pl.BlockSpec((nj, kb, ns), lambda b: (0, b, 0))
    bwd = pl.BlockSpec((nj, kb, ns), lambda b: (0, nblk - 1 - b, 0))
    whole = pl.BlockSpec(lam.shape, lambda b: (0, 0, 0))
    return pl.pallas_call(
        _s5_carry_kernel,
        grid=(nblk,),
        in_specs=[fwd, bwd, whole, whole],
        out_specs=[fwd, bwd, whole],
        out_shape=[jax.ShapeDtypeStruct(s_f.shape, F32), jax.ShapeDtypeStruct(s_b.shape, F32),
                   jax.ShapeDtypeStruct(lam.shape, F32)],
        scratch_shapes=[pltpu.VMEM((kb, nj, ns), F32), pltpu.VMEM((kb, nj, ns), F32), pltpu.VMEM(lam.shape, F32)],
        compiler_params=_cparams(("arbitrary",), 48),
        name="s5_carry",
    )(s_f, s_b, lam, h0)


def _s5_readout_kernel(u_ref, p_ref, wout_ref, tt_ref, add_ref, y_ref, *, reverse):
    nb = p_ref.shape[0]
    us = _sub_chunk_rows(u_ref, nb)
    xb = jnp.concatenate(us, axis=1).astype(BF16)
    pb = p_ref[...].astype(BF16)
    width = xb.shape[1]
    tile = 2 * LANES
    for m in range(width // tile):
        cols = slice(m * tile, (m + 1) * tile)
        rows = slice(m * tile, width) if reverse else slice(0, (m + 1) * tile)
        y = jnp.dot(pb, wout_ref[:, cols], preferred_element_type=F32)
        y = y + jnp.dot(xb[:, rows], tt_ref[rows, cols], preferred_element_type=F32)
        for q in range(tile // LANES):
            s = m * (tile // LANES) + q
            other = add_ref[pl.ds(s, nb, stride=S5_SUB), :] if reverse else add_ref[...] * us[s]
            y_ref[pl.ds(s, nb, stride=S5_SUB), :] = y[:, q * LANES:(q + 1) * LANES] + other


def _s5_readout(u, p, wout, tt, add, reverse):
    nj, t, _ = u.shape
    n = t // S5_SUB
    nb = min(n, 256)
    ns, width = wout.shape[1:]
    tokens = pl.BlockSpec((None, nb * S5_SUB, LANES), lambda j, b: (j, b, 0))
    add_spec = tokens if reverse else pl.BlockSpec((None, 1, LANES), lambda j, b: (j, 0, 0))
    return pl.pallas_call(
        functools.partial(_s5_readout_kernel, reverse=reverse),
        grid=(nj, n // nb),
        in_specs=[tokens,
                  pl.BlockSpec((None, nb, ns), lambda j, b: (j, b, 0)),
                  pl.BlockSpec((None, ns, width), lambda j, b: (j, 0, 0)),
                  pl.BlockSpec((None, width, width), lambda j, b: (j, 0, 0)),
                  add_spec],
        out_specs=tokens,
        out_shape=jax.ShapeDtypeStruct(u.shape, F32),
        compiler_params=_cparams(("arbitrary", "arbitrary"), 48),
        name="s5_readout_bwd" if reverse else "s5_readout_fwd",
    )(u, p, wout, tt, add)


def _odd_out_kernel(h_ref, y_ref, wv_ref, wg_ref, vec_ref, o_ref, a_ref):
    nj, grid_w, rb, _ = y_ref.shape
    for wi in range(grid_w):
        for j in range(nj):
            a_ref[j, pl.ds(wi, rb, stride=grid_w), :] = y_ref[j, wi]
    a = jax.nn.gelu(jnp.concatenate([a_ref[j] for j in range(nj)], axis=1)).astype(BF16)
    val = jnp.dot(a, wv_ref[...], preferred_element_type=F32)
    gate = jnp.dot(a, wg_ref[...], preferred_element_type=F32)
    o_ref[...] = h_ref[...] + vec_ref[0:1, :] * (val * jax.nn.sigmoid(gate))


def _odd_out(h, y, w_val, w_gate, vec, grid_w):
    t, d = h.shape
    rows = t // grid_w
    rb = min(rows, 2 * SUBLANES)
    nj = y.shape[0]
    tok = pl.BlockSpec((rb * grid_w, d), lambda i: (i, 0))
    return pl.pallas_call(
        _odd_out_kernel,
        grid=(rows // rb,),
        in_specs=[tok, pl.BlockSpec((nj, grid_w, rb, LANES), lambda i: (0, 0, i, 0)),
                  pl.BlockSpec(w_val.shape, lambda i: (0, 0)),
                  pl.BlockSpec(w_gate.shape, lambda i: (0, 0)),
                  pl.BlockSpec(vec.shape, lambda i: (0, 0))],
        out_specs=tok,
        out_shape=jax.ShapeDtypeStruct((t, d), F32),
        scratch_shapes=[pltpu.VMEM((nj, rb * grid_w, LANES), F32)],
        compiler_params=_cparams(("arbitrary",), 48),
        name="odd_out",
    )(h, y.reshape(nj, grid_w, rows, LANES), w_val, w_gate, vec)


def _s5_prep_kernel(prm_ref, bre_ref, bim_ref, colp_ref, cre_ref, cim_ref, win_ref, wout_ref, tt_ref, lam_ref, *,
                    reverse):
    T = S5_SUB
    ns = prm_ref.shape[1]
    hi = lax.Precision.HIGHEST
    a_re, a_im, dt = prm_ref[0:1, :], prm_ref[1:2, :], jnp.exp(prm_ref[2:3, :])
    tau = lax.broadcasted_iota(jnp.int32, (3 * SUBLANES, ns), 0).astype(F32)
    mag = jnp.exp(a_re * dt * tau)
    pr = mag * jnp.cos(a_im * dt * tau)
    pi = mag * jnp.sin(a_im * dt * tau)
    lr, li = pr[1:2], pi[1:2]
    den = a_re * a_re + a_im * a_im
    cr = ((lr - 1.0) * a_re + li * a_im) / den
    ci = (li * a_re - (lr - 1.0) * a_im) / den
    row_g = lax.broadcasted_iota(jnp.int32, (LANES, ns), 0) // S5_GROUP
    col_g = lax.broadcasted_iota(jnp.int32, (LANES, ns), 1) // S5_STATE
    same = row_g == col_g
    bbr = jnp.where(same, cr * bre_ref[...] - ci * bim_ref[...], 0.0)
    bbi = jnp.where(same, cr * bim_ref[...] + ci * bre_ref[...], 0.0)
    for s in range(T):
        e = s if reverse else T - 1 - s
        rows = slice(s * LANES, (s + 1) * LANES)
        win_ref[rows, :ns] = (pr[e:e + 1] * bbr - pi[e:e + 1] * bbi).astype(BF16)
        win_ref[rows, ns:] = (pr[e:e + 1] * bbi + pi[e:e + 1] * bbr).astype(BF16)
    lam_ref[...] = jnp.broadcast_to(jnp.concatenate([pr[T:T + 1], pi[T:T + 1]], axis=1), lam_ref.shape)
    dt_c = jnp.exp(colp_ref[2])
    mag_c = jnp.exp(colp_ref[0] * dt_c)
    lr_c = mag_c * jnp.cos(colp_ref[1] * dt_c)
    li_c = mag_c * jnp.sin(colp_ref[1] * dt_c)
    row_gc = lax.broadcasted_iota(jnp.int32, (ns, LANES), 0) // S5_STATE
    col_gc = lax.broadcasted_iota(jnp.int32, (ns, LANES), 1) // S5_GROUP
    same_c = row_gc == col_gc
    c_re = jnp.where(same_c, cre_ref[...], 0.0)
    c_im = jnp.where(same_c, cim_ref[...], 0.0)
    bb = jnp.concatenate([bbr, bbi], axis=1)
    cur_r = jnp.ones((ns, LANES), F32)
    cur_i = jnp.zeros((ns, LANES), F32)
    kblk = []
    for e in range(T + 1):
        wo_r = c_re * cur_r - c_im * cur_i
        wo_i = -(c_re * cur_i + c_im * cur_r)
        if e >= 1:
            t = (T - e) if reverse else e - 1
            wout_ref[:ns, t * LANES:(t + 1) * LANES] = wo_r.astype(BF16)
            wout_ref[ns:, t * LANES:(t + 1) * LANES] = wo_i.astype(BF16)
        if e < T:
            k = jnp.dot(bb, jnp.concatenate([wo_r, wo_i], axis=0), precision=hi, preferred_element_type=F32)
            kblk.append(k.astype(BF16))
        cur_r, cur_i = cur_r * lr_c - cur_i * li_c, cur_r * li_c + cur_i * lr_c
    zeros = jnp.zeros((LANES, LANES), BF16)
    for s in range(T):
        for t in range(T):
            lag = (s - t) if reverse else (t - s)
            tt_ref[s * LANES:(s + 1) * LANES, t * LANES:(t + 1) * LANES] = kblk[lag] if lag >= 0 else zeros


def _s5_prep(a_re, a_im, log_dt, b_re, b_im, c_re, c_im, reverse):
    g, p = a_re.shape
    cdim = b_re.shape[2]
    nj = g // S5_GPB
    ns = S5_GPB * p
    width = S5_SUB * LANES
    prm = jnp.zeros((nj, SUBLANES, ns), F32)
    prm = prm.at[:, 0].set(a_re.reshape(nj, ns)).at[:, 1].set(a_im.reshape(nj, ns))
    prm = prm.at[:, 2].set(jnp.repeat(log_dt, p).reshape(nj, ns))
    colp = jnp.broadcast_to(prm[:, :3, :, None], (nj, 3, ns, LANES))
    row_tile = lambda b: jnp.tile(b.reshape(nj, S5_GPB, p, cdim).transpose(0, 3, 1, 2).reshape(nj, 1, cdim, ns),
                                  (1, S5_GPB, 1, 1)).reshape(nj, LANES, ns)
    col_tile = lambda c_: jnp.tile(c_.reshape(nj, S5_GPB, cdim, p).transpose(0, 1, 3, 2).reshape(nj, ns, 1, cdim),
                                   (1, 1, S5_GPB, 1)).reshape(nj, ns, LANES)
    blk = lambda *shape: pl.BlockSpec((None,) + shape, lambda j: (j,) + (0,) * len(shape))
    return pl.pallas_call(
        functools.partial(_s5_prep_kernel, reverse=reverse),
        grid=(nj,),
        in_specs=[blk(SUBLANES, ns), blk(LANES, ns), blk(LANES, ns), blk(3, ns, LANES), blk(ns, LANES),
                  blk(ns, LANES)],
        out_specs=[blk(width, 2 * ns), blk(2 * ns, width), blk(width, width), blk(SUBLANES, 2 * ns)],
        out_shape=[jax.ShapeDtypeStruct((nj, width, 2 * ns), BF16), jax.ShapeDtypeStruct((nj, 2 * ns, width), BF16),
                   jax.ShapeDtypeStruct((nj, width, width), BF16), jax.ShapeDtypeStruct((nj, SUBLANES, 2 * ns), F32)],
        compiler_params=_cparams(("arbitrary",), 56),
        name="s5_prep_bwd" if reverse else "s5_prep_fwd",
    )(prm, row_tile(b_re), row_tile(b_im), colp, col_tile(c_re), col_tile(c_im))


def _rows(*vs, width):
    out = jnp.zeros((SUBLANES, width), F32)
    for i, v in enumerate(vs):
        out = out.at[i].set(v.astype(F32))
    return out


def kernel(x, c, ctx, c_ctx, ada_w, ada_b, norm_mix_g, norm_ffn_g, ffn_w_gate, ffn_w_up, ffn_w_down, final_norm_g, ev_w_in, ev_w_out, ssd_conv_w, ssd_conv_b, ssd_dt_bias, ssd_a_log, ssd_d, ssd_norm_g, hg_lb_logits, hg_norm_g, od_w_in, s5_a_re, s5_a_im, s5_log_dt, s5_b_re, s5_b_im, s5_c_re, s5_c_im, s5_d, od_w_val, od_w_gate):
    d = x.shape[-1]
    lat = x[0].astype(F32)
    hc = ctx[0].astype(F32)

    m = _ada(_rows(c[0], c_ctx, width=d), ada_w, ada_b)

    def ada_vecs(layer, stream):
        return [m[layer, stream, i * d:(i + 1) * d] for i in range(6)]

    w = ev_w_in[0]
    n_x = SSD_HEADS * SSD_HEAD_DIM
    n_xbc = n_x + 2 * SSD_GROUPS * SSD_STATE
    n_hg = HG_HEADS * HG_HEAD_DIM
    o_z, o_xbc, o_dt = 0, n_x, n_x + n_xbc
    o_q = o_dt + 2 * SSD_HEADS
    o_f = o_q + n_hg
    o_v = o_f + 2 * n_hg
    o_g = o_v + n_hg
    pad = lambda n: jnp.zeros((d, n), F32)
    w_in = jnp.concatenate([
        w[:, o_xbc:o_xbc + n_xbc],
        w[:, o_dt:o_dt + SSD_HEADS], pad(LANES - SSD_HEADS),
        w[:, o_dt + SSD_HEADS:o_dt + 2 * SSD_HEADS], pad(LANES - SSD_HEADS),
        pad(ZS_Q - ZS_XD),
        w[:, o_q:o_q + n_hg], w[:, o_v:o_v + n_hg], w[:, o_f:o_f + 2 * n_hg],
        w[:, o_z:o_z + n_x], w[:, o_g:o_g + n_hg]], axis=1).astype(BF16)
    cw = jnp.zeros((SUBLANES, n_xbc), F32).at[:SSD_CONV].set(ssd_conv_w[0])
    cb = ssd_conv_b[0].reshape(1, n_xbc)
    dtc = jnp.zeros((2, SUBLANES, LANES), F32)
    dtc = dtc.at[:, 0, :SSD_HEADS].set(ssd_dt_bias[0]).at[:, 1, :SSD_HEADS].set(ssd_a_log[0])
    dsk = jnp.repeat(ssd_d[0], SSD_HEAD_DIM).reshape(1, n_x)
    lower = jnp.cumsum(jax.nn.softmax(hg_lb_logits.astype(F32), axis=0), axis=0)[0].reshape(1, n_hg)
    head_of_lane = jnp.arange(n_x) // SSD_HEAD_DIM
    e16 = (jnp.arange(LANES)[:, None] == head_of_lane[None, :]).astype(BF16)
    consts = (dtc, dsk, e16)
    w_out = ev_w_out[0].astype(BF16)
    out_vec = lambda gate: _rows(ssd_norm_g[0], jnp.tile(hg_norm_g[0], HG_HEADS), gate, width=d)
    wg0, wu0, wd0 = (t_[0].astype(BF16) for t_ in (ffn_w_gate, ffn_w_up, ffn_w_down))

    s_state = jnp.zeros((2, SSD_GROUPS, SSD_STATE, n_x // SSD_GROUPS), F32)
    g_state = jnp.zeros((2, HG_HEADS, HG_HEAD_DIM, HG_HEAD_DIM), F32)
    layer0 = {}
    for stream, h in ((1, hc), (0, lat)):
        sm, scm, gm, sf, scf, gf = ada_vecs(0, stream)
        za, zb = _even_in(h, _rows(norm_mix_g[0], scm, sm, width=d), w_in, cw, cb, dtc, lower)
        yo_f, yo_b, s_state, g_state = _even_scan(za, zb, consts, s_state, g_state)
        h1 = _even_out(h, zb, yo_f, yo_b, w_out, out_vec(gm))
        layer0[stream] = _ffn(h1, _rows(norm_ffn_g[0], scf, sf, gf, width=d), wg0, wu0, wd0, final=False)

    lat, hc = layer0[0], layer0[1]
    w_s5 = od_w_in[0].astype(BF16)
    u = {}
    for stream, h, gw in ((1, hc, 1), (0, lat, GRID_W)):
        sm, scm = ada_vecs(1, stream)[:2]
        u[stream] = _odd_in(h, _rows(norm_mix_g[1], scm, sm, width=d), w_s5, gw)
    mats = [_s5_prep(s5_a_re[0, di], s5_a_im[0, di], s5_log_dt[0, di], s5_b_re[0, di], s5_b_im[0, di],
                     s5_c_re[0, di], s5_c_im[0, di], reverse=bool(di)) for di in range(2)]
    lam = jnp.stack([mats[0][3][:, 0], mats[1][3][:, 0]])
    carry = jnp.zeros_like(lam)
    for stream in (1, 0):
        s_f, s_b = (_s5_inject(u[stream], mats[di][0]) for di in range(2))
        p_f, p_b, carry = _s5_carry(s_f, s_b, lam, carry)
    nj = u[0].shape[0]
    y = _s5_readout(u[0], p_f, mats[0][1], mats[0][2], s5_d[0].reshape(nj, 1, LANES), reverse=False)
    y = _s5_readout(u[0], p_b, mats[1][1], mats[1][2], y, reverse=True)
    _, _, gm, sf, scf, gf = ada_vecs(1, 0)
    h3 = _odd_out(lat, y, od_w_val[0].astype(BF16), od_w_gate[0].astype(BF16), _rows(gm, width=d), GRID_W)
    wg1, wu1, wd1 = (t_[1].astype(BF16) for t_ in (ffn_w_gate, ffn_w_up, ffn_w_down))
    out = _ffn(h3, _rows(norm_ffn_g[1], scf, sf, gf, final_norm_g, width=d), wg1, wu1, wd1, final=True)
    return out[None].astype(x.dtype)
```

```python
import functools
import math

import jax
import jax.numpy as jnp
from jax import lax
from jax.experimental import pallas as pl
from jax.experimental.pallas import tpu as pltpu

F32 = jnp.float32
BF16 = jnp.bfloat16
EPS = 1e-6

LANES = 128
SUBLANES = 8
GRID_W = 64
SCAN_CHUNK = 64
SSD_HEADS = 16
SSD_HEAD_DIM = 64
SSD_STATE = 128
SSD_GROUPS = 2
SSD_CONV = 5
HG_HEADS = 8
HG_HEAD_DIM = 128
S5_GROUP = 16
S5_STATE = 64
S5_SUB = 16
S5_GPB = LANES // S5_GROUP

ZS_XBC = 0
ZS_DT = 1536
ZS_XD = 1792
ZS_Q = 2048
ZS_V = 3072
ZS_F = 4096
ZS_Z = 6144
ZS_G = 7168
ZS_WIDTH = 8192


def _cparams(semantics, vmem_mb):
    return pltpu.CompilerParams(dimension_semantics=semantics, vmem_limit_bytes=vmem_mb * 1024 * 1024)


def _silu(x):
    return x * jax.nn.sigmoid(x)


def _modulate(h, mod_ref):
    ms = jnp.mean(h * h, axis=-1, keepdims=True)
    return h * lax.rsqrt(ms + EPS) * (mod_ref[0:1, :] * (1.0 + mod_ref[1:2, :])) + mod_ref[2:3, :]


def _split_dot(x, w, passes):
    acc, rem = None, x
    for _ in range(passes):
        hi = rem.astype(BF16)
        d = jnp.dot(hi, w, preferred_element_type=F32)
        acc = d if acc is None else acc + d
        rem = rem - hi.astype(F32)
    return acc


def _chunk_cumsum(x, reverse):
    n = x.shape[0]
    pos = lax.broadcasted_iota(jnp.int32, (n, 1), 0) % SCAN_CHUNK
    s = 1
    while s < SCAN_CHUNK:
        if reverse:
            x = x + jnp.where(pos < SCAN_CHUNK - s, pltpu.roll(x, n - s, axis=0), 0.0)
        else:
            x = x + jnp.where(pos >= s, pltpu.roll(x, s, axis=0), 0.0)
        s *= 2
    return x


def _ada_kernel(c_ref, w_ref, b_ref, o_ref):
    o_ref[...] = jnp.dot(_silu(c_ref[...]), w_ref[...], precision=lax.Precision.HIGHEST,
                         preferred_element_type=F32) + b_ref[...]


def _ada(cvecs, ada_w, ada_b):
    depth, d, n = ada_w.shape
    tn = n // 4
    return pl.pallas_call(
        _ada_kernel,
        grid=(depth, n // tn),
        in_specs=[pl.BlockSpec((SUBLANES, d), lambda l, j: (0, 0)),
                  pl.BlockSpec((None, d, tn), lambda l, j: (l, 0, j)),
                  pl.BlockSpec((None, 1, tn), lambda l, j: (l, 0, j))],
        out_specs=pl.BlockSpec((None, SUBLANES, tn), lambda l, j: (l, 0, j)),
        out_shape=jax.ShapeDtypeStruct((depth, SUBLANES, n), F32),
        compiler_params=_cparams(("arbitrary", "arbitrary"), 40),
        name="ada",
    )(cvecs, ada_w, ada_b.reshape(depth, 1, n))


EVEN_TN = 1024
HALO = 2 * SUBLANES
EVEN_TILES = (("conv", 0, None), ("conv_dt", 1, None), ("silu", None, 0), ("copy", None, 1),
              ("gate", 2, 4), ("gate", 3, 5), ("silu", None, 2), ("silu", None, 3))


def _even_in_kernel(h_ref, hp_ref, hn_ref, mod_ref, w_ref, cw_ref, cb_ref, dtc_ref, lb_ref, za_ref, zb_ref, a_ref):
    i = pl.program_id(0)
    tm = h_ref.shape[0]
    n_lb = lb_ref.shape[1]
    a_ref[:HALO] = _modulate(hp_ref[...], mod_ref).astype(BF16)
    a_ref[HALO:HALO + tm] = _modulate(h_ref[...], mod_ref).astype(BF16)
    a_ref[HALO + tm:] = _modulate(hn_ref[...], mod_ref).astype(BF16)
    row = lax.broadcasted_iota(jnp.int32, (tm + 2 * HALO, 1), 0)
    outside = ((row < HALO) & (i == 0)) | ((row >= HALO + tm) & (i == pl.num_programs(0) - 1))

    def conv_silu(col0, lanes):
        acc = jnp.dot(a_ref[...], w_ref[:, col0:col0 + lanes], preferred_element_type=F32)
        acc = jnp.where(outside, 0.0, acc)
        conv = cb_ref[:, col0:col0 + lanes]
        for k in range(SSD_CONV):
            off = HALO - SSD_CONV // 2 + k
            conv = conv + cw_ref[k:k + 1, col0:col0 + lanes] * acc[off:off + tm]
        return _silu(conv)

    def main_dot(col0, lanes=EVEN_TN):
        return jnp.dot(a_ref[HALO:HALO + tm], w_ref[:, col0:col0 + lanes], preferred_element_type=F32)

    for tile, (kind, fa, fb) in enumerate(EVEN_TILES):
        col0 = tile * EVEN_TN
        if kind == "conv":
            za_ref[:, fa * EVEN_TN:(fa + 1) * EVEN_TN] = conv_silu(col0, EVEN_TN)
        elif kind == "conv_dt":
            n_c = ZS_DT - col0
            za_ref[:, col0:ZS_DT] = conv_silu(col0, n_c)
            dt = main_dot(ZS_DT, 2 * LANES)
            for d in range(2):
                za_ref[:, ZS_DT + d * LANES:ZS_DT + (d + 1) * LANES] = jax.nn.softplus(
                    dt[:, d * LANES:(d + 1) * LANES] + dtc_ref[d, 0:1, :])
            za_ref[:, ZS_XD:(fa + 1) * EVEN_TN] = jnp.zeros((tm, (fa + 1) * EVEN_TN - ZS_XD), F32)
        elif kind == "gate":
            lb = lb_ref[:, (col0 - ZS_F) % n_lb:(col0 - ZS_F) % n_lb + EVEN_TN]
            f = lb + (1.0 - lb) * jax.nn.sigmoid(main_dot(col0))
            backward = (col0 - ZS_F) // n_lb == 1
            za_ref[:, fa * EVEN_TN:(fa + 1) * EVEN_TN] = _chunk_cumsum(jnp.log(f), backward)
            zb_ref[:, fb * EVEN_TN:(fb + 1) * EVEN_TN] = (1.0 - f).astype(BF16)
        elif kind == "silu":
            zb_ref[:, fb * EVEN_TN:(fb + 1) * EVEN_TN] = _silu(main_dot(col0)).astype(BF16)
        else:
            zb_ref[:, fb * EVEN_TN:(fb + 1) * EVEN_TN] = main_dot(col0).astype(BF16)


def _even_in(h, mod, w, cw, cb, dtc, lb):
    t, d = h.shape
    tm = min(t, 512)
    hb = tm // HALO
    n_a = (1 + max(fa for _, fa, _ in EVEN_TILES if fa is not None)) * EVEN_TN
    n_b = (1 + max(fb for _, _, fb in EVEN_TILES if fb is not None)) * EVEN_TN
    whole = lambda a, **kw: pl.BlockSpec(a.shape, lambda i: (0,) * a.ndim, **kw)
    return pl.pallas_call(
        _even_in_kernel,
        grid=(t // tm,),
        in_specs=[pl.BlockSpec((tm, d), lambda i: (i, 0)),
                  pl.BlockSpec((HALO, d), lambda i: (jnp.maximum(i * hb - 1, 0), 0)),
                  pl.BlockSpec((HALO, d), lambda i: (jnp.minimum((i + 1) * hb, t // HALO - 1), 0)),
                  whole(mod), whole(w, pipeline_mode=pl.Buffered(1)),
                  whole(cw), whole(cb), whole(dtc), whole(lb)],
        out_specs=[pl.BlockSpec((tm, n_a), lambda i: (i, 0)), pl.BlockSpec((tm, n_b), lambda i: (i, 0))],
        out_shape=[jax.ShapeDtypeStruct((t, n_a), F32), jax.ShapeDtypeStruct((t, n_b), BF16)],
        scratch_shapes=[pltpu.VMEM((tm + 2 * HALO, d), BF16)],
        compiler_params=_cparams(("arbitrary",), 56),
        name="even_in",
    )(h, h, h, mod, w, cw, cb, dtc, lb)


def _scan_block_prep(xd_ref, qv_ref, bcum_ref, kk_ref, dtc_ref, d):
    n_hg = bcum_ref.shape[1]
    xbc = xd_ref[:, :ZS_DT]
    dtv = xd_ref[:, ZS_DT + d * LANES:ZS_DT + (d + 1) * LANES]
    acum = _chunk_cumsum(-dtv * jnp.exp(dtc_ref[d, 1:2, :]), bool(d))
    return xbc, dtv, acum, kk_ref[...].astype(F32), bcum_ref[...], qv_ref[:, :n_hg].astype(F32), qv_ref[:, n_hg:]


def _scan_chunk(prep, rows, dsk_ref, e_ref, st_ref, gt_ref, yo_ref, d, reverse):
    Q = SCAN_CHUNK
    xbc, dtv, acum, kk, bcum, qs, vb = (a[rows] for a in prep)
    ti = lax.broadcasted_iota(jnp.int32, (Q, Q), 0)
    si = lax.broadcasted_iota(jnp.int32, (Q, Q), 1)
    mask = (ti <= si) if reverse else (ti >= si)
    edge = 0 if reverse else Q - 1
    width = SSD_HEADS * SSD_HEAD_DIM
    gw = width // SSD_GROUPS
    xs = xbc[:, :width]

    last = acum[edge:edge + 1, :]
    acum_t = acum.T
    e_a = jnp.exp(acum)
    w_end = jnp.exp(last - acum)
    e_last = jnp.broadcast_to(jnp.exp(last), (SUBLANES, LANES))
    xe = _split_dot(jnp.concatenate([dtv, e_a, w_end], axis=0), e_ref[...], 1)
    dt_x, ea_x, wend_x = xe[:Q], xe[Q:2 * Q], xe[2 * Q:3 * Q]
    elast_x = _split_dot(e_last, e_ref[...], 3)[0:1]
    xdt = xs * dt_x
    xw = (xdt * wend_x).astype(BF16)
    xdt_b = xdt.astype(BF16)
    lane = lax.broadcasted_iota(jnp.int32, (Q, LANES), 1)
    lo = lane < SSD_HEAD_DIM
    zero_b = jnp.zeros((Q, LANES), BF16)
    ys = []
    for g in range(SSD_GROUPS):
        bm = xbc[:, width + g * SSD_STATE:width + (g + 1) * SSD_STATE].astype(BF16)
        cm = xbc[:, width + (SSD_GROUPS + g) * SSD_STATE:width + (SSD_GROUPS + g + 1) * SSD_STATE].astype(BF16)
        cb = lax.dot_general(cm, bm, (((1,), (1,)), ((), ())), preferred_element_type=F32)
        y_state = jnp.dot(cm, st_ref[d, g].astype(BF16), preferred_element_type=F32)
        pairs = []
        for hp in range(gw // LANES):
            xp = xdt_b[:, g * gw + hp * LANES:g * gw + (hp + 1) * LANES]
            acc = None
            for half in range(2):
                h = (g * gw + hp * LANES) // SSD_HEAD_DIM + half
                seg = acum[:, h:h + 1] - acum_t[h:h + 1, :]
                m = jnp.where(mask, cb * jnp.exp(seg), 0.0).astype(BF16)
                xh = jnp.where(lo, xp, zero_b) if half == 0 else jnp.where(lo, zero_b, xp)
                part = jnp.dot(m, xh, preferred_element_type=F32)
                acc = part if acc is None else acc + part
            pairs.append(acc)
        y_g = jnp.concatenate(pairs, axis=1) + y_state * ea_x[:, g * gw:(g + 1) * gw]
        ys.append(y_g)
        upd = lax.dot_general(bm, xw[:, g * gw:(g + 1) * gw], (((0,), (0,)), ((), ())),
                              preferred_element_type=F32)
        st_ref[d, g] = st_ref[d, g] * elast_x[:, g * gw:(g + 1) * gw] + upd
    y = jnp.concatenate(ys, axis=1)
    if not reverse:
        y = y + dsk_ref[...] * xs
    yo_ref[rows, :width] = y.astype(yo_ref.dtype)

    blast = bcum[edge:edge + 1, :]
    qe = (qs * jnp.exp(bcum)).astype(BF16)
    k_end = kk * jnp.exp(blast - bcum)
    k_til = (k_end * jnp.exp(-blast)).astype(BF16)
    k_end = k_end.astype(BF16)
    e_blast = jnp.exp(blast)
    outs = []
    for h in range(HG_HEADS):
        sl = slice(h * HG_HEAD_DIM, (h + 1) * HG_HEAD_DIM)
        att = lax.dot_general(qe[:, sl], k_til[:, sl], (((1,), (1,)), ((), ())), preferred_element_type=F32)
        att = jnp.where(mask, att, 0.0).astype(BF16)
        gt = gt_ref[d, h]
        o_h = jnp.dot(att, vb[:, sl], preferred_element_type=F32)
        o_h = o_h + lax.dot_general(qe[:, sl], gt.astype(BF16), (((1,), (1,)), ((), ())),
                                    preferred_element_type=F32)
        outs.append(o_h)
        upd = lax.dot_general(vb[:, sl], k_end[:, sl], (((0,), (0,)), ((), ())), preferred_element_type=F32)
        gt_ref[d, h] = gt * e_blast[:, sl] + upd
    yo_ref[rows, width:] = jnp.concatenate(outs, axis=1).astype(yo_ref.dtype)


def _even_scan_kernel(*refs, nblk, cpb):
    fwd_in, bwd_in = refs[0:4], refs[4:8]
    dtc_ref, dsk_ref, e_ref, h0s_ref, h0g_ref = refs[8:13]
    yof_ref, yob_ref, st_ref, gt_ref = refs[13:17]
    c = pl.program_id(0)

    @pl.when(c == 0)
    def _():
        st_ref[...] = h0s_ref[...]
        gt_ref[...] = h0g_ref[...]

    streams = ((False, fwd_in, yof_ref), (True, bwd_in, yob_ref))
    preps = [_scan_block_prep(*ins, dtc_ref, d) for d, (_, ins, _) in enumerate(streams)]
    for step in range(cpb):
        for d, (reverse, _, yo_ref) in enumerate(streams):
            ci = (cpb - 1 - step) if reverse else step
            rows = slice(ci * SCAN_CHUNK, (ci + 1) * SCAN_CHUNK)
            _scan_chunk(preps[d], rows, dsk_ref, e_ref, st_ref, gt_ref, yo_ref, d, reverse)


def _even_scan(za, zb, consts, h0s, h0g):
    t = za.shape[0]
    nch = t // SCAN_CHUNK
    cpb = max(k for k in (4, 2, 1) if nch % k == 0)
    rows = cpb * SCAN_CHUNK
    nblk = nch // cpb
    dtc, dsk, e16 = consts
    n_x = SSD_HEADS * SSD_HEAD_DIM
    n_hg = HG_HEADS * HG_HEAD_DIM
    gate_a, gate_b = (tuple(tile[i] for tile in EVEN_TILES if tile[0] == "gate") for i in (1, 2))

    def stream_specs(bidx, d):
        return [
            pl.BlockSpec((rows, ZS_XD), lambda c: (bidx(c), 0)),
            pl.BlockSpec((rows, 2 * n_hg), lambda c: (bidx(c), 0)),
            pl.BlockSpec((rows, n_hg), lambda c: (bidx(c), gate_a[d])),
            pl.BlockSpec((rows, n_hg), lambda c: (bidx(c), gate_b[d])),
        ]

    fwd = lambda c: c
    bwd = lambda c: nblk - 1 - c
    whole = lambda a: pl.BlockSpec(a.shape, lambda c: (0,) * a.ndim)
    in_specs = stream_specs(fwd, 0) + stream_specs(bwd, 1) + [whole(a) for a in (dtc, dsk, e16, h0s, h0g)]
    out_specs = [pl.BlockSpec((rows, n_x + n_hg), lambda c: (c, 0)),
                 pl.BlockSpec((rows, n_x + n_hg), lambda c: (bwd(c), 0)), whole(h0s), whole(h0g)]
    out_shape = [jax.ShapeDtypeStruct((t, n_x + n_hg), BF16), jax.ShapeDtypeStruct((t, n_x + n_hg), BF16),
                 jax.ShapeDtypeStruct(h0s.shape, F32), jax.ShapeDtypeStruct(h0g.shape, F32)]
    return pl.pallas_call(
        functools.partial(_even_scan_kernel, nblk=nblk, cpb=cpb),
        grid=(nblk,),
        in_specs=in_specs,
        out_specs=out_specs,
        out_shape=out_shape,
        compiler_params=_cparams(("arbitrary",), 48),
        name="even_scan",
    )(za, zb, za, zb, za, zb, za, zb, dtc, dsk, e16, h0s, h0g)


def _ffn_block(h, mod_ref, wg_ref, wu_ref, wd_ref, final):
    a = _modulate(h, mod_ref).astype(BF16)
    gate = jnp.dot(a, wg_ref[...], preferred_element_type=F32)
    up = jnp.dot(a, wu_ref[...], preferred_element_type=F32)
    act = (_silu(gate) * up).astype(BF16)
    out = h + mod_ref[3:4, :] * jnp.dot(act, wd_ref[...], preferred_element_type=F32)
    if final:
        ms = jnp.mean(out * out, axis=-1, keepdims=True)
        out = out * lax.rsqrt(ms + EPS) * mod_ref[4:5, :]
    return out


def _resident(a):
    return pl.BlockSpec(a.shape, lambda i: (0,) * a.ndim, pipeline_mode=pl.Buffered(1))


def _even_out_kernel(h_ref, z_ref, g_ref, yof_ref, yob_ref, w_ref, vec_ref, mod_ref, wg_ref, wu_ref, wd_ref, o_ref):
    n_x = z_ref.shape[1]
    yo = yof_ref[...].astype(F32) + yob_ref[...].astype(F32)
    y = yo[:, :n_x] * z_ref[...].astype(F32)
    o = yo[:, n_x:]
    gw = n_x // SSD_GROUPS
    parts = []
    for g in range(SSD_GROUPS):
        yg = y[:, g * gw:(g + 1) * gw]
        ms = jnp.mean(yg * yg, axis=-1, keepdims=True)
        parts.append(yg * lax.rsqrt(ms + EPS))
    yn = (jnp.concatenate(parts, axis=1) * vec_ref[0:1, :]).astype(BF16)
    parts = []
    for hh in range(HG_HEADS):
        oh = o[:, hh * HG_HEAD_DIM:(hh + 1) * HG_HEAD_DIM]
        ms = jnp.mean(oh * oh, axis=-1, keepdims=True)
        parts.append(oh * lax.rsqrt(ms + EPS))
    on = (jnp.concatenate(parts, axis=1) * vec_ref[1:2, :] * g_ref[...].astype(F32)).astype(BF16)
    half = yn.shape[1]
    mix = jnp.dot(yn, w_ref[:half, :], preferred_element_type=F32)
    mix = mix + jnp.dot(on, w_ref[half:, :], preferred_element_type=F32)
    h1 = h_ref[...] + vec_ref[2:3, :] * mix
    o_ref[...] = _ffn_block(h1, mod_ref, wg_ref, wu_ref, wd_ref, final=False)


def _even_out(h, zb, yo_f, yo_b, w_out, vec, mod, wg, wu, wd):
    t, d = h.shape
    tm = min(t, 256)
    n_x = SSD_HEADS * SSD_HEAD_DIM
    n_hg = HG_HEADS * HG_HEAD_DIM
    row = lambda i: (i, 0)
    z_tile, g_tile = (next(fb for k, (_, _, fb) in enumerate(EVEN_TILES) if k * EVEN_TN == start)
                      for start in (ZS_Z, ZS_G))
    return pl.pallas_call(
        _even_out_kernel,
        grid=(t // tm,),
        in_specs=[pl.BlockSpec((tm, d), row),
                  pl.BlockSpec((tm, n_x), lambda i: (i, z_tile)),
                  pl.BlockSpec((tm, n_hg), lambda i: (i, g_tile)),
                  pl.BlockSpec((tm, n_x + n_hg), row), pl.BlockSpec((tm, n_x + n_hg), row),
                  _resident(w_out), _resident(vec), _resident(mod), _resident(wg), _resident(wu), _resident(wd)],
        out_specs=pl.BlockSpec((tm, d), row),
        out_shape=jax.ShapeDtypeStruct((t, d), F32),
        compiler_params=_cparams(("arbitrary",), 56),
        name="even_out_ffn",
    )(h, zb, zb, yo_f, yo_b, w_out, vec, mod, wg, wu, wd)


def _odd_in_kernel(h_ref, mod_ref, w_ref, o_ref, u_ref):
    a = _modulate(h_ref[...], mod_ref).astype(BF16)
    u = jnp.dot(a, w_ref[...], preferred_element_type=F32)
    nj, grid_w, rb, _ = o_ref.shape
    for j in range(nj):
        u_ref[j] = u[:, j * LANES:(j + 1) * LANES]
    for wi in range(grid_w):
        for j in range(nj):
            o_ref[j, wi] = u_ref[j, pl.ds(wi, rb, stride=grid_w), :]


def _odd_in(h, mod, w, grid_w):
    t, d = h.shape
    rows = t // grid_w
    rb = min(rows, 4 * SUBLANES)
    n = w.shape[1]
    nj = n // LANES
    out = pl.pallas_call(
        _odd_in_kernel,
        grid=(rows // rb,),
        in_specs=[pl.BlockSpec((rb * grid_w, d), lambda i: (i, 0)),
                  pl.BlockSpec((SUBLANES, d), lambda i: (0, 0)),
                  pl.BlockSpec(w.shape, lambda i: (0, 0))],
        out_specs=pl.BlockSpec((nj, grid_w, rb, LANES), lambda i: (0, 0, i, 0)),
        out_shape=jax.ShapeDtypeStruct((nj, grid_w, rows, LANES), F32),
        scratch_shapes=[pltpu.VMEM((nj, rb * grid_w, LANES), F32)],
        compiler_params=_cparams(("arbitrary",), 48),
        name="odd_in",
    )(h, mod, w)
    return out.reshape(nj, t, LANES)


def _sub_chunk_rows(u_ref, nb):
    return [u_ref[pl.ds(s, nb, stride=S5_SUB), :] for s in range(S5_SUB)]


def _s5_inject_kernel(u_ref, win_ref, s_ref):
    x = jnp.concatenate(_sub_chunk_rows(u_ref, s_ref.shape[0]), axis=1).astype(BF16)
    s_ref[...] = jnp.dot(x, win_ref[...], preferred_element_type=F32)


def _s5_inject(u, win):
    nj, t, _ = u.shape
    n = t // S5_SUB
    nb = min(n, 512)
    width, ns = win.shape[1:]
    return pl.pallas_call(
        _s5_inject_kernel,
        grid=(nj, n // nb),
        in_specs=[pl.BlockSpec((None, nb * S5_SUB, LANES), lambda j, b: (j, b, 0)),
                  pl.BlockSpec((None, width, ns), lambda j, b: (j, 0, 0))],
        out_specs=pl.BlockSpec((None, nb, ns), lambda j, b: (j, b, 0)),
        out_shape=jax.ShapeDtypeStruct((nj, n, ns), F32),
        compiler_params=_cparams(("arbitrary", "arbitrary"), 40),
        name="s5_inject",
    )(u, win)


def _s5_carry_kernel(sf_ref, sb_ref, lam_ref, h0_ref, pf_ref, pb_ref, hf_ref, tf_ref, tb_ref, c_ref):
    b = pl.program_id(0)
    nj, kb, ns = sf_ref.shape
    half = ns // 2

    @pl.when(b == 0)
    def _():
        c_ref[...] = h0_ref[...]

    for j in range(nj):
        tf_ref[:, j, :] = sf_ref[j]
        tb_ref[:, j, :] = sb_ref[j]

    def advance(d, c, s):
        lr, li = lam_ref[d, :, :half], lam_ref[d, :, half:]
        pr, pi = c[:, :half], c[:, half:]
        return jnp.concatenate([lr * pr - li * pi + s[:, :half], lr * pi + li * pr + s[:, half:]], axis=1)

    def step(i, carry):
        cf, cb = carry
        kr = kb - 1 - i
        sf = tf_ref[i]
        sb = tb_ref[kr]
        tf_ref[i] = cf
        tb_ref[kr] = cb
        return advance(0, cf, sf), advance(1, cb, sb)

    cf, cb = lax.fori_loop(0, kb, step, (c_ref[0], c_ref[1]), unroll=4)
    c_ref[0] = cf
    c_ref[1] = cb
    for j in range(nj):
        pf_ref[j] = tf_ref[:, j, :]
        pb_ref[j] = tb_ref[:, j, :]

    @pl.when(b == pl.num_programs(0) - 1)
    def _():
        hf_ref[...] = c_ref[...]


def _s5_carry(s_f, s_b, lam, h0):
    nj, n, ns = s_f.shape
    kb = min(n, 256)
    nblk = n // kb
    fwd = pl.BlockSpec((nj, kb, ns), lambda b: (0, b, 0))
    bwd = pl.BlockSpec((nj, kb, ns), lambda b: (0, nblk - 1 - b, 0))
    whole = pl.BlockSpec(lam.shape, lambda b: (0, 0, 0))
    return pl.pallas_call(
        _s5_carry_kernel,
        grid=(nblk,),
        in_specs=[fwd, bwd, whole, whole],
        out_specs=[fwd, bwd, whole],
        out_shape=[jax.ShapeDtypeStruct(s_f.shape, F32), jax.ShapeDtypeStruct(s_b.shape, F32),
                   jax.ShapeDtypeStruct(lam.shape, F32)],
        scratch_shapes=[pltpu.VMEM((kb, nj, ns), F32), pltpu.VMEM((kb, nj, ns), F32), pltpu.VMEM(lam.shape, F32)],
        compiler_params=_cparams(("arbitrary",), 48),
        name="s5_carry",
    )(s_f, s_b, lam, h0)


def _s5_readout_kernel(u_ref, p_ref, wout_ref, tt_ref, add_ref, y_ref, *, reverse):
    nb = p_ref.shape[0]
    us = _sub_chunk_rows(u_ref, nb)
    xb = jnp.concatenate(us, axis=1).astype(BF16)
    pb = p_ref[...].astype(BF16)
    width = xb.shape[1]
    tile = 2 * LANES
    for m in range(width // tile):
        cols = slice(m * tile, (m + 1) * tile)
        rows = slice(m * tile, width) if reverse else slice(0, (m + 1) * tile)
        y = jnp.dot(pb, wout_ref[:, cols], preferred_element_type=F32)
        y = y + jnp.dot(xb[:, rows], tt_ref[rows, cols], preferred_element_type=F32)
        for q in range(tile // LANES):
            s = m * (tile // LANES) + q
            other = add_ref[pl.ds(s, nb, stride=S5_SUB), :] if reverse else add_ref[...] * us[s]
            y_ref[pl.ds(s, nb, stride=S5_SUB), :] = y[:, q * LANES:(q + 1) * LANES] + other


def _s5_readout(u, p, wout, tt, add, reverse):
    nj, t, _ = u.shape
    n = t // S5_SUB
    nb = min(n, 256)
    ns, width = wout.shape[1:]
    tokens = pl.BlockSpec((None, nb * S5_SUB, LANES), lambda j, b: (j, b, 0))
    add_spec = tokens if reverse else pl.BlockSpec((None, 1, LANES), lambda j, b: (j, 0, 0))
    return pl.pallas_call(
        functools.partial(_s5_readout_kernel, reverse=reverse),
        grid=(nj, n // nb),
        in_specs=[tokens,
                  pl.BlockSpec((None, nb, ns), lambda j, b: (j, b, 0)),
                  pl.BlockSpec((None, ns, width), lambda j, b: (j, 0, 0)),
                  pl.BlockSpec((None, width, width), lambda j, b: (j, 0, 0)),
                  add_spec],
        out_specs=tokens,
        out_shape=jax.ShapeDtypeStruct(u.shape, F32),
        compiler_params=_cparams(("arbitrary", "arbitrary"), 48),
        name="s5_readout_bwd" if reverse else "s5_readout_fwd",
    )(u, p, wout, tt, add)


def _odd_out_kernel(h_ref, y_ref, wv_ref, wgl_ref, vec_ref, mod_ref, wg_ref, wu_ref, wd_ref, o_ref, a_ref):
    nj, grid_w, rb, _ = y_ref.shape
    for wi in range(grid_w):
        for j in range(nj):
            a_ref[j, pl.ds(wi, rb, stride=grid_w), :] = y_ref[j, wi]
    a = jax.nn.gelu(jnp.concatenate([a_ref[j] for j in range(nj)], axis=1)).astype(BF16)
    val = jnp.dot(a, wv_ref[...], preferred_element_type=F32)
    gate = jnp.dot(a, wgl_ref[...], preferred_element_type=F32)
    h3 = h_ref[...] + vec_ref[0:1, :] * (val * jax.nn.sigmoid(gate))
    o_ref[...] = _ffn_block(h3, mod_ref, wg_ref, wu_ref, wd_ref, final=True)


def _odd_out(h, y, w_val, w_gate, vec, mod, wg, wu, wd, grid_w):
    t, d = h.shape
    rows = t // grid_w
    rb = min(rows, SUBLANES)
    nj = y.shape[0]
    tok = pl.BlockSpec((rb * grid_w, d), lambda i: (i, 0))
    return pl.pallas_call(
        _odd_out_kernel,
        grid=(rows // rb,),
        in_specs=[tok, pl.BlockSpec((nj, grid_w, rb, LANES), lambda i: (0, 0, i, 0)),
                  _resident(w_val), _resident(w_gate), _resident(vec), _resident(mod),
                  _resident(wg), _resident(wu), _resident(wd)],
        out_specs=tok,
        out_shape=jax.ShapeDtypeStruct((t, d), F32),
        scratch_shapes=[pltpu.VMEM((nj, rb * grid_w, LANES), F32)],
        compiler_params=_cparams(("arbitrary",), 56),
        name="odd_out_ffn",
    )(h, y.reshape(nj, grid_w, rows, LANES), w_val, w_gate, vec, mod, wg, wu, wd)


def _s5_prep_kernel(prm_ref, bre_ref, bim_ref, colp_ref, cre_ref, cim_ref, win_ref, wout_ref, tt_ref, lam_ref, *,
                    reverse):
    T = S5_SUB
    ns = prm_ref.shape[1]
    hi = lax.Precision.HIGHEST
    a_re, a_im, dt = prm_ref[0:1, :], prm_ref[1:2, :], jnp.exp(prm_ref[2:3, :])
    tau = lax.broadcasted_iota(jnp.int32, (3 * SUBLANES, ns), 0).astype(F32)
    mag = jnp.exp(a_re * dt * tau)
    pr = mag * jnp.cos(a_im * dt * tau)
    pi = mag * jnp.sin(a_im * dt * tau)
    lr, li = pr[1:2], pi[1:2]
    den = a_re * a_re + a_im * a_im
    cr = ((lr - 1.0) * a_re + li * a_im) / den
    ci = (li * a_re - (lr - 1.0) * a_im) / den
    row_g = lax.broadcasted_iota(jnp.int32, (LANES, ns), 0) // S5_GROUP
    col_g = lax.broadcasted_iota(jnp.int32, (LANES, ns), 1) // S5_STATE
    same = row_g == col_g
    bbr = jnp.where(same, cr * bre_ref[...] - ci * bim_ref[...], 0.0)
    bbi = jnp.where(same, cr * bim_ref[...] + ci * bre_ref[...], 0.0)
    for s in range(T):
        e = s if reverse else T - 1 - s
        rows = slice(s * LANES, (s + 1) * LANES)
        win_ref[rows, :ns] = (pr[e:e + 1] * bbr - pi[e:e + 1] * bbi).astype(BF16)
        win_ref[rows, ns:] = (pr[e:e + 1] * bbi + pi[e:e + 1] * bbr).astype(BF16)
    lam_ref[...] = jnp.broadcast_to(jnp.concatenate([pr[T:T + 1], pi[T:T + 1]], axis=1), lam_ref.shape)
    dt_c = jnp.exp(colp_ref[2])
    mag_c = jnp.exp(colp_ref[0] * dt_c)
    lr_c = mag_c * jnp.cos(colp_ref[1] * dt_c)
    li_c = mag_c * jnp.sin(colp_ref[1] * dt_c)
    row_gc = lax.broadcasted_iota(jnp.int32, (ns, LANES), 0) // S5_STATE
    col_gc = lax.broadcasted_iota(jnp.int32, (ns, LANES), 1) // S5_GROUP
    same_c = row_gc == col_gc
    c_re = jnp.where(same_c, cre_ref[...], 0.0)
    c_im = jnp.where(same_c, cim_ref[...], 0.0)
    bb = jnp.concatenate([bbr, bbi], axis=1)
    cur_r = jnp.ones((ns, LANES), F32)
    cur_i = jnp.zeros((ns, LANES), F32)
    kblk = []
    for e in range(T + 1):
        wo_r = c_re * cur_r - c_im * cur_i
        wo_i = -(c_re * cur_i + c_im * cur_r)
        if e >= 1:
            t = (T - e) if reverse else e - 1
            wout_ref[:ns, t * LANES:(t + 1) * LANES] = wo_r.astype(BF16)
            wout_ref[ns:, t * LANES:(t + 1) * LANES] = wo_i.astype(BF16)
        if e < T:
            k = jnp.dot(bb, jnp.concatenate([wo_r, wo_i], axis=0), precision=hi, preferred_element_type=F32)
            kblk.append(k.astype(BF16))
        cur_r, cur_i = cur_r * lr_c - cur_i * li_c, cur_r * li_c + cur_i * lr_c
    zeros = jnp.zeros((LANES, LANES), BF16)
    for s in range(T):
        for t in range(T):
            lag = (s - t) if reverse else (t - s)
            tt_ref[s * LANES:(s + 1) * LANES, t * LANES:(t + 1) * LANES] = kblk[lag] if lag >= 0 else zeros


def _s5_prep(a_re, a_im, log_dt, b_re, b_im, c_re, c_im, reverse):
    g, p = a_re.shape
    cdim = b_re.shape[2]
    nj = g // S5_GPB
    ns = S5_GPB * p
    width = S5_SUB * LANES
    prm = jnp.zeros((nj, SUBLANES, ns), F32)
    prm = prm.at[:, 0].set(a_re.reshape(nj, ns)).at[:, 1].set(a_im.reshape(nj, ns))
    prm = prm.at[:, 2].set(jnp.repeat(log_dt, p).reshape(nj, ns))
    colp = jnp.broadcast_to(prm[:, :3, :, None], (nj, 3, ns, LANES))
    row_tile = lambda b: jnp.tile(b.reshape(nj, S5_GPB, p, cdim).transpose(0, 3, 1, 2).reshape(nj, 1, cdim, ns),
                                  (1, S5_GPB, 1, 1)).reshape(nj, LANES, ns)
    col_tile = lambda c_: jnp.tile(c_.reshape(nj, S5_GPB, cdim, p).transpose(0, 1, 3, 2).reshape(nj, ns, 1, cdim),
                                   (1, 1, S5_GPB, 1)).reshape(nj, ns, LANES)
    blk = lambda *shape: pl.BlockSpec((None,) + shape, lambda j: (j,) + (0,) * len(shape))
    return pl.pallas_call(
        functools.partial(_s5_prep_kernel, reverse=reverse),
        grid=(nj,),
        in_specs=[blk(SUBLANES, ns), blk(LANES, ns), blk(LANES, ns), blk(3, ns, LANES), blk(ns, LANES),
                  blk(ns, LANES)],
        out_specs=[blk(width, 2 * ns), blk(2 * ns, width), blk(width, width), blk(SUBLANES, 2 * ns)],
        out_shape=[jax.ShapeDtypeStruct((nj, width, 2 * ns), BF16), jax.ShapeDtypeStruct((nj, 2 * ns, width), BF16),
                   jax.ShapeDtypeStruct((nj, width, width), BF16), jax.ShapeDtypeStruct((nj, SUBLANES, 2 * ns), F32)],
        compiler_params=_cparams(("arbitrary",), 56),
        name="s5_prep_bwd" if reverse else "s5_prep_fwd",
    )(prm, row_tile(b_re), row_tile(b_im), colp, col_tile(c_re), col_tile(c_im))


def _rows(*vs, width):
    out = jnp.zeros((SUBLANES, width), F32)
    for i, v in enumerate(vs):
        out = out.at[i].set(v.astype(F32))
    return out


def kernel(x, c, ctx, c_ctx, ada_w, ada_b, norm_mix_g, norm_ffn_g, ffn_w_gate, ffn_w_up, ffn_w_down, final_norm_g, ev_w_in, ev_w_out, ssd_conv_w, ssd_conv_b, ssd_dt_bias, ssd_a_log, ssd_d, ssd_norm_g, hg_lb_logits, hg_norm_g, od_w_in, s5_a_re, s5_a_im, s5_log_dt, s5_b_re, s5_b_im, s5_c_re, s5_c_im, s5_d, od_w_val, od_w_gate):
    d = x.shape[-1]
    lat = x[0].astype(F32)
    hc = ctx[0].astype(F32)

    m = _ada(_rows(c[0], c_ctx, width=d), ada_w, ada_b)

    def ada_vecs(layer, stream):
        return [m[layer, stream, i * d:(i + 1) * d] for i in range(6)]

    w = ev_w_in[0]
    n_x = SSD_HEADS * SSD_HEAD_DIM
    n_xbc = n_x + 2 * SSD_GROUPS * SSD_STATE
    n_hg = HG_HEADS * HG_HEAD_DIM
    o_z, o_xbc, o_dt = 0, n_x, n_x + n_xbc
    o_q = o_dt + 2 * SSD_HEADS
    o_f = o_q + n_hg
    o_v = o_f + 2 * n_hg
    o_g = o_v + n_hg
    pad = lambda n: jnp.zeros((d, n), F32)
    w_in = jnp.concatenate([
        w[:, o_xbc:o_xbc + n_xbc],
        w[:, o_dt:o_dt + SSD_HEADS], pad(LANES - SSD_HEADS),
        w[:, o_dt + SSD_HEADS:o_dt + 2 * SSD_HEADS], pad(LANES - SSD_HEADS),
        pad(ZS_Q - ZS_XD),
        w[:, o_q:o_q + n_hg], w[:, o_v:o_v + n_hg], w[:, o_f:o_f + 2 * n_hg],
        w[:, o_z:o_z + n_x], w[:, o_g:o_g + n_hg]], axis=1).astype(BF16)
    cw = jnp.zeros((SUBLANES, n_xbc), F32).at[:SSD_CONV].set(ssd_conv_w[0])
    cb = ssd_conv_b[0].reshape(1, n_xbc)
    dtc = jnp.zeros((2, SUBLANES, LANES), F32)
    dtc = dtc.at[:, 0, :SSD_HEADS].set(ssd_dt_bias[0]).at[:, 1, :SSD_HEADS].set(ssd_a_log[0])
    dsk = jnp.repeat(ssd_d[0], SSD_HEAD_DIM).reshape(1, n_x)
    lower = jnp.cumsum(jax.nn.softmax(hg_lb_logits.astype(F32), axis=0), axis=0)[0].reshape(1, n_hg)
    head_of_lane = jnp.arange(n_x) // SSD_HEAD_DIM
    e16 = (jnp.arange(LANES)[:, None] == head_of_lane[None, :]).astype(BF16)
    consts = (dtc, dsk, e16)
    w_out = ev_w_out[0].astype(BF16)
    out_vec = lambda gate: _rows(ssd_norm_g[0], jnp.tile(hg_norm_g[0], HG_HEADS), gate, width=d)
    wg0, wu0, wd0 = (t_[0].astype(BF16) for t_ in (ffn_w_gate, ffn_w_up, ffn_w_down))

    s_state = jnp.zeros((2, SSD_GROUPS, SSD_STATE, n_x // SSD_GROUPS), F32)
    g_state = jnp.zeros((2, HG_HEADS, HG_HEAD_DIM, HG_HEAD_DIM), F32)
    layer0 = {}
    for stream, h in ((1, hc), (0, lat)):
        sm, scm, gm, sf, scf, gf = ada_vecs(0, stream)
        za, zb = _even_in(h, _rows(norm_mix_g[0], scm, sm, width=d), w_in, cw, cb, dtc, lower)
        yo_f, yo_b, s_state, g_state = _even_scan(za, zb, consts, s_state, g_state)
        layer0[stream] = _even_out(h, zb, yo_f, yo_b, w_out, out_vec(gm),
                                   _rows(norm_ffn_g[0], scf, sf, gf, width=d), wg0, wu0, wd0)

    lat, hc = layer0[0], layer0[1]
    w_s5 = od_w_in[0].astype(BF16)
    u = {}
    for stream, h, gw in ((1, hc, 1), (0, lat, GRID_W)):
        sm, scm = ada_vecs(1, stream)[:2]
        u[stream] = _odd_in(h, _rows(norm_mix_g[1], scm, sm, width=d), w_s5, gw)
    mats = [_s5_prep(s5_a_re[0, di], s5_a_im[0, di], s5_log_dt[0, di], s5_b_re[0, di], s5_b_im[0, di],
                     s5_c_re[0, di], s5_c_im[0, di], reverse=bool(di)) for di in range(2)]
    lam = jnp.stack([mats[0][3][:, 0], mats[1][3][:, 0]])
    carry = jnp.zeros_like(lam)
    for stream in (1, 0):
        s_f, s_b = (_s5_inject(u[stream], mats[di][0]) for di in range(2))
        p_f, p_b, carry = _s5_carry(s_f, s_b, lam, carry)
    nj = u[0].shape[0]
    y = _s5_readout(u[0], p_f, mats[0][1], mats[0][2], s5_d[0].reshape(nj, 1, LANES), reverse=False)
    y = _s5_readout(u[0], p_b, mats[1][1], mats[1][2], y, reverse=True)
    _, _, gm, sf, scf, gf = ada_vecs(1, 0)
    wg1, wu1, wd1 = (t_[1].astype(BF16) for t_ in (ffn_w_gate, ffn_w_up, ffn_w_down))
    out = _odd_out(lat, y, od_w_val[0].astype(BF16), od_w_gate[0].astype(BF16), _rows(gm, width=d),
                   _rows(norm_ffn_g[1], scf, sf, gf, final_norm_g, width=d), wg1, wu1, wd1, GRID_W)
    return out[None].astype(x.dtype)
```

```python
import functools
import math

import jax
import jax.numpy as jnp
from jax import lax
from jax.experimental import pallas as pl
from jax.experimental.pallas import tpu as pltpu

F32 = jnp.float32
BF16 = jnp.bfloat16
EPS = 1e-6

LANES = 128
SUBLANES = 8
GRID_W = 64
SCAN_CHUNK = 64
SSD_HEADS = 16
SSD_HEAD_DIM = 64
SSD_STATE = 128
SSD_GROUPS = 2
SSD_CONV = 5
HG_HEADS = 8
HG_HEAD_DIM = 128
S5_GROUP = 16
S5_STATE = 64
S5_SUB = 16
S5_GPB = LANES // S5_GROUP

ZS_XBC = 0
ZS_DT = 1536
ZS_XD = 1792
ZS_Q = 2048
ZS_V = 3072
ZS_F = 4096
ZS_Z = 6144
ZS_G = 7168
ZS_WIDTH = 8192


def _cparams(semantics, vmem_mb):
    return pltpu.CompilerParams(dimension_semantics=semantics, vmem_limit_bytes=vmem_mb * 1024 * 1024)


def _silu(x):
    return x * jax.nn.sigmoid(x)


def _modulate(h, mod_ref):
    ms = jnp.mean(h * h, axis=-1, keepdims=True)
    return h * lax.rsqrt(ms + EPS) * (mod_ref[0:1, :] * (1.0 + mod_ref[1:2, :])) + mod_ref[2:3, :]


def _split_dot(x, w, passes):
    acc, rem = None, x
    for _ in range(passes):
        hi = rem.astype(BF16)
        d = jnp.dot(hi, w, preferred_element_type=F32)
        acc = d if acc is None else acc + d
        rem = rem - hi.astype(F32)
    return acc


def _chunk_cumsum(x, reverse):
    n = x.shape[0]
    pos = lax.broadcasted_iota(jnp.int32, (n, 1), 0) % SCAN_CHUNK
    s = 1
    while s < SCAN_CHUNK:
        if reverse:
            x = x + jnp.where(pos < SCAN_CHUNK - s, pltpu.roll(x, n - s, axis=0), 0.0)
        else:
            x = x + jnp.where(pos >= s, pltpu.roll(x, s, axis=0), 0.0)
        s *= 2
    return x


def _ada_kernel(c_ref, w_ref, b_ref, o_ref):
    o_ref[...] = jnp.dot(_silu(c_ref[...]), w_ref[...], precision=lax.Precision.HIGHEST,
                         preferred_element_type=F32) + b_ref[...]


def _ada(cvecs, ada_w, ada_b):
    depth, d, n = ada_w.shape
    tn = n // 4
    return pl.pallas_call(
        _ada_kernel,
        grid=(depth, n // tn),
        in_specs=[pl.BlockSpec((SUBLANES, d), lambda l, j: (0, 0)),
                  pl.BlockSpec((None, d, tn), lambda l, j: (l, 0, j)),
                  pl.BlockSpec((None, 1, tn), lambda l, j: (l, 0, j))],
        out_specs=pl.BlockSpec((None, SUBLANES, tn), lambda l, j: (l, 0, j)),
        out_shape=jax.ShapeDtypeStruct((depth, SUBLANES, n), F32),
        compiler_params=_cparams(("arbitrary", "arbitrary"), 40),
        name="ada",
    )(cvecs, ada_w, ada_b.reshape(depth, 1, n))


EVEN_TN = 1024
HALO = 2 * SUBLANES
EVEN_TILES = (("conv", 0, None), ("conv_dt", 1, None), ("silu", None, 0), ("copy", None, 1),
              ("gate", 2, 4), ("gate", 3, 5), ("silu", None, 2), ("silu", None, 3))


def _even_in_kernel(h_ref, hp_ref, hn_ref, mod_ref, w_ref, cw_ref, cb_ref, dtc_ref, lb_ref, za_ref, zb_ref, a_ref):
    i = pl.program_id(0)
    tm = h_ref.shape[0]
    n_lb = lb_ref.shape[1]
    a_ref[:HALO] = _modulate(hp_ref[...], mod_ref).astype(BF16)
    a_ref[HALO:HALO + tm] = _modulate(h_ref[...], mod_ref).astype(BF16)
    a_ref[HALO + tm:] = _modulate(hn_ref[...], mod_ref).astype(BF16)
    row = lax.broadcasted_iota(jnp.int32, (tm + 2 * HALO, 1), 0)
    outside = ((row < HALO) & (i == 0)) | ((row >= HALO + tm) & (i == pl.num_programs(0) - 1))

    def conv_silu(col0, lanes):
        acc = jnp.dot(a_ref[...], w_ref[:, col0:col0 + lanes], preferred_element_type=F32)
        acc = jnp.where(outside, 0.0, acc)
        conv = cb_ref[:, col0:col0 + lanes]
        for k in range(SSD_CONV):
            off = HALO - SSD_CONV // 2 + k
            conv = conv + cw_ref[k:k + 1, col0:col0 + lanes] * acc[off:off + tm]
        return _silu(conv)

    def main_dot(col0, lanes=EVEN_TN):
        return jnp.dot(a_ref[HALO:HALO + tm], w_ref[:, col0:col0 + lanes], preferred_element_type=F32)

    for tile, (kind, fa, fb) in enumerate(EVEN_TILES):
        col0 = tile * EVEN_TN
        if kind == "conv":
            za_ref[:, fa * EVEN_TN:(fa + 1) * EVEN_TN] = conv_silu(col0, EVEN_TN)
        elif kind == "conv_dt":
            n_c = ZS_DT - col0
            za_ref[:, col0:ZS_DT] = conv_silu(col0, n_c)
            dt = main_dot(ZS_DT, 2 * LANES)
            for d in range(2):
                za_ref[:, ZS_DT + d * LANES:ZS_DT + (d + 1) * LANES] = jax.nn.softplus(
                    dt[:, d * LANES:(d + 1) * LANES] + dtc_ref[d, 0:1, :])
            za_ref[:, ZS_XD:(fa + 1) * EVEN_TN] = jnp.zeros((tm, (fa + 1) * EVEN_TN - ZS_XD), F32)
        elif kind == "gate":
            lb = lb_ref[:, (col0 - ZS_F) % n_lb:(col0 - ZS_F) % n_lb + EVEN_TN]
            f = lb + (1.0 - lb) * jax.nn.sigmoid(main_dot(col0))
            backward = (col0 - ZS_F) // n_lb == 1
            za_ref[:, fa * EVEN_TN:(fa + 1) * EVEN_TN] = _chunk_cumsum(jnp.log(f), backward)
            zb_ref[:, fb * EVEN_TN:(fb + 1) * EVEN_TN] = (1.0 - f).astype(BF16)
        elif kind == "silu":
            zb_ref[:, fb * EVEN_TN:(fb + 1) * EVEN_TN] = _silu(main_dot(col0)).astype(BF16)
        else:
            zb_ref[:, fb * EVEN_TN:(fb + 1) * EVEN_TN] = main_dot(col0).astype(BF16)


def _even_in(h, mod, w, cw, cb, dtc, lb):
    t, d = h.shape
    tm = min(t, 512)
    hb = tm // HALO
    n_a = (1 + max(fa for _, fa, _ in EVEN_TILES if fa is not None)) * EVEN_TN
    n_b = (1 + max(fb for _, _, fb in EVEN_TILES if fb is not None)) * EVEN_TN
    whole = lambda a, **kw: pl.BlockSpec(a.shape, lambda i: (0,) * a.ndim, **kw)
    return pl.pallas_call(
        _even_in_kernel,
        grid=(t // tm,),
        in_specs=[pl.BlockSpec((tm, d), lambda i: (i, 0)),
                  pl.BlockSpec((HALO, d), lambda i: (jnp.maximum(i * hb - 1, 0), 0)),
                  pl.BlockSpec((HALO, d), lambda i: (jnp.minimum((i + 1) * hb, t // HALO - 1), 0)),
                  whole(mod), whole(w, pipeline_mode=pl.Buffered(1)),
                  whole(cw), whole(cb), whole(dtc), whole(lb)],
        out_specs=[pl.BlockSpec((tm, n_a), lambda i: (i, 0)), pl.BlockSpec((tm, n_b), lambda i: (i, 0))],
        out_shape=[jax.ShapeDtypeStruct((t, n_a), F32), jax.ShapeDtypeStruct((t, n_b), BF16)],
        scratch_shapes=[pltpu.VMEM((tm + 2 * HALO, d), BF16)],
        compiler_params=_cparams(("arbitrary",), 56),
        name="even_in",
    )(h, h, h, mod, w, cw, cb, dtc, lb)


def _scan_block_prep(xd_ref, qv_ref, bcum_ref, kk_ref, dtc_ref, d):
    n_hg = bcum_ref.shape[1]
    xbc = xd_ref[:, :ZS_DT]
    dtv = xd_ref[:, ZS_DT + d * LANES:ZS_DT + (d + 1) * LANES]
    acum = _chunk_cumsum(-dtv * jnp.exp(dtc_ref[d, 1:2, :]), bool(d))
    return xbc, dtv, acum, kk_ref[...].astype(F32), bcum_ref[...], qv_ref[:, :n_hg].astype(F32), qv_ref[:, n_hg:]


DECAY_LIMIT = 80.0


def _hg_intra_exact(qs, bcum, bcum_ref, kk_ref, v_ref, acc_ref, row0, reverse):
    Q = SCAN_CHUNK
    acc_ref[...] = jnp.zeros_like(acc_ref)
    t_idx = lax.broadcasted_iota(jnp.int32, (Q, 1), 0)

    def body(s, carry):
        r = row0 + s
        b_s, k_s, v_s = bcum_ref[pl.ds(r, 1), :], kk_ref[pl.ds(r, 1), :], v_ref[pl.ds(r, 1), :]
        allowed = (t_idx <= s) if reverse else (t_idx >= s)
        w = jnp.where(allowed, jnp.exp(jnp.minimum(bcum - b_s, 0.0)) * (qs * k_s), 0.0)
        for h in range(HG_HEADS):
            sl = slice(h * HG_HEAD_DIM, (h + 1) * HG_HEAD_DIM)
            acc_ref[:, sl] += jnp.sum(w[:, sl], axis=-1, keepdims=True) * v_s[:, sl]
        return carry

    lax.fori_loop(0, Q, body, 0)
    return acc_ref[...]


def _scan_chunk(prep, rows, dsk_ref, e_ref, st_ref, gt_ref, yo_ref, d, reverse, exact=None):
    Q = SCAN_CHUNK
    xbc, dtv, acum, kk, bcum, qs, vb = (a[rows] for a in prep)
    ti = lax.broadcasted_iota(jnp.int32, (Q, Q), 0)
    si = lax.broadcasted_iota(jnp.int32, (Q, Q), 1)
    mask = (ti <= si) if reverse else (ti >= si)
    edge = 0 if reverse else Q - 1
    width = SSD_HEADS * SSD_HEAD_DIM
    gw = width // SSD_GROUPS
    xs = xbc[:, :width]

    last = acum[edge:edge + 1, :]
    acum_t = acum.T
    e_a = jnp.exp(acum)
    w_end = jnp.exp(last - acum)
    e_last = jnp.broadcast_to(jnp.exp(last), (SUBLANES, LANES))
    xe = _split_dot(jnp.concatenate([dtv, e_a, w_end], axis=0), e_ref[...], 1)
    dt_x, ea_x, wend_x = xe[:Q], xe[Q:2 * Q], xe[2 * Q:3 * Q]
    elast_x = _split_dot(e_last, e_ref[...], 3)[0:1]
    xdt = xs * dt_x
    xw = (xdt * wend_x).astype(BF16)
    xdt_b = xdt.astype(BF16)
    lane = lax.broadcasted_iota(jnp.int32, (Q, LANES), 1)
    lo = lane < SSD_HEAD_DIM
    zero_b = jnp.zeros((Q, LANES), BF16)
    ys = []
    for g in range(SSD_GROUPS):
        bm = xbc[:, width + g * SSD_STATE:width + (g + 1) * SSD_STATE].astype(BF16)
        cm = xbc[:, width + (SSD_GROUPS + g) * SSD_STATE:width + (SSD_GROUPS + g + 1) * SSD_STATE].astype(BF16)
        cb = lax.dot_general(cm, bm, (((1,), (1,)), ((), ())), preferred_element_type=F32)
        y_state = jnp.dot(cm, st_ref[d, g].astype(BF16), preferred_element_type=F32)
        pairs = []
        for hp in range(gw // LANES):
            xp = xdt_b[:, g * gw + hp * LANES:g * gw + (hp + 1) * LANES]
            acc = None
            for half in range(2):
                h = (g * gw + hp * LANES) // SSD_HEAD_DIM + half
                seg = acum[:, h:h + 1] - acum_t[h:h + 1, :]
                m = jnp.where(mask, cb * jnp.exp(seg), 0.0).astype(BF16)
                xh = jnp.where(lo, xp, zero_b) if half == 0 else jnp.where(lo, zero_b, xp)
                part = jnp.dot(m, xh, preferred_element_type=F32)
                acc = part if acc is None else acc + part
            pairs.append(acc)
        y_g = jnp.concatenate(pairs, axis=1) + y_state * ea_x[:, g * gw:(g + 1) * gw]
        ys.append(y_g)
        upd = lax.dot_general(bm, xw[:, g * gw:(g + 1) * gw], (((0,), (0,)), ((), ())),
                              preferred_element_type=F32)
        st_ref[d, g] = st_ref[d, g] * elast_x[:, g * gw:(g + 1) * gw] + upd
    y = jnp.concatenate(ys, axis=1)
    if not reverse:
        y = y + dsk_ref[...] * xs
    yo_ref[rows, :width] = y.astype(yo_ref.dtype)

    blast = bcum[edge:edge + 1, :]
    qe = (qs * jnp.exp(bcum)).astype(BF16)
    k_end = kk * jnp.exp(blast - bcum)
    if exact is None:
        k_til = (k_end * jnp.exp(-blast)).astype(BF16)
    else:
        intra = _hg_intra_exact(qs, bcum, *exact, rows.start, reverse)
    k_end = k_end.astype(BF16)
    e_blast = jnp.exp(blast)
    outs = []
    for h in range(HG_HEADS):
        sl = slice(h * HG_HEAD_DIM, (h + 1) * HG_HEAD_DIM)
        gt = gt_ref[d, h]
        if exact is None:
            att = lax.dot_general(qe[:, sl], k_til[:, sl], (((1,), (1,)), ((), ())), preferred_element_type=F32)
            att = jnp.where(mask, att, 0.0).astype(BF16)
            o_h = jnp.dot(att, vb[:, sl], preferred_element_type=F32)
        else:
            o_h = intra[:, sl]
        o_h = o_h + lax.dot_general(qe[:, sl], gt.astype(BF16), (((1,), (1,)), ((), ())),
                                    preferred_element_type=F32)
        outs.append(o_h)
        upd = lax.dot_general(vb[:, sl], k_end[:, sl], (((0,), (0,)), ((), ())), preferred_element_type=F32)
        gt_ref[d, h] = gt * e_blast[:, sl] + upd
    yo_ref[rows, width:] = jnp.concatenate(outs, axis=1).astype(yo_ref.dtype)


def _even_scan_kernel(*refs, nblk, cpb):
    fwd_in, bwd_in = refs[0:4], refs[4:8]
    dtc_ref, dsk_ref, e_ref, h0s_ref, h0g_ref = refs[8:13]
    yof_ref, yob_ref, st_ref, gt_ref, kkf_ref, vf_ref, acc_ref = refs[13:20]
    c = pl.program_id(0)

    @pl.when(c == 0)
    def _():
        st_ref[...] = h0s_ref[...]
        gt_ref[...] = h0g_ref[...]

    streams = ((False, fwd_in, yof_ref), (True, bwd_in, yob_ref))

    def run(exact):
        preps = [_scan_block_prep(*ins, dtc_ref, d) for d, (_, ins, _) in enumerate(streams)]
        if exact:
            for d, (_, ins, _) in enumerate(streams):
                kkf_ref[d] = preps[d][3]
                vf_ref[d] = preps[d][6].astype(F32)
        for step in range(cpb):
            for d, (reverse, ins, yo_ref) in enumerate(streams):
                ci = (cpb - 1 - step) if reverse else step
                rows = slice(ci * SCAN_CHUNK, (ci + 1) * SCAN_CHUNK)
                refs_exact = (ins[2], kkf_ref.at[d], vf_ref.at[d], acc_ref) if exact else None
                _scan_chunk(preps[d], rows, dsk_ref, e_ref, st_ref, gt_ref, yo_ref, d, reverse, refs_exact)

    strongest = jnp.minimum(jnp.min(fwd_in[2][...]), jnp.min(bwd_in[2][...]))
    pl.when(strongest >= -DECAY_LIMIT)(lambda: run(False))
    pl.when(strongest < -DECAY_LIMIT)(lambda: run(True))


def _even_scan(za, zb, consts, h0s, h0g):
    t = za.shape[0]
    nch = t // SCAN_CHUNK
    cpb = max(k for k in (4, 2, 1) if nch % k == 0)
    rows = cpb * SCAN_CHUNK
    nblk = nch // cpb
    dtc, dsk, e16 = consts
    n_x = SSD_HEADS * SSD_HEAD_DIM
    n_hg = HG_HEADS * HG_HEAD_DIM
    gate_a, gate_b = (tuple(tile[i] for tile in EVEN_TILES if tile[0] == "gate") for i in (1, 2))

    def stream_specs(bidx, d):
        return [
            pl.BlockSpec((rows, ZS_XD), lambda c: (bidx(c), 0)),
            pl.BlockSpec((rows, 2 * n_hg), lambda c: (bidx(c), 0)),
            pl.BlockSpec((rows, n_hg), lambda c: (bidx(c), gate_a[d])),
            pl.BlockSpec((rows, n_hg), lambda c: (bidx(c), gate_b[d])),
        ]

    fwd = lambda c: c
    bwd = lambda c: nblk - 1 - c
    whole = lambda a: pl.BlockSpec(a.shape, lambda c: (0,) * a.ndim)
    in_specs = stream_specs(fwd, 0) + stream_specs(bwd, 1) + [whole(a) for a in (dtc, dsk, e16, h0s, h0g)]
    out_specs = [pl.BlockSpec((rows, n_x + n_hg), lambda c: (c, 0)),
                 pl.BlockSpec((rows, n_x + n_hg), lambda c: (bwd(c), 0)), whole(h0s), whole(h0g)]
    out_shape = [jax.ShapeDtypeStruct((t, n_x + n_hg), BF16), jax.ShapeDtypeStruct((t, n_x + n_hg), BF16),
                 jax.ShapeDtypeStruct(h0s.shape, F32), jax.ShapeDtypeStruct(h0g.shape, F32)]
    return pl.pallas_call(
        functools.partial(_even_scan_kernel, nblk=nblk, cpb=cpb),
        grid=(nblk,),
        in_specs=in_specs,
        out_specs=out_specs,
        out_shape=out_shape,
        scratch_shapes=[pltpu.VMEM((2, rows, n_hg), F32), pltpu.VMEM((2, rows, n_hg), F32),
                        pltpu.VMEM((SCAN_CHUNK, n_hg), F32)],
        compiler_params=_cparams(("arbitrary",), 48),
        name="even_scan",
    )(za, zb, za, zb, za, zb, za, zb, dtc, dsk, e16, h0s, h0g)


def _ffn_block(h, mod_ref, wg_ref, wu_ref, wd_ref, final):
    a = _modulate(h, mod_ref).astype(BF16)
    gate = jnp.dot(a, wg_ref[...], preferred_element_type=F32)
    up = jnp.dot(a, wu_ref[...], preferred_element_type=F32)
    act = (_silu(gate) * up).astype(BF16)
    out = h + mod_ref[3:4, :] * jnp.dot(act, wd_ref[...], preferred_element_type=F32)
    if final:
        ms = jnp.mean(out * out, axis=-1, keepdims=True)
        out = out * lax.rsqrt(ms + EPS) * mod_ref[4:5, :]
    return out


def _resident(a):
    return pl.BlockSpec(a.shape, lambda i: (0,) * a.ndim, pipeline_mode=pl.Buffered(1))


def _even_out_kernel(h_ref, z_ref, g_ref, yof_ref, yob_ref, w_ref, vec_ref, mod_ref, wg_ref, wu_ref, wd_ref, o_ref):
    n_x = z_ref.shape[1]
    yo = yof_ref[...].astype(F32) + yob_ref[...].astype(F32)
    y = yo[:, :n_x] * z_ref[...].astype(F32)
    o = yo[:, n_x:]
    gw = n_x // SSD_GROUPS
    parts = []
    for g in range(SSD_GROUPS):
        yg = y[:, g * gw:(g + 1) * gw]
        ms = jnp.mean(yg * yg, axis=-1, keepdims=True)
        parts.append(yg * lax.rsqrt(ms + EPS))
    yn = (jnp.concatenate(parts, axis=1) * vec_ref[0:1, :]).astype(BF16)
    parts = []
    for hh in range(HG_HEADS):
        oh = o[:, hh * HG_HEAD_DIM:(hh + 1) * HG_HEAD_DIM]
        ms = jnp.mean(oh * oh, axis=-1, keepdims=True)
        parts.append(oh * lax.rsqrt(ms + EPS))
    on = (jnp.concatenate(parts, axis=1) * vec_ref[1:2, :] * g_ref[...].astype(F32)).astype(BF16)
    half = yn.shape[1]
    mix = jnp.dot(yn, w_ref[:half, :], preferred_element_type=F32)
    mix = mix + jnp.dot(on, w_ref[half:, :], preferred_element_type=F32)
    h1 = h_ref[...] + vec_ref[2:3, :] * mix
    o_ref[...] = _ffn_block(h1, mod_ref, wg_ref, wu_ref, wd_ref, final=False)


def _even_out(h, zb, yo_f, yo_b, w_out, vec, mod, wg, wu, wd):
    t, d = h.shape
    tm = min(t, 256)
    n_x = SSD_HEADS * SSD_HEAD_DIM
    n_hg = HG_HEADS * HG_HEAD_DIM
    row = lambda i: (i, 0)
    z_tile, g_tile = (next(fb for k, (_, _, fb) in enumerate(EVEN_TILES) if k * EVEN_TN == start)
                      for start in (ZS_Z, ZS_G))
    return pl.pallas_call(
        _even_out_kernel,
        grid=(t // tm,),
        in_specs=[pl.BlockSpec((tm, d), row),
                  pl.BlockSpec((tm, n_x), lambda i: (i, z_tile)),
                  pl.BlockSpec((tm, n_hg), lambda i: (i, g_tile)),
                  pl.BlockSpec((tm, n_x + n_hg), row), pl.BlockSpec((tm, n_x + n_hg), row),
                  _resident(w_out), _resident(vec), _resident(mod), _resident(wg), _resident(wu), _resident(wd)],
        out_specs=pl.BlockSpec((tm, d), row),
        out_shape=jax.ShapeDtypeStruct((t, d), F32),
        compiler_params=_cparams(("arbitrary",), 56),
        name="even_out_ffn",
    )(h, zb, zb, yo_f, yo_b, w_out, vec, mod, wg, wu, wd)


def _odd_in_kernel(h_ref, mod_ref, w_ref, o_ref, u_ref):
    a = _modulate(h_ref[...], mod_ref).astype(BF16)
    u = jnp.dot(a, w_ref[...], preferred_element_type=F32)
    nj, grid_w, rb, _ = o_ref.shape
    for j in range(nj):
        u_ref[j] = u[:, j * LANES:(j + 1) * LANES]
    for wi in range(grid_w):
        for j in range(nj):
            o_ref[j, wi] = u_ref[j, pl.ds(wi, rb, stride=grid_w), :]


def _odd_in(h, mod, w, grid_w):
    t, d = h.shape
    rows = t // grid_w
    rb = min(rows, 4 * SUBLANES)
    n = w.shape[1]
    nj = n // LANES
    out = pl.pallas_call(
        _odd_in_kernel,
        grid=(rows // rb,),
        in_specs=[pl.BlockSpec((rb * grid_w, d), lambda i: (i, 0)),
                  pl.BlockSpec((SUBLANES, d), lambda i: (0, 0)),
                  pl.BlockSpec(w.shape, lambda i: (0, 0))],
        out_specs=pl.BlockSpec((nj, grid_w, rb, LANES), lambda i: (0, 0, i, 0)),
        out_shape=jax.ShapeDtypeStruct((nj, grid_w, rows, LANES), F32),
        scratch_shapes=[pltpu.VMEM((nj, rb * grid_w, LANES), F32)],
        compiler_params=_cparams(("arbitrary",), 48),
        name="odd_in",
    )(h, mod, w)
    return out.reshape(nj, t, LANES)


def _sub_chunk_rows(u_ref, nb):
    return [u_ref[pl.ds(s, nb, stride=S5_SUB), :] for s in range(S5_SUB)]


def _s5_inject_kernel(u_ref, win_ref, s_ref):
    x = jnp.concatenate(_sub_chunk_rows(u_ref, s_ref.shape[0]), axis=1).astype(BF16)
    s_ref[...] = jnp.dot(x, win_ref[...], preferred_element_type=F32)


def _s5_inject(u, win):
    nj, t, _ = u.shape
    n = t // S5_SUB
    nb = min(n, 512)
    width, ns = win.shape[1:]
    return pl.pallas_call(
        _s5_inject_kernel,
        grid=(nj, n // nb),
        in_specs=[pl.BlockSpec((None, nb * S5_SUB, LANES), lambda j, b: (j, b, 0)),
                  pl.BlockSpec((None, width, ns), lambda j, b: (j, 0, 0))],
        out_specs=pl.BlockSpec((None, nb, ns), lambda j, b: (j, b, 0)),
        out_shape=jax.ShapeDtypeStruct((nj, n, ns), F32),
        compiler_params=_cparams(("arbitrary", "arbitrary"), 40),
        name="s5_inject",
    )(u, win)


def _s5_carry_kernel(sf_ref, sb_ref, lam_ref, h0_ref, pf_ref, pb_ref, hf_ref, tf_ref, tb_ref, c_ref):
    b = pl.program_id(0)
    nj, kb, ns = sf_ref.shape
    half = ns // 2

    @pl.when(b == 0)
    def _():
        c_ref[...] = h0_ref[...]

    for j in range(nj):
        tf_ref[:, j, :] = sf_ref[j]
        tb_ref[:, j, :] = sb_ref[j]

    def advance(d, c, s):
        lr, li = lam_ref[d, :, :half], lam_ref[d, :, half:]
        pr, pi = c[:, :half], c[:, half:]
        return jnp.concatenate([lr * pr - li * pi + s[:, :half], lr * pi + li * pr + s[:, half:]], axis=1)

    def step(i, carry):
        cf, cb = carry
        kr = kb - 1 - i
        sf = tf_ref[i]
        sb = tb_ref[kr]
        tf_ref[i] = cf
        tb_ref[kr] = cb
        return advance(0, cf, sf), advance(1, cb, sb)

    cf, cb = lax.fori_loop(0, kb, step, (c_ref[0], c_ref[1]), unroll=4)
    c_ref[0] = cf
    c_ref[1] = cb
    for j in range(nj):
        pf_ref[j] = tf_ref[:, j, :]
        pb_ref[j] = tb_ref[:, j, :]

    @pl.when(b == pl.num_programs(0) - 1)
    def _():
        hf_ref[...] = c_ref[...]


def _s5_carry(s_f, s_b, lam, h0):
    nj, n, ns = s_f.shape
    kb = min(n, 256)
    nblk = n // kb
    fwd = pl.BlockSpec((nj, kb, ns), lambda b: (0, b, 0))
    bwd = pl.BlockSpec((nj, kb, ns), lambda b: (0, nblk - 1 - b, 0))
    whole = pl.BlockSpec(lam.shape, lambda b: (0, 0, 0))
    return pl.pallas_call(
        _s5_carry_kernel,
        grid=(nblk,),
        in_specs=[fwd, bwd, whole, whole],
        out_specs=[fwd, bwd, whole],
        out_shape=[jax.ShapeDtypeStruct(s_f.shape, F32), jax.ShapeDtypeStruct(s_b.shape, F32),
                   jax.ShapeDtypeStruct(lam.shape, F32)],
        scratch_shapes=[pltpu.VMEM((kb, nj, ns), F32), pltpu.VMEM((kb, nj, ns), F32), pltpu.VMEM(lam.shape, F32)],
        compiler_params=_cparams(("arbitrary",), 48),
        name="s5_carry",
    )(s_f, s_b, lam, h0)


def _s5_readout_kernel(u_ref, p_ref, wout_ref, tt_ref, add_ref, y_ref, *, reverse):
    nb = p_ref.shape[0]
    us = _sub_chunk_rows(u_ref, nb)
    xb = jnp.concatenate(us, axis=1).astype(BF16)
    pb = p_ref[...].astype(BF16)
    width = xb.shape[1]
    tile = 2 * LANES
    for m in range(width // tile):
        cols = slice(m * tile, (m + 1) * tile)
        rows = slice(m * tile, width) if reverse else slice(0, (m + 1) * tile)
        y = jnp.dot(pb, wout_ref[:, cols], preferred_element_type=F32)
        y = y + jnp.dot(xb[:, rows], tt_ref[rows, cols], preferred_element_type=F32)
        for q in range(tile // LANES):
            s = m * (tile // LANES) + q
            other = add_ref[pl.ds(s, nb, stride=S5_SUB), :] if reverse else add_ref[...] * us[s]
            y_ref[pl.ds(s, nb, stride=S5_SUB), :] = y[:, q * LANES:(q + 1) * LANES] + other


def _s5_readout(u, p, wout, tt, add, reverse):
    nj, t, _ = u.shape
    n = t // S5_SUB
    nb = min(n, 256)
    ns, width = wout.shape[1:]
    tokens = pl.BlockSpec((None, nb * S5_SUB, LANES), lambda j, b: (j, b, 0))
    add_spec = tokens if reverse else pl.BlockSpec((None, 1, LANES), lambda j, b: (j, 0, 0))
    return pl.pallas_call(
        functools.partial(_s5_readout_kernel, reverse=reverse),
        grid=(nj, n // nb),
        in_specs=[tokens,
                  pl.BlockSpec((None, nb, ns), lambda j, b: (j, b, 0)),
                  pl.BlockSpec((None, ns, width), lambda j, b: (j, 0, 0)),
                  pl.BlockSpec((None, width, width), lambda j, b: (j, 0, 0)),
                  add_spec],
        out_specs=tokens,
        out_shape=jax.ShapeDtypeStruct(u.shape, F32),
        compiler_params=_cparams(("arbitrary", "arbitrary"), 48),
        name="s5_readout_bwd" if reverse else "s5_readout_fwd",
    )(u, p, wout, tt, add)


def _odd_out_kernel(h_ref, y_ref, wv_ref, wgl_ref, vec_ref, mod_ref, wg_ref, wu_ref, wd_ref, o_ref, a_ref):
    nj, grid_w, rb, _ = y_ref.shape
    for wi in range(grid_w):
        for j in range(nj):
            a_ref[j, pl.ds(wi, rb, stride=grid_w), :] = y_ref[j, wi]
    a = jax.nn.gelu(jnp.concatenate([a_ref[j] for j in range(nj)], axis=1)).astype(BF16)
    val = jnp.dot(a, wv_ref[...], preferred_element_type=F32)
    gate = jnp.dot(a, wgl_ref[...], preferred_element_type=F32)
    h3 = h_ref[...] + vec_ref[0:1, :] * (val * jax.nn.sigmoid(gate))
    o_ref[...] = _ffn_block(h3, mod_ref, wg_ref, wu_ref, wd_ref, final=True)


def _odd_out(h, y, w_val, w_gate, vec, mod, wg, wu, wd, grid_w):
    t, d = h.shape
    rows = t // grid_w
    rb = min(rows, SUBLANES)
    nj = y.shape[0]
    tok = pl.BlockSpec((rb * grid_w, d), lambda i: (i, 0))
    return pl.pallas_call(
        _odd_out_kernel,
        grid=(rows // rb,),
        in_specs=[tok, pl.BlockSpec((nj, grid_w, rb, LANES), lambda i: (0, 0, i, 0)),
                  _resident(w_val), _resident(w_gate), _resident(vec), _resident(mod),
                  _resident(wg), _resident(wu), _resident(wd)],
        out_specs=tok,
        out_shape=jax.ShapeDtypeStruct((t, d), F32),
        scratch_shapes=[pltpu.VMEM((nj, rb * grid_w, LANES), F32)],
        compiler_params=_cparams(("arbitrary",), 56),
        name="odd_out_ffn",
    )(h, y.reshape(nj, grid_w, rows, LANES), w_val, w_gate, vec, mod, wg, wu, wd)


def _s5_prep_kernel(prm_ref, bre_ref, bim_ref, colp_ref, cre_ref, cim_ref, win_ref, wout_ref, tt_ref, lam_ref, *,
                    reverse):
    T = S5_SUB
    ns = prm_ref.shape[1]
    hi = lax.Precision.HIGHEST
    a_re, a_im, dt = prm_ref[0:1, :], prm_ref[1:2, :], jnp.exp(prm_ref[2:3, :])
    tau = lax.broadcasted_iota(jnp.int32, (3 * SUBLANES, ns), 0).astype(F32)
    mag = jnp.exp(a_re * dt * tau)
    pr = mag * jnp.cos(a_im * dt * tau)
    pi = mag * jnp.sin(a_im * dt * tau)
    lr, li = pr[1:2], pi[1:2]
    den = a_re * a_re + a_im * a_im
    cr = ((lr - 1.0) * a_re + li * a_im) / den
    ci = (li * a_re - (lr - 1.0) * a_im) / den
    row_g = lax.broadcasted_iota(jnp.int32, (LANES, ns), 0) // S5_GROUP
    col_g = lax.broadcasted_iota(jnp.int32, (LANES, ns), 1) // S5_STATE
    same = row_g == col_g
    bbr = jnp.where(same, cr * bre_ref[...] - ci * bim_ref[...], 0.0)
    bbi = jnp.where(same, cr * bim_ref[...] + ci * bre_ref[...], 0.0)
    for s in range(T):
        e = s if reverse else T - 1 - s
        rows = slice(s * LANES, (s + 1) * LANES)
        win_ref[rows, :ns] = (pr[e:e + 1] * bbr - pi[e:e + 1] * bbi).astype(BF16)
        win_ref[rows, ns:] = (pr[e:e + 1] * bbi + pi[e:e + 1] * bbr).astype(BF16)
    lam_ref[...] = jnp.broadcast_to(jnp.concatenate([pr[T:T + 1], pi[T:T + 1]], axis=1), lam_ref.shape)
    dt_c = jnp.exp(colp_ref[2])
    mag_c = jnp.exp(colp_ref[0] * dt_c)
    lr_c = mag_c * jnp.cos(colp_ref[1] * dt_c)
    li_c = mag_c * jnp.sin(colp_ref[1] * dt_c)
    row_gc = lax.broadcasted_iota(jnp.int32, (ns, LANES), 0) // S5_STATE
    col_gc = lax.broadcasted_iota(jnp.int32, (ns, LANES), 1) // S5_GROUP
    same_c = row_gc == col_gc
    c_re = jnp.where(same_c, cre_ref[...], 0.0)
    c_im = jnp.where(same_c, cim_ref[...], 0.0)
    bb = jnp.concatenate([bbr, bbi], axis=1)
    cur_r = jnp.ones((ns, LANES), F32)
    cur_i = jnp.zeros((ns, LANES), F32)
    kblk = []
    for e in range(T + 1):
        wo_r = c_re * cur_r - c_im * cur_i
        wo_i = -(c_re * cur_i + c_im * cur_r)
        if e >= 1:
            t = (T - e) if reverse else e - 1
            wout_ref[:ns, t * LANES:(t + 1) * LANES] = wo_r.astype(BF16)
            wout_ref[ns:, t * LANES:(t + 1) * LANES] = wo_i.astype(BF16)
        if e < T:
            k = jnp.dot(bb, jnp.concatenate([wo_r, wo_i], axis=0), precision=hi, preferred_element_type=F32)
            kblk.append(k.astype(BF16))
        cur_r, cur_i = cur_r * lr_c - cur_i * li_c, cur_r * li_c + cur_i * lr_c
    zeros = jnp.zeros((LANES, LANES), BF16)
    for s in range(T):
        for t in range(T):
            lag = (s - t) if reverse else (t - s)
            tt_ref[s * LANES:(s + 1) * LANES, t * LANES:(t + 1) * LANES] = kblk[lag] if lag >= 0 else zeros


def _s5_prep(a_re, a_im, log_dt, b_re, b_im, c_re, c_im, reverse):
    g, p = a_re.shape
    cdim = b_re.shape[2]
    nj = g // S5_GPB
    ns = S5_GPB * p
    width = S5_SUB * LANES
    prm = jnp.zeros((nj, SUBLANES, ns), F32)
    prm = prm.at[:, 0].set(a_re.reshape(nj, ns)).at[:, 1].set(a_im.reshape(nj, ns))
    prm = prm.at[:, 2].set(jnp.repeat(log_dt, p).reshape(nj, ns))
    colp = jnp.broadcast_to(prm[:, :3, :, None], (nj, 3, ns, LANES))
    row_tile = lambda b: jnp.tile(b.reshape(nj, S5_GPB, p, cdim).transpose(0, 3, 1, 2).reshape(nj, 1, cdim, ns),
                                  (1, S5_GPB, 1, 1)).reshape(nj, LANES, ns)
    col_tile = lambda c_: jnp.tile(c_.reshape(nj, S5_GPB, cdim, p).transpose(0, 1, 3, 2).reshape(nj, ns, 1, cdim),
                                   (1, 1, S5_GPB, 1)).reshape(nj, ns, LANES)
    blk = lambda *shape: pl.BlockSpec((None,) + shape, lambda j: (j,) + (0,) * len(shape))
    return pl.pallas_call(
        functools.partial(_s5_prep_kernel, reverse=reverse),
        grid=(nj,),
        in_specs=[blk(SUBLANES, ns), blk(LANES, ns), blk(LANES, ns), blk(3, ns, LANES), blk(ns, LANES),
                  blk(ns, LANES)],
        out_specs=[blk(width, 2 * ns), blk(2 * ns, width), blk(width, width), blk(SUBLANES, 2 * ns)],
        out_shape=[jax.ShapeDtypeStruct((nj, width, 2 * ns), BF16), jax.ShapeDtypeStruct((nj, 2 * ns, width), BF16),
                   jax.ShapeDtypeStruct((nj, width, width), BF16), jax.ShapeDtypeStruct((nj, SUBLANES, 2 * ns), F32)],
        compiler_params=_cparams(("arbitrary",), 56),
        name="s5_prep_bwd" if reverse else "s5_prep_fwd",
    )(prm, row_tile(b_re), row_tile(b_im), colp, col_tile(c_re), col_tile(c_im))


def _rows(*vs, width):
    out = jnp.zeros((SUBLANES, width), F32)
    for i, v in enumerate(vs):
        out = out.at[i].set(v.astype(F32))
    return out


def kernel(x, c, ctx, c_ctx, ada_w, ada_b, norm_mix_g, norm_ffn_g, ffn_w_gate, ffn_w_up, ffn_w_down, final_norm_g, ev_w_in, ev_w_out, ssd_conv_w, ssd_conv_b, ssd_dt_bias, ssd_a_log, ssd_d, ssd_norm_g, hg_lb_logits, hg_norm_g, od_w_in, s5_a_re, s5_a_im, s5_log_dt, s5_b_re, s5_b_im, s5_c_re, s5_c_im, s5_d, od_w_val, od_w_gate):
    d = x.shape[-1]
    lat = x[0].astype(F32)
    hc = ctx[0].astype(F32)

    m = _ada(_rows(c[0], c_ctx, width=d), ada_w, ada_b)

    def ada_vecs(layer, stream):
        return [m[layer, stream, i * d:(i + 1) * d] for i in range(6)]

    w = ev_w_in[0]
    n_x = SSD_HEADS * SSD_HEAD_DIM
    n_xbc = n_x + 2 * SSD_GROUPS * SSD_STATE
    n_hg = HG_HEADS * HG_HEAD_DIM
    o_z, o_xbc, o_dt = 0, n_x, n_x + n_xbc
    o_q = o_dt + 2 * SSD_HEADS
    o_f = o_q + n_hg
    o_v = o_f + 2 * n_hg
    o_g = o_v + n_hg
    pad = lambda n: jnp.zeros((d, n), F32)
    w_in = jnp.concatenate([
        w[:, o_xbc:o_xbc + n_xbc],
        w[:, o_dt:o_dt + SSD_HEADS], pad(LANES - SSD_HEADS),
        w[:, o_dt + SSD_HEADS:o_dt + 2 * SSD_HEADS], pad(LANES - SSD_HEADS),
        pad(ZS_Q - ZS_XD),
        w[:, o_q:o_q + n_hg], w[:, o_v:o_v + n_hg], w[:, o_f:o_f + 2 * n_hg],
        w[:, o_z:o_z + n_x], w[:, o_g:o_g + n_hg]], axis=1).astype(BF16)
    cw = jnp.zeros((SUBLANES, n_xbc), F32).at[:SSD_CONV].set(ssd_conv_w[0])
    cb = ssd_conv_b[0].reshape(1, n_xbc)
    dtc = jnp.zeros((2, SUBLANES, LANES), F32)
    dtc = dtc.at[:, 0, :SSD_HEADS].set(ssd_dt_bias[0]).at[:, 1, :SSD_HEADS].set(ssd_a_log[0])
    dsk = jnp.repeat(ssd_d[0], SSD_HEAD_DIM).reshape(1, n_x)
    lower = jnp.cumsum(jax.nn.softmax(hg_lb_logits.astype(F32), axis=0), axis=0)[0].reshape(1, n_hg)
    head_of_lane = jnp.arange(n_x) // SSD_HEAD_DIM
    e16 = (jnp.arange(LANES)[:, None] == head_of_lane[None, :]).astype(BF16)
    consts = (dtc, dsk, e16)
    w_out = ev_w_out[0].astype(BF16)
    out_vec = lambda gate: _rows(ssd_norm_g[0], jnp.tile(hg_norm_g[0], HG_HEADS), gate, width=d)
    wg0, wu0, wd0 = (t_[0].astype(BF16) for t_ in (ffn_w_gate, ffn_w_up, ffn_w_down))

    s_state = jnp.zeros((2, SSD_GROUPS, SSD_STATE, n_x // SSD_GROUPS), F32)
    g_state = jnp.zeros((2, HG_HEADS, HG_HEAD_DIM, HG_HEAD_DIM), F32)
    layer0 = {}
    for stream, h in ((1, hc), (0, lat)):
        sm, scm, gm, sf, scf, gf = ada_vecs(0, stream)
        za, zb = _even_in(h, _rows(norm_mix_g[0], scm, sm, width=d), w_in, cw, cb, dtc, lower)
        yo_f, yo_b, s_state, g_state = _even_scan(za, zb, consts, s_state, g_state)
        layer0[stream] = _even_out(h, zb, yo_f, yo_b, w_out, out_vec(gm),
                                   _rows(norm_ffn_g[0], scf, sf, gf, width=d), wg0, wu0, wd0)

    lat, hc = layer0[0], layer0[1]
    w_s5 = od_w_in[0].astype(BF16)
    u = {}
    for stream, h, gw in ((1, hc, 1), (0, lat, GRID_W)):
        sm, scm = ada_vecs(1, stream)[:2]
        u[stream] = _odd_in(h, _rows(norm_mix_g[1], scm, sm, width=d), w_s5, gw)
    mats = [_s5_prep(s5_a_re[0, di], s5_a_im[0, di], s5_log_dt[0, di], s5_b_re[0, di], s5_b_im[0, di],
                     s5_c_re[0, di], s5_c_im[0, di], reverse=bool(di)) for di in range(2)]
    lam = jnp.stack([mats[0][3][:, 0], mats[1][3][:, 0]])
    carry = jnp.zeros_like(lam)
    for stream in (1, 0):
        s_f, s_b = (_s5_inject(u[stream], mats[di][0]) for di in range(2))
        p_f, p_b, carry = _s5_carry(s_f, s_b, lam, carry)
    nj = u[0].shape[0]
    y = _s5_readout(u[0], p_f, mats[0][1], mats[0][2], s5_d[0].reshape(nj, 1, LANES), reverse=False)
    y = _s5_readout(u[0], p_b, mats[1][1], mats[1][2], y, reverse=True)
    _, _, gm, sf, scf, gf = ada_vecs(1, 0)
    wg1, wu1, wd1 = (t_[1].astype(BF16) for t_ in (ffn_w_gate, ffn_w_up, ffn_w_down))
    out = _odd_out(lat, y, od_w_val[0].astype(BF16), od_w_gate[0].astype(BF16), _rows(gm, width=d),
                   _rows(norm_ffn_g[1], scf, sf, gf, final_norm_g, width=d), wg1, wu1, wd1, GRID_W)
    return out[None].astype(x.dtype)
```

```python
import functools
import math

import jax
import jax.numpy as jnp
from jax import lax
from jax.experimental import pallas as pl
from jax.experimental.pallas import tpu as pltpu

F32 = jnp.float32
BF16 = jnp.bfloat16
EPS = 1e-6

LANES = 128
SUBLANES = 8
GRID_W = 64
SCAN_CHUNK = 64
SSD_HEADS = 16
SSD_HEAD_DIM = 64
SSD_STATE = 128
SSD_GROUPS = 2
SSD_CONV = 5
HG_HEADS = 8
HG_HEAD_DIM = 128
S5_GROUP = 16
S5_STATE = 64
S5_SUB = 16
S5_GPB = LANES // S5_GROUP

ZS_XBC = 0
ZS_DT = 1536
ZS_XD = 1792
ZS_Q = 2048
ZS_V = 3072
ZS_F = 4096
ZS_Z = 6144
ZS_G = 7168
ZS_WIDTH = 8192


def _cparams(semantics, vmem_mb):
    return pltpu.CompilerParams(dimension_semantics=semantics, vmem_limit_bytes=vmem_mb * 1024 * 1024)


def _silu(x):
    return x * jax.nn.sigmoid(x)


def _modulate(h, mod_ref):
    ms = jnp.mean(h * h, axis=-1, keepdims=True)
    return h * lax.rsqrt(ms + EPS) * (mod_ref[0:1, :] * (1.0 + mod_ref[1:2, :])) + mod_ref[2:3, :]


def _split_dot(x, w, passes):
    acc, rem = None, x
    for _ in range(passes):
        hi = rem.astype(BF16)
        d = jnp.dot(hi, w, preferred_element_type=F32)
        acc = d if acc is None else acc + d
        rem = rem - hi.astype(F32)
    return acc


def _chunk_cumsum(x, reverse):
    n = x.shape[0]
    pos = lax.broadcasted_iota(jnp.int32, (n, 1), 0) % SCAN_CHUNK
    s = 1
    while s < SCAN_CHUNK:
        if reverse:
            x = x + jnp.where(pos < SCAN_CHUNK - s, pltpu.roll(x, n - s, axis=0), 0.0)
        else:
            x = x + jnp.where(pos >= s, pltpu.roll(x, s, axis=0), 0.0)
        s *= 2
    return x


def _ada_kernel(c_ref, w_ref, b_ref, o_ref):
    o_ref[...] = jnp.dot(_silu(c_ref[...]), w_ref[...], precision=lax.Precision.HIGHEST,
                         preferred_element_type=F32) + b_ref[...]


def _ada(cvecs, ada_w, ada_b):
    depth, d, n = ada_w.shape
    tn = n // 4
    return pl.pallas_call(
        _ada_kernel,
        grid=(depth, n // tn),
        in_specs=[pl.BlockSpec((SUBLANES, d), lambda l, j: (0, 0)),
                  pl.BlockSpec((None, d, tn), lambda l, j: (l, 0, j)),
                  pl.BlockSpec((None, 1, tn), lambda l, j: (l, 0, j))],
        out_specs=pl.BlockSpec((None, SUBLANES, tn), lambda l, j: (l, 0, j)),
        out_shape=jax.ShapeDtypeStruct((depth, SUBLANES, n), F32),
        compiler_params=_cparams(("arbitrary", "arbitrary"), 40),
        name="ada",
    )(cvecs, ada_w, ada_b.reshape(depth, 1, n))


EVEN_TN = 1024
HALO = 2 * SUBLANES
EVEN_TILES = (("conv", 0, None), ("conv_dt", 1, None), ("silu", None, 0), ("copy", None, 1),
              ("gate", 2, 4), ("gate", 3, 5), ("silu", None, 2), ("silu", None, 3))


def _even_in_kernel(h_ref, hp_ref, hn_ref, mod_ref, w_ref, cw_ref, cb_ref, dtc_ref, lb_ref, za_ref, zb_ref, a_ref):
    i = pl.program_id(0)
    tm = h_ref.shape[0]
    n_lb = lb_ref.shape[1]
    a_ref[:HALO] = _modulate(hp_ref[...], mod_ref).astype(BF16)
    a_ref[HALO:HALO + tm] = _modulate(h_ref[...], mod_ref).astype(BF16)
    a_ref[HALO + tm:] = _modulate(hn_ref[...], mod_ref).astype(BF16)
    row = lax.broadcasted_iota(jnp.int32, (tm + 2 * HALO, 1), 0)
    outside = ((row < HALO) & (i == 0)) | ((row >= HALO + tm) & (i == pl.num_programs(0) - 1))

    def conv_silu(col0, lanes):
        acc = jnp.dot(a_ref[...], w_ref[:, col0:col0 + lanes], preferred_element_type=F32)
        acc = jnp.where(outside, 0.0, acc)
        conv = cb_ref[:, col0:col0 + lanes]
        for k in range(SSD_CONV):
            off = HALO - SSD_CONV // 2 + k
            conv = conv + cw_ref[k:k + 1, col0:col0 + lanes] * acc[off:off + tm]
        return _silu(conv)

    def main_dot(col0, lanes=EVEN_TN):
        return jnp.dot(a_ref[HALO:HALO + tm], w_ref[:, col0:col0 + lanes], preferred_element_type=F32)

    for tile, (kind, fa, fb) in enumerate(EVEN_TILES):
        col0 = tile * EVEN_TN
        if kind == "conv":
            za_ref[:, fa * EVEN_TN:(fa + 1) * EVEN_TN] = conv_silu(col0, EVEN_TN)
        elif kind == "conv_dt":
            n_c = ZS_DT - col0
            za_ref[:, col0:ZS_DT] = conv_silu(col0, n_c)
            dt = main_dot(ZS_DT, 2 * LANES)
            for d in range(2):
                za_ref[:, ZS_DT + d * LANES:ZS_DT + (d + 1) * LANES] = jax.nn.softplus(
                    dt[:, d * LANES:(d + 1) * LANES] + dtc_ref[d, 0:1, :])
            za_ref[:, ZS_XD:(fa + 1) * EVEN_TN] = jnp.zeros((tm, (fa + 1) * EVEN_TN - ZS_XD), F32)
        elif kind == "gate":
            lb = lb_ref[:, (col0 - ZS_F) % n_lb:(col0 - ZS_F) % n_lb + EVEN_TN]
            f = lb + (1.0 - lb) * jax.nn.sigmoid(main_dot(col0))
            backward = (col0 - ZS_F) // n_lb == 1
            za_ref[:, fa * EVEN_TN:(fa + 1) * EVEN_TN] = _chunk_cumsum(jnp.log(f), backward)
            zb_ref[:, fb * EVEN_TN:(fb + 1) * EVEN_TN] = (1.0 - f).astype(BF16)
        elif kind == "silu":
            zb_ref[:, fb * EVEN_TN:(fb + 1) * EVEN_TN] = _silu(main_dot(col0)).astype(BF16)
        else:
            zb_ref[:, fb * EVEN_TN:(fb + 1) * EVEN_TN] = main_dot(col0).astype(BF16)


def _even_in(h, mod, w, cw, cb, dtc, lb):
    t, d = h.shape
    tm = min(t, 512)
    hb = tm // HALO
    n_a = (1 + max(fa for _, fa, _ in EVEN_TILES if fa is not None)) * EVEN_TN
    n_b = (1 + max(fb for _, _, fb in EVEN_TILES if fb is not None)) * EVEN_TN
    whole = lambda a, **kw: pl.BlockSpec(a.shape, lambda i: (0,) * a.ndim, **kw)
    return pl.pallas_call(
        _even_in_kernel,
        grid=(t // tm,),
        in_specs=[pl.BlockSpec((tm, d), lambda i: (i, 0)),
                  pl.BlockSpec((HALO, d), lambda i: (jnp.maximum(i * hb - 1, 0), 0)),
                  pl.BlockSpec((HALO, d), lambda i: (jnp.minimum((i + 1) * hb, t // HALO - 1), 0)),
                  whole(mod), whole(w, pipeline_mode=pl.Buffered(1)),
                  whole(cw), whole(cb), whole(dtc), whole(lb)],
        out_specs=[pl.BlockSpec((tm, n_a), lambda i: (i, 0)), pl.BlockSpec((tm, n_b), lambda i: (i, 0))],
        out_shape=[jax.ShapeDtypeStruct((t, n_a), F32), jax.ShapeDtypeStruct((t, n_b), BF16)],
        scratch_shapes=[pltpu.VMEM((tm + 2 * HALO, d), BF16)],
        compiler_params=_cparams(("arbitrary",), 56),
        name="even_in",
    )(h, h, h, mod, w, cw, cb, dtc, lb)


def _scan_block_prep(xd_ref, qv_ref, bcum_ref, kk_ref, dtc_ref, d):
    n_hg = bcum_ref.shape[1]
    xbc = xd_ref[:, :ZS_DT]
    dtv = xd_ref[:, ZS_DT + d * LANES:ZS_DT + (d + 1) * LANES]
    acum = _chunk_cumsum(-dtv * jnp.exp(dtc_ref[d, 1:2, :]), bool(d))
    return xbc, dtv, acum, kk_ref[...].astype(F32), bcum_ref[...], qv_ref[:, :n_hg].astype(F32), qv_ref[:, n_hg:]


DECAY_LIMIT = 80.0


def _hg_intra_exact(qs, bcum, bcum_ref, kk_ref, v_ref, acc_ref, row0, reverse):
    Q = SCAN_CHUNK
    acc_ref[...] = jnp.zeros_like(acc_ref)
    t_idx = lax.broadcasted_iota(jnp.int32, (Q, 1), 0)

    def body(s, carry):
        r = row0 + s
        b_s, k_s, v_s = bcum_ref[pl.ds(r, 1), :], kk_ref[pl.ds(r, 1), :], v_ref[pl.ds(r, 1), :]
        allowed = (t_idx <= s) if reverse else (t_idx >= s)
        w = jnp.where(allowed, jnp.exp(jnp.minimum(bcum - b_s, 0.0)) * (qs * k_s), 0.0)
        for h in range(HG_HEADS):
            sl = slice(h * HG_HEAD_DIM, (h + 1) * HG_HEAD_DIM)
            acc_ref[:, sl] += jnp.sum(w[:, sl], axis=-1, keepdims=True) * v_s[:, sl]
        return carry

    lax.fori_loop(0, Q, body, 0)
    return acc_ref[...]


def _scan_chunk(prep, rows, dsk_ref, e_ref, st_ref, gt_ref, yo_ref, d, reverse, exact=None):
    Q = SCAN_CHUNK
    xbc, dtv, acum, kk, bcum, qs, vb = (a[rows] for a in prep)
    ti = lax.broadcasted_iota(jnp.int32, (Q, Q), 0)
    si = lax.broadcasted_iota(jnp.int32, (Q, Q), 1)
    mask = (ti <= si) if reverse else (ti >= si)
    edge = 0 if reverse else Q - 1
    width = SSD_HEADS * SSD_HEAD_DIM
    gw = width // SSD_GROUPS
    xs = xbc[:, :width]

    last = acum[edge:edge + 1, :]
    acum_t2 = jnp.concatenate([acum, acum], axis=0).T
    e_a = jnp.exp(acum)
    w_end = jnp.exp(last - acum)
    e_last = jnp.broadcast_to(jnp.exp(last), (SUBLANES, LANES))
    xe = _split_dot(jnp.concatenate([dtv, e_a, w_end], axis=0), e_ref[...], 1)
    dt_x, ea_x, wend_x = xe[:Q], xe[Q:2 * Q], xe[2 * Q:3 * Q]
    elast_x = _split_dot(e_last, e_ref[...], 3)[0:1]
    xdt = xs * dt_x
    xw = (xdt * wend_x).astype(BF16)
    xdt_b = xdt.astype(BF16)
    assert 2 * Q == LANES and 2 * SSD_HEAD_DIM == LANES
    lane = lax.broadcasted_iota(jnp.int32, (Q, LANES), 1)
    first = lane < Q
    t2 = lax.broadcasted_iota(jnp.int32, (Q, LANES), 0)
    s2 = lane % Q
    mask2 = (t2 <= s2) if reverse else (t2 >= s2)
    zero_b = jnp.zeros((Q, LANES), BF16)
    ys = []
    for g in range(SSD_GROUPS):
        bm = xbc[:, width + g * SSD_STATE:width + (g + 1) * SSD_STATE].astype(BF16)
        cm = xbc[:, width + (SSD_GROUPS + g) * SSD_STATE:width + (SSD_GROUPS + g + 1) * SSD_STATE].astype(BF16)
        cb2 = lax.dot_general(cm, jnp.concatenate([bm, bm], axis=0), (((1,), (1,)), ((), ())),
                              preferred_element_type=F32)
        y_state = jnp.dot(cm, st_ref[d, g].astype(BF16), preferred_element_type=F32)
        pairs = []
        for hp in range(gw // LANES):
            xp = xdt_b[:, g * gw + hp * LANES:g * gw + (hp + 1) * LANES]
            h = (g * gw + hp * LANES) // SSD_HEAD_DIM
            a_t = jnp.where(first, acum[:, h:h + 1], acum[:, h + 1:h + 2])
            a_s = jnp.where(first[0:1], acum_t2[h:h + 1, :], acum_t2[h + 1:h + 2, :])
            m2 = jnp.where(mask2, cb2 * jnp.exp(a_t - a_s), 0.0).astype(BF16)
            bdx = jnp.concatenate([jnp.where(first, xp, zero_b), jnp.where(first, zero_b, xp)], axis=0)
            pairs.append(jnp.dot(m2, bdx, preferred_element_type=F32))
        y_g = jnp.concatenate(pairs, axis=1) + y_state * ea_x[:, g * gw:(g + 1) * gw]
        ys.append(y_g)
        upd = lax.dot_general(bm, xw[:, g * gw:(g + 1) * gw], (((0,), (0,)), ((), ())),
                              preferred_element_type=F32)
        st_ref[d, g] = st_ref[d, g] * elast_x[:, g * gw:(g + 1) * gw] + upd
    y = jnp.concatenate(ys, axis=1)
    if not reverse:
        y = y + dsk_ref[...] * xs
    yo_ref[rows, :width] = y.astype(yo_ref.dtype)

    blast = bcum[edge:edge + 1, :]
    qe = (qs * jnp.exp(bcum)).astype(BF16)
    k_end = kk * jnp.exp(blast - bcum)
    if exact is None:
        k_til = (k_end * jnp.exp(-blast)).astype(BF16)
    else:
        intra = _hg_intra_exact(qs, bcum, *exact, rows.start, reverse)
    k_end = k_end.astype(BF16)
    e_blast = jnp.exp(blast)
    outs = []
    for h in range(HG_HEADS):
        sl = slice(h * HG_HEAD_DIM, (h + 1) * HG_HEAD_DIM)
        gt = gt_ref[d, h]
        if exact is None:
            att = lax.dot_general(qe[:, sl], k_til[:, sl], (((1,), (1,)), ((), ())), preferred_element_type=F32)
            att = jnp.where(mask, att, 0.0).astype(BF16)
            o_h = jnp.dot(att, vb[:, sl], preferred_element_type=F32)
        else:
            o_h = intra[:, sl]
        o_h = o_h + lax.dot_general(qe[:, sl], gt.astype(BF16), (((1,), (1,)), ((), ())),
                                    preferred_element_type=F32)
        outs.append(o_h)
        upd = lax.dot_general(vb[:, sl], k_end[:, sl], (((0,), (0,)), ((), ())), preferred_element_type=F32)
        gt_ref[d, h] = gt * e_blast[:, sl] + upd
    yo_ref[rows, width:] = jnp.concatenate(outs, axis=1).astype(yo_ref.dtype)


def _even_scan_kernel(*refs, nblk, cpb):
    fwd_in, bwd_in = refs[0:4], refs[4:8]
    dtc_ref, dsk_ref, e_ref, h0s_ref, h0g_ref = refs[8:13]
    yof_ref, yob_ref, st_ref, gt_ref, kkf_ref, vf_ref, acc_ref = refs[13:20]
    c = pl.program_id(0)

    @pl.when(c == 0)
    def _():
        st_ref[...] = h0s_ref[...]
        gt_ref[...] = h0g_ref[...]

    streams = ((False, fwd_in, yof_ref), (True, bwd_in, yob_ref))

    def run(exact):
        preps = [_scan_block_prep(*ins, dtc_ref, d) for d, (_, ins, _) in enumerate(streams)]
        if exact:
            for d, (_, ins, _) in enumerate(streams):
                kkf_ref[d] = preps[d][3]
                vf_ref[d] = preps[d][6].astype(F32)
        for step in range(cpb):
            for d, (reverse, ins, yo_ref) in enumerate(streams):
                ci = (cpb - 1 - step) if reverse else step
                rows = slice(ci * SCAN_CHUNK, (ci + 1) * SCAN_CHUNK)
                refs_exact = (ins[2], kkf_ref.at[d], vf_ref.at[d], acc_ref) if exact else None
                _scan_chunk(preps[d], rows, dsk_ref, e_ref, st_ref, gt_ref, yo_ref, d, reverse, refs_exact)

    strongest = jnp.minimum(jnp.min(fwd_in[2][...]), jnp.min(bwd_in[2][...]))
    pl.when(strongest >= -DECAY_LIMIT)(lambda: run(False))
    pl.when(strongest < -DECAY_LIMIT)(lambda: run(True))


def _even_scan(za, zb, consts, h0s, h0g):
    t = za.shape[0]
    nch = t // SCAN_CHUNK
    cpb = max(k for k in (4, 2, 1) if nch % k == 0)
    rows = cpb * SCAN_CHUNK
    nblk = nch // cpb
    dtc, dsk, e16 = consts
    n_x = SSD_HEADS * SSD_HEAD_DIM
    n_hg = HG_HEADS * HG_HEAD_DIM
    gate_a, gate_b = (tuple(tile[i] for tile in EVEN_TILES if tile[0] == "gate") for i in (1, 2))

    def stream_specs(bidx, d):
        return [
            pl.BlockSpec((rows, ZS_XD), lambda c: (bidx(c), 0)),
            pl.BlockSpec((rows, 2 * n_hg), lambda c: (bidx(c), 0)),
            pl.BlockSpec((rows, n_hg), lambda c: (bidx(c), gate_a[d])),
            pl.BlockSpec((rows, n_hg), lambda c: (bidx(c), gate_b[d])),
        ]

    fwd = lambda c: c
    bwd = lambda c: nblk - 1 - c
    whole = lambda a: pl.BlockSpec(a.shape, lambda c: (0,) * a.ndim)
    in_specs = stream_specs(fwd, 0) + stream_specs(bwd, 1) + [whole(a) for a in (dtc, dsk, e16, h0s, h0g)]
    out_specs = [pl.BlockSpec((rows, n_x + n_hg), lambda c: (c, 0)),
                 pl.BlockSpec((rows, n_x + n_hg), lambda c: (bwd(c), 0)), whole(h0s), whole(h0g)]
    out_shape = [jax.ShapeDtypeStruct((t, n_x + n_hg), BF16), jax.ShapeDtypeStruct((t, n_x + n_hg), BF16),
                 jax.ShapeDtypeStruct(h0s.shape, F32), jax.ShapeDtypeStruct(h0g.shape, F32)]
    return pl.pallas_call(
        functools.partial(_even_scan_kernel, nblk=nblk, cpb=cpb),
        grid=(nblk,),
        in_specs=in_specs,
        out_specs=out_specs,
        out_shape=out_shape,
        scratch_shapes=[pltpu.VMEM((2, rows, n_hg), F32), pltpu.VMEM((2, rows, n_hg), F32),
                        pltpu.VMEM((SCAN_CHUNK, n_hg), F32)],
        compiler_params=_cparams(("arbitrary",), 48),
        name="even_scan",
    )(za, zb, za, zb, za, zb, za, zb, dtc, dsk, e16, h0s, h0g)


def _ffn_block(h, mod_ref, wg_ref, wu_ref, wd_ref, final):
    a = _modulate(h, mod_ref).astype(BF16)
    gate = jnp.dot(a, wg_ref[...], preferred_element_type=F32)
    up = jnp.dot(a, wu_ref[...], preferred_element_type=F32)
    act = (_silu(gate) * up).astype(BF16)
    out = h + mod_ref[3:4, :] * jnp.dot(act, wd_ref[...], preferred_element_type=F32)
    if final:
        ms = jnp.mean(out * out, axis=-1, keepdims=True)
        out = out * lax.rsqrt(ms + EPS) * mod_ref[4:5, :]
    return out


def _resident(a):
    return pl.BlockSpec(a.shape, lambda i: (0,) * a.ndim, pipeline_mode=pl.Buffered(1))


def _even_out_kernel(h_ref, z_ref, g_ref, yof_ref, yob_ref, w_ref, vec_ref, mod_ref, wg_ref, wu_ref, wd_ref, o_ref):
    n_x = z_ref.shape[1]
    yo = yof_ref[...].astype(F32) + yob_ref[...].astype(F32)
    y = yo[:, :n_x] * z_ref[...].astype(F32)
    o = yo[:, n_x:]
    gw = n_x // SSD_GROUPS
    parts = []
    for g in range(SSD_GROUPS):
        yg = y[:, g * gw:(g + 1) * gw]
        ms = jnp.mean(yg * yg, axis=-1, keepdims=True)
        parts.append(yg * lax.rsqrt(ms + EPS))
    yn = (jnp.concatenate(parts, axis=1) * vec_ref[0:1, :]).astype(BF16)
    parts = []
    for hh in range(HG_HEADS):
        oh = o[:, hh * HG_HEAD_DIM:(hh + 1) * HG_HEAD_DIM]
        ms = jnp.mean(oh * oh, axis=-1, keepdims=True)
        parts.append(oh * lax.rsqrt(ms + EPS))
    on = (jnp.concatenate(parts, axis=1) * vec_ref[1:2, :] * g_ref[...].astype(F32)).astype(BF16)
    half = yn.shape[1]
    mix = jnp.dot(yn, w_ref[:half, :], preferred_element_type=F32)
    mix = mix + jnp.dot(on, w_ref[half:, :], preferred_element_type=F32)
    h1 = h_ref[...] + vec_ref[2:3, :] * mix
    o_ref[...] = _ffn_block(h1, mod_ref, wg_ref, wu_ref, wd_ref, final=False)


def _even_out(h, zb, yo_f, yo_b, w_out, vec, mod, wg, wu, wd):
    t, d = h.shape
    tm = min(t, 256)
    n_x = SSD_HEADS * SSD_HEAD_DIM
    n_hg = HG_HEADS * HG_HEAD_DIM
    row = lambda i: (i, 0)
    z_tile, g_tile = (next(fb for k, (_, _, fb) in enumerate(EVEN_TILES) if k * EVEN_TN == start)
                      for start in (ZS_Z, ZS_G))
    return pl.pallas_call(
        _even_out_kernel,
        grid=(t // tm,),
        in_specs=[pl.BlockSpec((tm, d), row),
                  pl.BlockSpec((tm, n_x), lambda i: (i, z_tile)),
                  pl.BlockSpec((tm, n_hg), lambda i: (i, g_tile)),
                  pl.BlockSpec((tm, n_x + n_hg), row), pl.BlockSpec((tm, n_x + n_hg), row),
                  _resident(w_out), _resident(vec), _resident(mod), _resident(wg), _resident(wu), _resident(wd)],
        out_specs=pl.BlockSpec((tm, d), row),
        out_shape=jax.ShapeDtypeStruct((t, d), F32),
        compiler_params=_cparams(("arbitrary",), 56),
        name="even_out_ffn",
    )(h, zb, zb, yo_f, yo_b, w_out, vec, mod, wg, wu, wd)


def _odd_in_kernel(h_ref, mod_ref, w_ref, o_ref, u_ref):
    a = _modulate(h_ref[...], mod_ref).astype(BF16)
    u = jnp.dot(a, w_ref[...], preferred_element_type=F32)
    nj, grid_w, rb, _ = o_ref.shape
    for j in range(nj):
        u_ref[j] = u[:, j * LANES:(j + 1) * LANES]
    for wi in range(grid_w):
        for j in range(nj):
            o_ref[j, wi] = u_ref[j, pl.ds(wi, rb, stride=grid_w), :]


def _odd_in(h, mod, w, grid_w):
    t, d = h.shape
    rows = t // grid_w
    rb = min(rows, 4 * SUBLANES)
    n = w.shape[1]
    nj = n // LANES
    out = pl.pallas_call(
        _odd_in_kernel,
        grid=(rows // rb,),
        in_specs=[pl.BlockSpec((rb * grid_w, d), lambda i: (i, 0)),
                  pl.BlockSpec((SUBLANES, d), lambda i: (0, 0)),
                  pl.BlockSpec(w.shape, lambda i: (0, 0))],
        out_specs=pl.BlockSpec((nj, grid_w, rb, LANES), lambda i: (0, 0, i, 0)),
        out_shape=jax.ShapeDtypeStruct((nj, grid_w, rows, LANES), F32),
        scratch_shapes=[pltpu.VMEM((nj, rb * grid_w, LANES), F32)],
        compiler_params=_cparams(("arbitrary",), 48),
        name="odd_in",
    )(h, mod, w)
    return out.reshape(nj, t, LANES)


def _sub_chunk_rows(u_ref, nb):
    return [u_ref[pl.ds(s, nb, stride=S5_SUB), :] for s in range(S5_SUB)]


def _s5_inject_kernel(u_ref, win_ref, s_ref):
    x = jnp.concatenate(_sub_chunk_rows(u_ref, s_ref.shape[0]), axis=1).astype(BF16)
    s_ref[...] = jnp.dot(x, win_ref[...], preferred_element_type=F32)


def _s5_inject(u, win):
    nj, t, _ = u.shape
    n = t // S5_SUB
    nb = min(n, 512)
    width, ns = win.shape[1:]
    return pl.pallas_call(
        _s5_inject_kernel,
        grid=(nj, n // nb),
        in_specs=[pl.BlockSpec((None, nb * S5_SUB, LANES), lambda j, b: (j, b, 0)),
                  pl.BlockSpec((None, width, ns), lambda j, b: (j, 0, 0))],
        out_specs=pl.BlockSpec((None, nb, ns), lambda j, b: (j, b, 0)),
        out_shape=jax.ShapeDtypeStruct((nj, n, ns), F32),
        compiler_params=_cparams(("arbitrary", "arbitrary"), 40),
        name="s5_inject",
    )(u, win)


def _s5_carry_kernel(sf_ref, sb_ref, lam_ref, h0_ref, pf_ref, pb_ref, hf_ref, tf_ref, tb_ref, c_ref):
    b = pl.program_id(0)
    nj, kb, ns = sf_ref.shape
    half = ns // 2

    @pl.when(b == 0)
    def _():
        c_ref[...] = h0_ref[...]

    for j in range(nj):
        tf_ref[:, j, :] = sf_ref[j]
        tb_ref[:, j, :] = sb_ref[j]

    def advance(d, c, s):
        lr, li = lam_ref[d, :, :half], lam_ref[d, :, half:]
        pr, pi = c[:, :half], c[:, half:]
        return jnp.concatenate([lr * pr - li * pi + s[:, :half], lr * pi + li * pr + s[:, half:]], axis=1)

    def step(i, carry):
        cf, cb = carry
        kr = kb - 1 - i
        sf = tf_ref[i]
        sb = tb_ref[kr]
        tf_ref[i] = cf
        tb_ref[kr] = cb
        return advance(0, cf, sf), advance(1, cb, sb)

    cf, cb = lax.fori_loop(0, kb, step, (c_ref[0], c_ref[1]), unroll=4)
    c_ref[0] = cf
    c_ref[1] = cb
    for j in range(nj):
        pf_ref[j] = tf_ref[:, j, :]
        pb_ref[j] = tb_ref[:, j, :]

    @pl.when(b == pl.num_programs(0) - 1)
    def _():
        hf_ref[...] = c_ref[...]


def _s5_carry(s_f, s_b, lam, h0):
    nj, n, ns = s_f.shape
    kb = min(n, 256)
    nblk = n // kb
    fwd = pl.BlockSpec((nj, kb, ns), lambda b: (0, b, 0))
    bwd = pl.BlockSpec((nj, kb, ns), lambda b: (0, nblk - 1 - b, 0))
    whole = pl.BlockSpec(lam.shape, lambda b: (0, 0, 0))
    return pl.pallas_call(
        _s5_carry_kernel,
        grid=(nblk,),
        in_specs=[fwd, bwd, whole, whole],
        out_specs=[fwd, bwd, whole],
        out_shape=[jax.ShapeDtypeStruct(s_f.shape, F32), jax.ShapeDtypeStruct(s_b.shape, F32),
                   jax.ShapeDtypeStruct(lam.shape, F32)],
        scratch_shapes=[pltpu.VMEM((kb, nj, ns), F32), pltpu.VMEM((kb, nj, ns), F32), pltpu.VMEM(lam.shape, F32)],
        compiler_params=_cparams(("arbitrary",), 48),
        name="s5_carry",
    )(s_f, s_b, lam, h0)


def _s5_readout_kernel(u_ref, p_ref, wout_ref, tt_ref, add_ref, y_ref, *, reverse):
    nb = p_ref.shape[0]
    us = _sub_chunk_rows(u_ref, nb)
    xb = jnp.concatenate(us, axis=1).astype(BF16)
    pb = p_ref[...].astype(BF16)
    width = xb.shape[1]
    tile = 2 * LANES
    for m in range(width // tile):
        cols = slice(m * tile, (m + 1) * tile)
        rows = slice(m * tile, width) if reverse else slice(0, (m + 1) * tile)
        y = jnp.dot(pb, wout_ref[:, cols], preferred_element_type=F32)
        y = y + jnp.dot(xb[:, rows], tt_ref[rows, cols], preferred_element_type=F32)
        for q in range(tile // LANES):
            s = m * (tile // LANES) + q
            other = add_ref[pl.ds(s, nb, stride=S5_SUB), :] if reverse else add_ref[...] * us[s]
            y_ref[pl.ds(s, nb, stride=S5_SUB), :] = y[:, q * LANES:(q + 1) * LANES] + other


def _s5_readout(u, p, wout, tt, add, reverse):
    nj, t, _ = u.shape
    n = t // S5_SUB
    nb = min(n, 512)
    ns, width = wout.shape[1:]
    tokens = pl.BlockSpec((None, nb * S5_SUB, LANES), lambda j, b: (j, b, 0))
    add_spec = tokens if reverse else pl.BlockSpec((None, 1, LANES), lambda j, b: (j, 0, 0))
    once = pl.Buffered(1)
    return pl.pallas_call(
        functools.partial(_s5_readout_kernel, reverse=reverse),
        grid=(nj, n // nb),
        in_specs=[tokens,
                  pl.BlockSpec((None, nb, ns), lambda j, b: (j, b, 0)),
                  pl.BlockSpec((None, ns, width), lambda j, b: (j, 0, 0), pipeline_mode=once),
                  pl.BlockSpec((None, width, width), lambda j, b: (j, 0, 0), pipeline_mode=once),
                  add_spec],
        out_specs=tokens,
        out_shape=jax.ShapeDtypeStruct(u.shape, F32),
        compiler_params=_cparams(("arbitrary", "arbitrary"), 56),
        name="s5_readout_bwd" if reverse else "s5_readout_fwd",
    )(u, p, wout, tt, add)


def _odd_out_kernel(h_ref, y_ref, wv_ref, wgl_ref, vec_ref, mod_ref, wg_ref, wu_ref, wd_ref, o_ref, a_ref):
    nj, grid_w, rb, _ = y_ref.shape
    for wi in range(grid_w):
        for j in range(nj):
            a_ref[j, pl.ds(wi, rb, stride=grid_w), :] = y_ref[j, wi]
    a = jax.nn.gelu(jnp.concatenate([a_ref[j] for j in range(nj)], axis=1)).astype(BF16)
    val = jnp.dot(a, wv_ref[...], preferred_element_type=F32)
    gate = jnp.dot(a, wgl_ref[...], preferred_element_type=F32)
    h3 = h_ref[...] + vec_ref[0:1, :] * (val * jax.nn.sigmoid(gate))
    o_ref[...] = _ffn_block(h3, mod_ref, wg_ref, wu_ref, wd_ref, final=True)


def _odd_out(h, y, w_val, w_gate, vec, mod, wg, wu, wd, grid_w):
    t, d = h.shape
    rows = t // grid_w
    rb = min(rows, SUBLANES)
    nj = y.shape[0]
    tok = pl.BlockSpec((rb * grid_w, d), lambda i: (i, 0))
    return pl.pallas_call(
        _odd_out_kernel,
        grid=(rows // rb,),
        in_specs=[tok, pl.BlockSpec((nj, grid_w, rb, LANES), lambda i: (0, 0, i, 0)),
                  _resident(w_val), _resident(w_gate), _resident(vec), _resident(mod),
                  _resident(wg), _resident(wu), _resident(wd)],
        out_specs=tok,
        out_shape=jax.ShapeDtypeStruct((t, d), F32),
        scratch_shapes=[pltpu.VMEM((nj, rb * grid_w, LANES), F32)],
        compiler_params=_cparams(("arbitrary",), 56),
        name="odd_out_ffn",
    )(h, y.reshape(nj, grid_w, rows, LANES), w_val, w_gate, vec, mod, wg, wu, wd)


def _s5_prep_kernel(prm_ref, bre_ref, bim_ref, colp_ref, cre_ref, cim_ref, win_ref, wout_ref, tt_ref, lam_ref, *,
                    reverse):
    T = S5_SUB
    ns = prm_ref.shape[1]
    a_re, a_im, dt = prm_ref[0:1, :], prm_ref[1:2, :], jnp.exp(prm_ref[2:3, :])
    tau = lax.broadcasted_iota(jnp.int32, (3 * SUBLANES, ns), 0).astype(F32)
    mag = jnp.exp(a_re * dt * tau)
    pr = mag * jnp.cos(a_im * dt * tau)
    pi = mag * jnp.sin(a_im * dt * tau)
    lr, li = pr[1:2], pi[1:2]
    den = a_re * a_re + a_im * a_im
    cr = ((lr - 1.0) * a_re + li * a_im) / den
    ci = (li * a_re - (lr - 1.0) * a_im) / den
    row_g = lax.broadcasted_iota(jnp.int32, (LANES, ns), 0) // S5_GROUP
    col_g = lax.broadcasted_iota(jnp.int32, (LANES, ns), 1) // S5_STATE
    same = row_g == col_g
    bbr = jnp.where(same, cr * bre_ref[...] - ci * bim_ref[...], 0.0)
    bbi = jnp.where(same, cr * bim_ref[...] + ci * bre_ref[...], 0.0)
    for s in range(T):
        e = s if reverse else T - 1 - s
        rows = slice(s * LANES, (s + 1) * LANES)
        win_ref[rows, :ns] = (pr[e:e + 1] * bbr - pi[e:e + 1] * bbi).astype(BF16)
        win_ref[rows, ns:] = (pr[e:e + 1] * bbi + pi[e:e + 1] * bbr).astype(BF16)
    lam_ref[...] = jnp.broadcast_to(jnp.concatenate([pr[T:T + 1], pi[T:T + 1]], axis=1), lam_ref.shape)
    dt_c = jnp.exp(colp_ref[2])
    mag_c = jnp.exp(colp_ref[0] * dt_c)
    lr_c = mag_c * jnp.cos(colp_ref[1] * dt_c)
    li_c = mag_c * jnp.sin(colp_ref[1] * dt_c)
    row_gc = lax.broadcasted_iota(jnp.int32, (ns, LANES), 0) // S5_STATE
    col_gc = lax.broadcasted_iota(jnp.int32, (ns, LANES), 1) // S5_GROUP
    same_c = row_gc == col_gc
    c_re = jnp.where(same_c, cre_ref[...], 0.0)
    c_im = jnp.where(same_c, cim_ref[...], 0.0)
    bb = jnp.concatenate([bbr, bbi], axis=1)
    bb_hi = bb.astype(BF16)
    bb_lo = (bb - bb_hi.astype(F32)).astype(BF16)
    cur_r = jnp.ones((ns, LANES), F32)
    cur_i = jnp.zeros((ns, LANES), F32)
    kblk = []
    for e in range(T + 1):
        wo_r = c_re * cur_r - c_im * cur_i
        wo_i = -(c_re * cur_i + c_im * cur_r)
        if e >= 1:
            t = (T - e) if reverse else e - 1
            wout_ref[:ns, t * LANES:(t + 1) * LANES] = wo_r.astype(BF16)
            wout_ref[ns:, t * LANES:(t + 1) * LANES] = wo_i.astype(BF16)
        if e < T:
            wo = jnp.concatenate([wo_r, wo_i], axis=0)
            wo_hi = wo.astype(BF16)
            wo_lo = (wo - wo_hi.astype(F32)).astype(BF16)
            k = jnp.dot(bb_hi, wo_hi, preferred_element_type=F32)
            k = k + jnp.dot(bb_hi, wo_lo, preferred_element_type=F32) + jnp.dot(bb_lo, wo_hi, preferred_element_type=F32)
            kblk.append(k.astype(BF16))
        cur_r, cur_i = cur_r * lr_c - cur_i * li_c, cur_r * li_c + cur_i * lr_c
    zeros = jnp.zeros((LANES, LANES), BF16)
    for s in range(T):
        for t in range(T):
            lag = (s - t) if reverse else (t - s)
            tt_ref[s * LANES:(s + 1) * LANES, t * LANES:(t + 1) * LANES] = kblk[lag] if lag >= 0 else zeros


def _s5_prep(a_re, a_im, log_dt, b_re, b_im, c_re, c_im, reverse):
    g, p = a_re.shape
    cdim = b_re.shape[2]
    nj = g // S5_GPB
    ns = S5_GPB * p
    width = S5_SUB * LANES
    prm = jnp.zeros((nj, SUBLANES, ns), F32)
    prm = prm.at[:, 0].set(a_re.reshape(nj, ns)).at[:, 1].set(a_im.reshape(nj, ns))
    prm = prm.at[:, 2].set(jnp.repeat(log_dt, p).reshape(nj, ns))
    colp = jnp.broadcast_to(prm[:, :3, :, None], (nj, 3, ns, LANES))
    row_tile = lambda b: jnp.tile(b.reshape(nj, S5_GPB, p, cdim).transpose(0, 3, 1, 2).reshape(nj, 1, cdim, ns),
                                  (1, S5_GPB, 1, 1)).reshape(nj, LANES, ns)
    col_tile = lambda c_: jnp.tile(c_.reshape(nj, S5_GPB, cdim, p).transpose(0, 1, 3, 2).reshape(nj, ns, 1, cdim),
                                   (1, 1, S5_GPB, 1)).reshape(nj, ns, LANES)
    blk = lambda *shape: pl.BlockSpec((None,) + shape, lambda j: (j,) + (0,) * len(shape))
    return pl.pallas_call(
        functools.partial(_s5_prep_kernel, reverse=reverse),
        grid=(nj,),
        in_specs=[blk(SUBLANES, ns), blk(LANES, ns), blk(LANES, ns), blk(3, ns, LANES), blk(ns, LANES),
                  blk(ns, LANES)],
        out_specs=[blk(width, 2 * ns), blk(2 * ns, width), blk(width, width), blk(SUBLANES, 2 * ns)],
        out_shape=[jax.ShapeDtypeStruct((nj, width, 2 * ns), BF16), jax.ShapeDtypeStruct((nj, 2 * ns, width), BF16),
                   jax.ShapeDtypeStruct((nj, width, width), BF16), jax.ShapeDtypeStruct((nj, SUBLANES, 2 * ns), F32)],
        compiler_params=_cparams(("arbitrary",), 56),
        name="s5_prep_bwd" if reverse else "s5_prep_fwd",
    )(prm, row_tile(b_re), row_tile(b_im), colp, col_tile(c_re), col_tile(c_im))


def _rows(*vs, width):
    out = jnp.zeros((SUBLANES, width), F32)
    for i, v in enumerate(vs):
        out = out.at[i].set(v.astype(F32))
    return out


def kernel(x, c, ctx, c_ctx, ada_w, ada_b, norm_mix_g, norm_ffn_g, ffn_w_gate, ffn_w_up, ffn_w_down, final_norm_g, ev_w_in, ev_w_out, ssd_conv_w, ssd_conv_b, ssd_dt_bias, ssd_a_log, ssd_d, ssd_norm_g, hg_lb_logits, hg_norm_g, od_w_in, s5_a_re, s5_a_im, s5_log_dt, s5_b_re, s5_b_im, s5_c_re, s5_c_im, s5_d, od_w_val, od_w_gate):
    d = x.shape[-1]
    lat = x[0].astype(F32)
    hc = ctx[0].astype(F32)

    m = _ada(_rows(c[0], c_ctx, width=d), ada_w, ada_b)

    def ada_vecs(layer, stream):
        return [m[layer, stream, i * d:(i + 1) * d] for i in range(6)]

    w = ev_w_in[0]
    n_x = SSD_HEADS * SSD_HEAD_DIM
    n_xbc = n_x + 2 * SSD_GROUPS * SSD_STATE
    n_hg = HG_HEADS * HG_HEAD_DIM
    o_z, o_xbc, o_dt = 0, n_x, n_x + n_xbc
    o_q = o_dt + 2 * SSD_HEADS
    o_f = o_q + n_hg
    o_v = o_f + 2 * n_hg
    o_g = o_v + n_hg
    pad = lambda n: jnp.zeros((d, n), F32)
    w_in = jnp.concatenate([
        w[:, o_xbc:o_xbc + n_xbc],
        w[:, o_dt:o_dt + SSD_HEADS], pad(LANES - SSD_HEADS),
        w[:, o_dt + SSD_HEADS:o_dt + 2 * SSD_HEADS], pad(LANES - SSD_HEADS),
        pad(ZS_Q - ZS_XD),
        w[:, o_q:o_q + n_hg], w[:, o_v:o_v + n_hg], w[:, o_f:o_f + 2 * n_hg],
        w[:, o_z:o_z + n_x], w[:, o_g:o_g + n_hg]], axis=1).astype(BF16)
    cw = jnp.zeros((SUBLANES, n_xbc), F32).at[:SSD_CONV].set(ssd_conv_w[0])
    cb = ssd_conv_b[0].reshape(1, n_xbc)
    dtc = jnp.zeros((2, SUBLANES, LANES), F32)
    dtc = dtc.at[:, 0, :SSD_HEADS].set(ssd_dt_bias[0]).at[:, 1, :SSD_HEADS].set(ssd_a_log[0])
    dsk = jnp.repeat(ssd_d[0], SSD_HEAD_DIM).reshape(1, n_x)
    lower = jnp.cumsum(jax.nn.softmax(hg_lb_logits.astype(F32), axis=0), axis=0)[0].reshape(1, n_hg)
    head_of_lane = jnp.arange(n_x) // SSD_HEAD_DIM
    e16 = (jnp.arange(LANES)[:, None] == head_of_lane[None, :]).astype(BF16)
    consts = (dtc, dsk, e16)
    w_out = ev_w_out[0].astype(BF16)
    out_vec = lambda gate: _rows(ssd_norm_g[0], jnp.tile(hg_norm_g[0], HG_HEADS), gate, width=d)
    wg0, wu0, wd0 = (t_[0].astype(BF16) for t_ in (ffn_w_gate, ffn_w_up, ffn_w_down))

    s_state = jnp.zeros((2, SSD_GROUPS, SSD_STATE, n_x // SSD_GROUPS), F32)
    g_state = jnp.zeros((2, HG_HEADS, HG_HEAD_DIM, HG_HEAD_DIM), F32)
    layer0 = {}
    for stream, h in ((1, hc), (0, lat)):
        sm, scm, gm, sf, scf, gf = ada_vecs(0, stream)
        za, zb = _even_in(h, _rows(norm_mix_g[0], scm, sm, width=d), w_in, cw, cb, dtc, lower)
        yo_f, yo_b, s_state, g_state = _even_scan(za, zb, consts, s_state, g_state)
        layer0[stream] = _even_out(h, zb, yo_f, yo_b, w_out, out_vec(gm),
                                   _rows(norm_ffn_g[0], scf, sf, gf, width=d), wg0, wu0, wd0)

    lat, hc = layer0[0], layer0[1]
    w_s5 = od_w_in[0].astype(BF16)
    u = {}
    for stream, h, gw in ((1, hc, 1), (0, lat, GRID_W)):
        sm, scm = ada_vecs(1, stream)[:2]
        u[stream] = _odd_in(h, _rows(norm_mix_g[1], scm, sm, width=d), w_s5, gw)
    mats = [_s5_prep(s5_a_re[0, di], s5_a_im[0, di], s5_log_dt[0, di], s5_b_re[0, di], s5_b_im[0, di],
                     s5_c_re[0, di], s5_c_im[0, di], reverse=bool(di)) for di in range(2)]
    lam = jnp.stack([mats[0][3][:, 0], mats[1][3][:, 0]])
    carry = jnp.zeros_like(lam)
    for stream in (1, 0):
        s_f, s_b = (_s5_inject(u[stream], mats[di][0]) for di in range(2))
        p_f, p_b, carry = _s5_carry(s_f, s_b, lam, carry)
    nj = u[0].shape[0]
    y = _s5_readout(u[0], p_f, mats[0][1], mats[0][2], s5_d[0].reshape(nj, 1, LANES), reverse=False)
    y = _s5_readout(u[0], p_b, mats[1][1], mats[1][2], y, reverse=True)
    _, _, gm, sf, scf, gf = ada_vecs(1, 0)
    wg1, wu1, wd1 = (t_[1].astype(BF16) for t_ in (ffn_w_gate, ffn_w_up, ffn_w_down))
    out = _odd_out(lat, y, od_w_val[0].astype(BF16), od_w_gate[0].astype(BF16), _rows(gm, width=d),
                   _rows(norm_ffn_g[1], scf, sf, gf, final_norm_g, width=d), wg1, wu1, wd1, GRID_W)
    return out[None].astype(x.dtype)
```

```python
import functools
import math

import jax
import jax.numpy as jnp
from jax import lax
from jax.experimental import pallas as pl
from jax.experimental.pallas import tpu as pltpu

F32 = jnp.float32
BF16 = jnp.bfloat16
EPS = 1e-6

LANES = 128
SUBLANES = 8
GRID_W = 64
SCAN_CHUNK = 64
SSD_HEADS = 16
SSD_HEAD_DIM = 64
SSD_STATE = 128
SSD_GROUPS = 2
SSD_CONV = 5
HG_HEADS = 8
HG_HEAD_DIM = 128
S5_GROUP = 16
S5_STATE = 64
S5_SUB = 16
S5_GPB = LANES // S5_GROUP

ZS_XBC = 0
ZS_DT = 1536
ZS_XD = 1792
ZS_Q = 2048
ZS_V = 3072
ZS_F = 4096
ZS_Z = 6144
ZS_G = 7168
ZS_WIDTH = 8192


def _cparams(semantics, vmem_mb):
    return pltpu.CompilerParams(dimension_semantics=semantics, vmem_limit_bytes=vmem_mb * 1024 * 1024)


def _silu(x):
    return x * jax.nn.sigmoid(x)


def _modulate(h, mod_ref):
    ms = jnp.mean(h * h, axis=-1, keepdims=True)
    return h * lax.rsqrt(ms + EPS) * (mod_ref[0:1, :] * (1.0 + mod_ref[1:2, :])) + mod_ref[2:3, :]


def _split_dot(x, w, passes):
    acc, rem = None, x
    for _ in range(passes):
        hi = rem.astype(BF16)
        d = jnp.dot(hi, w, preferred_element_type=F32)
        acc = d if acc is None else acc + d
        rem = rem - hi.astype(F32)
    return acc


def _chunk_cumsum(x, reverse):
    n = x.shape[0]
    pos = lax.broadcasted_iota(jnp.int32, (n, 1), 0) % SCAN_CHUNK
    s = 1
    while s < SCAN_CHUNK:
        if reverse:
            x = x + jnp.where(pos < SCAN_CHUNK - s, pltpu.roll(x, n - s, axis=0), 0.0)
        else:
            x = x + jnp.where(pos >= s, pltpu.roll(x, s, axis=0), 0.0)
        s *= 2
    return x


def _ada_kernel(c_ref, w_ref, b_ref, o_ref):
    o_ref[...] = jnp.dot(_silu(c_ref[...]), w_ref[...], precision=lax.Precision.HIGHEST,
                         preferred_element_type=F32) + b_ref[...]


def _ada(cvecs, ada_w, ada_b):
    depth, d, n = ada_w.shape
    tn = n // 4
    return pl.pallas_call(
        _ada_kernel,
        grid=(depth, n // tn),
        in_specs=[pl.BlockSpec((SUBLANES, d), lambda l, j: (0, 0)),
                  pl.BlockSpec((None, d, tn), lambda l, j: (l, 0, j)),
                  pl.BlockSpec((None, 1, tn), lambda l, j: (l, 0, j))],
        out_specs=pl.BlockSpec((None, SUBLANES, tn), lambda l, j: (l, 0, j)),
        out_shape=jax.ShapeDtypeStruct((depth, SUBLANES, n), F32),
        compiler_params=_cparams(("arbitrary", "arbitrary"), 40),
        name="ada",
    )(cvecs, ada_w, ada_b.reshape(depth, 1, n))


EVEN_TN = 1024
HALO = 2 * SUBLANES
EVEN_TILES = (("conv", 0, None), ("conv_dt", 1, None), ("silu", None, 0), ("copy", None, 1),
              ("gate", 2, 4), ("gate", 3, 5), ("silu", None, 2), ("silu", None, 3))


def _even_in_kernel(h_ref, hp_ref, hn_ref, mod_ref, w_ref, cw_ref, cb_ref, dtc_ref, lb_ref, za_ref, zb_ref, a_ref):
    i = pl.program_id(0)
    tm = h_ref.shape[0]
    n_lb = lb_ref.shape[1]
    a_ref[:HALO] = _modulate(hp_ref[...], mod_ref).astype(BF16)
    a_ref[HALO:HALO + tm] = _modulate(h_ref[...], mod_ref).astype(BF16)
    a_ref[HALO + tm:] = _modulate(hn_ref[...], mod_ref).astype(BF16)
    row = lax.broadcasted_iota(jnp.int32, (tm + 2 * HALO, 1), 0)
    outside = ((row < HALO) & (i == 0)) | ((row >= HALO + tm) & (i == pl.num_programs(0) - 1))

    def conv_silu(col0, lanes):
        acc = jnp.dot(a_ref[...], w_ref[:, col0:col0 + lanes], preferred_element_type=F32)
        acc = jnp.where(outside, 0.0, acc)
        conv = cb_ref[:, col0:col0 + lanes]
        for k in range(SSD_CONV):
            off = HALO - SSD_CONV // 2 + k
            conv = conv + cw_ref[k:k + 1, col0:col0 + lanes] * acc[off:off + tm]
        return _silu(conv)

    def main_dot(col0, lanes=EVEN_TN):
        return jnp.dot(a_ref[HALO:HALO + tm], w_ref[:, col0:col0 + lanes], preferred_element_type=F32)

    for tile, (kind, fa, fb) in enumerate(EVEN_TILES):
        col0 = tile * EVEN_TN
        if kind == "conv":
            za_ref[:, fa * EVEN_TN:(fa + 1) * EVEN_TN] = conv_silu(col0, EVEN_TN)
        elif kind == "conv_dt":
            n_c = ZS_DT - col0
            za_ref[:, col0:ZS_DT] = conv_silu(col0, n_c)
            dt = main_dot(ZS_DT, 2 * LANES)
            for d in range(2):
                za_ref[:, ZS_DT + d * LANES:ZS_DT + (d + 1) * LANES] = jax.nn.softplus(
                    dt[:, d * LANES:(d + 1) * LANES] + dtc_ref[d, 0:1, :])
            za_ref[:, ZS_XD:(fa + 1) * EVEN_TN] = jnp.zeros((tm, (fa + 1) * EVEN_TN - ZS_XD), F32)
        elif kind == "gate":
            lb = lb_ref[:, (col0 - ZS_F) % n_lb:(col0 - ZS_F) % n_lb + EVEN_TN]
            f = lb + (1.0 - lb) * jax.nn.sigmoid(main_dot(col0))
            backward = (col0 - ZS_F) // n_lb == 1
            za_ref[:, fa * EVEN_TN:(fa + 1) * EVEN_TN] = _chunk_cumsum(jnp.log(f), backward)
            zb_ref[:, fb * EVEN_TN:(fb + 1) * EVEN_TN] = (1.0 - f).astype(BF16)
        elif kind == "silu":
            zb_ref[:, fb * EVEN_TN:(fb + 1) * EVEN_TN] = _silu(main_dot(col0)).astype(BF16)
        else:
            zb_ref[:, fb * EVEN_TN:(fb + 1) * EVEN_TN] = main_dot(col0).astype(BF16)


def _even_in(h, mod, w, cw, cb, dtc, lb):
    t, d = h.shape
    tm = min(t, 512)
    hb = tm // HALO
    n_a = (1 + max(fa for _, fa, _ in EVEN_TILES if fa is not None)) * EVEN_TN
    n_b = (1 + max(fb for _, _, fb in EVEN_TILES if fb is not None)) * EVEN_TN
    whole = lambda a, **kw: pl.BlockSpec(a.shape, lambda i: (0,) * a.ndim, **kw)
    return pl.pallas_call(
        _even_in_kernel,
        grid=(t // tm,),
        in_specs=[pl.BlockSpec((tm, d), lambda i: (i, 0)),
                  pl.BlockSpec((HALO, d), lambda i: (jnp.maximum(i * hb - 1, 0), 0)),
                  pl.BlockSpec((HALO, d), lambda i: (jnp.minimum((i + 1) * hb, t // HALO - 1), 0)),
                  whole(mod), whole(w, pipeline_mode=pl.Buffered(1)),
                  whole(cw), whole(cb), whole(dtc), whole(lb)],
        out_specs=[pl.BlockSpec((tm, n_a), lambda i: (i, 0)), pl.BlockSpec((tm, n_b), lambda i: (i, 0))],
        out_shape=[jax.ShapeDtypeStruct((t, n_a), F32), jax.ShapeDtypeStruct((t, n_b), BF16)],
        scratch_shapes=[pltpu.VMEM((tm + 2 * HALO, d), BF16)],
        compiler_params=_cparams(("arbitrary",), 56),
        name="even_in",
    )(h, h, h, mod, w, cw, cb, dtc, lb)


def _scan_block_prep(xd_ref, qv_ref, bcum_ref, kk_ref, dtc_ref, d):
    n_hg = bcum_ref.shape[1]
    xbc = xd_ref[:, :ZS_DT]
    dtv = xd_ref[:, ZS_DT + d * LANES:ZS_DT + (d + 1) * LANES]
    acum = _chunk_cumsum(-dtv * jnp.exp(dtc_ref[d, 1:2, :]), bool(d))
    return xbc, dtv, acum, kk_ref[...].astype(F32), bcum_ref[...], qv_ref[:, :n_hg].astype(F32), qv_ref[:, n_hg:]


DECAY_LIMIT = 80.0


def _hg_intra_exact(qs, bcum, bcum_ref, kk_ref, v_ref, acc_ref, row0, reverse):
    Q = SCAN_CHUNK
    acc_ref[...] = jnp.zeros_like(acc_ref)
    t_idx = lax.broadcasted_iota(jnp.int32, (Q, 1), 0)

    def body(s, carry):
        r = row0 + s
        b_s, k_s, v_s = bcum_ref[pl.ds(r, 1), :], kk_ref[pl.ds(r, 1), :], v_ref[pl.ds(r, 1), :]
        allowed = (t_idx <= s) if reverse else (t_idx >= s)
        w = jnp.where(allowed, jnp.exp(jnp.minimum(bcum - b_s, 0.0)) * (qs * k_s), 0.0)
        for h in range(HG_HEADS):
            sl = slice(h * HG_HEAD_DIM, (h + 1) * HG_HEAD_DIM)
            acc_ref[:, sl] += jnp.sum(w[:, sl], axis=-1, keepdims=True) * v_s[:, sl]
        return carry

    lax.fori_loop(0, Q, body, 0)
    return acc_ref[...]


def _scan_chunk(prep, rows, dsk_ref, e_ref, st_ref, gt_ref, yo_ref, d, reverse, exact=None):
    Q = SCAN_CHUNK
    xbc, dtv, acum, kk, bcum, qs, vb = (a[rows] for a in prep)
    ti = lax.broadcasted_iota(jnp.int32, (Q, Q), 0)
    si = lax.broadcasted_iota(jnp.int32, (Q, Q), 1)
    mask = (ti <= si) if reverse else (ti >= si)
    edge = 0 if reverse else Q - 1
    width = SSD_HEADS * SSD_HEAD_DIM
    gw = width // SSD_GROUPS
    xs = xbc[:, :width]

    last = acum[edge:edge + 1, :]
    acum_t = acum.T
    e_a = jnp.exp(acum)
    w_end = jnp.exp(last - acum)
    e_last = jnp.broadcast_to(jnp.exp(last), (SUBLANES, LANES))
    xe = _split_dot(jnp.concatenate([dtv, e_a, w_end], axis=0), e_ref[...], 1)
    dt_x, ea_x, wend_x = xe[:Q], xe[Q:2 * Q], xe[2 * Q:3 * Q]
    elast_x = _split_dot(e_last, e_ref[...], 3)[0:1]
    xdt = xs * dt_x
    xw = (xdt * wend_x).astype(BF16)
    xdt_b = xdt.astype(BF16)
    lane = lax.broadcasted_iota(jnp.int32, (Q, LANES), 1)
    lo = lane < SSD_HEAD_DIM
    zero_b = jnp.zeros((Q, LANES), BF16)
    ys = []
    for g in range(SSD_GROUPS):
        bm = xbc[:, width + g * SSD_STATE:width + (g + 1) * SSD_STATE].astype(BF16)
        cm = xbc[:, width + (SSD_GROUPS + g) * SSD_STATE:width + (SSD_GROUPS + g + 1) * SSD_STATE].astype(BF16)
        cb = lax.dot_general(cm, bm, (((1,), (1,)), ((), ())), preferred_element_type=F32)
        y_state = jnp.dot(cm, st_ref[d, g].astype(BF16), preferred_element_type=F32)
        pairs = []
        for hp in range(gw // LANES):
            xp = xdt_b[:, g * gw + hp * LANES:g * gw + (hp + 1) * LANES]
            acc = None
            for half in range(2):
                h = (g * gw + hp * LANES) // SSD_HEAD_DIM + half
                seg = acum[:, h:h + 1] - acum_t[h:h + 1, :]
                m = jnp.where(mask, cb * jnp.exp(seg), 0.0).astype(BF16)
                xh = jnp.where(lo, xp, zero_b) if half == 0 else jnp.where(lo, zero_b, xp)
                part = jnp.dot(m, xh, preferred_element_type=F32)
                acc = part if acc is None else acc + part
            pairs.append(acc)
        y_g = jnp.concatenate(pairs, axis=1) + y_state * ea_x[:, g * gw:(g + 1) * gw]
        ys.append(y_g)
        upd = lax.dot_general(bm, xw[:, g * gw:(g + 1) * gw], (((0,), (0,)), ((), ())),
                              preferred_element_type=F32)
        st_ref[d, g] = st_ref[d, g] * elast_x[:, g * gw:(g + 1) * gw] + upd
    y = jnp.concatenate(ys, axis=1)
    if not reverse:
        y = y + dsk_ref[...] * xs
    yo_ref[rows, :width] = y.astype(yo_ref.dtype)

    blast = bcum[edge:edge + 1, :]
    qe = (qs * jnp.exp(bcum)).astype(BF16)
    k_end = kk * jnp.exp(blast - bcum)
    if exact is None:
        k_til = (k_end * jnp.exp(-blast)).astype(BF16)
    else:
        intra = _hg_intra_exact(qs, bcum, *exact, rows.start, reverse)
    k_end = k_end.astype(BF16)
    e_blast = jnp.exp(blast)
    outs = []
    for h in range(HG_HEADS):
        sl = slice(h * HG_HEAD_DIM, (h + 1) * HG_HEAD_DIM)
        gt = gt_ref[d, h]
        if exact is None:
            att = lax.dot_general(qe[:, sl], k_til[:, sl], (((1,), (1,)), ((), ())), preferred_element_type=F32)
            att = jnp.where(mask, att, 0.0).astype(BF16)
            o_h = jnp.dot(att, vb[:, sl], preferred_element_type=F32)
        else:
            o_h = intra[:, sl]
        o_h = o_h + lax.dot_general(qe[:, sl], gt.astype(BF16), (((1,), (1,)), ((), ())),
                                    preferred_element_type=F32)
        outs.append(o_h)
        upd = lax.dot_general(vb[:, sl], k_end[:, sl], (((0,), (0,)), ((), ())), preferred_element_type=F32)
        gt_ref[d, h] = gt * e_blast[:, sl] + upd
    yo_ref[rows, width:] = jnp.concatenate(outs, axis=1).astype(yo_ref.dtype)


def _even_scan_kernel(*refs, nblk, cpb):
    fwd_in, bwd_in = refs[0:4], refs[4:8]
    dtc_ref, dsk_ref, e_ref, h0s_ref, h0g_ref = refs[8:13]
    yof_ref, yob_ref, st_ref, gt_ref, kkf_ref, vf_ref, acc_ref = refs[13:20]
    c = pl.program_id(0)

    @pl.when(c == 0)
    def _():
        st_ref[...] = h0s_ref[...]
        gt_ref[...] = h0g_ref[...]

    streams = ((False, fwd_in, yof_ref), (True, bwd_in, yob_ref))

    def run(exact):
        preps = [_scan_block_prep(*ins, dtc_ref, d) for d, (_, ins, _) in enumerate(streams)]
        if exact:
            for d, (_, ins, _) in enumerate(streams):
                kkf_ref[d] = preps[d][3]
                vf_ref[d] = preps[d][6].astype(F32)
        for step in range(cpb):
            for d, (reverse, ins, yo_ref) in enumerate(streams):
                ci = (cpb - 1 - step) if reverse else step
                rows = slice(ci * SCAN_CHUNK, (ci + 1) * SCAN_CHUNK)
                refs_exact = (ins[2], kkf_ref.at[d], vf_ref.at[d], acc_ref) if exact else None
                _scan_chunk(preps[d], rows, dsk_ref, e_ref, st_ref, gt_ref, yo_ref, d, reverse, refs_exact)

    totals = [ref[ci * SCAN_CHUNK + edge:ci * SCAN_CHUNK + edge + 1, :]
              for ref, edge in ((fwd_in[2], SCAN_CHUNK - 1), (bwd_in[2], 0)) for ci in range(cpb)]
    strongest = jnp.min(jnp.concatenate(totals, axis=0))
    pl.when(strongest >= -DECAY_LIMIT)(lambda: run(False))
    pl.when(strongest < -DECAY_LIMIT)(lambda: run(True))


def _even_scan(za, zb, consts, h0s, h0g):
    t = za.shape[0]
    nch = t // SCAN_CHUNK
    cpb = max(k for k in (4, 2, 1) if nch % k == 0)
    rows = cpb * SCAN_CHUNK
    nblk = nch // cpb
    dtc, dsk, e16 = consts
    n_x = SSD_HEADS * SSD_HEAD_DIM
    n_hg = HG_HEADS * HG_HEAD_DIM
    gate_a, gate_b = (tuple(tile[i] for tile in EVEN_TILES if tile[0] == "gate") for i in (1, 2))

    def stream_specs(bidx, d):
        return [
            pl.BlockSpec((rows, ZS_XD), lambda c: (bidx(c), 0)),
            pl.BlockSpec((rows, 2 * n_hg), lambda c: (bidx(c), 0)),
            pl.BlockSpec((rows, n_hg), lambda c: (bidx(c), gate_a[d])),
            pl.BlockSpec((rows, n_hg), lambda c: (bidx(c), gate_b[d])),
        ]

    fwd = lambda c: c
    bwd = lambda c: nblk - 1 - c
    whole = lambda a: pl.BlockSpec(a.shape, lambda c: (0,) * a.ndim)
    in_specs = stream_specs(fwd, 0) + stream_specs(bwd, 1) + [whole(a) for a in (dtc, dsk, e16, h0s, h0g)]
    out_specs = [pl.BlockSpec((rows, n_x + n_hg), lambda c: (c, 0)),
                 pl.BlockSpec((rows, n_x + n_hg), lambda c: (bwd(c), 0)), whole(h0s), whole(h0g)]
    out_shape = [jax.ShapeDtypeStruct((t, n_x + n_hg), BF16), jax.ShapeDtypeStruct((t, n_x + n_hg), BF16),
                 jax.ShapeDtypeStruct(h0s.shape, F32), jax.ShapeDtypeStruct(h0g.shape, F32)]
    return pl.pallas_call(
        functools.partial(_even_scan_kernel, nblk=nblk, cpb=cpb),
        grid=(nblk,),
        in_specs=in_specs,
        out_specs=out_specs,
        out_shape=out_shape,
        scratch_shapes=[pltpu.VMEM((2, rows, n_hg), F32), pltpu.VMEM((2, rows, n_hg), F32),
                        pltpu.VMEM((SCAN_CHUNK, n_hg), F32)],
        compiler_params=_cparams(("arbitrary",), 48),
        name="even_scan",
    )(za, zb, za, zb, za, zb, za, zb, dtc, dsk, e16, h0s, h0g)


def _ffn_block(h, mod_ref, wg_ref, wu_ref, wd_ref, final):
    a = _modulate(h, mod_ref).astype(BF16)
    gate = jnp.dot(a, wg_ref[...], preferred_element_type=F32)
    up = jnp.dot(a, wu_ref[...], preferred_element_type=F32)
    act = (_silu(gate) * up).astype(BF16)
    out = h + mod_ref[3:4, :] * jnp.dot(act, wd_ref[...], preferred_element_type=F32)
    if final:
        ms = jnp.mean(out * out, axis=-1, keepdims=True)
        out = out * lax.rsqrt(ms + EPS) * mod_ref[4:5, :]
    return out


def _resident(a):
    return pl.BlockSpec(a.shape, lambda i: (0,) * a.ndim, pipeline_mode=pl.Buffered(1))


def _even_out_kernel(h_ref, z_ref, g_ref, yof_ref, yob_ref, w_ref, vec_ref, mod_ref, wg_ref, wu_ref, wd_ref, o_ref):
    n_x = z_ref.shape[1]
    yo = yof_ref[...].astype(F32) + yob_ref[...].astype(F32)
    y = yo[:, :n_x] * z_ref[...].astype(F32)
    o = yo[:, n_x:]
    gw = n_x // SSD_GROUPS
    parts = []
    for g in range(SSD_GROUPS):
        yg = y[:, g * gw:(g + 1) * gw]
        ms = jnp.mean(yg * yg, axis=-1, keepdims=True)
        parts.append(yg * lax.rsqrt(ms + EPS))
    yn = (jnp.concatenate(parts, axis=1) * vec_ref[0:1, :]).astype(BF16)
    parts = []
    for hh in range(HG_HEADS):
        oh = o[:, hh * HG_HEAD_DIM:(hh + 1) * HG_HEAD_DIM]
        ms = jnp.mean(oh * oh, axis=-1, keepdims=True)
        parts.append(oh * lax.rsqrt(ms + EPS))
    on = (jnp.concatenate(parts, axis=1) * vec_ref[1:2, :] * g_ref[...].astype(F32)).astype(BF16)
    half = yn.shape[1]
    mix = jnp.dot(yn, w_ref[:half, :], preferred_element_type=F32)
    mix = mix + jnp.dot(on, w_ref[half:, :], preferred_element_type=F32)
    h1 = h_ref[...] + vec_ref[2:3, :] * mix
    o_ref[...] = _ffn_block(h1, mod_ref, wg_ref, wu_ref, wd_ref, final=False)


def _even_out(h, zb, yo_f, yo_b, w_out, vec, mod, wg, wu, wd):
    t, d = h.shape
    tm = min(t, 256)
    n_x = SSD_HEADS * SSD_HEAD_DIM
    n_hg = HG_HEADS * HG_HEAD_DIM
    row = lambda i: (i, 0)
    z_tile, g_tile = (next(fb for k, (_, _, fb) in enumerate(EVEN_TILES) if k * EVEN_TN == start)
                      for start in (ZS_Z, ZS_G))
    return pl.pallas_call(
        _even_out_kernel,
        grid=(t // tm,),
        in_specs=[pl.BlockSpec((tm, d), row),
                  pl.BlockSpec((tm, n_x), lambda i: (i, z_tile)),
                  pl.BlockSpec((tm, n_hg), lambda i: (i, g_tile)),
                  pl.BlockSpec((tm, n_x + n_hg), row), pl.BlockSpec((tm, n_x + n_hg), row),
                  _resident(w_out), _resident(vec), _resident(mod), _resident(wg), _resident(wu), _resident(wd)],
        out_specs=pl.BlockSpec((tm, d), row),
        out_shape=jax.ShapeDtypeStruct((t, d), F32),
        compiler_params=_cparams(("arbitrary",), 56),
        name="even_out_ffn",
    )(h, zb, zb, yo_f, yo_b, w_out, vec, mod, wg, wu, wd)


def _odd_in_kernel(h_ref, mod_ref, w_ref, o_ref, u_ref):
    a = _modulate(h_ref[...], mod_ref).astype(BF16)
    u = jnp.dot(a, w_ref[...], preferred_element_type=F32)
    nj, grid_w, rb, _ = o_ref.shape
    for j in range(nj):
        u_ref[j] = u[:, j * LANES:(j + 1) * LANES]
    for wi in range(grid_w):
        for j in range(nj):
            o_ref[j, wi] = u_ref[j, pl.ds(wi, rb, stride=grid_w), :]


def _odd_in(h, mod, w, grid_w):
    t, d = h.shape
    rows = t // grid_w
    rb = min(rows, 4 * SUBLANES)
    n = w.shape[1]
    nj = n // LANES
    out = pl.pallas_call(
        _odd_in_kernel,
        grid=(rows // rb,),
        in_specs=[pl.BlockSpec((rb * grid_w, d), lambda i: (i, 0)),
                  pl.BlockSpec((SUBLANES, d), lambda i: (0, 0)),
                  pl.BlockSpec(w.shape, lambda i: (0, 0))],
        out_specs=pl.BlockSpec((nj, grid_w, rb, LANES), lambda i: (0, 0, i, 0)),
        out_shape=jax.ShapeDtypeStruct((nj, grid_w, rows, LANES), F32),
        scratch_shapes=[pltpu.VMEM((nj, rb * grid_w, LANES), F32)],
        compiler_params=_cparams(("arbitrary",), 48),
        name="odd_in",
    )(h, mod, w)
    return out.reshape(nj, t, LANES)


def _sub_chunk_rows(u_ref, nb):
    return [u_ref[pl.ds(s, nb, stride=S5_SUB), :] for s in range(S5_SUB)]


def _s5_inject_kernel(u_ref, win_ref, s_ref):
    x = jnp.concatenate(_sub_chunk_rows(u_ref, s_ref.shape[0]), axis=1).astype(BF16)
    s_ref[...] = jnp.dot(x, win_ref[...], preferred_element_type=F32)


def _s5_inject(u, win):
    nj, t, _ = u.shape
    n = t // S5_SUB
    nb = min(n, 512)
    width, ns = win.shape[1:]
    return pl.pallas_call(
        _s5_inject_kernel,
        grid=(nj, n // nb),
        in_specs=[pl.BlockSpec((None, nb * S5_SUB, LANES), lambda j, b: (j, b, 0)),
                  pl.BlockSpec((None, width, ns), lambda j, b: (j, 0, 0))],
        out_specs=pl.BlockSpec((None, nb, ns), lambda j, b: (j, b, 0)),
        out_shape=jax.ShapeDtypeStruct((nj, n, ns), F32),
        compiler_params=_cparams(("arbitrary", "arbitrary"), 40),
        name="s5_inject",
    )(u, win)


def _s5_carry_kernel(sf_ref, sb_ref, lam_ref, h0_ref, pf_ref, pb_ref, hf_ref, tf_ref, tb_ref, c_ref):
    b = pl.program_id(0)
    nj, kb, ns = sf_ref.shape
    half = ns // 2

    @pl.when(b == 0)
    def _():
        c_ref[...] = h0_ref[...]

    for j in range(nj):
        tf_ref[:, j, :] = sf_ref[j]
        tb_ref[:, j, :] = sb_ref[j]

    def advance(d, c, s):
        lr, li = lam_ref[d, :, :half], lam_ref[d, :, half:]
        pr, pi = c[:, :half], c[:, half:]
        return jnp.concatenate([lr * pr - li * pi + s[:, :half], lr * pi + li * pr + s[:, half:]], axis=1)

    def step(i, carry):
        cf, cb = carry
        kr = kb - 1 - i
        sf = tf_ref[i]
        sb = tb_ref[kr]
        tf_ref[i] = cf
        tb_ref[kr] = cb
        return advance(0, cf, sf), advance(1, cb, sb)

    cf, cb = lax.fori_loop(0, kb, step, (c_ref[0], c_ref[1]), unroll=4)
    c_ref[0] = cf
    c_ref[1] = cb
    for j in range(nj):
        pf_ref[j] = tf_ref[:, j, :]
        pb_ref[j] = tb_ref[:, j, :]

    @pl.when(b == pl.num_programs(0) - 1)
    def _():
        hf_ref[...] = c_ref[...]


def _s5_carry(s_f, s_b, lam, h0):
    nj, n, ns = s_f.shape
    kb = min(n, 256)
    nblk = n // kb
    fwd = pl.BlockSpec((nj, kb, ns), lambda b: (0, b, 0))
    bwd = pl.BlockSpec((nj, kb, ns), lambda b: (0, nblk - 1 - b, 0))
    whole = pl.BlockSpec(lam.shape, lambda b: (0, 0, 0))
    return pl.pallas_call(
        _s5_carry_kernel,
        grid=(nblk,),
        in_specs=[fwd, bwd, whole, whole],
        out_specs=[fwd, bwd, whole],
        out_shape=[jax.ShapeDtypeStruct(s_f.shape, F32), jax.ShapeDtypeStruct(s_b.shape, F32),
                   jax.ShapeDtypeStruct(lam.shape, F32)],
        scratch_shapes=[pltpu.VMEM((kb, nj, ns), F32), pltpu.VMEM((kb, nj, ns), F32), pltpu.VMEM(lam.shape, F32)],
        compiler_params=_cparams(("arbitrary",), 48),
        name="s5_carry",
    )(s_f, s_b, lam, h0)


def _s5_readout_kernel(u_ref, p_ref, wout_ref, tt_ref, add_ref, y_ref, *, reverse):
    nb = p_ref.shape[0]
    us = _sub_chunk_rows(u_ref, nb)
    xb = jnp.concatenate(us, axis=1).astype(BF16)
    pb = p_ref[...].astype(BF16)
    width = xb.shape[1]
    tile = 2 * LANES
    for m in range(width // tile):
        cols = slice(m * tile, (m + 1) * tile)
        rows = slice(m * tile, width) if reverse else slice(0, (m + 1) * tile)
        y = jnp.dot(pb, wout_ref[:, cols], preferred_element_type=F32)
        y = y + jnp.dot(xb[:, rows], tt_ref[rows, cols], preferred_element_type=F32)
        for q in range(tile // LANES):
            s = m * (tile // LANES) + q
            other = add_ref[pl.ds(s, nb, stride=S5_SUB), :] if reverse else add_ref[...] * us[s]
            y_ref[pl.ds(s, nb, stride=S5_SUB), :] = y[:, q * LANES:(q + 1) * LANES] + other


def _s5_readout(u, p, wout, tt, add, reverse):
    nj, t, _ = u.shape
    n = t // S5_SUB
    nb = min(n, 256)
    ns, width = wout.shape[1:]
    tokens = pl.BlockSpec((None, nb * S5_SUB, LANES), lambda j, b: (j, b, 0))
    add_spec = tokens if reverse else pl.BlockSpec((None, 1, LANES), lambda j, b: (j, 0, 0))
    return pl.pallas_call(
        functools.partial(_s5_readout_kernel, reverse=reverse),
        grid=(nj, n // nb),
        in_specs=[tokens,
                  pl.BlockSpec((None, nb, ns), lambda j, b: (j, b, 0)),
                  pl.BlockSpec((None, ns, width), lambda j, b: (j, 0, 0)),
                  pl.BlockSpec((None, width, width), lambda j, b: (j, 0, 0)),
                  add_spec],
        out_specs=tokens,
        out_shape=jax.ShapeDtypeStruct(u.shape, F32),
        compiler_params=_cparams(("arbitrary", "arbitrary"), 48),
        name="s5_readout_bwd" if reverse else "s5_readout_fwd",
    )(u, p, wout, tt, add)


def _odd_out_kernel(h_ref, y_ref, wv_ref, wgl_ref, vec_ref, mod_ref, wg_ref, wu_ref, wd_ref, o_ref, a_ref):
    nj, grid_w, rb, _ = y_ref.shape
    for wi in range(grid_w):
        for j in range(nj):
            a_ref[j, pl.ds(wi, rb, stride=grid_w), :] = y_ref[j, wi]
    a = jax.nn.gelu(jnp.concatenate([a_ref[j] for j in range(nj)], axis=1)).astype(BF16)
    val = jnp.dot(a, wv_ref[...], preferred_element_type=F32)
    gate = jnp.dot(a, wgl_ref[...], preferred_element_type=F32)
    h3 = h_ref[...] + vec_ref[0:1, :] * (val * jax.nn.sigmoid(gate))
    o_ref[...] = _ffn_block(h3, mod_ref, wg_ref, wu_ref, wd_ref, final=True)


def _odd_out(h, y, w_val, w_gate, vec, mod, wg, wu, wd, grid_w):
    t, d = h.shape
    rows = t // grid_w
    rb = min(rows, SUBLANES)
    nj = y.shape[0]
    tok = pl.BlockSpec((rb * grid_w, d), lambda i: (i, 0))
    return pl.pallas_call(
        _odd_out_kernel,
        grid=(rows // rb,),
        in_specs=[tok, pl.BlockSpec((nj, grid_w, rb, LANES), lambda i: (0, 0, i, 0)),
                  _resident(w_val), _resident(w_gate), _resident(vec), _resident(mod),
                  _resident(wg), _resident(wu), _resident(wd)],
        out_specs=tok,
        out_shape=jax.ShapeDtypeStruct((t, d), F32),
        scratch_shapes=[pltpu.VMEM((nj, rb * grid_w, LANES), F32)],
        compiler_params=_cparams(("arbitrary",), 56),
        name="odd_out_ffn",
    )(h, y.reshape(nj, grid_w, rows, LANES), w_val, w_gate, vec, mod, wg, wu, wd)


def _s5_prep_kernel(prm_ref, bre_ref, bim_ref, colp_ref, cre_ref, cim_ref, win_ref, wout_ref, tt_ref, lam_ref, *,
                    reverse):
    T = S5_SUB
    ns = prm_ref.shape[1]
    a_re, a_im, dt = prm_ref[0:1, :], prm_ref[1:2, :], jnp.exp(prm_ref[2:3, :])
    tau = lax.broadcasted_iota(jnp.int32, (3 * SUBLANES, ns), 0).astype(F32)
    mag = jnp.exp(a_re * dt * tau)
    pr = mag * jnp.cos(a_im * dt * tau)
    pi = mag * jnp.sin(a_im * dt * tau)
    lr, li = pr[1:2], pi[1:2]
    den = a_re * a_re + a_im * a_im
    cr = ((lr - 1.0) * a_re + li * a_im) / den
    ci = (li * a_re - (lr - 1.0) * a_im) / den
    row_g = lax.broadcasted_iota(jnp.int32, (LANES, ns), 0) // S5_GROUP
    col_g = lax.broadcasted_iota(jnp.int32, (LANES, ns), 1) // S5_STATE
    same = row_g == col_g
    bbr = jnp.where(same, cr * bre_ref[...] - ci * bim_ref[...], 0.0)
    bbi = jnp.where(same, cr * bim_ref[...] + ci * bre_ref[...], 0.0)
    for s in range(T):
        e = s if reverse else T - 1 - s
        rows = slice(s * LANES, (s + 1) * LANES)
        win_ref[rows, :ns] = (pr[e:e + 1] * bbr - pi[e:e + 1] * bbi).astype(BF16)
        win_ref[rows, ns:] = (pr[e:e + 1] * bbi + pi[e:e + 1] * bbr).astype(BF16)
    lam_ref[...] = jnp.broadcast_to(jnp.concatenate([pr[T:T + 1], pi[T:T + 1]], axis=1), lam_ref.shape)
    dt_c = jnp.exp(colp_ref[2])
    mag_c = jnp.exp(colp_ref[0] * dt_c)
    lr_c = mag_c * jnp.cos(colp_ref[1] * dt_c)
    li_c = mag_c * jnp.sin(colp_ref[1] * dt_c)
    row_gc = lax.broadcasted_iota(jnp.int32, (ns, LANES), 0) // S5_STATE
    col_gc = lax.broadcasted_iota(jnp.int32, (ns, LANES), 1) // S5_GROUP
    same_c = row_gc == col_gc
    c_re = jnp.where(same_c, cre_ref[...], 0.0)
    c_im = jnp.where(same_c, cim_ref[...], 0.0)
    bb = jnp.concatenate([bbr, bbi], axis=1)
    bb_hi = bb.astype(BF16)
    bb_lo = (bb - bb_hi.astype(F32)).astype(BF16)
    cur_r = jnp.ones((ns, LANES), F32)
    cur_i = jnp.zeros((ns, LANES), F32)
    kblk = []
    for e in range(T + 1):
        wo_r = c_re * cur_r - c_im * cur_i
        wo_i = -(c_re * cur_i + c_im * cur_r)
        if e >= 1:
            t = (T - e) if reverse else e - 1
            wout_ref[:ns, t * LANES:(t + 1) * LANES] = wo_r.astype(BF16)
            wout_ref[ns:, t * LANES:(t + 1) * LANES] = wo_i.astype(BF16)
        if e < T:
            wo = jnp.concatenate([wo_r, wo_i], axis=0)
            wo_hi = wo.astype(BF16)
            wo_lo = (wo - wo_hi.astype(F32)).astype(BF16)
            k = jnp.dot(bb_hi, wo_hi, preferred_element_type=F32)
            k = k + jnp.dot(bb_hi, wo_lo, preferred_element_type=F32) + jnp.dot(bb_lo, wo_hi, preferred_element_type=F32)
            kblk.append(k.astype(BF16))
        cur_r, cur_i = cur_r * lr_c - cur_i * li_c, cur_r * li_c + cur_i * lr_c
    zeros = jnp.zeros((LANES, LANES), BF16)
    for s in range(T):
        for t in range(T):
            lag = (s - t) if reverse else (t - s)
            tt_ref[s * LANES:(s + 1) * LANES, t * LANES:(t + 1) * LANES] = kblk[lag] if lag >= 0 else zeros


def _s5_prep(a_re, a_im, log_dt, b_re, b_im, c_re, c_im, reverse):
    g, p = a_re.shape
    cdim = b_re.shape[2]
    nj = g // S5_GPB
    ns = S5_GPB * p
    width = S5_SUB * LANES
    prm = jnp.zeros((nj, SUBLANES, ns), F32)
    prm = prm.at[:, 0].set(a_re.reshape(nj, ns)).at[:, 1].set(a_im.reshape(nj, ns))
    prm = prm.at[:, 2].set(jnp.repeat(log_dt, p).reshape(nj, ns))
    colp = jnp.broadcast_to(prm[:, :3, :, None], (nj, 3, ns, LANES))
    row_tile = lambda b: jnp.tile(b.reshape(nj, S5_GPB, p, cdim).transpose(0, 3, 1, 2).reshape(nj, 1, cdim, ns),
                                  (1, S5_GPB, 1, 1)).reshape(nj, LANES, ns)
    col_tile = lambda c_: jnp.tile(c_.reshape(nj, S5_GPB, cdim, p).transpose(0, 1, 3, 2).reshape(nj, ns, 1, cdim),
                                   (1, 1, S5_GPB, 1)).reshape(nj, ns, LANES)
    blk = lambda *shape: pl.BlockSpec((None,) + shape, lambda j: (j,) + (0,) * len(shape))
    return pl.pallas_call(
        functools.partial(_s5_prep_kernel, reverse=reverse),
        grid=(nj,),
        in_specs=[blk(SUBLANES, ns), blk(LANES, ns), blk(LANES, ns), blk(3, ns, LANES), blk(ns, LANES),
                  blk(ns, LANES)],
        out_specs=[blk(width, 2 * ns), blk(2 * ns, width), blk(width, width), blk(SUBLANES, 2 * ns)],
        out_shape=[jax.ShapeDtypeStruct((nj, width, 2 * ns), BF16), jax.ShapeDtypeStruct((nj, 2 * ns, width), BF16),
                   jax.ShapeDtypeStruct((nj, width, width), BF16), jax.ShapeDtypeStruct((nj, SUBLANES, 2 * ns), F32)],
        compiler_params=_cparams(("arbitrary",), 56),
        name="s5_prep_bwd" if reverse else "s5_prep_fwd",
    )(prm, row_tile(b_re), row_tile(b_im), colp, col_tile(c_re), col_tile(c_im))


def _regroup_kernel(w_ref, o_ref, *, pieces):
    col = 0
    for start, width in pieces:
        if start is None:
            o_ref[:, col:col + width] = jnp.zeros((o_ref.shape[0], width), o_ref.dtype)
        else:
            o_ref[:, col:col + width] = w_ref[:, start:start + width].astype(o_ref.dtype)
        col += width


def _regroup_bf16(w, layer, pieces):
    _, rows, cols = w.shape
    n_out = sum(width for _, width in pieces)
    block_bytes = 8 * 1024 * 1024
    tr = max(k for k in (512, 256, 128, 64, 32, 16) if rows % k == 0 and k * cols * 4 <= block_bytes)
    return pl.pallas_call(
        functools.partial(_regroup_kernel, pieces=pieces),
        grid=(rows // tr,),
        in_specs=[pl.BlockSpec((None, tr, cols), lambda i: (layer, i, 0))],
        out_specs=pl.BlockSpec((tr, n_out), lambda i: (i, 0)),
        out_shape=jax.ShapeDtypeStruct((rows, n_out), BF16),
        compiler_params=_cparams(("arbitrary",), 48),
        name="weights_bf16",
    )(w)


def _bf16(w, layer):
    return _regroup_bf16(w, layer, ((0, w.shape[2]),))


def _rows(*vs, width):
    out = jnp.zeros((SUBLANES, width), F32)
    for i, v in enumerate(vs):
        out = out.at[i].set(v.astype(F32))
    return out


def kernel(x, c, ctx, c_ctx, ada_w, ada_b, norm_mix_g, norm_ffn_g, ffn_w_gate, ffn_w_up, ffn_w_down, final_norm_g, ev_w_in, ev_w_out, ssd_conv_w, ssd_conv_b, ssd_dt_bias, ssd_a_log, ssd_d, ssd_norm_g, hg_lb_logits, hg_norm_g, od_w_in, s5_a_re, s5_a_im, s5_log_dt, s5_b_re, s5_b_im, s5_c_re, s5_c_im, s5_d, od_w_val, od_w_gate):
    d = x.shape[-1]
    lat = x[0].astype(F32)
    hc = ctx[0].astype(F32)

    m = _ada(_rows(c[0], c_ctx, width=d), ada_w, ada_b)

    def ada_vecs(layer, stream):
        return [m[layer, stream, i * d:(i + 1) * d] for i in range(6)]

    n_x = SSD_HEADS * SSD_HEAD_DIM
    n_xbc = n_x + 2 * SSD_GROUPS * SSD_STATE
    n_hg = HG_HEADS * HG_HEAD_DIM
    o_z, o_xbc, o_dt = 0, n_x, n_x + n_xbc
    o_q = o_dt + 2 * SSD_HEADS
    o_f = o_q + n_hg
    o_v = o_f + 2 * n_hg
    o_g = o_v + n_hg
    w_in = _regroup_bf16(ev_w_in, 0, (
        (o_xbc, n_xbc),
        (o_dt, SSD_HEADS), (None, LANES - SSD_HEADS), (o_dt + SSD_HEADS, SSD_HEADS), (None, LANES - SSD_HEADS),
        (None, ZS_Q - ZS_XD),
        (o_q, n_hg), (o_v, n_hg), (o_f, 2 * n_hg), (o_z, n_x), (o_g, n_hg)))
    cw = jnp.zeros((SUBLANES, n_xbc), F32).at[:SSD_CONV].set(ssd_conv_w[0])
    cb = ssd_conv_b[0].reshape(1, n_xbc)
    dtc = jnp.zeros((2, SUBLANES, LANES), F32)
    dtc = dtc.at[:, 0, :SSD_HEADS].set(ssd_dt_bias[0]).at[:, 1, :SSD_HEADS].set(ssd_a_log[0])
    dsk = jnp.repeat(ssd_d[0], SSD_HEAD_DIM).reshape(1, n_x)
    lower = jnp.cumsum(jax.nn.softmax(hg_lb_logits.astype(F32), axis=0), axis=0)[0].reshape(1, n_hg)
    head_of_lane = jnp.arange(n_x) // SSD_HEAD_DIM
    e16 = (jnp.arange(LANES)[:, None] == head_of_lane[None, :]).astype(BF16)
    consts = (dtc, dsk, e16)
    w_out = _bf16(ev_w_out, 0)
    out_vec = lambda gate: _rows(ssd_norm_g[0], jnp.tile(hg_norm_g[0], HG_HEADS), gate, width=d)
    wg0, wu0, wd0 = (_bf16(t_, 0) for t_ in (ffn_w_gate, ffn_w_up, ffn_w_down))

    s_state = jnp.zeros((2, SSD_GROUPS, SSD_STATE, n_x // SSD_GROUPS), F32)
    g_state = jnp.zeros((2, HG_HEADS, HG_HEAD_DIM, HG_HEAD_DIM), F32)
    layer0 = {}
    for stream, h in ((1, hc), (0, lat)):
        sm, scm, gm, sf, scf, gf = ada_vecs(0, stream)
        za, zb = _even_in(h, _rows(norm_mix_g[0], scm, sm, width=d), w_in, cw, cb, dtc, lower)
        yo_f, yo_b, s_state, g_state = _even_scan(za, zb, consts, s_state, g_state)
        layer0[stream] = _even_out(h, zb, yo_f, yo_b, w_out, out_vec(gm),
                                   _rows(norm_ffn_g[0], scf, sf, gf, width=d), wg0, wu0, wd0)

    lat, hc = layer0[0], layer0[1]
    w_s5 = _bf16(od_w_in, 0)
    u = {}
    for stream, h, gw in ((1, hc, 1), (0, lat, GRID_W)):
        sm, scm = ada_vecs(1, stream)[:2]
        u[stream] = _odd_in(h, _rows(norm_mix_g[1], scm, sm, width=d), w_s5, gw)
    mats = [_s5_prep(s5_a_re[0, di], s5_a_im[0, di], s5_log_dt[0, di], s5_b_re[0, di], s5_b_im[0, di],
                     s5_c_re[0, di], s5_c_im[0, di], reverse=bool(di)) for di in range(2)]
    lam = jnp.stack([mats[0][3][:, 0], mats[1][3][:, 0]])
    carry = jnp.zeros_like(lam)
    for stream in (1, 0):
        s_f, s_b = (_s5_inject(u[stream], mats[di][0]) for di in range(2))
        p_f, p_b, carry = _s5_carry(s_f, s_b, lam, carry)
    nj = u[0].shape[0]
    y = _s5_readout(u[0], p_f, mats[0][1], mats[0][2], s5_d[0].reshape(nj, 1, LANES), reverse=False)
    y = _s5_readout(u[0], p_b, mats[1][1], mats[1][2], y, reverse=True)
    _, _, gm, sf, scf, gf = ada_vecs(1, 0)
    wg1, wu1, wd1 = (_bf16(t_, 1) for t_ in (ffn_w_gate, ffn_w_up, ffn_w_down))
    out = _odd_out(lat, y, _bf16(od_w_val, 0), _bf16(od_w_gate, 0), _rows(gm, width=d),
                   _rows(norm_ffn_g[1], scf, sf, gf, final_norm_g, width=d), wg1, wu1, wd1, GRID_W)
    return out[None].astype(x.dtype)
```

```python
import functools

import jax
import jax.numpy as jnp
from jax import lax
from jax.experimental import pallas as pl
from jax.experimental.pallas import tpu as pltpu

F32 = jnp.float32
BF16 = jnp.bfloat16
EPS = 1e-6

LANES = 128
SUBLANES = 8
GRID_W = 64
SCAN_CHUNK = 64
SSD_HEADS = 16
SSD_HEAD_DIM = 64
SSD_STATE = 128
SSD_GROUPS = 2
SSD_CONV = 5
HG_HEADS = 8
HG_HEAD_DIM = 128
S5_GROUP = 16
S5_STATE = 64
S5_SUB = 16
S5_GPB = LANES // S5_GROUP

ZS_XBC = 0
ZS_DT = 1536
ZS_XD = 1792
ZS_Q = 2048
ZS_V = 3072
ZS_F = 4096
ZS_Z = 6144
ZS_G = 7168
ZS_WIDTH = 8192


VMEM_STREAMING_MIB = 40
VMEM_MATRICES_MIB = 48
VMEM_RESIDENT_MIB = 56

ROWS_EVEN_IN = 512
ROWS_FUSED_OUT = 256
ROWS_ODD_IN = 32 * GRID_W
ROWS_ODD_OUT = 8 * GRID_W
S5_INJECT_SUBCHUNKS = 512
S5_READOUT_SUBCHUNKS = 256
S5_CARRY_SUBCHUNKS = 256


def _cparams(semantics, vmem_mib):
    return pltpu.CompilerParams(dimension_semantics=semantics, vmem_limit_bytes=vmem_mib * 1024 * 1024)


def _silu(x):
    return x * jax.nn.sigmoid(x)


def _modulate(h, mod_ref):
    ms = jnp.mean(h * h, axis=-1, keepdims=True)
    return h * lax.rsqrt(ms + EPS) * (mod_ref[0:1, :] * (1.0 + mod_ref[1:2, :])) + mod_ref[2:3, :]


def _split_dot(x, w, passes):
    acc, rem = None, x
    for _ in range(passes):
        hi = rem.astype(BF16)
        d = jnp.dot(hi, w, preferred_element_type=F32)
        acc = d if acc is None else acc + d
        rem = rem - hi.astype(F32)
    return acc


def _chunk_cumsum(x, reverse):
    n, w = x.shape
    tiles = x.reshape(n // SUBLANES, SUBLANES, w)
    sub = lax.broadcasted_iota(jnp.int32, (1, SUBLANES, 1), 1)
    s = 1
    while s < SUBLANES:
        if reverse:
            tiles = tiles + jnp.where(sub < SUBLANES - s, pltpu.roll(tiles, SUBLANES - s, axis=1), 0.0)
        else:
            tiles = tiles + jnp.where(sub >= s, pltpu.roll(tiles, s, axis=1), 0.0)
        s *= 2
    per_chunk = SCAN_CHUNK // SUBLANES
    edge = 0 if reverse else SUBLANES - 1
    out = [None] * (n // SUBLANES)
    for c in range(n // SCAN_CHUNK):
        run = None
        order = range(per_chunk - 1, -1, -1) if reverse else range(per_chunk)
        for k in order:
            tile = tiles[c * per_chunk + k]
            out[c * per_chunk + k] = tile if run is None else tile + run
            total = tile[edge:edge + 1, :]
            run = total if run is None else run + total
    return jnp.concatenate(out, axis=0)


def _ada_kernel(c_ref, w_ref, b_ref, o_ref):
    o_ref[...] = jnp.dot(_silu(c_ref[...]), w_ref[...], precision=lax.Precision.HIGHEST,
                         preferred_element_type=F32) + b_ref[...]


def _ada(cvecs, ada_w, ada_b):
    depth, d, n = ada_w.shape
    tn = n // 4
    return pl.pallas_call(
        _ada_kernel,
        grid=(depth, n // tn),
        in_specs=[pl.BlockSpec((SUBLANES, d), lambda l, j: (0, 0)),
                  pl.BlockSpec((None, d, tn), lambda l, j: (l, 0, j)),
                  pl.BlockSpec((None, 1, tn), lambda l, j: (l, 0, j))],
        out_specs=pl.BlockSpec((None, SUBLANES, tn), lambda l, j: (l, 0, j)),
        out_shape=jax.ShapeDtypeStruct((depth, SUBLANES, n), F32),
        compiler_params=_cparams(("arbitrary", "arbitrary"), VMEM_STREAMING_MIB),
        name="ada",
    )(cvecs, ada_w, ada_b.reshape(depth, 1, n))


EVEN_TN = 1024
HALO = 2 * SUBLANES
EVEN_TILES = (("conv", 0, None), ("conv_dt", 1, None), ("silu", None, 0), ("copy", None, 1),
              ("gate", 2, 4), ("gate", 3, 5), ("silu", None, 2), ("silu", None, 3))


def _even_in_kernel(h_ref, hp_ref, hn_ref, mod_ref, w_ref, cw_ref, cb_ref, dtc_ref, lb_ref, za_ref, zb_ref, a_ref):
    i = pl.program_id(0)
    tm = h_ref.shape[0]
    n_lb = lb_ref.shape[1]
    a_ref[:HALO] = _modulate(hp_ref[...], mod_ref).astype(BF16)
    a_ref[HALO:HALO + tm] = _modulate(h_ref[...], mod_ref).astype(BF16)
    a_ref[HALO + tm:] = _modulate(hn_ref[...], mod_ref).astype(BF16)
    row = lax.broadcasted_iota(jnp.int32, (tm + 2 * HALO, 1), 0)
    outside = ((row < HALO) & (i == 0)) | ((row >= HALO + tm) & (i == pl.num_programs(0) - 1))

    def conv_silu(col0, lanes):
        acc = jnp.dot(a_ref[...], w_ref[:, col0:col0 + lanes], preferred_element_type=F32)
        acc = jnp.where(outside, 0.0, acc)
        conv = cb_ref[:, col0:col0 + lanes]
        for k in range(SSD_CONV):
            off = HALO - SSD_CONV // 2 + k
            conv = conv + cw_ref[k:k + 1, col0:col0 + lanes] * acc[off:off + tm]
        return _silu(conv)

    def main_dot(col0, lanes=EVEN_TN):
        return jnp.dot(a_ref[HALO:HALO + tm], w_ref[:, col0:col0 + lanes], preferred_element_type=F32)

    for tile, (kind, fa, fb) in enumerate(EVEN_TILES):
        col0 = tile * EVEN_TN
        if kind == "conv":
            za_ref[:, fa * EVEN_TN:(fa + 1) * EVEN_TN] = conv_silu(col0, EVEN_TN)
        elif kind == "conv_dt":
            n_c = ZS_DT - col0
            za_ref[:, col0:ZS_DT] = conv_silu(col0, n_c)
            dt = main_dot(ZS_DT, 2 * LANES)
            for d in range(2):
                za_ref[:, ZS_DT + d * LANES:ZS_DT + (d + 1) * LANES] = jax.nn.softplus(
                    dt[:, d * LANES:(d + 1) * LANES] + dtc_ref[d, 0:1, :])
            za_ref[:, ZS_XD:(fa + 1) * EVEN_TN] = jnp.zeros((tm, (fa + 1) * EVEN_TN - ZS_XD), F32)
        elif kind == "gate":
            lb = lb_ref[:, (col0 - ZS_F) % n_lb:(col0 - ZS_F) % n_lb + EVEN_TN]
            f = lb + (1.0 - lb) * jax.nn.sigmoid(main_dot(col0))
            backward = (col0 - ZS_F) // n_lb == 1
            za_ref[:, fa * EVEN_TN:(fa + 1) * EVEN_TN] = _chunk_cumsum(jnp.log(f), backward)
            zb_ref[:, fb * EVEN_TN:(fb + 1) * EVEN_TN] = (1.0 - f).astype(BF16)
        elif kind == "silu":
            zb_ref[:, fb * EVEN_TN:(fb + 1) * EVEN_TN] = _silu(main_dot(col0)).astype(BF16)
        else:
            zb_ref[:, fb * EVEN_TN:(fb + 1) * EVEN_TN] = main_dot(col0).astype(BF16)


def _even_in(h, mod, w, cw, cb, dtc, lb):
    t, d = h.shape
    tm = min(t, ROWS_EVEN_IN)
    hb = tm // HALO
    n_a = (1 + max(fa for _, fa, _ in EVEN_TILES if fa is not None)) * EVEN_TN
    n_b = (1 + max(fb for _, _, fb in EVEN_TILES if fb is not None)) * EVEN_TN
    whole = lambda a, **kw: pl.BlockSpec(a.shape, lambda i: (0,) * a.ndim, **kw)
    return pl.pallas_call(
        _even_in_kernel,
        grid=(t // tm,),
        in_specs=[pl.BlockSpec((tm, d), lambda i: (i, 0)),
                  pl.BlockSpec((HALO, d), lambda i: (jnp.maximum(i * hb - 1, 0), 0)),
                  pl.BlockSpec((HALO, d), lambda i: (jnp.minimum((i + 1) * hb, t // HALO - 1), 0)),
                  whole(mod), whole(w, pipeline_mode=pl.Buffered(1)),
                  whole(cw), whole(cb), whole(dtc), whole(lb)],
        out_specs=[pl.BlockSpec((tm, n_a), lambda i: (i, 0)), pl.BlockSpec((tm, n_b), lambda i: (i, 0))],
        out_shape=[jax.ShapeDtypeStruct((t, n_a), F32), jax.ShapeDtypeStruct((t, n_b), BF16)],
        scratch_shapes=[pltpu.VMEM((tm + 2 * HALO, d), BF16)],
        compiler_params=_cparams(("arbitrary",), VMEM_RESIDENT_MIB),
        name="even_in",
    )(h, h, h, mod, w, cw, cb, dtc, lb)


def _scan_block_prep(xd_ref, qv_ref, bcum_ref, kk_ref, dtc_ref, e_ref, d):
    Q = SCAN_CHUNK
    reverse = bool(d)
    dtv = xd_ref[:, ZS_DT + d * LANES:ZS_DT + (d + 1) * LANES]
    acum = _chunk_cumsum(-dtv * jnp.exp(dtc_ref[d, 1:2, :]), reverse)
    rows = acum.shape[0]
    nchunk = rows // Q
    edge = 0 if reverse else Q - 1
    lasts = [acum[c * Q + edge:c * Q + edge + 1, :] for c in range(nchunk)]
    last_rows = jnp.concatenate([jnp.broadcast_to(l, (Q, LANES)) for l in lasts], axis=0)
    e_last = jnp.concatenate([jnp.broadcast_to(jnp.exp(l), (SUBLANES, LANES)) for l in lasts], axis=0)
    terms, rem = [], e_last
    for _ in range(3):
        hi = rem.astype(BF16)
        terms.append(hi)
        rem = rem - hi.astype(F32)
    lhs = jnp.concatenate([dtv.astype(BF16), jnp.exp(acum).astype(BF16), jnp.exp(last_rows - acum).astype(BF16)]
                          + terms, axis=0)
    xe = jnp.dot(lhs, e_ref[...], preferred_element_type=F32)
    n8 = SUBLANES * nchunk
    el = xe[3 * rows:]
    expanded = (xe[:rows], xe[rows:2 * rows], xe[2 * rows:3 * rows], el[:n8] + el[n8:2 * n8] + el[2 * n8:])
    return xd_ref, qv_ref, bcum_ref, kk_ref, acum, expanded


DECAY_LIMIT = 80.0


def _hg_intra_exact(qs, bcum, bcum_ref, kk_ref, v_ref, acc_ref, row0, reverse):
    Q = SCAN_CHUNK
    acc_ref[...] = jnp.zeros_like(acc_ref)
    t_idx = lax.broadcasted_iota(jnp.int32, (Q, 1), 0)

    def body(s, carry):
        r = row0 + s
        b_s, k_s, v_s = bcum_ref[pl.ds(r, 1), :], kk_ref[pl.ds(r, 1), :], v_ref[pl.ds(r, 1), :]
        allowed = (t_idx <= s) if reverse else (t_idx >= s)
        w = jnp.where(allowed, jnp.exp(jnp.minimum(bcum - b_s, 0.0)) * (qs * k_s), 0.0)
        for h in range(HG_HEADS):
            sl = slice(h * HG_HEAD_DIM, (h + 1) * HG_HEAD_DIM)
            acc_ref[:, sl] += jnp.sum(w[:, sl], axis=-1, keepdims=True) * v_s[:, sl]
        return carry

    lax.fori_loop(0, Q, body, 0)
    return acc_ref[...]


def _scan_chunk(prep, rows, dsk_ref, st_ref, gt_ref, yo_ref, d, reverse, exact=None):
    Q = SCAN_CHUNK
    xd_ref, qv_ref, bcum_ref, kk_ref, acum, expanded = prep
    n_hg = bcum_ref.shape[1]
    ti = lax.broadcasted_iota(jnp.int32, (Q, Q), 0)
    si = lax.broadcasted_iota(jnp.int32, (Q, Q), 1)
    mask = (ti <= si) if reverse else (ti >= si)
    edge = 0 if reverse else Q - 1
    width = SSD_HEADS * SSD_HEAD_DIM
    ci = rows.start // Q
    dt_x, ea_x, wend_x, elast_x = expanded
    chunk_x = (dt_x[rows], ea_x[rows], wend_x[rows], elast_x[ci * SUBLANES:ci * SUBLANES + 1])
    _scan_chunk_ssd(xd_ref, chunk_x, acum[rows], rows, dsk_ref, st_ref, yo_ref, d, reverse, mask)
    if exact is not None:
        intra = _hg_intra_exact(qv_ref[rows, :n_hg].astype(F32), bcum_ref[rows, :], *exact, rows.start, reverse)
    hw = n_hg // 2
    for part in range(2):
        cols = slice(part * hw, (part + 1) * hw)
        kk, bcum = kk_ref[rows, cols].astype(F32), bcum_ref[rows, cols]
        qs = qv_ref[rows, cols].astype(F32)
        vb = qv_ref[rows, n_hg + part * hw:n_hg + (part + 1) * hw]
        blast = bcum[edge:edge + 1, :]
        qe = (qs * jnp.exp(bcum)).astype(BF16)
        k_end = kk * jnp.exp(blast - bcum)
        if exact is None:
            k_til = (k_end * jnp.exp(-blast)).astype(BF16)
        k_end = k_end.astype(BF16)
        e_blast = jnp.exp(blast)
        outs = []
        for hl in range(hw // HG_HEAD_DIM):
            h = part * (hw // HG_HEAD_DIM) + hl
            sl = slice(hl * HG_HEAD_DIM, (hl + 1) * HG_HEAD_DIM)
            gt = gt_ref[d, h]
            if exact is None:
                att = lax.dot_general(qe[:, sl], k_til[:, sl], (((1,), (1,)), ((), ())), preferred_element_type=F32)
                att = jnp.where(mask, att, 0.0).astype(BF16)
                o_h = jnp.dot(att, vb[:, sl], preferred_element_type=F32)
            else:
                o_h = intra[:, h * HG_HEAD_DIM:(h + 1) * HG_HEAD_DIM]
            o_h = o_h + lax.dot_general(qe[:, sl], gt.astype(BF16), (((1,), (1,)), ((), ())),
                                        preferred_element_type=F32)
            outs.append(o_h)
            upd = lax.dot_general(vb[:, sl], k_end[:, sl], (((0,), (0,)), ((), ())), preferred_element_type=F32)
            gt_ref[d, h] = gt * e_blast[:, sl] + upd
        yo_ref[rows, width + part * hw:width + (part + 1) * hw] = jnp.concatenate(outs, axis=1).astype(yo_ref.dtype)


def _scan_chunk_ssd(xd_ref, chunk_x, acum, rows, dsk_ref, st_ref, yo_ref, d, reverse, mask):
    Q = SCAN_CHUNK
    width = SSD_HEADS * SSD_HEAD_DIM
    gw = width // SSD_GROUPS
    dt_all, ea_all, wend_all, elast_all = chunk_x
    acum_t = acum.T
    lane = lax.broadcasted_iota(jnp.int32, (Q, LANES), 1)
    lo = lane < SSD_HEAD_DIM
    zero_b = jnp.zeros((Q, LANES), BF16)
    for g in range(SSD_GROUPS):
        cols = slice(g * gw, (g + 1) * gw)
        dt_x, ea_x, wend_x, elast_x = dt_all[:, cols], ea_all[:, cols], wend_all[:, cols], elast_all[:, cols]
        xs = xd_ref[rows, cols]
        xdt = xs * dt_x
        xw = (xdt * wend_x).astype(BF16)
        xdt_b = xdt.astype(BF16)
        bm = xd_ref[rows, width + g * SSD_STATE:width + (g + 1) * SSD_STATE].astype(BF16)
        cm = xd_ref[rows, width + (SSD_GROUPS + g) * SSD_STATE:width + (SSD_GROUPS + g + 1) * SSD_STATE].astype(BF16)
        cb = lax.dot_general(cm, bm, (((1,), (1,)), ((), ())), preferred_element_type=F32)
        y_state = jnp.dot(cm, st_ref[d, g].astype(BF16), preferred_element_type=F32)
        pairs = []
        for hp in range(gw // LANES):
            xp = xdt_b[:, hp * LANES:(hp + 1) * LANES]
            acc = None
            for half in range(2):
                h = (g * gw + hp * LANES) // SSD_HEAD_DIM + half
                seg = acum[:, h:h + 1] - acum_t[h:h + 1, :]
                m = jnp.where(mask, cb * jnp.exp(seg), 0.0).astype(BF16)
                xh = jnp.where(lo, xp, zero_b) if half == 0 else jnp.where(lo, zero_b, xp)
                part = jnp.dot(m, xh, preferred_element_type=F32)
                acc = part if acc is None else acc + part
            pairs.append(acc)
        y = jnp.concatenate(pairs, axis=1) + y_state * ea_x
        if not reverse:
            y = y + dsk_ref[:, cols] * xs
        yo_ref[rows, cols] = y.astype(yo_ref.dtype)
        upd = lax.dot_general(bm, xw, (((0,), (0,)), ((), ())), preferred_element_type=F32)
        st_ref[d, g] = st_ref[d, g] * elast_x + upd


def _even_scan_kernel(*refs, nblk, cpb):
    fwd_in, bwd_in = refs[0:4], refs[4:8]
    dtc_ref, dsk_ref, e_ref, h0s_ref, h0g_ref = refs[8:13]
    yof_ref, yob_ref, st_ref, gt_ref, kkf_ref, vf_ref, acc_ref = refs[13:20]
    c = pl.program_id(0)

    @pl.when(c == 0)
    def _():
        st_ref[...] = h0s_ref[...]
        gt_ref[...] = h0g_ref[...]

    streams = ((False, fwd_in, yof_ref), (True, bwd_in, yob_ref))

    def run(exact):
        preps = [_scan_block_prep(*ins, dtc_ref, e_ref, d) for d, (_, ins, _) in enumerate(streams)]
        if exact:
            for d, (_, ins, _) in enumerate(streams):
                kkf_ref[d] = ins[3][...].astype(F32)
                vf_ref[d] = ins[1][:, ins[2].shape[1]:].astype(F32)
        for step in range(cpb):
            for d, (reverse, ins, yo_ref) in enumerate(streams):
                ci = (cpb - 1 - step) if reverse else step
                rows = slice(ci * SCAN_CHUNK, (ci + 1) * SCAN_CHUNK)
                refs_exact = (ins[2], kkf_ref.at[d], vf_ref.at[d], acc_ref) if exact else None
                _scan_chunk(preps[d], rows, dsk_ref, st_ref, gt_ref, yo_ref, d, reverse, refs_exact)

    totals = [ref[ci * SCAN_CHUNK + edge:ci * SCAN_CHUNK + edge + 1, :]
              for ref, edge in ((fwd_in[2], SCAN_CHUNK - 1), (bwd_in[2], 0)) for ci in range(cpb)]
    strongest = jnp.min(jnp.concatenate(totals, axis=0))
    pl.when(strongest >= -DECAY_LIMIT)(lambda: run(False))
    pl.when(strongest < -DECAY_LIMIT)(lambda: run(True))


def _even_scan(za, zb, consts, h0s, h0g):
    t = za.shape[0]
    nch = t // SCAN_CHUNK
    cpb = max(k for k in (4, 2, 1) if nch % k == 0)
    rows = cpb * SCAN_CHUNK
    nblk = nch // cpb
    dtc, dsk, e16 = consts
    n_x = SSD_HEADS * SSD_HEAD_DIM
    n_hg = HG_HEADS * HG_HEAD_DIM
    gate_a, gate_b = (tuple(tile[i] for tile in EVEN_TILES if tile[0] == "gate") for i in (1, 2))

    def stream_specs(bidx, d):
        return [
            pl.BlockSpec((rows, ZS_XD), lambda c: (bidx(c), 0)),
            pl.BlockSpec((rows, 2 * n_hg), lambda c: (bidx(c), 0)),
            pl.BlockSpec((rows, n_hg), lambda c: (bidx(c), gate_a[d])),
            pl.BlockSpec((rows, n_hg), lambda c: (bidx(c), gate_b[d])),
        ]

    fwd = lambda c: c
    bwd = lambda c: nblk - 1 - c
    whole = lambda a: pl.BlockSpec(a.shape, lambda c: (0,) * a.ndim)
    in_specs = stream_specs(fwd, 0) + stream_specs(bwd, 1) + [whole(a) for a in (dtc, dsk, e16, h0s, h0g)]
    out_specs = [pl.BlockSpec((rows, n_x + n_hg), lambda c: (c, 0)),
                 pl.BlockSpec((rows, n_x + n_hg), lambda c: (bwd(c), 0)), whole(h0s), whole(h0g)]
    out_shape = [jax.ShapeDtypeStruct((t, n_x + n_hg), BF16), jax.ShapeDtypeStruct((t, n_x + n_hg), BF16),
                 jax.ShapeDtypeStruct(h0s.shape, F32), jax.ShapeDtypeStruct(h0g.shape, F32)]
    return pl.pallas_call(
        functools.partial(_even_scan_kernel, nblk=nblk, cpb=cpb),
        grid=(nblk,),
        in_specs=in_specs,
        out_specs=out_specs,
        out_shape=out_shape,
        scratch_shapes=[pltpu.VMEM((2, rows, n_hg), F32), pltpu.VMEM((2, rows, n_hg), F32),
                        pltpu.VMEM((SCAN_CHUNK, n_hg), F32)],
        compiler_params=_cparams(("arbitrary",), VMEM_MATRICES_MIB),
        name="even_scan",
    )(za, zb, za, zb, za, zb, za, zb, dtc, dsk, e16, h0s, h0g)


def _ffn_block(h, mod_ref, wg_ref, wu_ref, wd_ref, final):
    a = _modulate(h, mod_ref).astype(BF16)
    gate = jnp.dot(a, wg_ref[...], preferred_element_type=F32)
    up = jnp.dot(a, wu_ref[...], preferred_element_type=F32)
    act = (_silu(gate) * up).astype(BF16)
    out = h + mod_ref[3:4, :] * jnp.dot(act, wd_ref[...], preferred_element_type=F32)
    if final:
        ms = jnp.mean(out * out, axis=-1, keepdims=True)
        out = out * lax.rsqrt(ms + EPS) * mod_ref[4:5, :]
    return out


def _resident(a):
    return pl.BlockSpec(a.shape, lambda i: (0,) * a.ndim, pipeline_mode=pl.Buffered(1))


def _even_out_kernel(h_ref, z_ref, g_ref, yof_ref, yob_ref, w_ref, vec_ref, mod_ref, wg_ref, wu_ref, wd_ref, o_ref):
    n_x = z_ref.shape[1]
    yo = yof_ref[...].astype(F32) + yob_ref[...].astype(F32)
    y = yo[:, :n_x] * z_ref[...].astype(F32)
    o = yo[:, n_x:]
    gw = n_x // SSD_GROUPS
    parts = []
    for g in range(SSD_GROUPS):
        yg = y[:, g * gw:(g + 1) * gw]
        ms = jnp.mean(yg * yg, axis=-1, keepdims=True)
        parts.append(yg * lax.rsqrt(ms + EPS))
    yn = (jnp.concatenate(parts, axis=1) * vec_ref[0:1, :]).astype(BF16)
    parts = []
    for hh in range(HG_HEADS):
        oh = o[:, hh * HG_HEAD_DIM:(hh + 1) * HG_HEAD_DIM]
        ms = jnp.mean(oh * oh, axis=-1, keepdims=True)
        parts.append(oh * lax.rsqrt(ms + EPS))
    on = (jnp.concatenate(parts, axis=1) * vec_ref[1:2, :] * g_ref[...].astype(F32)).astype(BF16)
    half = yn.shape[1]
    mix = jnp.dot(yn, w_ref[:half, :], preferred_element_type=F32)
    mix = mix + jnp.dot(on, w_ref[half:, :], preferred_element_type=F32)
    h1 = h_ref[...] + vec_ref[2:3, :] * mix
    o_ref[...] = _ffn_block(h1, mod_ref, wg_ref, wu_ref, wd_ref, final=False)


def _even_out(h, zb, yo_f, yo_b, w_out, vec, mod, wg, wu, wd):
    t, d = h.shape
    tm = min(t, ROWS_FUSED_OUT)
    n_x = SSD_HEADS * SSD_HEAD_DIM
    n_hg = HG_HEADS * HG_HEAD_DIM
    row = lambda i: (i, 0)
    z_tile, g_tile = (next(fb for k, (_, _, fb) in enumerate(EVEN_TILES) if k * EVEN_TN == start)
                      for start in (ZS_Z, ZS_G))
    return pl.pallas_call(
        _even_out_kernel,
        grid=(t // tm,),
        in_specs=[pl.BlockSpec((tm, d), row),
                  pl.BlockSpec((tm, n_x), lambda i: (i, z_tile)),
                  pl.BlockSpec((tm, n_hg), lambda i: (i, g_tile)),
                  pl.BlockSpec((tm, n_x + n_hg), row), pl.BlockSpec((tm, n_x + n_hg), row),
                  _resident(w_out), _resident(vec), _resident(mod), _resident(wg), _resident(wu), _resident(wd)],
        out_specs=pl.BlockSpec((tm, d), row),
        out_shape=jax.ShapeDtypeStruct((t, d), F32),
        compiler_params=_cparams(("arbitrary",), VMEM_RESIDENT_MIB),
        name="even_out_ffn",
    )(h, zb, zb, yo_f, yo_b, w_out, vec, mod, wg, wu, wd)


def _odd_in_kernel(h_ref, mod_ref, w_ref, o_ref, u_ref):
    a = _modulate(h_ref[...], mod_ref).astype(BF16)
    u = jnp.dot(a, w_ref[...], preferred_element_type=F32)
    nj, grid_w, rb, _ = o_ref.shape
    for j in range(nj):
        u_ref[j] = u[:, j * LANES:(j + 1) * LANES]
    for wi in range(grid_w):
        for j in range(nj):
            o_ref[j, wi] = u_ref[j, pl.ds(wi, rb, stride=grid_w), :]


def _odd_in(h, mod, w, grid_w):
    t, d = h.shape
    rows = t // grid_w
    rb = min(rows, ROWS_ODD_IN // GRID_W)
    n = w.shape[1]
    nj = n // LANES
    out = pl.pallas_call(
        _odd_in_kernel,
        grid=(rows // rb,),
        in_specs=[pl.BlockSpec((rb * grid_w, d), lambda i: (i, 0)),
                  pl.BlockSpec((SUBLANES, d), lambda i: (0, 0)),
                  pl.BlockSpec(w.shape, lambda i: (0, 0))],
        out_specs=pl.BlockSpec((nj, grid_w, rb, LANES), lambda i: (0, 0, i, 0)),
        out_shape=jax.ShapeDtypeStruct((nj, grid_w, rows, LANES), F32),
        scratch_shapes=[pltpu.VMEM((nj, rb * grid_w, LANES), F32)],
        compiler_params=_cparams(("arbitrary",), VMEM_MATRICES_MIB),
        name="odd_in",
    )(h, mod, w)
    return out.reshape(nj, t, LANES)


def _sub_chunk_rows(u_ref, nb):
    return [u_ref[pl.ds(s, nb, stride=S5_SUB), :] for s in range(S5_SUB)]


def _s5_inject_kernel(u_ref, win_ref, s_ref):
    x = jnp.concatenate(_sub_chunk_rows(u_ref, s_ref.shape[0]), axis=1).astype(BF16)
    s_ref[...] = jnp.dot(x, win_ref[...], preferred_element_type=F32)


def _s5_inject(u, win):
    nj, t, _ = u.shape
    n = t // S5_SUB
    nb = min(n, S5_INJECT_SUBCHUNKS)
    width, ns = win.shape[1:]
    return pl.pallas_call(
        _s5_inject_kernel,
        grid=(nj, n // nb),
        in_specs=[pl.BlockSpec((None, nb * S5_SUB, LANES), lambda j, b: (j, b, 0)),
                  pl.BlockSpec((None, width, ns), lambda j, b: (j, 0, 0))],
        out_specs=pl.BlockSpec((None, nb, ns), lambda j, b: (j, b, 0)),
        out_shape=jax.ShapeDtypeStruct((nj, n, ns), F32),
        compiler_params=_cparams(("arbitrary", "arbitrary"), VMEM_STREAMING_MIB),
        name="s5_inject",
    )(u, win)


def _s5_carry_kernel(sf_ref, sb_ref, lam_ref, h0_ref, pf_ref, pb_ref, hf_ref, tf_ref, tb_ref, c_ref):
    b = pl.program_id(0)
    nj, kb, ns = sf_ref.shape
    half = ns // 2

    @pl.when(b == 0)
    def _():
        c_ref[...] = h0_ref[...]

    for j in range(nj):
        tf_ref[:, j, :] = sf_ref[j]
        tb_ref[:, j, :] = sb_ref[j]

    def advance(d, c, s):
        lr, li = lam_ref[d, :, :half], lam_ref[d, :, half:]
        pr, pi = c[:, :half], c[:, half:]
        return jnp.concatenate([lr * pr - li * pi + s[:, :half], lr * pi + li * pr + s[:, half:]], axis=1)

    def step(i, carry):
        cf, cb = carry
        kr = kb - 1 - i
        sf = tf_ref[i]
        sb = tb_ref[kr]
        tf_ref[i] = cf
        tb_ref[kr] = cb
        return advance(0, cf, sf), advance(1, cb, sb)

    cf, cb = lax.fori_loop(0, kb, step, (c_ref[0], c_ref[1]), unroll=4)
    c_ref[0] = cf
    c_ref[1] = cb
    for j in range(nj):
        pf_ref[j] = tf_ref[:, j, :]
        pb_ref[j] = tb_ref[:, j, :]

    @pl.when(b == pl.num_programs(0) - 1)
    def _():
        hf_ref[...] = c_ref[...]


def _s5_carry(s_f, s_b, lam, h0):
    nj, n, ns = s_f.shape
    kb = min(n, S5_CARRY_SUBCHUNKS)
    nblk = n // kb
    fwd = pl.BlockSpec((nj, kb, ns), lambda b: (0, b, 0))
    bwd = pl.BlockSpec((nj, kb, ns), lambda b: (0, nblk - 1 - b, 0))
    whole = pl.BlockSpec(lam.shape, lambda b: (0, 0, 0))
    return pl.pallas_call(
        _s5_carry_kernel,
        grid=(nblk,),
        in_specs=[fwd, bwd, whole, whole],
        out_specs=[fwd, bwd, whole],
        out_shape=[jax.ShapeDtypeStruct(s_f.shape, F32), jax.ShapeDtypeStruct(s_b.shape, F32),
                   jax.ShapeDtypeStruct(lam.shape, F32)],
        scratch_shapes=[pltpu.VMEM((kb, nj, ns), F32), pltpu.VMEM((kb, nj, ns), F32), pltpu.VMEM(lam.shape, F32)],
        compiler_params=_cparams(("arbitrary",), VMEM_MATRICES_MIB),
        name="s5_carry",
    )(s_f, s_b, lam, h0)


def _s5_readout_kernel(u_ref, p_ref, wout_ref, tt_ref, add_ref, y_ref, *, reverse):
    nb = p_ref.shape[0]
    us = _sub_chunk_rows(u_ref, nb)
    xb = jnp.concatenate(us, axis=1).astype(BF16)
    pb = p_ref[...].astype(BF16)
    width = xb.shape[1]
    tile = 2 * LANES
    for m in range(width // tile):
        cols = slice(m * tile, (m + 1) * tile)
        rows = slice(m * tile, width) if reverse else slice(0, (m + 1) * tile)
        y = jnp.dot(pb, wout_ref[:, cols], preferred_element_type=F32)
        y = y + jnp.dot(xb[:, rows], tt_ref[rows, cols], preferred_element_type=F32)
        for q in range(tile // LANES):
            s = m * (tile // LANES) + q
            other = add_ref[pl.ds(s, nb, stride=S5_SUB), :] if reverse else add_ref[...] * us[s]
            y_ref[pl.ds(s, nb, stride=S5_SUB), :] = y[:, q * LANES:(q + 1) * LANES] + other


def _s5_readout(u, p, wout, tt, add, reverse):
    nj, t, _ = u.shape
    n = t // S5_SUB
    nb = min(n, S5_READOUT_SUBCHUNKS)
    ns, width = wout.shape[1:]
    tokens = pl.BlockSpec((None, nb * S5_SUB, LANES), lambda j, b: (j, b, 0))
    add_spec = tokens if reverse else pl.BlockSpec((None, 1, LANES), lambda j, b: (j, 0, 0))
    return pl.pallas_call(
        functools.partial(_s5_readout_kernel, reverse=reverse),
        grid=(nj, n // nb),
        in_specs=[tokens,
                  pl.BlockSpec((None, nb, ns), lambda j, b: (j, b, 0)),
                  pl.BlockSpec((None, ns, width), lambda j, b: (j, 0, 0)),
                  pl.BlockSpec((None, width, width), lambda j, b: (j, 0, 0)),
                  add_spec],
        out_specs=tokens,
        out_shape=jax.ShapeDtypeStruct(u.shape, F32),
        compiler_params=_cparams(("arbitrary", "arbitrary"), VMEM_MATRICES_MIB),
        name="s5_readout_bwd" if reverse else "s5_readout_fwd",
    )(u, p, wout, tt, add)


def _odd_out_kernel(h_ref, y_ref, wv_ref, wgl_ref, vec_ref, mod_ref, wg_ref, wu_ref, wd_ref, o_ref, a_ref):
    nj, grid_w, rb, _ = y_ref.shape
    for wi in range(grid_w):
        for j in range(nj):
            a_ref[j, pl.ds(wi, rb, stride=grid_w), :] = y_ref[j, wi]
    a = jax.nn.gelu(jnp.concatenate([a_ref[j] for j in range(nj)], axis=1)).astype(BF16)
    val = jnp.dot(a, wv_ref[...], preferred_element_type=F32)
    gate = jnp.dot(a, wgl_ref[...], preferred_element_type=F32)
    h3 = h_ref[...] + vec_ref[0:1, :] * (val * jax.nn.sigmoid(gate))
    o_ref[...] = _ffn_block(h3, mod_ref, wg_ref, wu_ref, wd_ref, final=True)


def _odd_out(h, y, w_val, w_gate, vec, mod, wg, wu, wd, grid_w):
    t, d = h.shape
    rows = t // grid_w
    rb = min(rows, ROWS_ODD_OUT // GRID_W)
    nj = y.shape[0]
    tok = pl.BlockSpec((rb * grid_w, d), lambda i: (i, 0))
    return pl.pallas_call(
        _odd_out_kernel,
        grid=(rows // rb,),
        in_specs=[tok, pl.BlockSpec((nj, grid_w, rb, LANES), lambda i: (0, 0, i, 0)),
                  _resident(w_val), _resident(w_gate), _resident(vec), _resident(mod),
                  _resident(wg), _resident(wu), _resident(wd)],
        out_specs=tok,
        out_shape=jax.ShapeDtypeStruct((t, d), F32),
        scratch_shapes=[pltpu.VMEM((nj, rb * grid_w, LANES), F32)],
        compiler_params=_cparams(("arbitrary",), VMEM_RESIDENT_MIB),
        name="odd_out_ffn",
    )(h, y.reshape(nj, grid_w, rows, LANES), w_val, w_gate, vec, mod, wg, wu, wd)


def _s5_prep_kernel(prm_ref, bre_ref, bim_ref, colp_ref, cre_ref, cim_ref, win_ref, wout_ref, tt_ref, lam_ref, *,
                    reverse):
    T = S5_SUB
    ns = prm_ref.shape[1]
    a_re, a_im, dt = prm_ref[0:1, :], prm_ref[1:2, :], jnp.exp(prm_ref[2:3, :])
    tau = lax.broadcasted_iota(jnp.int32, (3 * SUBLANES, ns), 0).astype(F32)
    mag = jnp.exp(a_re * dt * tau)
    pr = mag * jnp.cos(a_im * dt * tau)
    pi = mag * jnp.sin(a_im * dt * tau)
    lr, li = pr[1:2], pi[1:2]
    den = a_re * a_re + a_im * a_im
    cr = ((lr - 1.0) * a_re + li * a_im) / den
    ci = (li * a_re - (lr - 1.0) * a_im) / den
    row_g = lax.broadcasted_iota(jnp.int32, (LANES, ns), 0) // S5_GROUP
    col_g = lax.broadcasted_iota(jnp.int32, (LANES, ns), 1) // S5_STATE
    same = row_g == col_g
    bbr = jnp.where(same, cr * bre_ref[...] - ci * bim_ref[...], 0.0)
    bbi = jnp.where(same, cr * bim_ref[...] + ci * bre_ref[...], 0.0)
    for s in range(T):
        e = s if reverse else T - 1 - s
        rows = slice(s * LANES, (s + 1) * LANES)
        win_ref[rows, :ns] = (pr[e:e + 1] * bbr - pi[e:e + 1] * bbi).astype(BF16)
        win_ref[rows, ns:] = (pr[e:e + 1] * bbi + pi[e:e + 1] * bbr).astype(BF16)
    lam_ref[...] = jnp.broadcast_to(jnp.concatenate([pr[T:T + 1], pi[T:T + 1]], axis=1), lam_ref.shape)
    dt_c = jnp.exp(colp_ref[2])
    mag_c = jnp.exp(colp_ref[0] * dt_c)
    lr_c = mag_c * jnp.cos(colp_ref[1] * dt_c)
    li_c = mag_c * jnp.sin(colp_ref[1] * dt_c)
    row_gc = lax.broadcasted_iota(jnp.int32, (ns, LANES), 0) // S5_STATE
    col_gc = lax.broadcasted_iota(jnp.int32, (ns, LANES), 1) // S5_GROUP
    same_c = row_gc == col_gc
    c_re = jnp.where(same_c, cre_ref[...], 0.0)
    c_im = jnp.where(same_c, cim_ref[...], 0.0)
    bb = jnp.concatenate([bbr, bbi], axis=1)
    bb_hi = bb.astype(BF16)
    bb_lo = (bb - bb_hi.astype(F32)).astype(BF16)
    cur_r = jnp.ones((ns, LANES), F32)
    cur_i = jnp.zeros((ns, LANES), F32)
    kblk = []
    for e in range(T + 1):
        wo_r = c_re * cur_r - c_im * cur_i
        wo_i = -(c_re * cur_i + c_im * cur_r)
        if e >= 1:
            t = (T - e) if reverse else e - 1
            wout_ref[:ns, t * LANES:(t + 1) * LANES] = wo_r.astype(BF16)
            wout_ref[ns:, t * LANES:(t + 1) * LANES] = wo_i.astype(BF16)
        if e < T:
            wo = jnp.concatenate([wo_r, wo_i], axis=0)
            wo_hi = wo.astype(BF16)
            kblk.append((wo_hi, (wo - wo_hi.astype(F32)).astype(BF16)))
        cur_r, cur_i = cur_r * lr_c - cur_i * li_c, cur_r * li_c + cur_i * lr_c
    wo_hi = jnp.concatenate([hi for hi, _ in kblk], axis=1)
    wo_lo = jnp.concatenate([lo for _, lo in kblk], axis=1)
    k_all = jnp.dot(bb_hi, wo_hi, preferred_element_type=F32)
    k_all = k_all + jnp.dot(bb_hi, wo_lo, preferred_element_type=F32) + jnp.dot(bb_lo, wo_hi, preferred_element_type=F32)
    kblk = [k_all[:, e * LANES:(e + 1) * LANES].astype(BF16) for e in range(T)]
    zeros = jnp.zeros((LANES, LANES), BF16)
    for s in range(T):
        for t in range(T):
            lag = (s - t) if reverse else (t - s)
            tt_ref[s * LANES:(s + 1) * LANES, t * LANES:(t + 1) * LANES] = kblk[lag] if lag >= 0 else zeros


def _s5_prep(a_re, a_im, log_dt, b_re, b_im, c_re, c_im, reverse):
    g, p = a_re.shape
    cdim = b_re.shape[2]
    nj = g // S5_GPB
    ns = S5_GPB * p
    width = S5_SUB * LANES
    prm = jnp.zeros((nj, SUBLANES, ns), F32)
    prm = prm.at[:, 0].set(a_re.reshape(nj, ns)).at[:, 1].set(a_im.reshape(nj, ns))
    prm = prm.at[:, 2].set(jnp.repeat(log_dt, p).reshape(nj, ns))
    colp = jnp.broadcast_to(prm[:, :3, :, None], (nj, 3, ns, LANES))
    row_tile = lambda b: jnp.tile(b.reshape(nj, S5_GPB, p, cdim).transpose(0, 3, 1, 2).reshape(nj, 1, cdim, ns),
                                  (1, S5_GPB, 1, 1)).reshape(nj, LANES, ns)
    col_tile = lambda c_: jnp.tile(c_.reshape(nj, S5_GPB, cdim, p).transpose(0, 1, 3, 2).reshape(nj, ns, 1, cdim),
                                   (1, 1, S5_GPB, 1)).reshape(nj, ns, LANES)
    blk = lambda *shape: pl.BlockSpec((None,) + shape, lambda j: (j,) + (0,) * len(shape))
    return pl.pallas_call(
        functools.partial(_s5_prep_kernel, reverse=reverse),
        grid=(nj,),
        in_specs=[blk(SUBLANES, ns), blk(LANES, ns), blk(LANES, ns), blk(3, ns, LANES), blk(ns, LANES),
                  blk(ns, LANES)],
        out_specs=[blk(width, 2 * ns), blk(2 * ns, width), blk(width, width), blk(SUBLANES, 2 * ns)],
        out_shape=[jax.ShapeDtypeStruct((nj, width, 2 * ns), BF16), jax.ShapeDtypeStruct((nj, 2 * ns, width), BF16),
                   jax.ShapeDtypeStruct((nj, width, width), BF16), jax.ShapeDtypeStruct((nj, SUBLANES, 2 * ns), F32)],
        compiler_params=_cparams(("arbitrary",), VMEM_RESIDENT_MIB),
        name="s5_prep_bwd" if reverse else "s5_prep_fwd",
    )(prm, row_tile(b_re), row_tile(b_im), colp, col_tile(c_re), col_tile(c_im))


def _regroup_kernel(w_ref, o_ref, *, pieces):
    col = 0
    for start, width in pieces:
        if start is None:
            o_ref[:, col:col + width] = jnp.zeros((o_ref.shape[0], width), o_ref.dtype)
        else:
            o_ref[:, col:col + width] = w_ref[:, start:start + width].astype(o_ref.dtype)
        col += width


def _regroup_bf16(w, layer, pieces):
    _, rows, cols = w.shape
    n_out = sum(width for _, width in pieces)
    block_bytes = 8 * 1024 * 1024
    tr = max(k for k in (512, 256, 128, 64, 32, 16) if rows % k == 0 and k * cols * 4 <= block_bytes)
    return pl.pallas_call(
        functools.partial(_regroup_kernel, pieces=pieces),
        grid=(rows // tr,),
        in_specs=[pl.BlockSpec((None, tr, cols), lambda i: (layer, i, 0))],
        out_specs=pl.BlockSpec((tr, n_out), lambda i: (i, 0)),
        out_shape=jax.ShapeDtypeStruct((rows, n_out), BF16),
        compiler_params=_cparams(("arbitrary",), VMEM_MATRICES_MIB),
        name="weights_bf16",
    )(w)


def _bf16(w, layer):
    return _regroup_bf16(w, layer, ((0, w.shape[2]),))


def _rows(*vs, width):
    out = jnp.zeros((SUBLANES, width), F32)
    for i, v in enumerate(vs):
        out = out.at[i].set(v.astype(F32))
    return out


def kernel(x, c, ctx, c_ctx, ada_w, ada_b, norm_mix_g, norm_ffn_g, ffn_w_gate, ffn_w_up, ffn_w_down, final_norm_g, ev_w_in, ev_w_out, ssd_conv_w, ssd_conv_b, ssd_dt_bias, ssd_a_log, ssd_d, ssd_norm_g, hg_lb_logits, hg_norm_g, od_w_in, s5_a_re, s5_a_im, s5_log_dt, s5_b_re, s5_b_im, s5_c_re, s5_c_im, s5_d, od_w_val, od_w_gate):
    d = x.shape[-1]
    lat = x[0].astype(F32)
    hc = ctx[0].astype(F32)

    m = _ada(_rows(c[0], c_ctx, width=d), ada_w, ada_b)

    def ada_vecs(layer, stream):
        return [m[layer, stream, i * d:(i + 1) * d] for i in range(6)]

    n_x = SSD_HEADS * SSD_HEAD_DIM
    n_xbc = n_x + 2 * SSD_GROUPS * SSD_STATE
    n_hg = HG_HEADS * HG_HEAD_DIM
    o_z, o_xbc, o_dt = 0, n_x, n_x + n_xbc
    o_q = o_dt + 2 * SSD_HEADS
    o_f = o_q + n_hg
    o_v = o_f + 2 * n_hg
    o_g = o_v + n_hg
    w_in = _regroup_bf16(ev_w_in, 0, (
        (o_xbc, n_xbc),
        (o_dt, SSD_HEADS), (None, LANES - SSD_HEADS), (o_dt + SSD_HEADS, SSD_HEADS), (None, LANES - SSD_HEADS),
        (None, ZS_Q - ZS_XD),
        (o_q, n_hg), (o_v, n_hg), (o_f, 2 * n_hg), (o_z, n_x), (o_g, n_hg)))
    cw = jnp.zeros((SUBLANES, n_xbc), F32).at[:SSD_CONV].set(ssd_conv_w[0])
    cb = ssd_conv_b[0].reshape(1, n_xbc)
    dtc = jnp.zeros((2, SUBLANES, LANES), F32)
    dtc = dtc.at[:, 0, :SSD_HEADS].set(ssd_dt_bias[0]).at[:, 1, :SSD_HEADS].set(ssd_a_log[0])
    dsk = jnp.repeat(ssd_d[0], SSD_HEAD_DIM).reshape(1, n_x)
    lower = jnp.cumsum(jax.nn.softmax(hg_lb_logits.astype(F32), axis=0), axis=0)[0].reshape(1, n_hg)
    head_of_lane = jnp.arange(n_x) // SSD_HEAD_DIM
    e16 = (jnp.arange(LANES)[:, None] == head_of_lane[None, :]).astype(BF16)
    consts = (dtc, dsk, e16)
    w_out = _bf16(ev_w_out, 0)
    out_vec = lambda gate: _rows(ssd_norm_g[0], jnp.tile(hg_norm_g[0], HG_HEADS), gate, width=d)
    wg0, wu0, wd0 = (_bf16(t_, 0) for t_ in (ffn_w_gate, ffn_w_up, ffn_w_down))

    s_state = jnp.zeros((2, SSD_GROUPS, SSD_STATE, n_x // SSD_GROUPS), F32)
    g_state = jnp.zeros((2, HG_HEADS, HG_HEAD_DIM, HG_HEAD_DIM), F32)
    layer0 = {}
    for stream, h in ((1, hc), (0, lat)):
        sm, scm, gm, sf, scf, gf = ada_vecs(0, stream)
        za, zb = _even_in(h, _rows(norm_mix_g[0], scm, sm, width=d), w_in, cw, cb, dtc, lower)
        yo_f, yo_b, s_state, g_state = _even_scan(za, zb, consts, s_state, g_state)
        layer0[stream] = _even_out(h, zb, yo_f, yo_b, w_out, out_vec(gm),
                                   _rows(norm_ffn_g[0], scf, sf, gf, width=d), wg0, wu0, wd0)

    lat, hc = layer0[0], layer0[1]
    w_s5 = _bf16(od_w_in, 0)
    u = {}
    for stream, h, gw in ((1, hc, 1), (0, lat, GRID_W)):
        sm, scm = ada_vecs(1, stream)[:2]
        u[stream] = _odd_in(h, _rows(norm_mix_g[1], scm, sm, width=d), w_s5, gw)
    mats = [_s5_prep(s5_a_re[0, di], s5_a_im[0, di], s5_log_dt[0, di], s5_b_re[0, di], s5_b_im[0, di],
                     s5_c_re[0, di], s5_c_im[0, di], reverse=bool(di)) for di in range(2)]
    lam = jnp.stack([mats[0][3][:, 0], mats[1][3][:, 0]])
    carry = jnp.zeros_like(lam)
    for stream in (1, 0):
        s_f, s_b = (_s5_inject(u[stream], mats[di][0]) for di in range(2))
        p_f, p_b, carry = _s5_carry(s_f, s_b, lam, carry)
    nj = u[0].shape[0]
    y = _s5_readout(u[0], p_f, mats[0][1], mats[0][2], s5_d[0].reshape(nj, 1, LANES), reverse=False)
    y = _s5_readout(u[0], p_b, mats[1][1], mats[1][2], y, reverse=True)
    _, _, gm, sf, scf, gf = ada_vecs(1, 0)
    wg1, wu1, wd1 = (_bf16(t_, 1) for t_ in (ffn_w_gate, ffn_w_up, ffn_w_down))
    out = _odd_out(lat, y, _bf16(od_w_val, 0), _bf16(od_w_gate, 0), _rows(gm, width=d),
                   _rows(norm_ffn_g[1], scf, sf, gf, final_norm_g, width=d), wg1, wu1, wd1, GRID_W)
    return out[None].astype(x.dtype)
```

```python
import functools

import jax
import jax.numpy as jnp
from jax import lax
from jax.experimental import pallas as pl
from jax.experimental.pallas import tpu as pltpu

F32 = jnp.float32
BF16 = jnp.bfloat16
EPS = 1e-6

LANES = 128
SUBLANES = 8
GRID_W = 64
SCAN_CHUNK = 64
SSD_HEADS = 16
SSD_HEAD_DIM = 64
SSD_STATE = 128
SSD_GROUPS = 2
SSD_CONV = 5
HG_HEADS = 8
HG_HEAD_DIM = 128
S5_GROUP = 16
S5_STATE = 64
S5_SUB = 16
S5_GPB = LANES // S5_GROUP

ZS_XBC = 0
ZS_DT = 1536
ZS_XD = 1792
ZS_Q = 2048
ZS_V = 3072
ZS_F = 4096
ZS_Z = 6144
ZS_G = 7168
ZS_WIDTH = 8192


VMEM_STREAMING_MIB = 40
VMEM_MATRICES_MIB = 48
VMEM_RESIDENT_MIB = 56

ROWS_EVEN_IN = 512
ROWS_FUSED_OUT = 512
FFN_SPLITS = 1
ROWS_ODD_IN = 32 * GRID_W
ROWS_ODD_OUT = 8 * GRID_W
S5_INJECT_SUBCHUNKS = 512
S5_READOUT_SUBCHUNKS = 256
S5_CARRY_SUBCHUNKS = 256


def _cparams(semantics, vmem_mib):
    return pltpu.CompilerParams(dimension_semantics=semantics, vmem_limit_bytes=vmem_mib * 1024 * 1024)


def _silu(x):
    return x * jax.nn.sigmoid(x)


def _modulate(h, mod_ref):
    ms = jnp.mean(h * h, axis=-1, keepdims=True)
    return h * lax.rsqrt(ms + EPS) * (mod_ref[0:1, :] * (1.0 + mod_ref[1:2, :])) + mod_ref[2:3, :]


def _split_dot(x, w, passes):
    acc, rem = None, x
    for _ in range(passes):
        hi = rem.astype(BF16)
        d = jnp.dot(hi, w, preferred_element_type=F32)
        acc = d if acc is None else acc + d
        rem = rem - hi.astype(F32)
    return acc


def _chunk_cumsum(x, reverse):
    n, w = x.shape
    tiles = x.reshape(n // SUBLANES, SUBLANES, w)
    sub = lax.broadcasted_iota(jnp.int32, (1, SUBLANES, 1), 1)
    s = 1
    while s < SUBLANES:
        if reverse:
            tiles = tiles + jnp.where(sub < SUBLANES - s, pltpu.roll(tiles, SUBLANES - s, axis=1), 0.0)
        else:
            tiles = tiles + jnp.where(sub >= s, pltpu.roll(tiles, s, axis=1), 0.0)
        s *= 2
    per_chunk = SCAN_CHUNK // SUBLANES
    edge = 0 if reverse else SUBLANES - 1
    out = [None] * (n // SUBLANES)
    for c in range(n // SCAN_CHUNK):
        run = None
        order = range(per_chunk - 1, -1, -1) if reverse else range(per_chunk)
        for k in order:
            tile = tiles[c * per_chunk + k]
            out[c * per_chunk + k] = tile if run is None else tile + run
            total = tile[edge:edge + 1, :]
            run = total if run is None else run + total
    return jnp.concatenate(out, axis=0)


def _ada_kernel(c_ref, w_ref, b_ref, o_ref):
    o_ref[...] = jnp.dot(_silu(c_ref[...]), w_ref[...], precision=lax.Precision.HIGHEST,
                         preferred_element_type=F32) + b_ref[...]


def _ada(cvecs, ada_w, ada_b):
    depth, d, n = ada_w.shape
    tn = n // 4
    return pl.pallas_call(
        _ada_kernel,
        grid=(depth, n // tn),
        in_specs=[pl.BlockSpec((SUBLANES, d), lambda l, j: (0, 0)),
                  pl.BlockSpec((None, d, tn), lambda l, j: (l, 0, j)),
                  pl.BlockSpec((None, 1, tn), lambda l, j: (l, 0, j))],
        out_specs=pl.BlockSpec((None, SUBLANES, tn), lambda l, j: (l, 0, j)),
        out_shape=jax.ShapeDtypeStruct((depth, SUBLANES, n), F32),
        compiler_params=_cparams(("arbitrary", "arbitrary"), VMEM_STREAMING_MIB),
        name="ada",
    )(cvecs, ada_w, ada_b.reshape(depth, 1, n))


EVEN_TN = 1024
HALO = 2 * SUBLANES
EVEN_TILES = (("conv", 0, None), ("conv_dt", 1, None), ("silu", None, 0), ("copy", None, 1),
              ("gate", 2, 4), ("gate", 3, 5), ("silu", None, 2), ("silu", None, 3))


def _even_in_kernel(h_ref, hp_ref, hn_ref, mod_ref, w_ref, cw_ref, cb_ref, dtc_ref, lb_ref, za_ref, zb_ref, a_ref):
    i = pl.program_id(0)
    tm = h_ref.shape[0]
    n_lb = lb_ref.shape[1]
    a_ref[:HALO] = _modulate(hp_ref[...], mod_ref).astype(BF16)
    a_ref[HALO:HALO + tm] = _modulate(h_ref[...], mod_ref).astype(BF16)
    a_ref[HALO + tm:] = _modulate(hn_ref[...], mod_ref).astype(BF16)
    row = lax.broadcasted_iota(jnp.int32, (tm + 2 * HALO, 1), 0)
    outside = ((row < HALO) & (i == 0)) | ((row >= HALO + tm) & (i == pl.num_programs(0) - 1))

    def conv_silu(col0, lanes):
        acc = jnp.dot(a_ref[...], w_ref[:, col0:col0 + lanes], preferred_element_type=F32)
        acc = jnp.where(outside, 0.0, acc)
        conv = cb_ref[:, col0:col0 + lanes]
        for k in range(SSD_CONV):
            off = HALO - SSD_CONV // 2 + k
            conv = conv + cw_ref[k:k + 1, col0:col0 + lanes] * acc[off:off + tm]
        return _silu(conv)

    def main_dot(col0, lanes=EVEN_TN):
        return jnp.dot(a_ref[HALO:HALO + tm], w_ref[:, col0:col0 + lanes], preferred_element_type=F32)

    for tile, (kind, fa, fb) in enumerate(EVEN_TILES):
        col0 = tile * EVEN_TN
        if kind == "conv":
            za_ref[:, fa * EVEN_TN:(fa + 1) * EVEN_TN] = conv_silu(col0, EVEN_TN)
        elif kind == "conv_dt":
            n_c = ZS_DT - col0
            za_ref[:, col0:ZS_DT] = conv_silu(col0, n_c)
            dt = main_dot(ZS_DT, 2 * LANES)
            for d in range(2):
                za_ref[:, ZS_DT + d * LANES:ZS_DT + (d + 1) * LANES] = jax.nn.softplus(
                    dt[:, d * LANES:(d + 1) * LANES] + dtc_ref[d, 0:1, :])
            za_ref[:, ZS_XD:(fa + 1) * EVEN_TN] = jnp.zeros((tm, (fa + 1) * EVEN_TN - ZS_XD), F32)
        elif kind == "gate":
            lb = lb_ref[:, (col0 - ZS_F) % n_lb:(col0 - ZS_F) % n_lb + EVEN_TN]
            f = lb + (1.0 - lb) * jax.nn.sigmoid(main_dot(col0))
            backward = (col0 - ZS_F) // n_lb == 1
            za_ref[:, fa * EVEN_TN:(fa + 1) * EVEN_TN] = _chunk_cumsum(jnp.log(f), backward)
            zb_ref[:, fb * EVEN_TN:(fb + 1) * EVEN_TN] = (1.0 - f).astype(BF16)
        elif kind == "silu":
            zb_ref[:, fb * EVEN_TN:(fb + 1) * EVEN_TN] = _silu(main_dot(col0)).astype(BF16)
        else:
            zb_ref[:, fb * EVEN_TN:(fb + 1) * EVEN_TN] = main_dot(col0).astype(BF16)


def _even_in(h, mod, w, cw, cb, dtc, lb):
    t, d = h.shape
    tm = min(t, ROWS_EVEN_IN)
    hb = tm // HALO
    n_a = (1 + max(fa for _, fa, _ in EVEN_TILES if fa is not None)) * EVEN_TN
    n_b = (1 + max(fb for _, _, fb in EVEN_TILES if fb is not None)) * EVEN_TN
    whole = lambda a, **kw: pl.BlockSpec(a.shape, lambda i: (0,) * a.ndim, **kw)
    return pl.pallas_call(
        _even_in_kernel,
        grid=(t // tm,),
        in_specs=[pl.BlockSpec((tm, d), lambda i: (i, 0)),
                  pl.BlockSpec((HALO, d), lambda i: (jnp.maximum(i * hb - 1, 0), 0)),
                  pl.BlockSpec((HALO, d), lambda i: (jnp.minimum((i + 1) * hb, t // HALO - 1), 0)),
                  whole(mod), whole(w, pipeline_mode=pl.Buffered(1)),
                  whole(cw), whole(cb), whole(dtc), whole(lb)],
        out_specs=[pl.BlockSpec((tm, n_a), lambda i: (i, 0)), pl.BlockSpec((tm, n_b), lambda i: (i, 0))],
        out_shape=[jax.ShapeDtypeStruct((t, n_a), F32), jax.ShapeDtypeStruct((t, n_b), BF16)],
        scratch_shapes=[pltpu.VMEM((tm + 2 * HALO, d), BF16)],
        compiler_params=_cparams(("arbitrary",), VMEM_RESIDENT_MIB),
        name="even_in",
    )(h, h, h, mod, w, cw, cb, dtc, lb)


def _scan_block_prep(xd_ref, qv_ref, bcum_ref, kk_ref, dtc_ref, e_ref, d):
    Q = SCAN_CHUNK
    reverse = bool(d)
    dtv = xd_ref[:, ZS_DT + d * LANES:ZS_DT + (d + 1) * LANES]
    acum = _chunk_cumsum(-dtv * jnp.exp(dtc_ref[d, 1:2, :]), reverse)
    rows = acum.shape[0]
    nchunk = rows // Q
    edge = 0 if reverse else Q - 1
    lasts = [acum[c * Q + edge:c * Q + edge + 1, :] for c in range(nchunk)]
    last_rows = jnp.concatenate([jnp.broadcast_to(l, (Q, LANES)) for l in lasts], axis=0)
    e_last = jnp.concatenate([jnp.broadcast_to(jnp.exp(l), (SUBLANES, LANES)) for l in lasts], axis=0)
    terms, rem = [], e_last
    for _ in range(3):
        hi = rem.astype(BF16)
        terms.append(hi)
        rem = rem - hi.astype(F32)
    lhs = jnp.concatenate([dtv.astype(BF16), jnp.exp(acum).astype(BF16), jnp.exp(last_rows - acum).astype(BF16)]
                          + terms, axis=0)
    xe = jnp.dot(lhs, e_ref[...], preferred_element_type=F32)
    n8 = SUBLANES * nchunk
    el = xe[3 * rows:]
    expanded = (xe[:rows], xe[rows:2 * rows], xe[2 * rows:3 * rows], el[:n8] + el[n8:2 * n8] + el[2 * n8:])
    return xd_ref, qv_ref, bcum_ref, kk_ref, acum, expanded


DECAY_LIMIT = 80.0


def _hg_intra_exact(qs, bcum, bcum_ref, kk_ref, v_ref, acc_ref, row0, reverse):
    Q = SCAN_CHUNK
    acc_ref[...] = jnp.zeros_like(acc_ref)
    t_idx = lax.broadcasted_iota(jnp.int32, (Q, 1), 0)

    def body(s, carry):
        r = row0 + s
        b_s, k_s, v_s = bcum_ref[pl.ds(r, 1), :], kk_ref[pl.ds(r, 1), :], v_ref[pl.ds(r, 1), :]
        allowed = (t_idx <= s) if reverse else (t_idx >= s)
        w = jnp.where(allowed, jnp.exp(jnp.minimum(bcum - b_s, 0.0)) * (qs * k_s), 0.0)
        for h in range(HG_HEADS):
            sl = slice(h * HG_HEAD_DIM, (h + 1) * HG_HEAD_DIM)
            acc_ref[:, sl] += jnp.sum(w[:, sl], axis=-1, keepdims=True) * v_s[:, sl]
        return carry

    lax.fori_loop(0, Q, body, 0)
    return acc_ref[...]


def _scan_chunk(prep, rows, dsk_ref, st_ref, gt_ref, yo_ref, d, reverse, exact=None):
    Q = SCAN_CHUNK
    xd_ref, qv_ref, bcum_ref, kk_ref, acum, expanded = prep
    n_hg = bcum_ref.shape[1]
    ti = lax.broadcasted_iota(jnp.int32, (Q, Q), 0)
    si = lax.broadcasted_iota(jnp.int32, (Q, Q), 1)
    mask = (ti <= si) if reverse else (ti >= si)
    edge = 0 if reverse else Q - 1
    width = SSD_HEADS * SSD_HEAD_DIM
    ci = rows.start // Q
    dt_x, ea_x, wend_x, elast_x = expanded
    chunk_x = (dt_x[rows], ea_x[rows], wend_x[rows], elast_x[ci * SUBLANES:ci * SUBLANES + 1])
    _scan_chunk_ssd(xd_ref, chunk_x, acum[rows], rows, dsk_ref, st_ref, yo_ref, d, reverse, mask)
    if exact is not None:
        intra = _hg_intra_exact(qv_ref[rows, :n_hg].astype(F32), bcum_ref[rows, :], *exact, rows.start, reverse)
    hw = n_hg // 2
    for part in range(2):
        cols = slice(part * hw, (part + 1) * hw)
        kk, bcum = kk_ref[rows, cols].astype(F32), bcum_ref[rows, cols]
        qs = qv_ref[rows, cols].astype(F32)
        vb = qv_ref[rows, n_hg + part * hw:n_hg + (part + 1) * hw]
        blast = bcum[edge:edge + 1, :]
        qe = (qs * jnp.exp(bcum)).astype(BF16)
        k_end = kk * jnp.exp(blast - bcum)
        if exact is None:
            k_til = (k_end * jnp.exp(-blast)).astype(BF16)
        k_end = k_end.astype(BF16)
        e_blast = jnp.exp(blast)
        outs = []
        for hl in range(hw // HG_HEAD_DIM):
            h = part * (hw // HG_HEAD_DIM) + hl
            sl = slice(hl * HG_HEAD_DIM, (hl + 1) * HG_HEAD_DIM)
            gt = gt_ref[d, h]
            if exact is None:
                att = lax.dot_general(qe[:, sl], k_til[:, sl], (((1,), (1,)), ((), ())), preferred_element_type=F32)
                att = jnp.where(mask, att, 0.0).astype(BF16)
                o_h = jnp.dot(att, vb[:, sl], preferred_element_type=F32)
            else:
                o_h = intra[:, h * HG_HEAD_DIM:(h + 1) * HG_HEAD_DIM]
            o_h = o_h + lax.dot_general(qe[:, sl], gt.astype(BF16), (((1,), (1,)), ((), ())),
                                        preferred_element_type=F32)
            outs.append(o_h)
            upd = lax.dot_general(vb[:, sl], k_end[:, sl], (((0,), (0,)), ((), ())), preferred_element_type=F32)
            gt_ref[d, h] = gt * e_blast[:, sl] + upd
        yo_ref[rows, width + part * hw:width + (part + 1) * hw] = jnp.concatenate(outs, axis=1).astype(yo_ref.dtype)


def _scan_chunk_ssd(xd_ref, chunk_x, acum, rows, dsk_ref, st_ref, yo_ref, d, reverse, mask):
    Q = SCAN_CHUNK
    width = SSD_HEADS * SSD_HEAD_DIM
    gw = width // SSD_GROUPS
    dt_all, ea_all, wend_all, elast_all = chunk_x
    acum_t = acum.T
    lane = lax.broadcasted_iota(jnp.int32, (Q, LANES), 1)
    lo = lane < SSD_HEAD_DIM
    zero_b = jnp.zeros((Q, LANES), BF16)
    for g in range(SSD_GROUPS):
        cols = slice(g * gw, (g + 1) * gw)
        dt_x, ea_x, wend_x, elast_x = dt_all[:, cols], ea_all[:, cols], wend_all[:, cols], elast_all[:, cols]
        xs = xd_ref[rows, cols]
        xdt = xs * dt_x
        xw = (xdt * wend_x).astype(BF16)
        xdt_b = xdt.astype(BF16)
        bm = xd_ref[rows, width + g * SSD_STATE:width + (g + 1) * SSD_STATE].astype(BF16)
        cm = xd_ref[rows, width + (SSD_GROUPS + g) * SSD_STATE:width + (SSD_GROUPS + g + 1) * SSD_STATE].astype(BF16)
        cb = lax.dot_general(cm, bm, (((1,), (1,)), ((), ())), preferred_element_type=F32)
        y_state = jnp.dot(cm, st_ref[d, g].astype(BF16), preferred_element_type=F32)
        pairs = []
        for hp in range(gw // LANES):
            xp = xdt_b[:, hp * LANES:(hp + 1) * LANES]
            acc = None
            for half in range(2):
                h = (g * gw + hp * LANES) // SSD_HEAD_DIM + half
                seg = acum[:, h:h + 1] - acum_t[h:h + 1, :]
                m = jnp.where(mask, cb * jnp.exp(seg), 0.0).astype(BF16)
                xh = jnp.where(lo, xp, zero_b) if half == 0 else jnp.where(lo, zero_b, xp)
                part = jnp.dot(m, xh, preferred_element_type=F32)
                acc = part if acc is None else acc + part
            pairs.append(acc)
        y = jnp.concatenate(pairs, axis=1) + y_state * ea_x
        if not reverse:
            y = y + dsk_ref[:, cols] * xs
        yo_ref[rows, cols] = y.astype(yo_ref.dtype)
        upd = lax.dot_general(bm, xw, (((0,), (0,)), ((), ())), preferred_element_type=F32)
        st_ref[d, g] = st_ref[d, g] * elast_x + upd


def _even_scan_kernel(*refs, nblk, cpb):
    fwd_in, bwd_in = refs[0:4], refs[4:8]
    dtc_ref, dsk_ref, e_ref, h0s_ref, h0g_ref = refs[8:13]
    yof_ref, yob_ref, st_ref, gt_ref, kkf_ref, vf_ref, acc_ref = refs[13:20]
    c = pl.program_id(0)

    @pl.when(c == 0)
    def _():
        st_ref[...] = h0s_ref[...]
        gt_ref[...] = h0g_ref[...]

    streams = ((False, fwd_in, yof_ref), (True, bwd_in, yob_ref))

    def run(exact):
        preps = [_scan_block_prep(*ins, dtc_ref, e_ref, d) for d, (_, ins, _) in enumerate(streams)]
        if exact:
            for d, (_, ins, _) in enumerate(streams):
                kkf_ref[d] = ins[3][...].astype(F32)
                vf_ref[d] = ins[1][:, ins[2].shape[1]:].astype(F32)
        for step in range(cpb):
            for d, (reverse, ins, yo_ref) in enumerate(streams):
                ci = (cpb - 1 - step) if reverse else step
                rows = slice(ci * SCAN_CHUNK, (ci + 1) * SCAN_CHUNK)
                refs_exact = (ins[2], kkf_ref.at[d], vf_ref.at[d], acc_ref) if exact else None
                _scan_chunk(preps[d], rows, dsk_ref, st_ref, gt_ref, yo_ref, d, reverse, refs_exact)

    totals = [ref[ci * SCAN_CHUNK + edge:ci * SCAN_CHUNK + edge + 1, :]
              for ref, edge in ((fwd_in[2], SCAN_CHUNK - 1), (bwd_in[2], 0)) for ci in range(cpb)]
    strongest = jnp.min(jnp.concatenate(totals, axis=0))
    pl.when(strongest >= -DECAY_LIMIT)(lambda: run(False))
    pl.when(strongest < -DECAY_LIMIT)(lambda: run(True))


def _even_scan(za, zb, consts, h0s, h0g):
    t = za.shape[0]
    nch = t // SCAN_CHUNK
    cpb = max(k for k in (4, 2, 1) if nch % k == 0)
    rows = cpb * SCAN_CHUNK
    nblk = nch // cpb
    dtc, dsk, e16 = consts
    n_x = SSD_HEADS * SSD_HEAD_DIM
    n_hg = HG_HEADS * HG_HEAD_DIM
    gate_a, gate_b = (tuple(tile[i] for tile in EVEN_TILES if tile[0] == "gate") for i in (1, 2))

    def stream_specs(bidx, d):
        return [
            pl.BlockSpec((rows, ZS_XD), lambda c: (bidx(c), 0)),
            pl.BlockSpec((rows, 2 * n_hg), lambda c: (bidx(c), 0)),
            pl.BlockSpec((rows, n_hg), lambda c: (bidx(c), gate_a[d])),
            pl.BlockSpec((rows, n_hg), lambda c: (bidx(c), gate_b[d])),
        ]

    fwd = lambda c: c
    bwd = lambda c: nblk - 1 - c
    whole = lambda a: pl.BlockSpec(a.shape, lambda c: (0,) * a.ndim)
    in_specs = stream_specs(fwd, 0) + stream_specs(bwd, 1) + [whole(a) for a in (dtc, dsk, e16, h0s, h0g)]
    out_specs = [pl.BlockSpec((rows, n_x + n_hg), lambda c: (c, 0)),
                 pl.BlockSpec((rows, n_x + n_hg), lambda c: (bwd(c), 0)), whole(h0s), whole(h0g)]
    out_shape = [jax.ShapeDtypeStruct((t, n_x + n_hg), BF16), jax.ShapeDtypeStruct((t, n_x + n_hg), BF16),
                 jax.ShapeDtypeStruct(h0s.shape, F32), jax.ShapeDtypeStruct(h0g.shape, F32)]
    return pl.pallas_call(
        functools.partial(_even_scan_kernel, nblk=nblk, cpb=cpb),
        grid=(nblk,),
        in_specs=in_specs,
        out_specs=out_specs,
        out_shape=out_shape,
        scratch_shapes=[pltpu.VMEM((2, rows, n_hg), F32), pltpu.VMEM((2, rows, n_hg), F32),
                        pltpu.VMEM((SCAN_CHUNK, n_hg), F32)],
        compiler_params=_cparams(("arbitrary",), VMEM_MATRICES_MIB),
        name="even_scan",
    )(za, zb, za, zb, za, zb, za, zb, dtc, dsk, e16, h0s, h0g)


def _ffn_block(h, mod_ref, wg_ref, wu_ref, wd_ref, final):
    a = _modulate(h, mod_ref).astype(BF16)
    f = wg_ref.shape[1]
    acc = None
    for k in range(FFN_SPLITS):
        cols = slice(k * f // FFN_SPLITS, (k + 1) * f // FFN_SPLITS)
        gate = jnp.dot(a, wg_ref[:, cols], preferred_element_type=F32)
        up = jnp.dot(a, wu_ref[:, cols], preferred_element_type=F32)
        act = (_silu(gate) * up).astype(BF16)
        part = jnp.dot(act, wd_ref[cols, :], preferred_element_type=F32)
        acc = part if acc is None else acc + part
    out = h + mod_ref[3:4, :] * acc
    if final:
        ms = jnp.mean(out * out, axis=-1, keepdims=True)
        out = out * lax.rsqrt(ms + EPS) * mod_ref[4:5, :]
    return out


def _resident(a):
    return pl.BlockSpec(a.shape, lambda i: (0,) * a.ndim, pipeline_mode=pl.Buffered(1))


def _even_out_kernel(h_ref, z_ref, g_ref, yof_ref, yob_ref, w_ref, vec_ref, mod_ref, wg_ref, wu_ref, wd_ref, o_ref):
    n_x = z_ref.shape[1]
    yo = yof_ref[...].astype(F32) + yob_ref[...].astype(F32)
    y = yo[:, :n_x] * z_ref[...].astype(F32)
    o = yo[:, n_x:]
    gw = n_x // SSD_GROUPS
    parts = []
    for g in range(SSD_GROUPS):
        yg = y[:, g * gw:(g + 1) * gw]
        ms = jnp.mean(yg * yg, axis=-1, keepdims=True)
        parts.append(yg * lax.rsqrt(ms + EPS))
    yn = (jnp.concatenate(parts, axis=1) * vec_ref[0:1, :]).astype(BF16)
    parts = []
    for hh in range(HG_HEADS):
        oh = o[:, hh * HG_HEAD_DIM:(hh + 1) * HG_HEAD_DIM]
        ms = jnp.mean(oh * oh, axis=-1, keepdims=True)
        parts.append(oh * lax.rsqrt(ms + EPS))
    on = (jnp.concatenate(parts, axis=1) * vec_ref[1:2, :] * g_ref[...].astype(F32)).astype(BF16)
    half = yn.shape[1]
    mix = jnp.dot(yn, w_ref[:half, :], preferred_element_type=F32)
    mix = mix + jnp.dot(on, w_ref[half:, :], preferred_element_type=F32)
    h1 = h_ref[...] + vec_ref[2:3, :] * mix
    o_ref[...] = _ffn_block(h1, mod_ref, wg_ref, wu_ref, wd_ref, final=False)


def _even_out(h, zb, yo_f, yo_b, w_out, vec, mod, wg, wu, wd):
    t, d = h.shape
    tm = min(t, ROWS_FUSED_OUT)
    n_x = SSD_HEADS * SSD_HEAD_DIM
    n_hg = HG_HEADS * HG_HEAD_DIM
    row = lambda i: (i, 0)
    z_tile, g_tile = (next(fb for k, (_, _, fb) in enumerate(EVEN_TILES) if k * EVEN_TN == start)
                      for start in (ZS_Z, ZS_G))
    return pl.pallas_call(
        _even_out_kernel,
        grid=(t // tm,),
        in_specs=[pl.BlockSpec((tm, d), row),
                  pl.BlockSpec((tm, n_x), lambda i: (i, z_tile)),
                  pl.BlockSpec((tm, n_hg), lambda i: (i, g_tile)),
                  pl.BlockSpec((tm, n_x + n_hg), row), pl.BlockSpec((tm, n_x + n_hg), row),
                  _resident(w_out), _resident(vec), _resident(mod), _resident(wg), _resident(wu), _resident(wd)],
        out_specs=pl.BlockSpec((tm, d), row),
        out_shape=jax.ShapeDtypeStruct((t, d), F32),
        compiler_params=_cparams(("arbitrary",), VMEM_RESIDENT_MIB),
        name="even_out_ffn",
    )(h, zb, zb, yo_f, yo_b, w_out, vec, mod, wg, wu, wd)


def _odd_in_kernel(h_ref, mod_ref, w_ref, o_ref, u_ref):
    a = _modulate(h_ref[...], mod_ref).astype(BF16)
    u = jnp.dot(a, w_ref[...], preferred_element_type=F32)
    nj, grid_w, rb, _ = o_ref.shape
    for j in range(nj):
        u_ref[j] = u[:, j * LANES:(j + 1) * LANES]
    for wi in range(grid_w):
        for j in range(nj):
            o_ref[j, wi] = u_ref[j, pl.ds(wi, rb, stride=grid_w), :]


def _odd_in(h, mod, w, grid_w):
    t, d = h.shape
    rows = t // grid_w
    rb = min(rows, ROWS_ODD_IN // GRID_W)
    n = w.shape[1]
    nj = n // LANES
    out = pl.pallas_call(
        _odd_in_kernel,
        grid=(rows // rb,),
        in_specs=[pl.BlockSpec((rb * grid_w, d), lambda i: (i, 0)),
                  pl.BlockSpec((SUBLANES, d), lambda i: (0, 0)),
                  pl.BlockSpec(w.shape, lambda i: (0, 0))],
        out_specs=pl.BlockSpec((nj, grid_w, rb, LANES), lambda i: (0, 0, i, 0)),
        out_shape=jax.ShapeDtypeStruct((nj, grid_w, rows, LANES), F32),
        scratch_shapes=[pltpu.VMEM((nj, rb * grid_w, LANES), F32)],
        compiler_params=_cparams(("arbitrary",), VMEM_MATRICES_MIB),
        name="odd_in",
    )(h, mod, w)
    return out.reshape(nj, t, LANES)


def _sub_chunk_rows(u_ref, nb):
    return [u_ref[pl.ds(s, nb, stride=S5_SUB), :] for s in range(S5_SUB)]


def _s5_inject_kernel(u_ref, win_ref, s_ref):
    x = jnp.concatenate(_sub_chunk_rows(u_ref, s_ref.shape[0]), axis=1).astype(BF16)
    s_ref[...] = jnp.dot(x, win_ref[...], preferred_element_type=F32)


def _s5_inject(u, win):
    nj, t, _ = u.shape
    n = t // S5_SUB
    nb = min(n, S5_INJECT_SUBCHUNKS)
    width, ns = win.shape[1:]
    return pl.pallas_call(
        _s5_inject_kernel,
        grid=(nj, n // nb),
        in_specs=[pl.BlockSpec((None, nb * S5_SUB, LANES), lambda j, b: (j, b, 0)),
                  pl.BlockSpec((None, width, ns), lambda j, b: (j, 0, 0))],
        out_specs=pl.BlockSpec((None, nb, ns), lambda j, b: (j, b, 0)),
        out_shape=jax.ShapeDtypeStruct((nj, n, ns), F32),
        compiler_params=_cparams(("arbitrary", "arbitrary"), VMEM_STREAMING_MIB),
        name="s5_inject",
    )(u, win)


def _s5_carry_kernel(sf_ref, sb_ref, lam_ref, h0_ref, pf_ref, pb_ref, hf_ref, tf_ref, tb_ref, c_ref):
    b = pl.program_id(0)
    nj, kb, ns = sf_ref.shape
    half = ns // 2

    @pl.when(b == 0)
    def _():
        c_ref[...] = h0_ref[...]

    for j in range(nj):
        tf_ref[:, j, :] = sf_ref[j]
        tb_ref[:, j, :] = sb_ref[j]

    def advance(d, c, s):
        lr, li = lam_ref[d, :, :half], lam_ref[d, :, half:]
        pr, pi = c[:, :half], c[:, half:]
        return jnp.concatenate([lr * pr - li * pi + s[:, :half], lr * pi + li * pr + s[:, half:]], axis=1)

    def step(i, carry):
        cf, cb = carry
        kr = kb - 1 - i
        sf = tf_ref[i]
        sb = tb_ref[kr]
        tf_ref[i] = cf
        tb_ref[kr] = cb
        return advance(0, cf, sf), advance(1, cb, sb)

    cf, cb = lax.fori_loop(0, kb, step, (c_ref[0], c_ref[1]), unroll=4)
    c_ref[0] = cf
    c_ref[1] = cb
    for j in range(nj):
        pf_ref[j] = tf_ref[:, j, :]
        pb_ref[j] = tb_ref[:, j, :]

    @pl.when(b == pl.num_programs(0) - 1)
    def _():
        hf_ref[...] = c_ref[...]


def _s5_carry(s_f, s_b, lam, h0):
    nj, n, ns = s_f.shape
    kb = min(n, S5_CARRY_SUBCHUNKS)
    nblk = n // kb
    fwd = pl.BlockSpec((nj, kb, ns), lambda b: (0, b, 0))
    bwd = pl.BlockSpec((nj, kb, ns), lambda b: (0, nblk - 1 - b, 0))
    whole = pl.BlockSpec(lam.shape, lambda b: (0, 0, 0))
    return pl.pallas_call(
        _s5_carry_kernel,
        grid=(nblk,),
        in_specs=[fwd, bwd, whole, whole],
        out_specs=[fwd, bwd, whole],
        out_shape=[jax.ShapeDtypeStruct(s_f.shape, F32), jax.ShapeDtypeStruct(s_b.shape, F32),
                   jax.ShapeDtypeStruct(lam.shape, F32)],
        scratch_shapes=[pltpu.VMEM((kb, nj, ns), F32), pltpu.VMEM((kb, nj, ns), F32), pltpu.VMEM(lam.shape, F32)],
        compiler_params=_cparams(("arbitrary",), VMEM_MATRICES_MIB),
        name="s5_carry",
    )(s_f, s_b, lam, h0)


def _s5_readout_kernel(u_ref, p_ref, wout_ref, tt_ref, add_ref, y_ref, *, reverse):
    nb = p_ref.shape[0]
    us = _sub_chunk_rows(u_ref, nb)
    xb = jnp.concatenate(us, axis=1).astype(BF16)
    pb = p_ref[...].astype(BF16)
    width = xb.shape[1]
    tile = 2 * LANES
    for m in range(width // tile):
        cols = slice(m * tile, (m + 1) * tile)
        rows = slice(m * tile, width) if reverse else slice(0, (m + 1) * tile)
        y = jnp.dot(pb, wout_ref[:, cols], preferred_element_type=F32)
        y = y + jnp.dot(xb[:, rows], tt_ref[rows, cols], preferred_element_type=F32)
        for q in range(tile // LANES):
            s = m * (tile // LANES) + q
            other = add_ref[pl.ds(s, nb, stride=S5_SUB), :] if reverse else add_ref[...] * us[s]
            y_ref[pl.ds(s, nb, stride=S5_SUB), :] = y[:, q * LANES:(q + 1) * LANES] + other


def _s5_readout(u, p, wout, tt, add, reverse):
    nj, t, _ = u.shape
    n = t // S5_SUB
    nb = min(n, S5_READOUT_SUBCHUNKS)
    ns, width = wout.shape[1:]
    tokens = pl.BlockSpec((None, nb * S5_SUB, LANES), lambda j, b: (j, b, 0))
    add_spec = tokens if reverse else pl.BlockSpec((None, 1, LANES), lambda j, b: (j, 0, 0))
    return pl.pallas_call(
        functools.partial(_s5_readout_kernel, reverse=reverse),
        grid=(nj, n // nb),
        in_specs=[tokens,
                  pl.BlockSpec((None, nb, ns), lambda j, b: (j, b, 0)),
                  pl.BlockSpec((None, ns, width), lambda j, b: (j, 0, 0)),
                  pl.BlockSpec((None, width, width), lambda j, b: (j, 0, 0)),
                  add_spec],
        out_specs=tokens,
        out_shape=jax.ShapeDtypeStruct(u.shape, F32),
        compiler_params=_cparams(("arbitrary", "arbitrary"), VMEM_MATRICES_MIB),
        name="s5_readout_bwd" if reverse else "s5_readout_fwd",
    )(u, p, wout, tt, add)


def _odd_out_kernel(h_ref, y_ref, wv_ref, wgl_ref, vec_ref, mod_ref, wg_ref, wu_ref, wd_ref, o_ref, a_ref):
    nj, grid_w, rb, _ = y_ref.shape
    for wi in range(grid_w):
        for j in range(nj):
            a_ref[j, pl.ds(wi, rb, stride=grid_w), :] = y_ref[j, wi]
    a = jax.nn.gelu(jnp.concatenate([a_ref[j] for j in range(nj)], axis=1)).astype(BF16)
    val = jnp.dot(a, wv_ref[...], preferred_element_type=F32)
    gate = jnp.dot(a, wgl_ref[...], preferred_element_type=F32)
    h3 = h_ref[...] + vec_ref[0:1, :] * (val * jax.nn.sigmoid(gate))
    o_ref[...] = _ffn_block(h3, mod_ref, wg_ref, wu_ref, wd_ref, final=True)


def _odd_out(h, y, w_val, w_gate, vec, mod, wg, wu, wd, grid_w):
    t, d = h.shape
    rows = t // grid_w
    rb = min(rows, ROWS_ODD_OUT // GRID_W)
    nj = y.shape[0]
    tok = pl.BlockSpec((rb * grid_w, d), lambda i: (i, 0))
    return pl.pallas_call(
        _odd_out_kernel,
        grid=(rows // rb,),
        in_specs=[tok, pl.BlockSpec((nj, grid_w, rb, LANES), lambda i: (0, 0, i, 0)),
                  _resident(w_val), _resident(w_gate), _resident(vec), _resident(mod),
                  _resident(wg), _resident(wu), _resident(wd)],
        out_specs=tok,
        out_shape=jax.ShapeDtypeStruct((t, d), F32),
        scratch_shapes=[pltpu.VMEM((nj, rb * grid_w, LANES), F32)],
        compiler_params=_cparams(("arbitrary",), VMEM_RESIDENT_MIB),
        name="odd_out_ffn",
    )(h, y.reshape(nj, grid_w, rows, LANES), w_val, w_gate, vec, mod, wg, wu, wd)


def _s5_prep_kernel(prm_ref, bre_ref, bim_ref, colp_ref, cre_ref, cim_ref, win_ref, wout_ref, tt_ref, lam_ref, *,
                    reverse):
    T = S5_SUB
    ns = prm_ref.shape[1]
    a_re, a_im, dt = prm_ref[0:1, :], prm_ref[1:2, :], jnp.exp(prm_ref[2:3, :])
    tau = lax.broadcasted_iota(jnp.int32, (3 * SUBLANES, ns), 0).astype(F32)
    mag = jnp.exp(a_re * dt * tau)
    pr = mag * jnp.cos(a_im * dt * tau)
    pi = mag * jnp.sin(a_im * dt * tau)
    lr, li = pr[1:2], pi[1:2]
    den = a_re * a_re + a_im * a_im
    cr = ((lr - 1.0) * a_re + li * a_im) / den
    ci = (li * a_re - (lr - 1.0) * a_im) / den
    row_g = lax.broadcasted_iota(jnp.int32, (LANES, ns), 0) // S5_GROUP
    col_g = lax.broadcasted_iota(jnp.int32, (LANES, ns), 1) // S5_STATE
    same = row_g == col_g
    bbr = jnp.where(same, cr * bre_ref[...] - ci * bim_ref[...], 0.0)
    bbi = jnp.where(same, cr * bim_ref[...] + ci * bre_ref[...], 0.0)
    for s in range(T):
        e = s if reverse else T - 1 - s
        rows = slice(s * LANES, (s + 1) * LANES)
        win_ref[rows, :ns] = (pr[e:e + 1] * bbr - pi[e:e + 1] * bbi).astype(BF16)
        win_ref[rows, ns:] = (pr[e:e + 1] * bbi + pi[e:e + 1] * bbr).astype(BF16)
    lam_ref[...] = jnp.broadcast_to(jnp.concatenate([pr[T:T + 1], pi[T:T + 1]], axis=1), lam_ref.shape)
    dt_c = jnp.exp(colp_ref[2])
    mag_c = jnp.exp(colp_ref[0] * dt_c)
    lr_c = mag_c * jnp.cos(colp_ref[1] * dt_c)
    li_c = mag_c * jnp.sin(colp_ref[1] * dt_c)
    row_gc = lax.broadcasted_iota(jnp.int32, (ns, LANES), 0) // S5_STATE
    col_gc = lax.broadcasted_iota(jnp.int32, (ns, LANES), 1) // S5_GROUP
    same_c = row_gc == col_gc
    c_re = jnp.where(same_c, cre_ref[...], 0.0)
    c_im = jnp.where(same_c, cim_ref[...], 0.0)
    bb = jnp.concatenate([bbr, bbi], axis=1)
    bb_hi = bb.astype(BF16)
    bb_lo = (bb - bb_hi.astype(F32)).astype(BF16)
    cur_r = jnp.ones((ns, LANES), F32)
    cur_i = jnp.zeros((ns, LANES), F32)
    kblk = []
    for e in range(T + 1):
        wo_r = c_re * cur_r - c_im * cur_i
        wo_i = -(c_re * cur_i + c_im * cur_r)
        if e >= 1:
            t = (T - e) if reverse else e - 1
            wout_ref[:ns, t * LANES:(t + 1) * LANES] = wo_r.astype(BF16)
            wout_ref[ns:, t * LANES:(t + 1) * LANES] = wo_i.astype(BF16)
        if e < T:
            wo = jnp.concatenate([wo_r, wo_i], axis=0)
            wo_hi = wo.astype(BF16)
            kblk.append((wo_hi, (wo - wo_hi.astype(F32)).astype(BF16)))
        cur_r, cur_i = cur_r * lr_c - cur_i * li_c, cur_r * li_c + cur_i * lr_c
    wo_hi = jnp.concatenate([hi for hi, _ in kblk], axis=1)
    wo_lo = jnp.concatenate([lo for _, lo in kblk], axis=1)
    k_all = jnp.dot(bb_hi, wo_hi, preferred_element_type=F32)
    k_all = k_all + jnp.dot(bb_hi, wo_lo, preferred_element_type=F32) + jnp.dot(bb_lo, wo_hi, preferred_element_type=F32)
    kblk = [k_all[:, e * LANES:(e + 1) * LANES].astype(BF16) for e in range(T)]
    zeros = jnp.zeros((LANES, LANES), BF16)
    for s in range(T):
        for t in range(T):
            lag = (s - t) if reverse else (t - s)
            tt_ref[s * LANES:(s + 1) * LANES, t * LANES:(t + 1) * LANES] = kblk[lag] if lag >= 0 else zeros


def _s5_prep(a_re, a_im, log_dt, b_re, b_im, c_re, c_im, reverse):
    g, p = a_re.shape
    cdim = b_re.shape[2]
    nj = g // S5_GPB
    ns = S5_GPB * p
    width = S5_SUB * LANES
    prm = jnp.zeros((nj, SUBLANES, ns), F32)
    prm = prm.at[:, 0].set(a_re.reshape(nj, ns)).at[:, 1].set(a_im.reshape(nj, ns))
    prm = prm.at[:, 2].set(jnp.repeat(log_dt, p).reshape(nj, ns))
    colp = jnp.broadcast_to(prm[:, :3, :, None], (nj, 3, ns, LANES))
    row_tile = lambda b: jnp.tile(b.reshape(nj, S5_GPB, p, cdim).transpose(0, 3, 1, 2).reshape(nj, 1, cdim, ns),
                                  (1, S5_GPB, 1, 1)).reshape(nj, LANES, ns)
    col_tile = lambda c_: jnp.tile(c_.reshape(nj, S5_GPB, cdim, p).transpose(0, 1, 3, 2).reshape(nj, ns, 1, cdim),
                                   (1, 1, S5_GPB, 1)).reshape(nj, ns, LANES)
    blk = lambda *shape: pl.BlockSpec((None,) + shape, lambda j: (j,) + (0,) * len(shape))
    return pl.pallas_call(
        functools.partial(_s5_prep_kernel, reverse=reverse),
        grid=(nj,),
        in_specs=[blk(SUBLANES, ns), blk(LANES, ns), blk(LANES, ns), blk(3, ns, LANES), blk(ns, LANES),
                  blk(ns, LANES)],
        out_specs=[blk(width, 2 * ns), blk(2 * ns, width), blk(width, width), blk(SUBLANES, 2 * ns)],
        out_shape=[jax.ShapeDtypeStruct((nj, width, 2 * ns), BF16), jax.ShapeDtypeStruct((nj, 2 * ns, width), BF16),
                   jax.ShapeDtypeStruct((nj, width, width), BF16), jax.ShapeDtypeStruct((nj, SUBLANES, 2 * ns), F32)],
        compiler_params=_cparams(("arbitrary",), VMEM_RESIDENT_MIB),
        name="s5_prep_bwd" if reverse else "s5_prep_fwd",
    )(prm, row_tile(b_re), row_tile(b_im), colp, col_tile(c_re), col_tile(c_im))


def _regroup_kernel(w_ref, o_ref, *, pieces):
    col = 0
    for start, width in pieces:
        if start is None:
            o_ref[:, col:col + width] = jnp.zeros((o_ref.shape[0], width), o_ref.dtype)
        else:
            o_ref[:, col:col + width] = w_ref[:, start:start + width].astype(o_ref.dtype)
        col += width


def _regroup_bf16(w, layer, pieces):
    _, rows, cols = w.shape
    n_out = sum(width for _, width in pieces)
    block_bytes = 8 * 1024 * 1024
    tr = max(k for k in (512, 256, 128, 64, 32, 16) if rows % k == 0 and k * cols * 4 <= block_bytes)
    return pl.pallas_call(
        functools.partial(_regroup_kernel, pieces=pieces),
        grid=(rows // tr,),
        in_specs=[pl.BlockSpec((None, tr, cols), lambda i: (layer, i, 0))],
        out_specs=pl.BlockSpec((tr, n_out), lambda i: (i, 0)),
        out_shape=jax.ShapeDtypeStruct((rows, n_out), BF16),
        compiler_params=_cparams(("arbitrary",), VMEM_MATRICES_MIB),
        name="weights_bf16",
    )(w)


def _bf16(w, layer):
    return _regroup_bf16(w, layer, ((0, w.shape[2]),))


def _rows(*vs, width):
    out = jnp.zeros((SUBLANES, width), F32)
    for i, v in enumerate(vs):
        out = out.at[i].set(v.astype(F32))
    return out


def kernel(x, c, ctx, c_ctx, ada_w, ada_b, norm_mix_g, norm_ffn_g, ffn_w_gate, ffn_w_up, ffn_w_down, final_norm_g, ev_w_in, ev_w_out, ssd_conv_w, ssd_conv_b, ssd_dt_bias, ssd_a_log, ssd_d, ssd_norm_g, hg_lb_logits, hg_norm_g, od_w_in, s5_a_re, s5_a_im, s5_log_dt, s5_b_re, s5_b_im, s5_c_re, s5_c_im, s5_d, od_w_val, od_w_gate):
    d = x.shape[-1]
    lat = x[0].astype(F32)
    hc = ctx[0].astype(F32)

    m = _ada(_rows(c[0], c_ctx, width=d), ada_w, ada_b)

    def ada_vecs(layer, stream):
        return [m[layer, stream, i * d:(i + 1) * d] for i in range(6)]

    n_x = SSD_HEADS * SSD_HEAD_DIM
    n_xbc = n_x + 2 * SSD_GROUPS * SSD_STATE
    n_hg = HG_HEADS * HG_HEAD_DIM
    o_z, o_xbc, o_dt = 0, n_x, n_x + n_xbc
    o_q = o_dt + 2 * SSD_HEADS
    o_f = o_q + n_hg
    o_v = o_f + 2 * n_hg
    o_g = o_v + n_hg
    w_in = _regroup_bf16(ev_w_in, 0, (
        (o_xbc, n_xbc),
        (o_dt, SSD_HEADS), (None, LANES - SSD_HEADS), (o_dt + SSD_HEADS, SSD_HEADS), (None, LANES - SSD_HEADS),
        (None, ZS_Q - ZS_XD),
        (o_q, n_hg), (o_v, n_hg), (o_f, 2 * n_hg), (o_z, n_x), (o_g, n_hg)))
    cw = jnp.zeros((SUBLANES, n_xbc), F32).at[:SSD_CONV].set(ssd_conv_w[0])
    cb = ssd_conv_b[0].reshape(1, n_xbc)
    dtc = jnp.zeros((2, SUBLANES, LANES), F32)
    dtc = dtc.at[:, 0, :SSD_HEADS].set(ssd_dt_bias[0]).at[:, 1, :SSD_HEADS].set(ssd_a_log[0])
    dsk = jnp.repeat(ssd_d[0], SSD_HEAD_DIM).reshape(1, n_x)
    lower = jnp.cumsum(jax.nn.softmax(hg_lb_logits.astype(F32), axis=0), axis=0)[0].reshape(1, n_hg)
    head_of_lane = jnp.arange(n_x) // SSD_HEAD_DIM
    e16 = (jnp.arange(LANES)[:, None] == head_of_lane[None, :]).astype(BF16)
    consts = (dtc, dsk, e16)
    w_out = _bf16(ev_w_out, 0)
    out_vec = lambda gate: _rows(ssd_norm_g[0], jnp.tile(hg_norm_g[0], HG_HEADS), gate, width=d)
    wg0, wu0, wd0 = (_bf16(t_, 0) for t_ in (ffn_w_gate, ffn_w_up, ffn_w_down))

    s_state = jnp.zeros((2, SSD_GROUPS, SSD_STATE, n_x // SSD_GROUPS), F32)
    g_state = jnp.zeros((2, HG_HEADS, HG_HEAD_DIM, HG_HEAD_DIM), F32)
    layer0 = {}
    for stream, h in ((1, hc), (0, lat)):
        sm, scm, gm, sf, scf, gf = ada_vecs(0, stream)
        za, zb = _even_in(h, _rows(norm_mix_g[0], scm, sm, width=d), w_in, cw, cb, dtc, lower)
        yo_f, yo_b, s_state, g_state = _even_scan(za, zb, consts, s_state, g_state)
        layer0[stream] = _even_out(h, zb, yo_f, yo_b, w_out, out_vec(gm),
                                   _rows(norm_ffn_g[0], scf, sf, gf, width=d), wg0, wu0, wd0)

    lat, hc = layer0[0], layer0[1]
    w_s5 = _bf16(od_w_in, 0)
    u = {}
    for stream, h, gw in ((1, hc, 1), (0, lat, GRID_W)):
        sm, scm = ada_vecs(1, stream)[:2]
        u[stream] = _odd_in(h, _rows(norm_mix_g[1], scm, sm, width=d), w_s5, gw)
    mats = [_s5_prep(s5_a_re[0, di], s5_a_im[0, di], s5_log_dt[0, di], s5_b_re[0, di], s5_b_im[0, di],
                     s5_c_re[0, di], s5_c_im[0, di], reverse=bool(di)) for di in range(2)]
    lam = jnp.stack([mats[0][3][:, 0], mats[1][3][:, 0]])
    carry = jnp.zeros_like(lam)
    for stream in (1, 0):
        s_f, s_b = (_s5_inject(u[stream], mats[di][0]) for di in range(2))
        p_f, p_b, carry = _s5_carry(s_f, s_b, lam, carry)
    nj = u[0].shape[0]
    y = _s5_readout(u[0], p_f, mats[0][1], mats[0][2], s5_d[0].reshape(nj, 1, LANES), reverse=False)
    y = _s5_readout(u[0], p_b, mats[1][1], mats[1][2], y, reverse=True)
    _, _, gm, sf, scf, gf = ada_vecs(1, 0)
    wg1, wu1, wd1 = (_bf16(t_, 1) for t_ in (ffn_w_gate, ffn_w_up, ffn_w_down))
    out = _odd_out(lat, y, _bf16(od_w_val, 0), _bf16(od_w_gate, 0), _rows(gm, width=d),
                   _rows(norm_ffn_g[1], scf, sf, gf, final_norm_g, width=d), wg1, wu1, wd1, GRID_W)
    return out[None].astype(x.dtype)
```

```python
import functools

import jax
import jax.numpy as jnp
from jax import lax
from jax.experimental import pallas as pl
from jax.experimental.pallas import tpu as pltpu

F32 = jnp.float32
BF16 = jnp.bfloat16
EPS = 1e-6

LANES = 128
SUBLANES = 8
GRID_W = 64
SCAN_CHUNK = 64
SSD_HEADS = 16
SSD_HEAD_DIM = 64
SSD_STATE = 128
SSD_GROUPS = 2
SSD_CONV = 5
HG_HEADS = 8
HG_HEAD_DIM = 128
S5_GROUP = 16
S5_STATE = 64
S5_SUB = 16
S5_GPB = LANES // S5_GROUP

ZS_XBC = 0
ZS_DT = 1536
ZS_XD = 1792
ZS_Q = 2048
ZS_V = 3072
ZS_F = 4096
ZS_Z = 6144
ZS_G = 7168
ZS_WIDTH = 8192


VMEM_STREAMING_MIB = 40
VMEM_MATRICES_MIB = 48
VMEM_RESIDENT_MIB = 56

ROWS_EVEN_IN = 512
ROWS_FUSED_OUT = 512
FFN_SPLITS = 1
ROWS_ODD_IN = 32 * GRID_W
ROWS_ODD_OUT = 8 * GRID_W
S5_INJECT_SUBCHUNKS = 1024
S5_READOUT_SUBCHUNKS = 256
S5_CARRY_SUBCHUNKS = 256


def _cparams(semantics, vmem_mib):
    return pltpu.CompilerParams(dimension_semantics=semantics, vmem_limit_bytes=vmem_mib * 1024 * 1024)


def _silu(x):
    return x * jax.nn.sigmoid(x)


def _modulate(h, mod_ref):
    ms = jnp.mean(h * h, axis=-1, keepdims=True)
    return h * lax.rsqrt(ms + EPS) * (mod_ref[0:1, :] * (1.0 + mod_ref[1:2, :])) + mod_ref[2:3, :]


def _split_dot(x, w, passes):
    acc, rem = None, x
    for _ in range(passes):
        hi = rem.astype(BF16)
        d = jnp.dot(hi, w, preferred_element_type=F32)
        acc = d if acc is None else acc + d
        rem = rem - hi.astype(F32)
    return acc


def _chunk_cumsum(x, reverse):
    n, w = x.shape
    tiles = x.reshape(n // SUBLANES, SUBLANES, w)
    sub = lax.broadcasted_iota(jnp.int32, (1, SUBLANES, 1), 1)
    s = 1
    while s < SUBLANES:
        if reverse:
            tiles = tiles + jnp.where(sub < SUBLANES - s, pltpu.roll(tiles, SUBLANES - s, axis=1), 0.0)
        else:
            tiles = tiles + jnp.where(sub >= s, pltpu.roll(tiles, s, axis=1), 0.0)
        s *= 2
    per_chunk = SCAN_CHUNK // SUBLANES
    edge = 0 if reverse else SUBLANES - 1
    out = [None] * (n // SUBLANES)
    for c in range(n // SCAN_CHUNK):
        run = None
        order = range(per_chunk - 1, -1, -1) if reverse else range(per_chunk)
        for k in order:
            tile = tiles[c * per_chunk + k]
            out[c * per_chunk + k] = tile if run is None else tile + run
            total = tile[edge:edge + 1, :]
            run = total if run is None else run + total
    return jnp.concatenate(out, axis=0)


def _ada_kernel(c_ref, w_ref, b_ref, o_ref):
    o_ref[...] = jnp.dot(_silu(c_ref[...]), w_ref[...], precision=lax.Precision.HIGHEST,
                         preferred_element_type=F32) + b_ref[...]


def _ada(cvecs, ada_w, ada_b):
    depth, d, n = ada_w.shape
    tn = n // 4
    return pl.pallas_call(
        _ada_kernel,
        grid=(depth, n // tn),
        in_specs=[pl.BlockSpec((SUBLANES, d), lambda l, j: (0, 0)),
                  pl.BlockSpec((None, d, tn), lambda l, j: (l, 0, j)),
                  pl.BlockSpec((None, 1, tn), lambda l, j: (l, 0, j))],
        out_specs=pl.BlockSpec((None, SUBLANES, tn), lambda l, j: (l, 0, j)),
        out_shape=jax.ShapeDtypeStruct((depth, SUBLANES, n), F32),
        compiler_params=_cparams(("arbitrary", "arbitrary"), VMEM_STREAMING_MIB),
        name="ada",
    )(cvecs, ada_w, ada_b.reshape(depth, 1, n))


EVEN_TN = 1024
HALO = 2 * SUBLANES
EVEN_TILES = (("conv", 0, None), ("conv_dt", 1, None), ("silu", None, 0), ("copy", None, 1),
              ("gate", 2, 4), ("gate", 3, 5), ("silu", None, 2), ("silu", None, 3))


def _even_in_kernel(h_ref, hp_ref, hn_ref, mod_ref, w_ref, cw_ref, cb_ref, dtc_ref, lb_ref, za_ref, zb_ref, a_ref):
    i = pl.program_id(0)
    tm = h_ref.shape[0]
    n_lb = lb_ref.shape[1]
    a_ref[:HALO] = _modulate(hp_ref[...], mod_ref).astype(BF16)
    a_ref[HALO:HALO + tm] = _modulate(h_ref[...], mod_ref).astype(BF16)
    a_ref[HALO + tm:] = _modulate(hn_ref[...], mod_ref).astype(BF16)
    row = lax.broadcasted_iota(jnp.int32, (tm + 2 * HALO, 1), 0)
    outside = ((row < HALO) & (i == 0)) | ((row >= HALO + tm) & (i == pl.num_programs(0) - 1))

    def conv_silu(col0, lanes):
        acc = jnp.dot(a_ref[...], w_ref[:, col0:col0 + lanes], preferred_element_type=F32)
        acc = jnp.where(outside, 0.0, acc)
        conv = cb_ref[:, col0:col0 + lanes]
        for k in range(SSD_CONV):
            off = HALO - SSD_CONV // 2 + k
            conv = conv + cw_ref[k:k + 1, col0:col0 + lanes] * acc[off:off + tm]
        return _silu(conv)

    def main_dot(col0, lanes=EVEN_TN):
        return jnp.dot(a_ref[HALO:HALO + tm], w_ref[:, col0:col0 + lanes], preferred_element_type=F32)

    for tile, (kind, fa, fb) in enumerate(EVEN_TILES):
        col0 = tile * EVEN_TN
        if kind == "conv":
            za_ref[:, fa * EVEN_TN:(fa + 1) * EVEN_TN] = conv_silu(col0, EVEN_TN)
        elif kind == "conv_dt":
            n_c = ZS_DT - col0
            za_ref[:, col0:ZS_DT] = conv_silu(col0, n_c)
            dt = main_dot(ZS_DT, 2 * LANES)
            for d in range(2):
                za_ref[:, ZS_DT + d * LANES:ZS_DT + (d + 1) * LANES] = jax.nn.softplus(
                    dt[:, d * LANES:(d + 1) * LANES] + dtc_ref[d, 0:1, :])
            za_ref[:, ZS_XD:(fa + 1) * EVEN_TN] = jnp.zeros((tm, (fa + 1) * EVEN_TN - ZS_XD), F32)
        elif kind == "gate":
            lb = lb_ref[:, (col0 - ZS_F) % n_lb:(col0 - ZS_F) % n_lb + EVEN_TN]
            f = lb + (1.0 - lb) * jax.nn.sigmoid(main_dot(col0))
            backward = (col0 - ZS_F) // n_lb == 1
            za_ref[:, fa * EVEN_TN:(fa + 1) * EVEN_TN] = _chunk_cumsum(jnp.log(f), backward)
            zb_ref[:, fb * EVEN_TN:(fb + 1) * EVEN_TN] = (1.0 - f).astype(BF16)
        elif kind == "silu":
            zb_ref[:, fb * EVEN_TN:(fb + 1) * EVEN_TN] = _silu(main_dot(col0)).astype(BF16)
        else:
            zb_ref[:, fb * EVEN_TN:(fb + 1) * EVEN_TN] = main_dot(col0).astype(BF16)


def _even_in(h, mod, w, cw, cb, dtc, lb):
    t, d = h.shape
    tm = min(t, ROWS_EVEN_IN)
    hb = tm // HALO
    n_a = (1 + max(fa for _, fa, _ in EVEN_TILES if fa is not None)) * EVEN_TN
    n_b = (1 + max(fb for _, _, fb in EVEN_TILES if fb is not None)) * EVEN_TN
    whole = lambda a, **kw: pl.BlockSpec(a.shape, lambda i: (0,) * a.ndim, **kw)
    return pl.pallas_call(
        _even_in_kernel,
        grid=(t // tm,),
        in_specs=[pl.BlockSpec((tm, d), lambda i: (i, 0)),
                  pl.BlockSpec((HALO, d), lambda i: (jnp.maximum(i * hb - 1, 0), 0)),
                  pl.BlockSpec((HALO, d), lambda i: (jnp.minimum((i + 1) * hb, t // HALO - 1), 0)),
                  whole(mod), whole(w, pipeline_mode=pl.Buffered(1)),
                  whole(cw), whole(cb), whole(dtc), whole(lb)],
        out_specs=[pl.BlockSpec((tm, n_a), lambda i: (i, 0)), pl.BlockSpec((tm, n_b), lambda i: (i, 0))],
        out_shape=[jax.ShapeDtypeStruct((t, n_a), F32), jax.ShapeDtypeStruct((t, n_b), BF16)],
        scratch_shapes=[pltpu.VMEM((tm + 2 * HALO, d), BF16)],
        compiler_params=_cparams(("arbitrary",), VMEM_RESIDENT_MIB),
        name="even_in",
    )(h, h, h, mod, w, cw, cb, dtc, lb)


def _scan_block_prep(xd_ref, qv_ref, bcum_ref, kk_ref, dtc_ref, e_ref, d):
    Q = SCAN_CHUNK
    reverse = bool(d)
    dtv = xd_ref[:, ZS_DT + d * LANES:ZS_DT + (d + 1) * LANES]
    acum = _chunk_cumsum(-dtv * jnp.exp(dtc_ref[d, 1:2, :]), reverse)
    rows = acum.shape[0]
    nchunk = rows // Q
    edge = 0 if reverse else Q - 1
    lasts = [acum[c * Q + edge:c * Q + edge + 1, :] for c in range(nchunk)]
    last_rows = jnp.concatenate([jnp.broadcast_to(l, (Q, LANES)) for l in lasts], axis=0)
    e_last = jnp.concatenate([jnp.broadcast_to(jnp.exp(l), (SUBLANES, LANES)) for l in lasts], axis=0)
    terms, rem = [], e_last
    for _ in range(3):
        hi = rem.astype(BF16)
        terms.append(hi)
        rem = rem - hi.astype(F32)
    lhs = jnp.concatenate([dtv.astype(BF16), jnp.exp(acum).astype(BF16), jnp.exp(last_rows - acum).astype(BF16)]
                          + terms, axis=0)
    xe = jnp.dot(lhs, e_ref[...], preferred_element_type=F32)
    n8 = SUBLANES * nchunk
    el = xe[3 * rows:]
    expanded = (xe[:rows], xe[rows:2 * rows], xe[2 * rows:3 * rows], el[:n8] + el[n8:2 * n8] + el[2 * n8:])
    return xd_ref, qv_ref, bcum_ref, kk_ref, acum, expanded


DECAY_LIMIT = 80.0


def _hg_intra_exact(qs, bcum, bcum_ref, kk_ref, v_ref, acc_ref, row0, reverse):
    Q = SCAN_CHUNK
    acc_ref[...] = jnp.zeros_like(acc_ref)
    t_idx = lax.broadcasted_iota(jnp.int32, (Q, 1), 0)

    def body(s, carry):
        r = row0 + s
        b_s, k_s, v_s = bcum_ref[pl.ds(r, 1), :], kk_ref[pl.ds(r, 1), :], v_ref[pl.ds(r, 1), :]
        allowed = (t_idx <= s) if reverse else (t_idx >= s)
        w = jnp.where(allowed, jnp.exp(jnp.minimum(bcum - b_s, 0.0)) * (qs * k_s), 0.0)
        for h in range(HG_HEADS):
            sl = slice(h * HG_HEAD_DIM, (h + 1) * HG_HEAD_DIM)
            acc_ref[:, sl] += jnp.sum(w[:, sl], axis=-1, keepdims=True) * v_s[:, sl]
        return carry

    lax.fori_loop(0, Q, body, 0)
    return acc_ref[...]


def _scan_chunk(prep, rows, dsk_ref, st_ref, gt_ref, yo_ref, d, reverse, exact=None):
    Q = SCAN_CHUNK
    xd_ref, qv_ref, bcum_ref, kk_ref, acum, expanded = prep
    n_hg = bcum_ref.shape[1]
    ti = lax.broadcasted_iota(jnp.int32, (Q, Q), 0)
    si = lax.broadcasted_iota(jnp.int32, (Q, Q), 1)
    mask = (ti <= si) if reverse else (ti >= si)
    edge = 0 if reverse else Q - 1
    width = SSD_HEADS * SSD_HEAD_DIM
    ci = rows.start // Q
    dt_x, ea_x, wend_x, elast_x = expanded
    chunk_x = (dt_x[rows], ea_x[rows], wend_x[rows], elast_x[ci * SUBLANES:ci * SUBLANES + 1])
    _scan_chunk_ssd(xd_ref, chunk_x, acum[rows], rows, dsk_ref, st_ref, yo_ref, d, reverse, mask)
    if exact is not None:
        intra = _hg_intra_exact(qv_ref[rows, :n_hg].astype(F32), bcum_ref[rows, :], *exact, rows.start, reverse)
    hw = n_hg // 2
    for part in range(2):
        cols = slice(part * hw, (part + 1) * hw)
        kk, bcum = kk_ref[rows, cols].astype(F32), bcum_ref[rows, cols]
        qs = qv_ref[rows, cols].astype(F32)
        vb = qv_ref[rows, n_hg + part * hw:n_hg + (part + 1) * hw]
        blast = bcum[edge:edge + 1, :]
        qe = (qs * jnp.exp(bcum)).astype(BF16)
        k_end = kk * jnp.exp(blast - bcum)
        if exact is None:
            k_til = (k_end * jnp.exp(-blast)).astype(BF16)
        k_end = k_end.astype(BF16)
        e_blast = jnp.exp(blast)
        outs = []
        for hl in range(hw // HG_HEAD_DIM):
            h = part * (hw // HG_HEAD_DIM) + hl
            sl = slice(hl * HG_HEAD_DIM, (hl + 1) * HG_HEAD_DIM)
            gt = gt_ref[d, h]
            if exact is None:
                att = lax.dot_general(qe[:, sl], k_til[:, sl], (((1,), (1,)), ((), ())), preferred_element_type=F32)
                att = jnp.where(mask, att, 0.0).astype(BF16)
                o_h = jnp.dot(att, vb[:, sl], preferred_element_type=F32)
            else:
                o_h = intra[:, h * HG_HEAD_DIM:(h + 1) * HG_HEAD_DIM]
            o_h = o_h + lax.dot_general(qe[:, sl], gt.astype(BF16), (((1,), (1,)), ((), ())),
                                        preferred_element_type=F32)
            outs.append(o_h)
            upd = lax.dot_general(vb[:, sl], k_end[:, sl], (((0,), (0,)), ((), ())), preferred_element_type=F32)
            gt_ref[d, h] = gt * e_blast[:, sl] + upd
        yo_ref[rows, width + part * hw:width + (part + 1) * hw] = jnp.concatenate(outs, axis=1).astype(yo_ref.dtype)


def _scan_chunk_ssd(xd_ref, chunk_x, acum, rows, dsk_ref, st_ref, yo_ref, d, reverse, mask):
    Q = SCAN_CHUNK
    width = SSD_HEADS * SSD_HEAD_DIM
    gw = width // SSD_GROUPS
    dt_all, ea_all, wend_all, elast_all = chunk_x
    acum_t = acum.T
    lane = lax.broadcasted_iota(jnp.int32, (Q, LANES), 1)
    lo = lane < SSD_HEAD_DIM
    zero_b = jnp.zeros((Q, LANES), BF16)
    for g in range(SSD_GROUPS):
        cols = slice(g * gw, (g + 1) * gw)
        dt_x, ea_x, wend_x, elast_x = dt_all[:, cols], ea_all[:, cols], wend_all[:, cols], elast_all[:, cols]
        xs = xd_ref[rows, cols]
        xdt = xs * dt_x
        xw = (xdt * wend_x).astype(BF16)
        xdt_b = xdt.astype(BF16)
        bm = xd_ref[rows, width + g * SSD_STATE:width + (g + 1) * SSD_STATE].astype(BF16)
        cm = xd_ref[rows, width + (SSD_GROUPS + g) * SSD_STATE:width + (SSD_GROUPS + g + 1) * SSD_STATE].astype(BF16)
        cb = lax.dot_general(cm, bm, (((1,), (1,)), ((), ())), preferred_element_type=F32)
        y_state = jnp.dot(cm, st_ref[d, g].astype(BF16), preferred_element_type=F32)
        pairs = []
        for hp in range(gw // LANES):
            xp = xdt_b[:, hp * LANES:(hp + 1) * LANES]
            acc = None
            for half in range(2):
                h = (g * gw + hp * LANES) // SSD_HEAD_DIM + half
                seg = acum[:, h:h + 1] - acum_t[h:h + 1, :]
                m = jnp.where(mask, cb * jnp.exp(seg), 0.0).astype(BF16)
                xh = jnp.where(lo, xp, zero_b) if half == 0 else jnp.where(lo, zero_b, xp)
                part = jnp.dot(m, xh, preferred_element_type=F32)
                acc = part if acc is None else acc + part
            pairs.append(acc)
        y = jnp.concatenate(pairs, axis=1) + y_state * ea_x
        if not reverse:
            y = y + dsk_ref[:, cols] * xs
        yo_ref[rows, cols] = y.astype(yo_ref.dtype)
        upd = lax.dot_general(bm, xw, (((0,), (0,)), ((), ())), preferred_element_type=F32)
        st_ref[d, g] = st_ref[d, g] * elast_x + upd


def _even_scan_kernel(*refs, nblk, cpb):
    fwd_in, bwd_in = refs[0:4], refs[4:8]
    dtc_ref, dsk_ref, e_ref, h0s_ref, h0g_ref = refs[8:13]
    yof_ref, yob_ref, st_ref, gt_ref, kkf_ref, vf_ref, acc_ref = refs[13:20]
    c = pl.program_id(0)

    @pl.when(c == 0)
    def _():
        st_ref[...] = h0s_ref[...]
        gt_ref[...] = h0g_ref[...]

    streams = ((False, fwd_in, yof_ref), (True, bwd_in, yob_ref))

    def run(exact):
        preps = [_scan_block_prep(*ins, dtc_ref, e_ref, d) for d, (_, ins, _) in enumerate(streams)]
        if exact:
            for d, (_, ins, _) in enumerate(streams):
                kkf_ref[d] = ins[3][...].astype(F32)
                vf_ref[d] = ins[1][:, ins[2].shape[1]:].astype(F32)
        for step in range(cpb):
            for d, (reverse, ins, yo_ref) in enumerate(streams):
                ci = (cpb - 1 - step) if reverse else step
                rows = slice(ci * SCAN_CHUNK, (ci + 1) * SCAN_CHUNK)
                refs_exact = (ins[2], kkf_ref.at[d], vf_ref.at[d], acc_ref) if exact else None
                _scan_chunk(preps[d], rows, dsk_ref, st_ref, gt_ref, yo_ref, d, reverse, refs_exact)

    totals = [ref[ci * SCAN_CHUNK + edge:ci * SCAN_CHUNK + edge + 1, :]
              for ref, edge in ((fwd_in[2], SCAN_CHUNK - 1), (bwd_in[2], 0)) for ci in range(cpb)]
    strongest = jnp.min(jnp.concatenate(totals, axis=0))
    pl.when(strongest >= -DECAY_LIMIT)(lambda: run(False))
    pl.when(strongest < -DECAY_LIMIT)(lambda: run(True))


def _even_scan(za, zb, consts, h0s, h0g):
    t = za.shape[0]
    nch = t // SCAN_CHUNK
    cpb = max(k for k in (4, 2, 1) if nch % k == 0)
    rows = cpb * SCAN_CHUNK
    nblk = nch // cpb
    dtc, dsk, e16 = consts
    n_x = SSD_HEADS * SSD_HEAD_DIM
    n_hg = HG_HEADS * HG_HEAD_DIM
    gate_a, gate_b = (tuple(tile[i] for tile in EVEN_TILES if tile[0] == "gate") for i in (1, 2))

    def stream_specs(bidx, d):
        return [
            pl.BlockSpec((rows, ZS_XD), lambda c: (bidx(c), 0)),
            pl.BlockSpec((rows, 2 * n_hg), lambda c: (bidx(c), 0)),
            pl.BlockSpec((rows, n_hg), lambda c: (bidx(c), gate_a[d])),
            pl.BlockSpec((rows, n_hg), lambda c: (bidx(c), gate_b[d])),
        ]

    fwd = lambda c: c
    bwd = lambda c: nblk - 1 - c
    whole = lambda a: pl.BlockSpec(a.shape, lambda c: (0,) * a.ndim)
    in_specs = stream_specs(fwd, 0) + stream_specs(bwd, 1) + [whole(a) for a in (dtc, dsk, e16, h0s, h0g)]
    out_specs = [pl.BlockSpec((rows, n_x + n_hg), lambda c: (c, 0)),
                 pl.BlockSpec((rows, n_x + n_hg), lambda c: (bwd(c), 0)), whole(h0s), whole(h0g)]
    out_shape = [jax.ShapeDtypeStruct((t, n_x + n_hg), BF16), jax.ShapeDtypeStruct((t, n_x + n_hg), BF16),
                 jax.ShapeDtypeStruct(h0s.shape, F32), jax.ShapeDtypeStruct(h0g.shape, F32)]
    return pl.pallas_call(
        functools.partial(_even_scan_kernel, nblk=nblk, cpb=cpb),
        grid=(nblk,),
        in_specs=in_specs,
        out_specs=out_specs,
        out_shape=out_shape,
        scratch_shapes=[pltpu.VMEM((2, rows, n_hg), F32), pltpu.VMEM((2, rows, n_hg), F32),
                        pltpu.VMEM((SCAN_CHUNK, n_hg), F32)],
        compiler_params=_cparams(("arbitrary",), VMEM_MATRICES_MIB),
        name="even_scan",
    )(za, zb, za, zb, za, zb, za, zb, dtc, dsk, e16, h0s, h0g)


def _ffn_block(h, mod_ref, wg_ref, wu_ref, wd_ref, final):
    a = _modulate(h, mod_ref).astype(BF16)
    f = wg_ref.shape[1]
    acc = None
    for k in range(FFN_SPLITS):
        cols = slice(k * f // FFN_SPLITS, (k + 1) * f // FFN_SPLITS)
        gate = jnp.dot(a, wg_ref[:, cols], preferred_element_type=F32)
        up = jnp.dot(a, wu_ref[:, cols], preferred_element_type=F32)
        act = (_silu(gate) * up).astype(BF16)
        part = jnp.dot(act, wd_ref[cols, :], preferred_element_type=F32)
        acc = part if acc is None else acc + part
    out = h + mod_ref[3:4, :] * acc
    if final:
        ms = jnp.mean(out * out, axis=-1, keepdims=True)
        out = out * lax.rsqrt(ms + EPS) * mod_ref[4:5, :]
    return out


def _resident(a):
    return pl.BlockSpec(a.shape, lambda i: (0,) * a.ndim, pipeline_mode=pl.Buffered(1))


def _even_out_kernel(h_ref, z_ref, g_ref, yof_ref, yob_ref, w_ref, vec_ref, mod_ref, wg_ref, wu_ref, wd_ref, o_ref):
    n_x = z_ref.shape[1]
    yo = yof_ref[...].astype(F32) + yob_ref[...].astype(F32)
    y = yo[:, :n_x] * z_ref[...].astype(F32)
    o = yo[:, n_x:]
    gw = n_x // SSD_GROUPS
    parts = []
    for g in range(SSD_GROUPS):
        yg = y[:, g * gw:(g + 1) * gw]
        ms = jnp.mean(yg * yg, axis=-1, keepdims=True)
        parts.append(yg * lax.rsqrt(ms + EPS))
    yn = (jnp.concatenate(parts, axis=1) * vec_ref[0:1, :]).astype(BF16)
    parts = []
    for hh in range(HG_HEADS):
        oh = o[:, hh * HG_HEAD_DIM:(hh + 1) * HG_HEAD_DIM]
        ms = jnp.mean(oh * oh, axis=-1, keepdims=True)
        parts.append(oh * lax.rsqrt(ms + EPS))
    on = (jnp.concatenate(parts, axis=1) * vec_ref[1:2, :] * g_ref[...].astype(F32)).astype(BF16)
    half = yn.shape[1]
    mix = jnp.dot(yn, w_ref[:half, :], preferred_element_type=F32)
    mix = mix + jnp.dot(on, w_ref[half:, :], preferred_element_type=F32)
    h1 = h_ref[...] + vec_ref[2:3, :] * mix
    o_ref[...] = _ffn_block(h1, mod_ref, wg_ref, wu_ref, wd_ref, final=False)


def _even_out(h, zb, yo_f, yo_b, w_out, vec, mod, wg, wu, wd):
    t, d = h.shape
    tm = min(t, ROWS_FUSED_OUT)
    n_x = SSD_HEADS * SSD_HEAD_DIM
    n_hg = HG_HEADS * HG_HEAD_DIM
    row = lambda i: (i, 0)
    z_tile, g_tile = (next(fb for k, (_, _, fb) in enumerate(EVEN_TILES) if k * EVEN_TN == start)
                      for start in (ZS_Z, ZS_G))
    return pl.pallas_call(
        _even_out_kernel,
        grid=(t // tm,),
        in_specs=[pl.BlockSpec((tm, d), row),
                  pl.BlockSpec((tm, n_x), lambda i: (i, z_tile)),
                  pl.BlockSpec((tm, n_hg), lambda i: (i, g_tile)),
                  pl.BlockSpec((tm, n_x + n_hg), row), pl.BlockSpec((tm, n_x + n_hg), row),
                  _resident(w_out), _resident(vec), _resident(mod), _resident(wg), _resident(wu), _resident(wd)],
        out_specs=pl.BlockSpec((tm, d), row),
        out_shape=jax.ShapeDtypeStruct((t, d), F32),
        compiler_params=_cparams(("arbitrary",), VMEM_RESIDENT_MIB),
        name="even_out_ffn",
    )(h, zb, zb, yo_f, yo_b, w_out, vec, mod, wg, wu, wd)


def _odd_in_kernel(h_ref, mod_ref, w_ref, o_ref, u_ref):
    a = _modulate(h_ref[...], mod_ref).astype(BF16)
    u = jnp.dot(a, w_ref[...], preferred_element_type=F32)
    nj, grid_w, rb, _ = o_ref.shape
    for j in range(nj):
        u_ref[j] = u[:, j * LANES:(j + 1) * LANES]
    for wi in range(grid_w):
        for j in range(nj):
            o_ref[j, wi] = u_ref[j, pl.ds(wi, rb, stride=grid_w), :]


def _odd_in(h, mod, w, grid_w):
    t, d = h.shape
    rows = t // grid_w
    rb = min(rows, ROWS_ODD_IN // GRID_W)
    n = w.shape[1]
    nj = n // LANES
    out = pl.pallas_call(
        _odd_in_kernel,
        grid=(rows // rb,),
        in_specs=[pl.BlockSpec((rb * grid_w, d), lambda i: (i, 0)),
                  pl.BlockSpec((SUBLANES, d), lambda i: (0, 0)),
                  pl.BlockSpec(w.shape, lambda i: (0, 0))],
        out_specs=pl.BlockSpec((nj, grid_w, rb, LANES), lambda i: (0, 0, i, 0)),
        out_shape=jax.ShapeDtypeStruct((nj, grid_w, rows, LANES), F32),
        scratch_shapes=[pltpu.VMEM((nj, rb * grid_w, LANES), F32)],
        compiler_params=_cparams(("arbitrary",), VMEM_MATRICES_MIB),
        name="odd_in",
    )(h, mod, w)
    return out.reshape(nj, t, LANES)


def _sub_chunk_rows(u_ref, nb):
    return [u_ref[pl.ds(s, nb, stride=S5_SUB), :] for s in range(S5_SUB)]


def _s5_inject_kernel(u_ref, win_ref, s_ref):
    x = jnp.concatenate(_sub_chunk_rows(u_ref, s_ref.shape[0]), axis=1).astype(BF16)
    s_ref[...] = jnp.dot(x, win_ref[...], preferred_element_type=F32)


def _s5_inject(u, win):
    nj, t, _ = u.shape
    n = t // S5_SUB
    nb = min(n, S5_INJECT_SUBCHUNKS)
    width, ns = win.shape[1:]
    return pl.pallas_call(
        _s5_inject_kernel,
        grid=(nj, n // nb),
        in_specs=[pl.BlockSpec((None, nb * S5_SUB, LANES), lambda j, b: (j, b, 0)),
                  pl.BlockSpec((None, width, ns), lambda j, b: (j, 0, 0))],
        out_specs=pl.BlockSpec((None, nb, ns), lambda j, b: (j, b, 0)),
        out_shape=jax.ShapeDtypeStruct((nj, n, ns), F32),
        compiler_params=_cparams(("arbitrary", "arbitrary"), VMEM_STREAMING_MIB),
        name="s5_inject",
    )(u, win)


def _s5_carry_kernel(sf_ref, sb_ref, lam_ref, h0_ref, pf_ref, pb_ref, hf_ref, tf_ref, tb_ref, c_ref):
    b = pl.program_id(0)
    nj, kb, ns = sf_ref.shape
    half = ns // 2

    @pl.when(b == 0)
    def _():
        c_ref[...] = h0_ref[...]

    for j in range(nj):
        tf_ref[:, j, :] = sf_ref[j]
        tb_ref[:, j, :] = sb_ref[j]

    def advance(d, c, s):
        lr, li = lam_ref[d, :, :half], lam_ref[d, :, half:]
        pr, pi = c[:, :half], c[:, half:]
        return jnp.concatenate([lr * pr - li * pi + s[:, :half], lr * pi + li * pr + s[:, half:]], axis=1)

    def step(i, carry):
        cf, cb = carry
        kr = kb - 1 - i
        sf = tf_ref[i]
        sb = tb_ref[kr]
        tf_ref[i] = cf
        tb_ref[kr] = cb
        return advance(0, cf, sf), advance(1, cb, sb)

    cf, cb = lax.fori_loop(0, kb, step, (c_ref[0], c_ref[1]), unroll=4)
    c_ref[0] = cf
    c_ref[1] = cb
    for j in range(nj):
        pf_ref[j] = tf_ref[:, j, :]
        pb_ref[j] = tb_ref[:, j, :]

    @pl.when(b == pl.num_programs(0) - 1)
    def _():
        hf_ref[...] = c_ref[...]


def _s5_carry(s_f, s_b, lam, h0):
    nj, n, ns = s_f.shape
    kb = min(n, S5_CARRY_SUBCHUNKS)
    nblk = n // kb
    fwd = pl.BlockSpec((nj, kb, ns), lambda b: (0, b, 0))
    bwd = pl.BlockSpec((nj, kb, ns), lambda b: (0, nblk - 1 - b, 0))
    whole = pl.BlockSpec(lam.shape, lambda b: (0, 0, 0))
    return pl.pallas_call(
        _s5_carry_kernel,
        grid=(nblk,),
        in_specs=[fwd, bwd, whole, whole],
        out_specs=[fwd, bwd, whole],
        out_shape=[jax.ShapeDtypeStruct(s_f.shape, F32), jax.ShapeDtypeStruct(s_b.shape, F32),
                   jax.ShapeDtypeStruct(lam.shape, F32)],
        scratch_shapes=[pltpu.VMEM((kb, nj, ns), F32), pltpu.VMEM((kb, nj, ns), F32), pltpu.VMEM(lam.shape, F32)],
        compiler_params=_cparams(("arbitrary",), VMEM_MATRICES_MIB),
        name="s5_carry",
    )(s_f, s_b, lam, h0)


def _s5_readout_kernel(u_ref, p_ref, wout_ref, tt_ref, add_ref, y_ref, *, reverse):
    nb = p_ref.shape[0]
    us = _sub_chunk_rows(u_ref, nb)
    xb = jnp.concatenate(us, axis=1).astype(BF16)
    pb = p_ref[...].astype(BF16)
    width = xb.shape[1]
    tile = 2 * LANES
    for m in range(width // tile):
        cols = slice(m * tile, (m + 1) * tile)
        rows = slice(m * tile, width) if reverse else slice(0, (m + 1) * tile)
        y = jnp.dot(pb, wout_ref[:, cols], preferred_element_type=F32)
        y = y + jnp.dot(xb[:, rows], tt_ref[rows, cols], preferred_element_type=F32)
        for q in range(tile // LANES):
            s = m * (tile // LANES) + q
            other = add_ref[pl.ds(s, nb, stride=S5_SUB), :] if reverse else add_ref[...] * us[s]
            y_ref[pl.ds(s, nb, stride=S5_SUB), :] = y[:, q * LANES:(q + 1) * LANES] + other


def _s5_readout(u, p, wout, tt, add, reverse):
    nj, t, _ = u.shape
    n = t // S5_SUB
    nb = min(n, S5_READOUT_SUBCHUNKS)
    ns, width = wout.shape[1:]
    tokens = pl.BlockSpec((None, nb * S5_SUB, LANES), lambda j, b: (j, b, 0))
    add_spec = tokens if reverse else pl.BlockSpec((None, 1, LANES), lambda j, b: (j, 0, 0))
    return pl.pallas_call(
        functools.partial(_s5_readout_kernel, reverse=reverse),
        grid=(nj, n // nb),
        in_specs=[tokens,
                  pl.BlockSpec((None, nb, ns), lambda j, b: (j, b, 0)),
                  pl.BlockSpec((None, ns, width), lambda j, b: (j, 0, 0)),
                  pl.BlockSpec((None, width, width), lambda j, b: (j, 0, 0)),
                  add_spec],
        out_specs=tokens,
        out_shape=jax.ShapeDtypeStruct(u.shape, F32),
        compiler_params=_cparams(("arbitrary", "arbitrary"), VMEM_MATRICES_MIB),
        name="s5_readout_bwd" if reverse else "s5_readout_fwd",
    )(u, p, wout, tt, add)


def _odd_out_kernel(h_ref, y_ref, wv_ref, wgl_ref, vec_ref, mod_ref, wg_ref, wu_ref, wd_ref, o_ref, a_ref):
    nj, grid_w, rb, _ = y_ref.shape
    for wi in range(grid_w):
        for j in range(nj):
            a_ref[j, pl.ds(wi, rb, stride=grid_w), :] = y_ref[j, wi]
    a = jax.nn.gelu(jnp.concatenate([a_ref[j] for j in range(nj)], axis=1)).astype(BF16)
    val = jnp.dot(a, wv_ref[...], preferred_element_type=F32)
    gate = jnp.dot(a, wgl_ref[...], preferred_element_type=F32)
    h3 = h_ref[...] + vec_ref[0:1, :] * (val * jax.nn.sigmoid(gate))
    o_ref[...] = _ffn_block(h3, mod_ref, wg_ref, wu_ref, wd_ref, final=True)


def _odd_out(h, y, w_val, w_gate, vec, mod, wg, wu, wd, grid_w):
    t, d = h.shape
    rows = t // grid_w
    rb = min(rows, ROWS_ODD_OUT // GRID_W)
    nj = y.shape[0]
    tok = pl.BlockSpec((rb * grid_w, d), lambda i: (i, 0))
    return pl.pallas_call(
        _odd_out_kernel,
        grid=(rows // rb,),
        in_specs=[tok, pl.BlockSpec((nj, grid_w, rb, LANES), lambda i: (0, 0, i, 0)),
                  _resident(w_val), _resident(w_gate), _resident(vec), _resident(mod),
                  _resident(wg), _resident(wu), _resident(wd)],
        out_specs=tok,
        out_shape=jax.ShapeDtypeStruct((t, d), F32),
        scratch_shapes=[pltpu.VMEM((nj, rb * grid_w, LANES), F32)],
        compiler_params=_cparams(("arbitrary",), VMEM_RESIDENT_MIB),
        name="odd_out_ffn",
    )(h, y.reshape(nj, grid_w, rows, LANES), w_val, w_gate, vec, mod, wg, wu, wd)


def _s5_prep_kernel(prm_ref, bre_ref, bim_ref, colp_ref, cre_ref, cim_ref, win_ref, wout_ref, tt_ref, lam_ref, *,
                    reverse):
    T = S5_SUB
    ns = prm_ref.shape[1]
    a_re, a_im, dt = prm_ref[0:1, :], prm_ref[1:2, :], jnp.exp(prm_ref[2:3, :])
    tau = lax.broadcasted_iota(jnp.int32, (3 * SUBLANES, ns), 0).astype(F32)
    mag = jnp.exp(a_re * dt * tau)
    pr = mag * jnp.cos(a_im * dt * tau)
    pi = mag * jnp.sin(a_im * dt * tau)
    lr, li = pr[1:2], pi[1:2]
    den = a_re * a_re + a_im * a_im
    cr = ((lr - 1.0) * a_re + li * a_im) / den
    ci = (li * a_re - (lr - 1.0) * a_im) / den
    row_g = lax.broadcasted_iota(jnp.int32, (LANES, ns), 0) // S5_GROUP
    col_g = lax.broadcasted_iota(jnp.int32, (LANES, ns), 1) // S5_STATE
    same = row_g == col_g
    bbr = jnp.where(same, cr * bre_ref[...] - ci * bim_ref[...], 0.0)
    bbi = jnp.where(same, cr * bim_ref[...] + ci * bre_ref[...], 0.0)
    for s in range(T):
        e = s if reverse else T - 1 - s
        rows = slice(s * LANES, (s + 1) * LANES)
        win_ref[rows, :ns] = (pr[e:e + 1] * bbr - pi[e:e + 1] * bbi).astype(BF16)
        win_ref[rows, ns:] = (pr[e:e + 1] * bbi + pi[e:e + 1] * bbr).astype(BF16)
    lam_ref[...] = jnp.broadcast_to(jnp.concatenate([pr[T:T + 1], pi[T:T + 1]], axis=1), lam_ref.shape)
    dt_c = jnp.exp(colp_ref[2])
    mag_c = jnp.exp(colp_ref[0] * dt_c)
    lr_c = mag_c * jnp.cos(colp_ref[1] * dt_c)
    li_c = mag_c * jnp.sin(colp_ref[1] * dt_c)
    row_gc = lax.broadcasted_iota(jnp.int32, (ns, LANES), 0) // S5_STATE
    col_gc = lax.broadcasted_iota(jnp.int32, (ns, LANES), 1) // S5_GROUP
    same_c = row_gc == col_gc
    c_re = jnp.where(same_c, cre_ref[...], 0.0)
    c_im = jnp.where(same_c, cim_ref[...], 0.0)
    bb = jnp.concatenate([bbr, bbi], axis=1)
    bb_hi = bb.astype(BF16)
    bb_lo = (bb - bb_hi.astype(F32)).astype(BF16)
    cur_r = jnp.ones((ns, LANES), F32)
    cur_i = jnp.zeros((ns, LANES), F32)
    kblk = []
    for e in range(T + 1):
        wo_r = c_re * cur_r - c_im * cur_i
        wo_i = -(c_re * cur_i + c_im * cur_r)
        if e >= 1:
            t = (T - e) if reverse else e - 1
            wout_ref[:ns, t * LANES:(t + 1) * LANES] = wo_r.astype(BF16)
            wout_ref[ns:, t * LANES:(t + 1) * LANES] = wo_i.astype(BF16)
        if e < T:
            wo = jnp.concatenate([wo_r, wo_i], axis=0)
            wo_hi = wo.astype(BF16)
            kblk.append((wo_hi, (wo - wo_hi.astype(F32)).astype(BF16)))
        cur_r, cur_i = cur_r * lr_c - cur_i * li_c, cur_r * li_c + cur_i * lr_c
    wo_hi = jnp.concatenate([hi for hi, _ in kblk], axis=1)
    wo_lo = jnp.concatenate([lo for _, lo in kblk], axis=1)
    k_all = jnp.dot(bb_hi, wo_hi, preferred_element_type=F32)
    k_all = k_all + jnp.dot(bb_hi, wo_lo, preferred_element_type=F32) + jnp.dot(bb_lo, wo_hi, preferred_element_type=F32)
    kblk = [k_all[:, e * LANES:(e + 1) * LANES].astype(BF16) for e in range(T)]
    zeros = jnp.zeros((LANES, LANES), BF16)
    for s in range(T):
        for t in range(T):
            lag = (s - t) if reverse else (t - s)
            tt_ref[s * LANES:(s + 1) * LANES, t * LANES:(t + 1) * LANES] = kblk[lag] if lag >= 0 else zeros


def _s5_prep(a_re, a_im, log_dt, b_re, b_im, c_re, c_im, reverse):
    g, p = a_re.shape
    cdim = b_re.shape[2]
    nj = g // S5_GPB
    ns = S5_GPB * p
    width = S5_SUB * LANES
    prm = jnp.zeros((nj, SUBLANES, ns), F32)
    prm = prm.at[:, 0].set(a_re.reshape(nj, ns)).at[:, 1].set(a_im.reshape(nj, ns))
    prm = prm.at[:, 2].set(jnp.repeat(log_dt, p).reshape(nj, ns))
    colp = jnp.broadcast_to(prm[:, :3, :, None], (nj, 3, ns, LANES))
    row_tile = lambda b: jnp.tile(b.reshape(nj, S5_GPB, p, cdim).transpose(0, 3, 1, 2).reshape(nj, 1, cdim, ns),
                                  (1, S5_GPB, 1, 1)).reshape(nj, LANES, ns)
    col_tile = lambda c_: jnp.tile(c_.reshape(nj, S5_GPB, cdim, p).transpose(0, 1, 3, 2).reshape(nj, ns, 1, cdim),
                                   (1, 1, S5_GPB, 1)).reshape(nj, ns, LANES)
    blk = lambda *shape: pl.BlockSpec((None,) + shape, lambda j: (j,) + (0,) * len(shape))
    return pl.pallas_call(
        functools.partial(_s5_prep_kernel, reverse=reverse),
        grid=(nj,),
        in_specs=[blk(SUBLANES, ns), blk(LANES, ns), blk(LANES, ns), blk(3, ns, LANES), blk(ns, LANES),
                  blk(ns, LANES)],
        out_specs=[blk(width, 2 * ns), blk(2 * ns, width), blk(width, width), blk(SUBLANES, 2 * ns)],
        out_shape=[jax.ShapeDtypeStruct((nj, width, 2 * ns), BF16), jax.ShapeDtypeStruct((nj, 2 * ns, width), BF16),
                   jax.ShapeDtypeStruct((nj, width, width), BF16), jax.ShapeDtypeStruct((nj, SUBLANES, 2 * ns), F32)],
        compiler_params=_cparams(("arbitrary",), VMEM_RESIDENT_MIB),
        name="s5_prep_bwd" if reverse else "s5_prep_fwd",
    )(prm, row_tile(b_re), row_tile(b_im), colp, col_tile(c_re), col_tile(c_im))


def _regroup_kernel(w_ref, o_ref, *, pieces):
    col = 0
    for start, width in pieces:
        if start is None:
            o_ref[:, col:col + width] = jnp.zeros((o_ref.shape[0], width), o_ref.dtype)
        else:
            o_ref[:, col:col + width] = w_ref[:, start:start + width].astype(o_ref.dtype)
        col += width


def _regroup_bf16(w, layer, pieces):
    _, rows, cols = w.shape
    n_out = sum(width for _, width in pieces)
    block_bytes = 8 * 1024 * 1024
    tr = max(k for k in (512, 256, 128, 64, 32, 16) if rows % k == 0 and k * cols * 4 <= block_bytes)
    return pl.pallas_call(
        functools.partial(_regroup_kernel, pieces=pieces),
        grid=(rows // tr,),
        in_specs=[pl.BlockSpec((None, tr, cols), lambda i: (layer, i, 0))],
        out_specs=pl.BlockSpec((tr, n_out), lambda i: (i, 0)),
        out_shape=jax.ShapeDtypeStruct((rows, n_out), BF16),
        compiler_params=_cparams(("arbitrary",), VMEM_MATRICES_MIB),
        name="weights_bf16",
    )(w)


def _bf16(w, layer):
    return _regroup_bf16(w, layer, ((0, w.shape[2]),))


def _rows(*vs, width):
    out = jnp.zeros((SUBLANES, width), F32)
    for i, v in enumerate(vs):
        out = out.at[i].set(v.astype(F32))
    return out


def kernel(x, c, ctx, c_ctx, ada_w, ada_b, norm_mix_g, norm_ffn_g, ffn_w_gate, ffn_w_up, ffn_w_down, final_norm_g, ev_w_in, ev_w_out, ssd_conv_w, ssd_conv_b, ssd_dt_bias, ssd_a_log, ssd_d, ssd_norm_g, hg_lb_logits, hg_norm_g, od_w_in, s5_a_re, s5_a_im, s5_log_dt, s5_b_re, s5_b_im, s5_c_re, s5_c_im, s5_d, od_w_val, od_w_gate):
    d = x.shape[-1]
    lat = x[0].astype(F32)
    hc = ctx[0].astype(F32)

    m = _ada(_rows(c[0], c_ctx, width=d), ada_w, ada_b)

    def ada_vecs(layer, stream):
        return [m[layer, stream, i * d:(i + 1) * d] for i in range(6)]

    n_x = SSD_HEADS * SSD_HEAD_DIM
    n_xbc = n_x + 2 * SSD_GROUPS * SSD_STATE
    n_hg = HG_HEADS * HG_HEAD_DIM
    o_z, o_xbc, o_dt = 0, n_x, n_x + n_xbc
    o_q = o_dt + 2 * SSD_HEADS
    o_f = o_q + n_hg
    o_v = o_f + 2 * n_hg
    o_g = o_v + n_hg
    w_in = _regroup_bf16(ev_w_in, 0, (
        (o_xbc, n_xbc),
        (o_dt, SSD_HEADS), (None, LANES - SSD_HEADS), (o_dt + SSD_HEADS, SSD_HEADS), (None, LANES - SSD_HEADS),
        (None, ZS_Q - ZS_XD),
        (o_q, n_hg), (o_v, n_hg), (o_f, 2 * n_hg), (o_z, n_x), (o_g, n_hg)))
    cw = jnp.zeros((SUBLANES, n_xbc), F32).at[:SSD_CONV].set(ssd_conv_w[0])
    cb = ssd_conv_b[0].reshape(1, n_xbc)
    dtc = jnp.zeros((2, SUBLANES, LANES), F32)
    dtc = dtc.at[:, 0, :SSD_HEADS].set(ssd_dt_bias[0]).at[:, 1, :SSD_HEADS].set(ssd_a_log[0])
    dsk = jnp.repeat(ssd_d[0], SSD_HEAD_DIM).reshape(1, n_x)
    lower = jnp.cumsum(jax.nn.softmax(hg_lb_logits.astype(F32), axis=0), axis=0)[0].reshape(1, n_hg)
    head_of_lane = jnp.arange(n_x) // SSD_HEAD_DIM
    e16 = (jnp.arange(LANES)[:, None] == head_of_lane[None, :]).astype(BF16)
    consts = (dtc, dsk, e16)
    w_out = _bf16(ev_w_out, 0)
    out_vec = lambda gate: _rows(ssd_norm_g[0], jnp.tile(hg_norm_g[0], HG_HEADS), gate, width=d)
    wg0, wu0, wd0 = (_bf16(t_, 0) for t_ in (ffn_w_gate, ffn_w_up, ffn_w_down))

    s_state = jnp.zeros((2, SSD_GROUPS, SSD_STATE, n_x // SSD_GROUPS), F32)
    g_state = jnp.zeros((2, HG_HEADS, HG_HEAD_DIM, HG_HEAD_DIM), F32)
    layer0 = {}
    for stream, h in ((1, hc), (0, lat)):
        sm, scm, gm, sf, scf, gf = ada_vecs(0, stream)
        za, zb = _even_in(h, _rows(norm_mix_g[0], scm, sm, width=d), w_in, cw, cb, dtc, lower)
        yo_f, yo_b, s_state, g_state = _even_scan(za, zb, consts, s_state, g_state)
        layer0[stream] = _even_out(h, zb, yo_f, yo_b, w_out, out_vec(gm),
                                   _rows(norm_ffn_g[0], scf, sf, gf, width=d), wg0, wu0, wd0)

    lat, hc = layer0[0], layer0[1]
    w_s5 = _bf16(od_w_in, 0)
    u = {}
    for stream, h, gw in ((1, hc, 1), (0, lat, GRID_W)):
        sm, scm = ada_vecs(1, stream)[:2]
        u[stream] = _odd_in(h, _rows(norm_mix_g[1], scm, sm, width=d), w_s5, gw)
    mats = [_s5_prep(s5_a_re[0, di], s5_a_im[0, di], s5_log_dt[0, di], s5_b_re[0, di], s5_b_im[0, di],
                     s5_c_re[0, di], s5_c_im[0, di], reverse=bool(di)) for di in range(2)]
    lam = jnp.stack([mats[0][3][:, 0], mats[1][3][:, 0]])
    carry = jnp.zeros_like(lam)
    for stream in (1, 0):
        s_f, s_b = (_s5_inject(u[stream], mats[di][0]) for di in range(2))
        p_f, p_b, carry = _s5_carry(s_f, s_b, lam, carry)
    nj = u[0].shape[0]
    y = _s5_readout(u[0], p_f, mats[0][1], mats[0][2], s5_d[0].reshape(nj, 1, LANES), reverse=False)
    y = _s5_readout(u[0], p_b, mats[1][1], mats[1][2], y, reverse=True)
    _, _, gm, sf, scf, gf = ada_vecs(1, 0)
    wg1, wu1, wd1 = (_bf16(t_, 1) for t_ in (ffn_w_gate, ffn_w_up, ffn_w_down))
    out = _odd_out(lat, y, _bf16(od_w_val, 0), _bf16(od_w_gate, 0), _rows(gm, width=d),
                   _rows(norm_ffn_g[1], scf, sf, gf, final_norm_g, width=d), wg1, wu1, wd1, GRID_W)
    return out[None].astype(x.dtype)
```

```python
import functools

import jax
import jax.numpy as jnp
from jax import lax
from jax.experimental import pallas as pl
from jax.experimental.pallas import tpu as pltpu

F32 = jnp.float32
BF16 = jnp.bfloat16
EPS = 1e-6

LANES = 128
SUBLANES = 8
GRID_W = 64
SCAN_CHUNK = 64
SSD_HEADS = 16
SSD_HEAD_DIM = 64
SSD_STATE = 128
SSD_GROUPS = 2
SSD_CONV = 5
HG_HEADS = 8
HG_HEAD_DIM = 128
S5_GROUP = 16
S5_STATE = 64
S5_SUB = 16
S5_GPB = LANES // S5_GROUP

ZS_XBC = 0
ZS_DT = 1536
ZS_XD = 1792
ZS_Q = 2048
ZS_V = 3072
ZS_F = 4096
ZS_Z = 6144
ZS_G = 7168
ZS_WIDTH = 8192


VMEM_STREAMING_MIB = 40
VMEM_MATRICES_MIB = 48
VMEM_RESIDENT_MIB = 56

ROWS_EVEN_IN = 512
ROWS_FUSED_OUT = 512
ROWS_ODD_IN = 32 * GRID_W
ROWS_ODD_OUT = 8 * GRID_W
S5_INJECT_SUBCHUNKS = 1024
S5_READOUT_SUBCHUNKS = 256
S5_CARRY_SUBCHUNKS = 256


def _cparams(semantics, vmem_mib):
    return pltpu.CompilerParams(dimension_semantics=semantics, vmem_limit_bytes=vmem_mib * 1024 * 1024)


def _silu(x):
    return x * jax.nn.sigmoid(x)


def _modulate(h, mod_ref):
    ms = jnp.mean(h * h, axis=-1, keepdims=True)
    return h * lax.rsqrt(ms + EPS) * (mod_ref[0:1, :] * (1.0 + mod_ref[1:2, :])) + mod_ref[2:3, :]


def _chunk_cumsum(x, reverse):
    n, w = x.shape
    tiles = x.reshape(n // SUBLANES, SUBLANES, w)
    sub = lax.broadcasted_iota(jnp.int32, (1, SUBLANES, 1), 1)
    s = 1
    while s < SUBLANES:
        if reverse:
            tiles = tiles + jnp.where(sub < SUBLANES - s, pltpu.roll(tiles, SUBLANES - s, axis=1), 0.0)
        else:
            tiles = tiles + jnp.where(sub >= s, pltpu.roll(tiles, s, axis=1), 0.0)
        s *= 2
    per_chunk = SCAN_CHUNK // SUBLANES
    edge = 0 if reverse else SUBLANES - 1
    out = [None] * (n // SUBLANES)
    for c in range(n // SCAN_CHUNK):
        run = None
        order = range(per_chunk - 1, -1, -1) if reverse else range(per_chunk)
        for k in order:
            tile = tiles[c * per_chunk + k]
            out[c * per_chunk + k] = tile if run is None else tile + run
            total = tile[edge:edge + 1, :]
            run = total if run is None else run + total
    return jnp.concatenate(out, axis=0)


def _ada_kernel(c_ref, w_ref, b_ref, o_ref):
    o_ref[...] = jnp.dot(_silu(c_ref[...]), w_ref[...], precision=lax.Precision.HIGHEST,
                         preferred_element_type=F32) + b_ref[...]


def _ada(cvecs, ada_w, ada_b):
    depth, d, n = ada_w.shape
    tn = n // 4
    return pl.pallas_call(
        _ada_kernel,
        grid=(depth, n // tn),
        in_specs=[pl.BlockSpec((SUBLANES, d), lambda l, j: (0, 0)),
                  pl.BlockSpec((None, d, tn), lambda l, j: (l, 0, j)),
                  pl.BlockSpec((None, 1, tn), lambda l, j: (l, 0, j))],
        out_specs=pl.BlockSpec((None, SUBLANES, tn), lambda l, j: (l, 0, j)),
        out_shape=jax.ShapeDtypeStruct((depth, SUBLANES, n), F32),
        compiler_params=_cparams(("arbitrary", "arbitrary"), VMEM_STREAMING_MIB),
        name="ada",
    )(cvecs, ada_w, ada_b.reshape(depth, 1, n))


EVEN_TN = 1024
HALO = 2 * SUBLANES
EVEN_TILES = (("conv", 0, None), ("conv_dt", 1, None), ("silu", None, 0), ("copy", None, 1),
              ("gate", 2, 4), ("gate", 3, 5), ("silu", None, 2), ("silu", None, 3))


def _even_in_kernel(h_ref, hp_ref, hn_ref, mod_ref, w_ref, cw_ref, cb_ref, dtc_ref, lb_ref, za_ref, zb_ref, a_ref):
    i = pl.program_id(0)
    tm = h_ref.shape[0]
    n_lb = lb_ref.shape[1]
    a_ref[:HALO] = _modulate(hp_ref[...], mod_ref).astype(BF16)
    a_ref[HALO:HALO + tm] = _modulate(h_ref[...], mod_ref).astype(BF16)
    a_ref[HALO + tm:] = _modulate(hn_ref[...], mod_ref).astype(BF16)
    row = lax.broadcasted_iota(jnp.int32, (tm + 2 * HALO, 1), 0)
    outside = ((row < HALO) & (i == 0)) | ((row >= HALO + tm) & (i == pl.num_programs(0) - 1))

    def conv_silu(col0, lanes):
        acc = jnp.dot(a_ref[...], w_ref[:, col0:col0 + lanes], preferred_element_type=F32)
        acc = jnp.where(outside, 0.0, acc)
        conv = cb_ref[:, col0:col0 + lanes]
        for k in range(SSD_CONV):
            off = HALO - SSD_CONV // 2 + k
            conv = conv + cw_ref[k:k + 1, col0:col0 + lanes] * acc[off:off + tm]
        return _silu(conv)

    def main_dot(col0, lanes=EVEN_TN):
        return jnp.dot(a_ref[HALO:HALO + tm], w_ref[:, col0:col0 + lanes], preferred_element_type=F32)

    for tile, (kind, fa, fb) in enumerate(EVEN_TILES):
        col0 = tile * EVEN_TN
        if kind == "conv":
            za_ref[:, fa * EVEN_TN:(fa + 1) * EVEN_TN] = conv_silu(col0, EVEN_TN)
        elif kind == "conv_dt":
            n_c = ZS_DT - col0
            za_ref[:, col0:ZS_DT] = conv_silu(col0, n_c)
            dt = main_dot(ZS_DT, 2 * LANES)
            for d in range(2):
                za_ref[:, ZS_DT + d * LANES:ZS_DT + (d + 1) * LANES] = jax.nn.softplus(
                    dt[:, d * LANES:(d + 1) * LANES] + dtc_ref[d, 0:1, :])
            za_ref[:, ZS_XD:(fa + 1) * EVEN_TN] = jnp.zeros((tm, (fa + 1) * EVEN_TN - ZS_XD), F32)
        elif kind == "gate":
            lb = lb_ref[:, (col0 - ZS_F) % n_lb:(col0 - ZS_F) % n_lb + EVEN_TN]
            f = lb + (1.0 - lb) * jax.nn.sigmoid(main_dot(col0))
            backward = (col0 - ZS_F) // n_lb == 1
            za_ref[:, fa * EVEN_TN:(fa + 1) * EVEN_TN] = _chunk_cumsum(jnp.log(f), backward)
            zb_ref[:, fb * EVEN_TN:(fb + 1) * EVEN_TN] = (1.0 - f).astype(BF16)
        elif kind == "silu":
            zb_ref[:, fb * EVEN_TN:(fb + 1) * EVEN_TN] = _silu(main_dot(col0)).astype(BF16)
        else:
            zb_ref[:, fb * EVEN_TN:(fb + 1) * EVEN_TN] = main_dot(col0).astype(BF16)


def _even_in(h, mod, w, cw, cb, dtc, lb):
    t, d = h.shape
    tm = min(t, ROWS_EVEN_IN)
    hb = tm // HALO
    n_a = (1 + max(fa for _, fa, _ in EVEN_TILES if fa is not None)) * EVEN_TN
    n_b = (1 + max(fb for _, _, fb in EVEN_TILES if fb is not None)) * EVEN_TN
    whole = lambda a, **kw: pl.BlockSpec(a.shape, lambda i: (0,) * a.ndim, **kw)
    return pl.pallas_call(
        _even_in_kernel,
        grid=(t // tm,),
        in_specs=[pl.BlockSpec((tm, d), lambda i: (i, 0)),
                  pl.BlockSpec((HALO, d), lambda i: (jnp.maximum(i * hb - 1, 0), 0)),
                  pl.BlockSpec((HALO, d), lambda i: (jnp.minimum((i + 1) * hb, t // HALO - 1), 0)),
                  whole(mod), whole(w, pipeline_mode=pl.Buffered(1)),
                  whole(cw), whole(cb), whole(dtc), whole(lb)],
        out_specs=[pl.BlockSpec((tm, n_a), lambda i: (i, 0)), pl.BlockSpec((tm, n_b), lambda i: (i, 0))],
        out_shape=[jax.ShapeDtypeStruct((t, n_a), F32), jax.ShapeDtypeStruct((t, n_b), BF16)],
        scratch_shapes=[pltpu.VMEM((tm + 2 * HALO, d), BF16)],
        compiler_params=_cparams(("arbitrary",), VMEM_RESIDENT_MIB),
        name="even_in",
    )(h, h, h, mod, w, cw, cb, dtc, lb)


def _scan_block_prep(xd_ref, qv_ref, bcum_ref, kk_ref, dtc_ref, e_ref, d):
    Q = SCAN_CHUNK
    reverse = bool(d)
    dtv = xd_ref[:, ZS_DT + d * LANES:ZS_DT + (d + 1) * LANES]
    acum = _chunk_cumsum(-dtv * jnp.exp(dtc_ref[d, 1:2, :]), reverse)
    rows = acum.shape[0]
    nchunk = rows // Q
    edge = 0 if reverse else Q - 1
    lasts = [acum[c * Q + edge:c * Q + edge + 1, :] for c in range(nchunk)]
    last_rows = jnp.concatenate([jnp.broadcast_to(l, (Q, LANES)) for l in lasts], axis=0)
    e_last = jnp.concatenate([jnp.broadcast_to(jnp.exp(l), (SUBLANES, LANES)) for l in lasts], axis=0)
    terms, rem = [], e_last
    for _ in range(3):
        hi = rem.astype(BF16)
        terms.append(hi)
        rem = rem - hi.astype(F32)
    lhs = jnp.concatenate([dtv.astype(BF16), jnp.exp(acum).astype(BF16), jnp.exp(last_rows - acum).astype(BF16)]
                          + terms, axis=0)
    xe = jnp.dot(lhs, e_ref[...], preferred_element_type=F32)
    n8 = SUBLANES * nchunk
    el = xe[3 * rows:]
    expanded = (xe[:rows], xe[rows:2 * rows], xe[2 * rows:3 * rows], el[:n8] + el[n8:2 * n8] + el[2 * n8:])
    return xd_ref, qv_ref, bcum_ref, kk_ref, acum, expanded


DECAY_LIMIT = 80.0


def _hg_intra_exact(qs, bcum, bcum_ref, kk_ref, v_ref, acc_ref, row0, reverse):
    Q = SCAN_CHUNK
    acc_ref[...] = jnp.zeros_like(acc_ref)
    t_idx = lax.broadcasted_iota(jnp.int32, (Q, 1), 0)

    def body(s, carry):
        r = row0 + s
        b_s, k_s, v_s = bcum_ref[pl.ds(r, 1), :], kk_ref[pl.ds(r, 1), :], v_ref[pl.ds(r, 1), :]
        allowed = (t_idx <= s) if reverse else (t_idx >= s)
        w = jnp.where(allowed, jnp.exp(jnp.minimum(bcum - b_s, 0.0)) * (qs * k_s), 0.0)
        for h in range(HG_HEADS):
            sl = slice(h * HG_HEAD_DIM, (h + 1) * HG_HEAD_DIM)
            acc_ref[:, sl] += jnp.sum(w[:, sl], axis=-1, keepdims=True) * v_s[:, sl]
        return carry

    lax.fori_loop(0, Q, body, 0)
    return acc_ref[...]


def _scan_chunk(prep, rows, dsk_ref, st_ref, gt_ref, yo_ref, d, reverse, exact=None):
    Q = SCAN_CHUNK
    xd_ref, qv_ref, bcum_ref, kk_ref, acum, expanded = prep
    n_hg = bcum_ref.shape[1]
    ti = lax.broadcasted_iota(jnp.int32, (Q, Q), 0)
    si = lax.broadcasted_iota(jnp.int32, (Q, Q), 1)
    mask = (ti <= si) if reverse else (ti >= si)
    edge = 0 if reverse else Q - 1
    width = SSD_HEADS * SSD_HEAD_DIM
    ci = rows.start // Q
    dt_x, ea_x, wend_x, elast_x = expanded
    chunk_x = (dt_x[rows], ea_x[rows], wend_x[rows], elast_x[ci * SUBLANES:ci * SUBLANES + 1])
    _scan_chunk_ssd(xd_ref, chunk_x, acum[rows], rows, dsk_ref, st_ref, yo_ref, d, reverse, mask)
    if exact is not None:
        intra = _hg_intra_exact(qv_ref[rows, :n_hg].astype(F32), bcum_ref[rows, :], *exact, rows.start, reverse)
    hw = n_hg // 2
    for part in range(2):
        cols = slice(part * hw, (part + 1) * hw)
        kk, bcum = kk_ref[rows, cols].astype(F32), bcum_ref[rows, cols]
        qs = qv_ref[rows, cols].astype(F32)
        vb = qv_ref[rows, n_hg + part * hw:n_hg + (part + 1) * hw]
        blast = bcum[edge:edge + 1, :]
        qe = (qs * jnp.exp(bcum)).astype(BF16)
        k_end = kk * jnp.exp(blast - bcum)
        if exact is None:
            k_til = (k_end * jnp.exp(-blast)).astype(BF16)
        k_end = k_end.astype(BF16)
        e_blast = jnp.exp(blast)
        outs = []
        for hl in range(hw // HG_HEAD_DIM):
            h = part * (hw // HG_HEAD_DIM) + hl
            sl = slice(hl * HG_HEAD_DIM, (hl + 1) * HG_HEAD_DIM)
            gt = gt_ref[d, h]
            if exact is None:
                att = lax.dot_general(qe[:, sl], k_til[:, sl], (((1,), (1,)), ((), ())), preferred_element_type=F32)
                att = jnp.where(mask, att, 0.0).astype(BF16)
                o_h = jnp.dot(att, vb[:, sl], preferred_element_type=F32)
            else:
                o_h = intra[:, h * HG_HEAD_DIM:(h + 1) * HG_HEAD_DIM]
            o_h = o_h + lax.dot_general(qe[:, sl], gt.astype(BF16), (((1,), (1,)), ((), ())),
                                        preferred_element_type=F32)
            outs.append(o_h)
            upd = lax.dot_general(vb[:, sl], k_end[:, sl], (((0,), (0,)), ((), ())), preferred_element_type=F32)
            gt_ref[d, h] = gt * e_blast[:, sl] + upd
        yo_ref[rows, width + part * hw:width + (part + 1) * hw] = jnp.concatenate(outs, axis=1).astype(yo_ref.dtype)


def _scan_chunk_ssd(xd_ref, chunk_x, acum, rows, dsk_ref, st_ref, yo_ref, d, reverse, mask):
    Q = SCAN_CHUNK
    width = SSD_HEADS * SSD_HEAD_DIM
    gw = width // SSD_GROUPS
    dt_all, ea_all, wend_all, elast_all = chunk_x
    acum_t = acum.T
    lane = lax.broadcasted_iota(jnp.int32, (Q, LANES), 1)
    lo = lane < SSD_HEAD_DIM
    zero_b = jnp.zeros((Q, LANES), BF16)
    for g in range(SSD_GROUPS):
        cols = slice(g * gw, (g + 1) * gw)
        dt_x, ea_x, wend_x, elast_x = dt_all[:, cols], ea_all[:, cols], wend_all[:, cols], elast_all[:, cols]
        xs = xd_ref[rows, cols]
        xdt = xs * dt_x
        xw = (xdt * wend_x).astype(BF16)
        xdt_b = xdt.astype(BF16)
        bm = xd_ref[rows, width + g * SSD_STATE:width + (g + 1) * SSD_STATE].astype(BF16)
        cm = xd_ref[rows, width + (SSD_GROUPS + g) * SSD_STATE:width + (SSD_GROUPS + g + 1) * SSD_STATE].astype(BF16)
        cb = lax.dot_general(cm, bm, (((1,), (1,)), ((), ())), preferred_element_type=F32)
        y_state = jnp.dot(cm, st_ref[d, g].astype(BF16), preferred_element_type=F32)
        pairs = []
        for hp in range(gw // LANES):
            xp = xdt_b[:, hp * LANES:(hp + 1) * LANES]
            acc = None
            for half in range(2):
                h = (g * gw + hp * LANES) // SSD_HEAD_DIM + half
                seg = acum[:, h:h + 1] - acum_t[h:h + 1, :]
                m = jnp.where(mask, cb * jnp.exp(seg), 0.0).astype(BF16)
                xh = jnp.where(lo, xp, zero_b) if half == 0 else jnp.where(lo, zero_b, xp)
                part = jnp.dot(m, xh, preferred_element_type=F32)
                acc = part if acc is None else acc + part
            pairs.append(acc)
        y = jnp.concatenate(pairs, axis=1) + y_state * ea_x
        if not reverse:
            y = y + dsk_ref[:, cols] * xs
        yo_ref[rows, cols] = y.astype(yo_ref.dtype)
        upd = lax.dot_general(bm, xw, (((0,), (0,)), ((), ())), preferred_element_type=F32)
        st_ref[d, g] = st_ref[d, g] * elast_x + upd


def _even_scan_kernel(*refs, nblk, cpb):
    fwd_in, bwd_in = refs[0:4], refs[4:8]
    dtc_ref, dsk_ref, e_ref, h0s_ref, h0g_ref = refs[8:13]
    yof_ref, yob_ref, st_ref, gt_ref, kkf_ref, vf_ref, acc_ref = refs[13:20]
    c = pl.program_id(0)

    @pl.when(c == 0)
    def _():
        st_ref[...] = h0s_ref[...]
        gt_ref[...] = h0g_ref[...]

    streams = ((False, fwd_in, yof_ref), (True, bwd_in, yob_ref))

    def run(exact):
        preps = [_scan_block_prep(*ins, dtc_ref, e_ref, d) for d, (_, ins, _) in enumerate(streams)]
        if exact:
            for d, (_, ins, _) in enumerate(streams):
                kkf_ref[d] = ins[3][...].astype(F32)
                vf_ref[d] = ins[1][:, ins[2].shape[1]:].astype(F32)
        for step in range(cpb):
            for d, (reverse, ins, yo_ref) in enumerate(streams):
                ci = (cpb - 1 - step) if reverse else step
                rows = slice(ci * SCAN_CHUNK, (ci + 1) * SCAN_CHUNK)
                refs_exact = (ins[2], kkf_ref.at[d], vf_ref.at[d], acc_ref) if exact else None
                _scan_chunk(preps[d], rows, dsk_ref, st_ref, gt_ref, yo_ref, d, reverse, refs_exact)

    totals = [ref[ci * SCAN_CHUNK + edge:ci * SCAN_CHUNK + edge + 1, :]
              for ref, edge in ((fwd_in[2], SCAN_CHUNK - 1), (bwd_in[2], 0)) for ci in range(cpb)]
    strongest = jnp.min(jnp.concatenate(totals, axis=0))
    pl.when(strongest >= -DECAY_LIMIT)(lambda: run(False))
    pl.when(strongest < -DECAY_LIMIT)(lambda: run(True))


def _even_scan(za, zb, consts, h0s, h0g):
    t = za.shape[0]
    nch = t // SCAN_CHUNK
    cpb = max(k for k in (4, 2, 1) if nch % k == 0)
    rows = cpb * SCAN_CHUNK
    nblk = nch // cpb
    dtc, dsk, e16 = consts
    n_x = SSD_HEADS * SSD_HEAD_DIM
    n_hg = HG_HEADS * HG_HEAD_DIM
    gate_a, gate_b = (tuple(tile[i] for tile in EVEN_TILES if tile[0] == "gate") for i in (1, 2))

    def stream_specs(bidx, d):
        return [
            pl.BlockSpec((rows, ZS_XD), lambda c: (bidx(c), 0)),
            pl.BlockSpec((rows, 2 * n_hg), lambda c: (bidx(c), 0)),
            pl.BlockSpec((rows, n_hg), lambda c: (bidx(c), gate_a[d])),
            pl.BlockSpec((rows, n_hg), lambda c: (bidx(c), gate_b[d])),
        ]

    fwd = lambda c: c
    bwd = lambda c: nblk - 1 - c
    whole = lambda a: pl.BlockSpec(a.shape, lambda c: (0,) * a.ndim)
    in_specs = stream_specs(fwd, 0) + stream_specs(bwd, 1) + [whole(a) for a in (dtc, dsk, e16, h0s, h0g)]
    out_specs = [pl.BlockSpec((rows, n_x + n_hg), lambda c: (c, 0)),
                 pl.BlockSpec((rows, n_x + n_hg), lambda c: (bwd(c), 0)), whole(h0s), whole(h0g)]
    out_shape = [jax.ShapeDtypeStruct((t, n_x + n_hg), BF16), jax.ShapeDtypeStruct((t, n_x + n_hg), BF16),
                 jax.ShapeDtypeStruct(h0s.shape, F32), jax.ShapeDtypeStruct(h0g.shape, F32)]
    return pl.pallas_call(
        functools.partial(_even_scan_kernel, nblk=nblk, cpb=cpb),
        grid=(nblk,),
        in_specs=in_specs,
        out_specs=out_specs,
        out_shape=out_shape,
        scratch_shapes=[pltpu.VMEM((2, rows, n_hg), F32), pltpu.VMEM((2, rows, n_hg), F32),
                        pltpu.VMEM((SCAN_CHUNK, n_hg), F32)],
        compiler_params=_cparams(("arbitrary",), VMEM_MATRICES_MIB),
        name="even_scan",
    )(za, zb, za, zb, za, zb, za, zb, dtc, dsk, e16, h0s, h0g)


def _ffn_block(h, mod_ref, wg_ref, wu_ref, wd_ref, final):
    a = _modulate(h, mod_ref).astype(BF16)
    gate = jnp.dot(a, wg_ref[...], preferred_element_type=F32)
    up = jnp.dot(a, wu_ref[...], preferred_element_type=F32)
    act = (_silu(gate) * up).astype(BF16)
    out = h + mod_ref[3:4, :] * jnp.dot(act, wd_ref[...], preferred_element_type=F32)
    if final:
        ms = jnp.mean(out * out, axis=-1, keepdims=True)
        out = out * lax.rsqrt(ms + EPS) * mod_ref[4:5, :]
    return out


def _resident(a):
    return pl.BlockSpec(a.shape, lambda i: (0,) * a.ndim, pipeline_mode=pl.Buffered(1))


def _even_out_kernel(h_ref, z_ref, g_ref, yof_ref, yob_ref, w_ref, vec_ref, mod_ref, wg_ref, wu_ref, wd_ref, o_ref):
    n_x = z_ref.shape[1]
    yo = yof_ref[...].astype(F32) + yob_ref[...].astype(F32)
    y = yo[:, :n_x] * z_ref[...].astype(F32)
    o = yo[:, n_x:]
    gw = n_x // SSD_GROUPS
    parts = []
    for g in range(SSD_GROUPS):
        yg = y[:, g * gw:(g + 1) * gw]
        ms = jnp.mean(yg * yg, axis=-1, keepdims=True)
        parts.append(yg * lax.rsqrt(ms + EPS))
    yn = (jnp.concatenate(parts, axis=1) * vec_ref[0:1, :]).astype(BF16)
    parts = []
    for hh in range(HG_HEADS):
        oh = o[:, hh * HG_HEAD_DIM:(hh + 1) * HG_HEAD_DIM]
        ms = jnp.mean(oh * oh, axis=-1, keepdims=True)
        parts.append(oh * lax.rsqrt(ms + EPS))
    on = (jnp.concatenate(parts, axis=1) * vec_ref[1:2, :] * g_ref[...].astype(F32)).astype(BF16)
    half = yn.shape[1]
    mix = jnp.dot(yn, w_ref[:half, :], preferred_element_type=F32)
    mix = mix + jnp.dot(on, w_ref[half:, :], preferred_element_type=F32)
    h1 = h_ref[...] + vec_ref[2:3, :] * mix
    o_ref[...] = _ffn_block(h1, mod_ref, wg_ref, wu_ref, wd_ref, final=False)


def _even_out(h, zb, yo_f, yo_b, w_out, vec, mod, wg, wu, wd):
    t, d = h.shape
    tm = min(t, ROWS_FUSED_OUT)
    n_x = SSD_HEADS * SSD_HEAD_DIM
    n_hg = HG_HEADS * HG_HEAD_DIM
    row = lambda i: (i, 0)
    z_tile, g_tile = (next(fb for k, (_, _, fb) in enumerate(EVEN_TILES) if k * EVEN_TN == start)
                      for start in (ZS_Z, ZS_G))
    return pl.pallas_call(
        _even_out_kernel,
        grid=(t // tm,),
        in_specs=[pl.BlockSpec((tm, d), row),
                  pl.BlockSpec((tm, n_x), lambda i: (i, z_tile)),
                  pl.BlockSpec((tm, n_hg), lambda i: (i, g_tile)),
                  pl.BlockSpec((tm, n_x + n_hg), row), pl.BlockSpec((tm, n_x + n_hg), row),
                  _resident(w_out), _resident(vec), _resident(mod), _resident(wg), _resident(wu), _resident(wd)],
        out_specs=pl.BlockSpec((tm, d), row),
        out_shape=jax.ShapeDtypeStruct((t, d), F32),
        compiler_params=_cparams(("arbitrary",), VMEM_RESIDENT_MIB),
        name="even_out_ffn",
    )(h, zb, zb, yo_f, yo_b, w_out, vec, mod, wg, wu, wd)


def _odd_in_kernel(h_ref, mod_ref, w_ref, o_ref, u_ref):
    a = _modulate(h_ref[...], mod_ref).astype(BF16)
    u = jnp.dot(a, w_ref[...], preferred_element_type=F32)
    nj, grid_w, rb, _ = o_ref.shape
    for j in range(nj):
        u_ref[j] = u[:, j * LANES:(j + 1) * LANES]
    for wi in range(grid_w):
        for j in range(nj):
            o_ref[j, wi] = u_ref[j, pl.ds(wi, rb, stride=grid_w), :]


def _odd_in(h, mod, w, grid_w):
    t, d = h.shape
    rows = t // grid_w
    rb = min(rows, ROWS_ODD_IN // GRID_W)
    n = w.shape[1]
    nj = n // LANES
    out = pl.pallas_call(
        _odd_in_kernel,
        grid=(rows // rb,),
        in_specs=[pl.BlockSpec((rb * grid_w, d), lambda i: (i, 0)),
                  pl.BlockSpec((SUBLANES, d), lambda i: (0, 0)),
                  pl.BlockSpec(w.shape, lambda i: (0, 0))],
        out_specs=pl.BlockSpec((nj, grid_w, rb, LANES), lambda i: (0, 0, i, 0)),
        out_shape=jax.ShapeDtypeStruct((nj, grid_w, rows, LANES), F32),
        scratch_shapes=[pltpu.VMEM((nj, rb * grid_w, LANES), F32)],
        compiler_params=_cparams(("arbitrary",), VMEM_MATRICES_MIB),
        name="odd_in",
    )(h, mod, w)
    return out.reshape(nj, t, LANES)


def _sub_chunk_rows(u_ref, nb):
    return [u_ref[pl.ds(s, nb, stride=S5_SUB), :] for s in range(S5_SUB)]


def _s5_inject_kernel(u_ref, win_ref, s_ref):
    x = jnp.concatenate(_sub_chunk_rows(u_ref, s_ref.shape[0]), axis=1).astype(BF16)
    s_ref[...] = jnp.dot(x, win_ref[...], preferred_element_type=F32)


def _s5_inject(u, win):
    nj, t, _ = u.shape
    n = t // S5_SUB
    nb = min(n, S5_INJECT_SUBCHUNKS)
    width, ns = win.shape[1:]
    return pl.pallas_call(
        _s5_inject_kernel,
        grid=(nj, n // nb),
        in_specs=[pl.BlockSpec((None, nb * S5_SUB, LANES), lambda j, b: (j, b, 0)),
                  pl.BlockSpec((None, width, ns), lambda j, b: (j, 0, 0))],
        out_specs=pl.BlockSpec((None, nb, ns), lambda j, b: (j, b, 0)),
        out_shape=jax.ShapeDtypeStruct((nj, n, ns), F32),
        compiler_params=_cparams(("arbitrary", "arbitrary"), VMEM_STREAMING_MIB),
        name="s5_inject",
    )(u, win)


def _s5_carry_kernel(sf_ref, sb_ref, lam_ref, h0_ref, pf_ref, pb_ref, hf_ref, tf_ref, tb_ref, c_ref):
    b = pl.program_id(0)
    nj, kb, ns = sf_ref.shape
    half = ns // 2

    @pl.when(b == 0)
    def _():
        c_ref[...] = h0_ref[...]

    for j in range(nj):
        tf_ref[:, j, :] = sf_ref[j]
        tb_ref[:, j, :] = sb_ref[j]

    def advance(d, c, s):
        lr, li = lam_ref[d, :, :half], lam_ref[d, :, half:]
        pr, pi = c[:, :half], c[:, half:]
        return jnp.concatenate([lr * pr - li * pi + s[:, :half], lr * pi + li * pr + s[:, half:]], axis=1)

    def step(i, carry):
        cf, cb = carry
        kr = kb - 1 - i
        sf = tf_ref[i]
        sb = tb_ref[kr]
        tf_ref[i] = cf
        tb_ref[kr] = cb
        return advance(0, cf, sf), advance(1, cb, sb)

    cf, cb = lax.fori_loop(0, kb, step, (c_ref[0], c_ref[1]), unroll=4)
    c_ref[0] = cf
    c_ref[1] = cb
    for j in range(nj):
        pf_ref[j] = tf_ref[:, j, :]
        pb_ref[j] = tb_ref[:, j, :]

    @pl.when(b == pl.num_programs(0) - 1)
    def _():
        hf_ref[...] = c_ref[...]


def _s5_carry(s_f, s_b, lam, h0):
    nj, n, ns = s_f.shape
    kb = min(n, S5_CARRY_SUBCHUNKS)
    nblk = n // kb
    fwd = pl.BlockSpec((nj, kb, ns), lambda b: (0, b, 0))
    bwd = pl.BlockSpec((nj, kb, ns), lambda b: (0, nblk - 1 - b, 0))
    whole = pl.BlockSpec(lam.shape, lambda b: (0, 0, 0))
    return pl.pallas_call(
        _s5_carry_kernel,
        grid=(nblk,),
        in_specs=[fwd, bwd, whole, whole],
        out_specs=[fwd, bwd, whole],
        out_shape=[jax.ShapeDtypeStruct(s_f.shape, F32), jax.ShapeDtypeStruct(s_b.shape, F32),
                   jax.ShapeDtypeStruct(lam.shape, F32)],
        scratch_shapes=[pltpu.VMEM((kb, nj, ns), F32), pltpu.VMEM((kb, nj, ns), F32), pltpu.VMEM(lam.shape, F32)],
        compiler_params=_cparams(("arbitrary",), VMEM_MATRICES_MIB),
        name="s5_carry",
    )(s_f, s_b, lam, h0)


def _s5_readout_kernel(u_ref, p_ref, wout_ref, tt_ref, add_ref, y_ref, *, reverse):
    nb = p_ref.shape[0]
    us = _sub_chunk_rows(u_ref, nb)
    xb = jnp.concatenate(us, axis=1).astype(BF16)
    pb = p_ref[...].astype(BF16)
    width = xb.shape[1]
    tile = 2 * LANES
    for m in range(width // tile):
        cols = slice(m * tile, (m + 1) * tile)
        rows = slice(m * tile, width) if reverse else slice(0, (m + 1) * tile)
        y = jnp.dot(pb, wout_ref[:, cols], preferred_element_type=F32)
        y = y + jnp.dot(xb[:, rows], tt_ref[rows, cols], preferred_element_type=F32)
        for q in range(tile // LANES):
            s = m * (tile // LANES) + q
            other = add_ref[pl.ds(s, nb, stride=S5_SUB), :] if reverse else add_ref[...] * us[s]
            y_ref[pl.ds(s, nb, stride=S5_SUB), :] = y[:, q * LANES:(q + 1) * LANES] + other


def _s5_readout(u, p, wout, tt, add, reverse):
    nj, t, _ = u.shape
    n = t // S5_SUB
    nb = min(n, S5_READOUT_SUBCHUNKS)
    ns, width = wout.shape[1:]
    tokens = pl.BlockSpec((None, nb * S5_SUB, LANES), lambda j, b: (j, b, 0))
    add_spec = tokens if reverse else pl.BlockSpec((None, 1, LANES), lambda j, b: (j, 0, 0))
    return pl.pallas_call(
        functools.partial(_s5_readout_kernel, reverse=reverse),
        grid=(nj, n // nb),
        in_specs=[tokens,
                  pl.BlockSpec((None, nb, ns), lambda j, b: (j, b, 0)),
                  pl.BlockSpec((None, ns, width), lambda j, b: (j, 0, 0)),
                  pl.BlockSpec((None, width, width), lambda j, b: (j, 0, 0)),
                  add_spec],
        out_specs=tokens,
        out_shape=jax.ShapeDtypeStruct(u.shape, F32),
        compiler_params=_cparams(("arbitrary", "arbitrary"), VMEM_MATRICES_MIB),
        name="s5_readout_bwd" if reverse else "s5_readout_fwd",
    )(u, p, wout, tt, add)


def _odd_out_kernel(h_ref, y_ref, wv_ref, wgl_ref, vec_ref, mod_ref, wg_ref, wu_ref, wd_ref, o_ref, a_ref):
    nj, grid_w, rb, _ = y_ref.shape
    for wi in range(grid_w):
        for j in range(nj):
            a_ref[j, pl.ds(wi, rb, stride=grid_w), :] = y_ref[j, wi]
    a = jax.nn.gelu(jnp.concatenate([a_ref[j] for j in range(nj)], axis=1)).astype(BF16)
    val = jnp.dot(a, wv_ref[...], preferred_element_type=F32)
    gate = jnp.dot(a, wgl_ref[...], preferred_element_type=F32)
    h3 = h_ref[...] + vec_ref[0:1, :] * (val * jax.nn.sigmoid(gate))
    o_ref[...] = _ffn_block(h3, mod_ref, wg_ref, wu_ref, wd_ref, final=True)


def _odd_out(h, y, w_val, w_gate, vec, mod, wg, wu, wd, grid_w):
    t, d = h.shape
    rows = t // grid_w
    rb = min(rows, ROWS_ODD_OUT // GRID_W)
    nj = y.shape[0]
    tok = pl.BlockSpec((rb * grid_w, d), lambda i: (i, 0))
    return pl.pallas_call(
        _odd_out_kernel,
        grid=(rows // rb,),
        in_specs=[tok, pl.BlockSpec((nj, grid_w, rb, LANES), lambda i: (0, 0, i, 0)),
                  _resident(w_val), _resident(w_gate), _resident(vec), _resident(mod),
                  _resident(wg), _resident(wu), _resident(wd)],
        out_specs=tok,
        out_shape=jax.ShapeDtypeStruct((t, d), F32),
        scratch_shapes=[pltpu.VMEM((nj, rb * grid_w, LANES), F32)],
        compiler_params=_cparams(("arbitrary",), VMEM_RESIDENT_MIB),
        name="odd_out_ffn",
    )(h, y.reshape(nj, grid_w, rows, LANES), w_val, w_gate, vec, mod, wg, wu, wd)


def _s5_prep_kernel(prm_ref, bre_ref, bim_ref, colp_ref, cre_ref, cim_ref, win_ref, wout_ref, tt_ref, lam_ref, *,
                    reverse):
    T = S5_SUB
    ns = prm_ref.shape[1]
    a_re, a_im, dt = prm_ref[0:1, :], prm_ref[1:2, :], jnp.exp(prm_ref[2:3, :])
    tau = lax.broadcasted_iota(jnp.int32, (3 * SUBLANES, ns), 0).astype(F32)
    mag = jnp.exp(a_re * dt * tau)
    pr = mag * jnp.cos(a_im * dt * tau)
    pi = mag * jnp.sin(a_im * dt * tau)
    lr, li = pr[1:2], pi[1:2]
    den = a_re * a_re + a_im * a_im
    cr = ((lr - 1.0) * a_re + li * a_im) / den
    ci = (li * a_re - (lr - 1.0) * a_im) / den
    row_g = lax.broadcasted_iota(jnp.int32, (LANES, ns), 0) // S5_GROUP
    col_g = lax.broadcasted_iota(jnp.int32, (LANES, ns), 1) // S5_STATE
    same = row_g == col_g
    bbr = jnp.where(same, cr * bre_ref[...] - ci * bim_ref[...], 0.0)
    bbi = jnp.where(same, cr * bim_ref[...] + ci * bre_ref[...], 0.0)
    for s in range(T):
        e = s if reverse else T - 1 - s
        rows = slice(s * LANES, (s + 1) * LANES)
        win_ref[rows, :ns] = (pr[e:e + 1] * bbr - pi[e:e + 1] * bbi).astype(BF16)
        win_ref[rows, ns:] = (pr[e:e + 1] * bbi + pi[e:e + 1] * bbr).astype(BF16)
    lam_ref[...] = jnp.broadcast_to(jnp.concatenate([pr[T:T + 1], pi[T:T + 1]], axis=1), lam_ref.shape)
    dt_c = jnp.exp(colp_ref[2])
    mag_c = jnp.exp(colp_ref[0] * dt_c)
    lr_c = mag_c * jnp.cos(colp_ref[1] * dt_c)
    li_c = mag_c * jnp.sin(colp_ref[1] * dt_c)
    row_gc = lax.broadcasted_iota(jnp.int32, (ns, LANES), 0) // S5_STATE
    col_gc = lax.broadcasted_iota(jnp.int32, (ns, LANES), 1) // S5_GROUP
    same_c = row_gc == col_gc
    c_re = jnp.where(same_c, cre_ref[...], 0.0)
    c_im = jnp.where(same_c, cim_ref[...], 0.0)
    bb = jnp.concatenate([bbr, bbi], axis=1)
    bb_hi = bb.astype(BF16)
    bb_lo = (bb - bb_hi.astype(F32)).astype(BF16)
    cur_r = jnp.ones((ns, LANES), F32)
    cur_i = jnp.zeros((ns, LANES), F32)
    kblk = []
    for e in range(T + 1):
        wo_r = c_re * cur_r - c_im * cur_i
        wo_i = -(c_re * cur_i + c_im * cur_r)
        if e >= 1:
            t = (T - e) if reverse else e - 1
            wout_ref[:ns, t * LANES:(t + 1) * LANES] = wo_r.astype(BF16)
            wout_ref[ns:, t * LANES:(t + 1) * LANES] = wo_i.astype(BF16)
        if e < T:
            wo = jnp.concatenate([wo_r, wo_i], axis=0)
            wo_hi = wo.astype(BF16)
            kblk.append((wo_hi, (wo - wo_hi.astype(F32)).astype(BF16)))
        cur_r, cur_i = cur_r * lr_c - cur_i * li_c, cur_r * li_c + cur_i * lr_c
    wo_hi = jnp.concatenate([hi for hi, _ in kblk], axis=1)
    wo_lo = jnp.concatenate([lo for _, lo in kblk], axis=1)
    k_all = jnp.dot(bb_hi, wo_hi, preferred_element_type=F32)
    k_all = k_all + jnp.dot(bb_hi, wo_lo, preferred_element_type=F32) + jnp.dot(bb_lo, wo_hi, preferred_element_type=F32)
    kblk = [k_all[:, e * LANES:(e + 1) * LANES].astype(BF16) for e in range(T)]
    zeros = jnp.zeros((LANES, LANES), BF16)
    for s in range(T):
        for t in range(T):
            lag = (s - t) if reverse else (t - s)
            tt_ref[s * LANES:(s + 1) * LANES, t * LANES:(t + 1) * LANES] = kblk[lag] if lag >= 0 else zeros


def _s5_prep(a_re, a_im, log_dt, b_re, b_im, c_re, c_im, reverse):
    g, p = a_re.shape
    cdim = b_re.shape[2]
    nj = g // S5_GPB
    ns = S5_GPB * p
    width = S5_SUB * LANES
    prm = jnp.zeros((nj, SUBLANES, ns), F32)
    prm = prm.at[:, 0].set(a_re.reshape(nj, ns)).at[:, 1].set(a_im.reshape(nj, ns))
    prm = prm.at[:, 2].set(jnp.repeat(log_dt, p).reshape(nj, ns))
    colp = jnp.broadcast_to(prm[:, :3, :, None], (nj, 3, ns, LANES))
    row_tile = lambda b: jnp.tile(b.reshape(nj, S5_GPB, p, cdim).transpose(0, 3, 1, 2).reshape(nj, 1, cdim, ns),
                                  (1, S5_GPB, 1, 1)).reshape(nj, LANES, ns)
    col_tile = lambda c_: jnp.tile(c_.reshape(nj, S5_GPB, cdim, p).transpose(0, 1, 3, 2).reshape(nj, ns, 1, cdim),
                                   (1, 1, S5_GPB, 1)).reshape(nj, ns, LANES)
    blk = lambda *shape: pl.BlockSpec((None,) + shape, lambda j: (j,) + (0,) * len(shape))
    return pl.pallas_call(
        functools.partial(_s5_prep_kernel, reverse=reverse),
        grid=(nj,),
        in_specs=[blk(SUBLANES, ns), blk(LANES, ns), blk(LANES, ns), blk(3, ns, LANES), blk(ns, LANES),
                  blk(ns, LANES)],
        out_specs=[blk(width, 2 * ns), blk(2 * ns, width), blk(width, width), blk(SUBLANES, 2 * ns)],
        out_shape=[jax.ShapeDtypeStruct((nj, width, 2 * ns), BF16), jax.ShapeDtypeStruct((nj, 2 * ns, width), BF16),
                   jax.ShapeDtypeStruct((nj, width, width), BF16), jax.ShapeDtypeStruct((nj, SUBLANES, 2 * ns), F32)],
        compiler_params=_cparams(("arbitrary",), VMEM_RESIDENT_MIB),
        name="s5_prep_bwd" if reverse else "s5_prep_fwd",
    )(prm, row_tile(b_re), row_tile(b_im), colp, col_tile(c_re), col_tile(c_im))


def _regroup_kernel(w_ref, o_ref, *, pieces):
    col = 0
    for start, width in pieces:
        if start is None:
            o_ref[:, col:col + width] = jnp.zeros((o_ref.shape[0], width), o_ref.dtype)
        else:
            o_ref[:, col:col + width] = w_ref[:, start:start + width].astype(o_ref.dtype)
        col += width


def _regroup_bf16(w, layer, pieces):
    _, rows, cols = w.shape
    n_out = sum(width for _, width in pieces)
    block_bytes = 8 * 1024 * 1024
    tr = max(k for k in (512, 256, 128, 64, 32, 16) if rows % k == 0 and k * cols * 4 <= block_bytes)
    return pl.pallas_call(
        functools.partial(_regroup_kernel, pieces=pieces),
        grid=(rows // tr,),
        in_specs=[pl.BlockSpec((None, tr, cols), lambda i: (layer, i, 0))],
        out_specs=pl.BlockSpec((tr, n_out), lambda i: (i, 0)),
        out_shape=jax.ShapeDtypeStruct((rows, n_out), BF16),
        compiler_params=_cparams(("arbitrary",), VMEM_MATRICES_MIB),
        name="weights_bf16",
    )(w)


def _bf16(w, layer):
    return _regroup_bf16(w, layer, ((0, w.shape[2]),))


def _rows(*vs, width):
    out = jnp.zeros((SUBLANES, width), F32)
    for i, v in enumerate(vs):
        out = out.at[i].set(v.astype(F32))
    return out


def kernel(x, c, ctx, c_ctx, ada_w, ada_b, norm_mix_g, norm_ffn_g, ffn_w_gate, ffn_w_up, ffn_w_down, final_norm_g, ev_w_in, ev_w_out, ssd_conv_w, ssd_conv_b, ssd_dt_bias, ssd_a_log, ssd_d, ssd_norm_g, hg_lb_logits, hg_norm_g, od_w_in, s5_a_re, s5_a_im, s5_log_dt, s5_b_re, s5_b_im, s5_c_re, s5_c_im, s5_d, od_w_val, od_w_gate):
    d = x.shape[-1]
    lat = x[0].astype(F32)
    hc = ctx[0].astype(F32)

    m = _ada(_rows(c[0], c_ctx, width=d), ada_w, ada_b)

    def ada_vecs(layer, stream):
        return [m[layer, stream, i * d:(i + 1) * d] for i in range(6)]

    n_x = SSD_HEADS * SSD_HEAD_DIM
    n_xbc = n_x + 2 * SSD_GROUPS * SSD_STATE
    n_hg = HG_HEADS * HG_HEAD_DIM
    o_z, o_xbc, o_dt = 0, n_x, n_x + n_xbc
    o_q = o_dt + 2 * SSD_HEADS
    o_f = o_q + n_hg
    o_v = o_f + 2 * n_hg
    o_g = o_v + n_hg
    w_in = _regroup_bf16(ev_w_in, 0, (
        (o_xbc, n_xbc),
        (o_dt, SSD_HEADS), (None, LANES - SSD_HEADS), (o_dt + SSD_HEADS, SSD_HEADS), (None, LANES - SSD_HEADS),
        (None, ZS_Q - ZS_XD),
        (o_q, n_hg), (o_v, n_hg), (o_f, 2 * n_hg), (o_z, n_x), (o_g, n_hg)))
    cw = jnp.zeros((SUBLANES, n_xbc), F32).at[:SSD_CONV].set(ssd_conv_w[0])
    cb = ssd_conv_b[0].reshape(1, n_xbc)
    dtc = jnp.zeros((2, SUBLANES, LANES), F32)
    dtc = dtc.at[:, 0, :SSD_HEADS].set(ssd_dt_bias[0]).at[:, 1, :SSD_HEADS].set(ssd_a_log[0])
    dsk = jnp.repeat(ssd_d[0], SSD_HEAD_DIM).reshape(1, n_x)
    lower = jnp.cumsum(jax.nn.softmax(hg_lb_logits.astype(F32), axis=0), axis=0)[0].reshape(1, n_hg)
    head_of_lane = jnp.arange(n_x) // SSD_HEAD_DIM
    e16 = (jnp.arange(LANES)[:, None] == head_of_lane[None, :]).astype(BF16)
    consts = (dtc, dsk, e16)
    w_out = _bf16(ev_w_out, 0)
    out_vec = lambda gate: _rows(ssd_norm_g[0], jnp.tile(hg_norm_g[0], HG_HEADS), gate, width=d)
    wg0, wu0, wd0 = (_bf16(t_, 0) for t_ in (ffn_w_gate, ffn_w_up, ffn_w_down))

    s_state = jnp.zeros((2, SSD_GROUPS, SSD_STATE, n_x // SSD_GROUPS), F32)
    g_state = jnp.zeros((2, HG_HEADS, HG_HEAD_DIM, HG_HEAD_DIM), F32)
    layer0 = {}
    for stream, h in ((1, hc), (0, lat)):
        sm, scm, gm, sf, scf, gf = ada_vecs(0, stream)
        za, zb = _even_in(h, _rows(norm_mix_g[0], scm, sm, width=d), w_in, cw, cb, dtc, lower)
        yo_f, yo_b, s_state, g_state = _even_scan(za, zb, consts, s_state, g_state)
        layer0[stream] = _even_out(h, zb, yo_f, yo_b, w_out, out_vec(gm),
                                   _rows(norm_ffn_g[0], scf, sf, gf, width=d), wg0, wu0, wd0)

    lat, hc = layer0[0], layer0[1]
    w_s5 = _bf16(od_w_in, 0)
    u = {}
    for stream, h, gw in ((1, hc, 1), (0, lat, GRID_W)):
        sm, scm = ada_vecs(1, stream)[:2]
        u[stream] = _odd_in(h, _rows(norm_mix_g[1], scm, sm, width=d), w_s5, gw)
    mats = [_s5_prep(s5_a_re[0, di], s5_a_im[0, di], s5_log_dt[0, di], s5_b_re[0, di], s5_b_im[0, di],
                     s5_c_re[0, di], s5_c_im[0, di], reverse=bool(di)) for di in range(2)]
    lam = jnp.stack([mats[0][3][:, 0], mats[1][3][:, 0]])
    carry = jnp.zeros_like(lam)
    for stream in (1, 0):
        s_f, s_b = (_s5_inject(u[stream], mats[di][0]) for di in range(2))
        p_f, p_b, carry = _s5_carry(s_f, s_b, lam, carry)
    nj = u[0].shape[0]
    y = _s5_readout(u[0], p_f, mats[0][1], mats[0][2], s5_d[0].reshape(nj, 1, LANES), reverse=False)
    y = _s5_readout(u[0], p_b, mats[1][1], mats[1][2], y, reverse=True)
    _, _, gm, sf, scf, gf = ada_vecs(1, 0)
    wg1, wu1, wd1 = (_bf16(t_, 1) for t_ in (ffn_w_gate, ffn_w_up, ffn_w_down))
    out = _odd_out(lat, y, _bf16(od_w_val, 0), _bf16(od_w_gate, 0), _rows(gm, width=d),
                   _rows(norm_ffn_g[1], scf, sf, gf, final_norm_g, width=d), wg1, wu1, wd1, GRID_W)
    return out[None].astype(x.dtype)
```

```python
import functools

import jax
import jax.numpy as jnp
from jax import lax
from jax.experimental import pallas as pl
from jax.experimental.pallas import tpu as pltpu

F32 = jnp.float32
BF16 = jnp.bfloat16
EPS = 1e-6

LANES = 128
SUBLANES = 8
GRID_W = 64
SCAN_CHUNK = 64
SSD_HEADS = 16
SSD_HEAD_DIM = 64
SSD_STATE = 128
SSD_GROUPS = 2
SSD_CONV = 5
HG_HEADS = 8
HG_HEAD_DIM = 128
S5_GROUP = 16
S5_STATE = 64
S5_SUB = 16
S5_GPB = LANES // S5_GROUP

ZS_XBC = 0
ZS_DT = 1536
ZS_XD = 1792
ZS_Q = 2048
ZS_V = 3072
ZS_F = 4096
ZS_Z = 6144
ZS_G = 7168
ZS_WIDTH = 8192


VMEM_STREAMING_MIB = 40
VMEM_MATRICES_MIB = 48
VMEM_RESIDENT_MIB = 56

ROWS_EVEN_IN = 512
ROWS_FUSED_OUT = 512
ROWS_ODD_IN = 32 * GRID_W
ROWS_ODD_OUT = 8 * GRID_W
S5_INJECT_SUBCHUNKS = 1024
S5_READOUT_SUBCHUNKS = 512
S5_CARRY_SUBCHUNKS = 256


def _cparams(semantics, vmem_mib):
    return pltpu.CompilerParams(dimension_semantics=semantics, vmem_limit_bytes=vmem_mib * 1024 * 1024)


def _silu(x):
    return x * jax.nn.sigmoid(x)


def _modulate(h, mod_ref):
    ms = jnp.mean(h * h, axis=-1, keepdims=True)
    return h * lax.rsqrt(ms + EPS) * (mod_ref[0:1, :] * (1.0 + mod_ref[1:2, :])) + mod_ref[2:3, :]


def _chunk_cumsum(x, reverse):
    n, w = x.shape
    tiles = x.reshape(n // SUBLANES, SUBLANES, w)
    sub = lax.broadcasted_iota(jnp.int32, (1, SUBLANES, 1), 1)
    s = 1
    while s < SUBLANES:
        if reverse:
            tiles = tiles + jnp.where(sub < SUBLANES - s, pltpu.roll(tiles, SUBLANES - s, axis=1), 0.0)
        else:
            tiles = tiles + jnp.where(sub >= s, pltpu.roll(tiles, s, axis=1), 0.0)
        s *= 2
    per_chunk = SCAN_CHUNK // SUBLANES
    edge = 0 if reverse else SUBLANES - 1
    out = [None] * (n // SUBLANES)
    for c in range(n // SCAN_CHUNK):
        run = None
        order = range(per_chunk - 1, -1, -1) if reverse else range(per_chunk)
        for k in order:
            tile = tiles[c * per_chunk + k]
            out[c * per_chunk + k] = tile if run is None else tile + run
            total = tile[edge:edge + 1, :]
            run = total if run is None else run + total
    return jnp.concatenate(out, axis=0)


def _ada_kernel(c_ref, w_ref, b_ref, o_ref):
    o_ref[...] = jnp.dot(_silu(c_ref[...]), w_ref[...], precision=lax.Precision.HIGHEST,
                         preferred_element_type=F32) + b_ref[...]


def _ada(cvecs, ada_w, ada_b):
    depth, d, n = ada_w.shape
    tn = n // 4
    return pl.pallas_call(
        _ada_kernel,
        grid=(depth, n // tn),
        in_specs=[pl.BlockSpec((SUBLANES, d), lambda l, j: (0, 0)),
                  pl.BlockSpec((None, d, tn), lambda l, j: (l, 0, j)),
                  pl.BlockSpec((None, 1, tn), lambda l, j: (l, 0, j))],
        out_specs=pl.BlockSpec((None, SUBLANES, tn), lambda l, j: (l, 0, j)),
        out_shape=jax.ShapeDtypeStruct((depth, SUBLANES, n), F32),
        compiler_params=_cparams(("arbitrary", "arbitrary"), VMEM_STREAMING_MIB),
        name="ada",
    )(cvecs, ada_w, ada_b.reshape(depth, 1, n))


EVEN_TN = 1024
HALO = 2 * SUBLANES
EVEN_TILES = (("conv", 0, None), ("conv_dt", 1, None), ("silu", None, 0), ("copy", None, 1),
              ("gate", 2, 4), ("gate", 3, 5), ("silu", None, 2), ("silu", None, 3))


def _even_in_kernel(h_ref, hp_ref, hn_ref, mod_ref, w_ref, cw_ref, cb_ref, dtc_ref, lb_ref, za_ref, zb_ref, a_ref):
    i = pl.program_id(0)
    tm = h_ref.shape[0]
    n_lb = lb_ref.shape[1]
    a_ref[:HALO] = _modulate(hp_ref[...], mod_ref).astype(BF16)
    a_ref[HALO:HALO + tm] = _modulate(h_ref[...], mod_ref).astype(BF16)
    a_ref[HALO + tm:] = _modulate(hn_ref[...], mod_ref).astype(BF16)
    row = lax.broadcasted_iota(jnp.int32, (tm + 2 * HALO, 1), 0)
    outside = ((row < HALO) & (i == 0)) | ((row >= HALO + tm) & (i == pl.num_programs(0) - 1))

    def conv_silu(col0, lanes):
        acc = jnp.dot(a_ref[...], w_ref[:, col0:col0 + lanes], preferred_element_type=F32)
        acc = jnp.where(outside, 0.0, acc)
        conv = cb_ref[:, col0:col0 + lanes]
        for k in range(SSD_CONV):
            off = HALO - SSD_CONV // 2 + k
            conv = conv + cw_ref[k:k + 1, col0:col0 + lanes] * acc[off:off + tm]
        return _silu(conv)

    def main_dot(col0, lanes=EVEN_TN):
        return jnp.dot(a_ref[HALO:HALO + tm], w_ref[:, col0:col0 + lanes], preferred_element_type=F32)

    for tile, (kind, fa, fb) in enumerate(EVEN_TILES):
        col0 = tile * EVEN_TN
        if kind == "conv":
            za_ref[:, fa * EVEN_TN:(fa + 1) * EVEN_TN] = conv_silu(col0, EVEN_TN)
        elif kind == "conv_dt":
            n_c = ZS_DT - col0
            za_ref[:, col0:ZS_DT] = conv_silu(col0, n_c)
            dt = main_dot(ZS_DT, 2 * LANES)
            for d in range(2):
                za_ref[:, ZS_DT + d * LANES:ZS_DT + (d + 1) * LANES] = jax.nn.softplus(
                    dt[:, d * LANES:(d + 1) * LANES] + dtc_ref[d, 0:1, :])
            za_ref[:, ZS_XD:(fa + 1) * EVEN_TN] = jnp.zeros((tm, (fa + 1) * EVEN_TN - ZS_XD), F32)
        elif kind == "gate":
            lb = lb_ref[:, (col0 - ZS_F) % n_lb:(col0 - ZS_F) % n_lb + EVEN_TN]
            f = lb + (1.0 - lb) * jax.nn.sigmoid(main_dot(col0))
            backward = (col0 - ZS_F) // n_lb == 1
            za_ref[:, fa * EVEN_TN:(fa + 1) * EVEN_TN] = _chunk_cumsum(jnp.log(f), backward)
            zb_ref[:, fb * EVEN_TN:(fb + 1) * EVEN_TN] = (1.0 - f).astype(BF16)
        elif kind == "silu":
            zb_ref[:, fb * EVEN_TN:(fb + 1) * EVEN_TN] = _silu(main_dot(col0)).astype(BF16)
        else:
            zb_ref[:, fb * EVEN_TN:(fb + 1) * EVEN_TN] = main_dot(col0).astype(BF16)


def _even_in(h, mod, w, cw, cb, dtc, lb):
    t, d = h.shape
    tm = min(t, ROWS_EVEN_IN)
    hb = tm // HALO
    n_a = (1 + max(fa for _, fa, _ in EVEN_TILES if fa is not None)) * EVEN_TN
    n_b = (1 + max(fb for _, _, fb in EVEN_TILES if fb is not None)) * EVEN_TN
    whole = lambda a, **kw: pl.BlockSpec(a.shape, lambda i: (0,) * a.ndim, **kw)
    return pl.pallas_call(
        _even_in_kernel,
        grid=(t // tm,),
        in_specs=[pl.BlockSpec((tm, d), lambda i: (i, 0)),
                  pl.BlockSpec((HALO, d), lambda i: (jnp.maximum(i * hb - 1, 0), 0)),
                  pl.BlockSpec((HALO, d), lambda i: (jnp.minimum((i + 1) * hb, t // HALO - 1), 0)),
                  whole(mod), whole(w, pipeline_mode=pl.Buffered(1)),
                  whole(cw), whole(cb), whole(dtc), whole(lb)],
        out_specs=[pl.BlockSpec((tm, n_a), lambda i: (i, 0)), pl.BlockSpec((tm, n_b), lambda i: (i, 0))],
        out_shape=[jax.ShapeDtypeStruct((t, n_a), F32), jax.ShapeDtypeStruct((t, n_b), BF16)],
        scratch_shapes=[pltpu.VMEM((tm + 2 * HALO, d), BF16)],
        compiler_params=_cparams(("arbitrary",), VMEM_RESIDENT_MIB),
        name="even_in",
    )(h, h, h, mod, w, cw, cb, dtc, lb)


def _scan_block_prep(xd_ref, qv_ref, bcum_ref, kk_ref, dtc_ref, e_ref, d):
    Q = SCAN_CHUNK
    reverse = bool(d)
    dtv = xd_ref[:, ZS_DT + d * LANES:ZS_DT + (d + 1) * LANES]
    acum = _chunk_cumsum(-dtv * jnp.exp(dtc_ref[d, 1:2, :]), reverse)
    rows = acum.shape[0]
    nchunk = rows // Q
    edge = 0 if reverse else Q - 1
    lasts = [acum[c * Q + edge:c * Q + edge + 1, :] for c in range(nchunk)]
    last_rows = jnp.concatenate([jnp.broadcast_to(l, (Q, LANES)) for l in lasts], axis=0)
    e_last = jnp.concatenate([jnp.broadcast_to(jnp.exp(l), (SUBLANES, LANES)) for l in lasts], axis=0)
    terms, rem = [], e_last
    for _ in range(3):
        hi = rem.astype(BF16)
        terms.append(hi)
        rem = rem - hi.astype(F32)
    lhs = jnp.concatenate([dtv.astype(BF16), jnp.exp(acum).astype(BF16), jnp.exp(last_rows - acum).astype(BF16)]
                          + terms, axis=0)
    xe = jnp.dot(lhs, e_ref[...], preferred_element_type=F32)
    n8 = SUBLANES * nchunk
    el = xe[3 * rows:]
    expanded = (xe[:rows], xe[rows:2 * rows], xe[2 * rows:3 * rows], el[:n8] + el[n8:2 * n8] + el[2 * n8:])
    return xd_ref, qv_ref, bcum_ref, kk_ref, acum, expanded


DECAY_LIMIT = 80.0


def _hg_intra_exact(qs, bcum, bcum_ref, kk_ref, v_ref, acc_ref, row0, reverse):
    Q = SCAN_CHUNK
    acc_ref[...] = jnp.zeros_like(acc_ref)
    t_idx = lax.broadcasted_iota(jnp.int32, (Q, 1), 0)

    def body(s, carry):
        r = row0 + s
        b_s, k_s, v_s = bcum_ref[pl.ds(r, 1), :], kk_ref[pl.ds(r, 1), :], v_ref[pl.ds(r, 1), :]
        allowed = (t_idx <= s) if reverse else (t_idx >= s)
        w = jnp.where(allowed, jnp.exp(jnp.minimum(bcum - b_s, 0.0)) * (qs * k_s), 0.0)
        for h in range(HG_HEADS):
            sl = slice(h * HG_HEAD_DIM, (h + 1) * HG_HEAD_DIM)
            acc_ref[:, sl] += jnp.sum(w[:, sl], axis=-1, keepdims=True) * v_s[:, sl]
        return carry

    lax.fori_loop(0, Q, body, 0)
    return acc_ref[...]


def _scan_chunk(prep, rows, dsk_ref, st_ref, gt_ref, yo_ref, d, reverse, exact=None):
    Q = SCAN_CHUNK
    xd_ref, qv_ref, bcum_ref, kk_ref, acum, expanded = prep
    n_hg = bcum_ref.shape[1]
    ti = lax.broadcasted_iota(jnp.int32, (Q, Q), 0)
    si = lax.broadcasted_iota(jnp.int32, (Q, Q), 1)
    mask = (ti <= si) if reverse else (ti >= si)
    edge = 0 if reverse else Q - 1
    width = SSD_HEADS * SSD_HEAD_DIM
    ci = rows.start // Q
    dt_x, ea_x, wend_x, elast_x = expanded
    chunk_x = (dt_x[rows], ea_x[rows], wend_x[rows], elast_x[ci * SUBLANES:ci * SUBLANES + 1])
    _scan_chunk_ssd(xd_ref, chunk_x, acum[rows], rows, dsk_ref, st_ref, yo_ref, d, reverse, mask)
    if exact is not None:
        intra = _hg_intra_exact(qv_ref[rows, :n_hg].astype(F32), bcum_ref[rows, :], *exact, rows.start, reverse)
    hw = n_hg // 2
    for part in range(2):
        cols = slice(part * hw, (part + 1) * hw)
        kk, bcum = kk_ref[rows, cols].astype(F32), bcum_ref[rows, cols]
        qs = qv_ref[rows, cols].astype(F32)
        vb = qv_ref[rows, n_hg + part * hw:n_hg + (part + 1) * hw]
        blast = bcum[edge:edge + 1, :]
        qe = (qs * jnp.exp(bcum)).astype(BF16)
        k_end = kk * jnp.exp(blast - bcum)
        if exact is None:
            k_til = (k_end * jnp.exp(-blast)).astype(BF16)
        k_end = k_end.astype(BF16)
        e_blast = jnp.exp(blast)
        outs = []
        for hl in range(hw // HG_HEAD_DIM):
            h = part * (hw // HG_HEAD_DIM) + hl
            sl = slice(hl * HG_HEAD_DIM, (hl + 1) * HG_HEAD_DIM)
            gt = gt_ref[d, h]
            if exact is None:
                att = lax.dot_general(qe[:, sl], k_til[:, sl], (((1,), (1,)), ((), ())), preferred_element_type=F32)
                att = jnp.where(mask, att, 0.0).astype(BF16)
                o_h = jnp.dot(att, vb[:, sl], preferred_element_type=F32)
            else:
                o_h = intra[:, h * HG_HEAD_DIM:(h + 1) * HG_HEAD_DIM]
            o_h = o_h + lax.dot_general(qe[:, sl], gt.astype(BF16), (((1,), (1,)), ((), ())),
                                        preferred_element_type=F32)
            outs.append(o_h)
            upd = lax.dot_general(vb[:, sl], k_end[:, sl], (((0,), (0,)), ((), ())), preferred_element_type=F32)
            gt_ref[d, h] = gt * e_blast[:, sl] + upd
        yo_ref[rows, width + part * hw:width + (part + 1) * hw] = jnp.concatenate(outs, axis=1).astype(yo_ref.dtype)


def _scan_chunk_ssd(xd_ref, chunk_x, acum, rows, dsk_ref, st_ref, yo_ref, d, reverse, mask):
    Q = SCAN_CHUNK
    width = SSD_HEADS * SSD_HEAD_DIM
    gw = width // SSD_GROUPS
    dt_all, ea_all, wend_all, elast_all = chunk_x
    acum_t = acum.T
    lane = lax.broadcasted_iota(jnp.int32, (Q, LANES), 1)
    lo = lane < SSD_HEAD_DIM
    zero_b = jnp.zeros((Q, LANES), BF16)
    for g in range(SSD_GROUPS):
        cols = slice(g * gw, (g + 1) * gw)
        dt_x, ea_x, wend_x, elast_x = dt_all[:, cols], ea_all[:, cols], wend_all[:, cols], elast_all[:, cols]
        xs = xd_ref[rows, cols]
        xdt = xs * dt_x
        xw = (xdt * wend_x).astype(BF16)
        xdt_b = xdt.astype(BF16)
        bm = xd_ref[rows, width + g * SSD_STATE:width + (g + 1) * SSD_STATE].astype(BF16)
        cm = xd_ref[rows, width + (SSD_GROUPS + g) * SSD_STATE:width + (SSD_GROUPS + g + 1) * SSD_STATE].astype(BF16)
        cb = lax.dot_general(cm, bm, (((1,), (1,)), ((), ())), preferred_element_type=F32)
        y_state = jnp.dot(cm, st_ref[d, g].astype(BF16), preferred_element_type=F32)
        pairs = []
        for hp in range(gw // LANES):
            xp = xdt_b[:, hp * LANES:(hp + 1) * LANES]
            acc = None
            for half in range(2):
                h = (g * gw + hp * LANES) // SSD_HEAD_DIM + half
                seg = acum[:, h:h + 1] - acum_t[h:h + 1, :]
                m = jnp.where(mask, cb * jnp.exp(seg), 0.0).astype(BF16)
                xh = jnp.where(lo, xp, zero_b) if half == 0 else jnp.where(lo, zero_b, xp)
                part = jnp.dot(m, xh, preferred_element_type=F32)
                acc = part if acc is None else acc + part
            pairs.append(acc)
        y = jnp.concatenate(pairs, axis=1) + y_state * ea_x
        if not reverse:
            y = y + dsk_ref[:, cols] * xs
        yo_ref[rows, cols] = y.astype(yo_ref.dtype)
        upd = lax.dot_general(bm, xw, (((0,), (0,)), ((), ())), preferred_element_type=F32)
        st_ref[d, g] = st_ref[d, g] * elast_x + upd


def _even_scan_kernel(*refs, nblk, cpb):
    fwd_in, bwd_in = refs[0:4], refs[4:8]
    dtc_ref, dsk_ref, e_ref, h0s_ref, h0g_ref = refs[8:13]
    yof_ref, yob_ref, st_ref, gt_ref, kkf_ref, vf_ref, acc_ref = refs[13:20]
    c = pl.program_id(0)

    @pl.when(c == 0)
    def _():
        st_ref[...] = h0s_ref[...]
        gt_ref[...] = h0g_ref[...]

    streams = ((False, fwd_in, yof_ref), (True, bwd_in, yob_ref))

    def run(exact):
        preps = [_scan_block_prep(*ins, dtc_ref, e_ref, d) for d, (_, ins, _) in enumerate(streams)]
        if exact:
            for d, (_, ins, _) in enumerate(streams):
                kkf_ref[d] = ins[3][...].astype(F32)
                vf_ref[d] = ins[1][:, ins[2].shape[1]:].astype(F32)
        for step in range(cpb):
            for d, (reverse, ins, yo_ref) in enumerate(streams):
                ci = (cpb - 1 - step) if reverse else step
                rows = slice(ci * SCAN_CHUNK, (ci + 1) * SCAN_CHUNK)
                refs_exact = (ins[2], kkf_ref.at[d], vf_ref.at[d], acc_ref) if exact else None
                _scan_chunk(preps[d], rows, dsk_ref, st_ref, gt_ref, yo_ref, d, reverse, refs_exact)

    totals = [ref[ci * SCAN_CHUNK + edge:ci * SCAN_CHUNK + edge + 1, :]
              for ref, edge in ((fwd_in[2], SCAN_CHUNK - 1), (bwd_in[2], 0)) for ci in range(cpb)]
    strongest = jnp.min(jnp.concatenate(totals, axis=0))
    pl.when(strongest >= -DECAY_LIMIT)(lambda: run(False))
    pl.when(strongest < -DECAY_LIMIT)(lambda: run(True))


def _even_scan(za, zb, consts, h0s, h0g):
    t = za.shape[0]
    nch = t // SCAN_CHUNK
    cpb = max(k for k in (4, 2, 1) if nch % k == 0)
    rows = cpb * SCAN_CHUNK
    nblk = nch // cpb
    dtc, dsk, e16 = consts
    n_x = SSD_HEADS * SSD_HEAD_DIM
    n_hg = HG_HEADS * HG_HEAD_DIM
    gate_a, gate_b = (tuple(tile[i] for tile in EVEN_TILES if tile[0] == "gate") for i in (1, 2))

    def stream_specs(bidx, d):
        return [
            pl.BlockSpec((rows, ZS_XD), lambda c: (bidx(c), 0)),
            pl.BlockSpec((rows, 2 * n_hg), lambda c: (bidx(c), 0)),
            pl.BlockSpec((rows, n_hg), lambda c: (bidx(c), gate_a[d])),
            pl.BlockSpec((rows, n_hg), lambda c: (bidx(c), gate_b[d])),
        ]

    fwd = lambda c: c
    bwd = lambda c: nblk - 1 - c
    whole = lambda a: pl.BlockSpec(a.shape, lambda c: (0,) * a.ndim)
    in_specs = stream_specs(fwd, 0) + stream_specs(bwd, 1) + [whole(a) for a in (dtc, dsk, e16, h0s, h0g)]
    out_specs = [pl.BlockSpec((rows, n_x + n_hg), lambda c: (c, 0)),
                 pl.BlockSpec((rows, n_x + n_hg), lambda c: (bwd(c), 0)), whole(h0s), whole(h0g)]
    out_shape = [jax.ShapeDtypeStruct((t, n_x + n_hg), BF16), jax.ShapeDtypeStruct((t, n_x + n_hg), BF16),
                 jax.ShapeDtypeStruct(h0s.shape, F32), jax.ShapeDtypeStruct(h0g.shape, F32)]
    return pl.pallas_call(
        functools.partial(_even_scan_kernel, nblk=nblk, cpb=cpb),
        grid=(nblk,),
        in_specs=in_specs,
        out_specs=out_specs,
        out_shape=out_shape,
        scratch_shapes=[pltpu.VMEM((2, rows, n_hg), F32), pltpu.VMEM((2, rows, n_hg), F32),
                        pltpu.VMEM((SCAN_CHUNK, n_hg), F32)],
        compiler_params=_cparams(("arbitrary",), VMEM_MATRICES_MIB),
        name="even_scan",
    )(za, zb, za, zb, za, zb, za, zb, dtc, dsk, e16, h0s, h0g)


def _ffn_block(h, mod_ref, wg_ref, wu_ref, wd_ref, final):
    a = _modulate(h, mod_ref).astype(BF16)
    gate = jnp.dot(a, wg_ref[...], preferred_element_type=F32)
    up = jnp.dot(a, wu_ref[...], preferred_element_type=F32)
    act = (_silu(gate) * up).astype(BF16)
    out = h + mod_ref[3:4, :] * jnp.dot(act, wd_ref[...], preferred_element_type=F32)
    if final:
        ms = jnp.mean(out * out, axis=-1, keepdims=True)
        out = out * lax.rsqrt(ms + EPS) * mod_ref[4:5, :]
    return out


def _resident(a):
    return pl.BlockSpec(a.shape, lambda i: (0,) * a.ndim, pipeline_mode=pl.Buffered(1))


def _even_out_kernel(h_ref, z_ref, g_ref, yof_ref, yob_ref, w_ref, vec_ref, mod_ref, wg_ref, wu_ref, wd_ref, o_ref):
    n_x = z_ref.shape[1]
    yo = yof_ref[...].astype(F32) + yob_ref[...].astype(F32)
    y = yo[:, :n_x] * z_ref[...].astype(F32)
    o = yo[:, n_x:]
    gw = n_x // SSD_GROUPS
    parts = []
    for g in range(SSD_GROUPS):
        yg = y[:, g * gw:(g + 1) * gw]
        ms = jnp.mean(yg * yg, axis=-1, keepdims=True)
        parts.append(yg * lax.rsqrt(ms + EPS))
    yn = (jnp.concatenate(parts, axis=1) * vec_ref[0:1, :]).astype(BF16)
    parts = []
    for hh in range(HG_HEADS):
        oh = o[:, hh * HG_HEAD_DIM:(hh + 1) * HG_HEAD_DIM]
        ms = jnp.mean(oh * oh, axis=-1, keepdims=True)
        parts.append(oh * lax.rsqrt(ms + EPS))
    on = (jnp.concatenate(parts, axis=1) * vec_ref[1:2, :] * g_ref[...].astype(F32)).astype(BF16)
    half = yn.shape[1]
    mix = jnp.dot(yn, w_ref[:half, :], preferred_element_type=F32)
    mix = mix + jnp.dot(on, w_ref[half:, :], preferred_element_type=F32)
    h1 = h_ref[...] + vec_ref[2:3, :] * mix
    o_ref[...] = _ffn_block(h1, mod_ref, wg_ref, wu_ref, wd_ref, final=False)


def _even_out(h, zb, yo_f, yo_b, w_out, vec, mod, wg, wu, wd):
    t, d = h.shape
    tm = min(t, ROWS_FUSED_OUT)
    n_x = SSD_HEADS * SSD_HEAD_DIM
    n_hg = HG_HEADS * HG_HEAD_DIM
    row = lambda i: (i, 0)
    z_tile, g_tile = (next(fb for k, (_, _, fb) in enumerate(EVEN_TILES) if k * EVEN_TN == start)
                      for start in (ZS_Z, ZS_G))
    return pl.pallas_call(
        _even_out_kernel,
        grid=(t // tm,),
        in_specs=[pl.BlockSpec((tm, d), row),
                  pl.BlockSpec((tm, n_x), lambda i: (i, z_tile)),
                  pl.BlockSpec((tm, n_hg), lambda i: (i, g_tile)),
                  pl.BlockSpec((tm, n_x + n_hg), row), pl.BlockSpec((tm, n_x + n_hg), row),
                  _resident(w_out), _resident(vec), _resident(mod), _resident(wg), _resident(wu), _resident(wd)],
        out_specs=pl.BlockSpec((tm, d), row),
        out_shape=jax.ShapeDtypeStruct((t, d), F32),
        compiler_params=_cparams(("arbitrary",), VMEM_RESIDENT_MIB),
        name="even_out_ffn",
    )(h, zb, zb, yo_f, yo_b, w_out, vec, mod, wg, wu, wd)


def _odd_in_kernel(h_ref, mod_ref, w_ref, o_ref, u_ref):
    a = _modulate(h_ref[...], mod_ref).astype(BF16)
    u = jnp.dot(a, w_ref[...], preferred_element_type=F32)
    nj, grid_w, rb, _ = o_ref.shape
    for j in range(nj):
        u_ref[j] = u[:, j * LANES:(j + 1) * LANES]
    for wi in range(grid_w):
        for j in range(nj):
            o_ref[j, wi] = u_ref[j, pl.ds(wi, rb, stride=grid_w), :]


def _odd_in(h, mod, w, grid_w):
    t, d = h.shape
    rows = t // grid_w
    rb = min(rows, ROWS_ODD_IN // GRID_W)
    n = w.shape[1]
    nj = n // LANES
    out = pl.pallas_call(
        _odd_in_kernel,
        grid=(rows // rb,),
        in_specs=[pl.BlockSpec((rb * grid_w, d), lambda i: (i, 0)),
                  pl.BlockSpec((SUBLANES, d), lambda i: (0, 0)),
                  pl.BlockSpec(w.shape, lambda i: (0, 0))],
        out_specs=pl.BlockSpec((nj, grid_w, rb, LANES), lambda i: (0, 0, i, 0)),
        out_shape=jax.ShapeDtypeStruct((nj, grid_w, rows, LANES), F32),
        scratch_shapes=[pltpu.VMEM((nj, rb * grid_w, LANES), F32)],
        compiler_params=_cparams(("arbitrary",), VMEM_MATRICES_MIB),
        name="odd_in",
    )(h, mod, w)
    return out.reshape(nj, t, LANES)


def _sub_chunk_rows(u_ref, nb):
    return [u_ref[pl.ds(s, nb, stride=S5_SUB), :] for s in range(S5_SUB)]


def _s5_inject_kernel(u_ref, win_ref, s_ref):
    x = jnp.concatenate(_sub_chunk_rows(u_ref, s_ref.shape[0]), axis=1).astype(BF16)
    s_ref[...] = jnp.dot(x, win_ref[...], preferred_element_type=F32)


def _s5_inject(u, win):
    nj, t, _ = u.shape
    n = t // S5_SUB
    nb = min(n, S5_INJECT_SUBCHUNKS)
    width, ns = win.shape[1:]
    return pl.pallas_call(
        _s5_inject_kernel,
        grid=(nj, n // nb),
        in_specs=[pl.BlockSpec((None, nb * S5_SUB, LANES), lambda j, b: (j, b, 0)),
                  pl.BlockSpec((None, width, ns), lambda j, b: (j, 0, 0))],
        out_specs=pl.BlockSpec((None, nb, ns), lambda j, b: (j, b, 0)),
        out_shape=jax.ShapeDtypeStruct((nj, n, ns), F32),
        compiler_params=_cparams(("arbitrary", "arbitrary"), VMEM_STREAMING_MIB),
        name="s5_inject",
    )(u, win)


def _s5_carry_kernel(sf_ref, sb_ref, lam_ref, h0_ref, pf_ref, pb_ref, hf_ref, tf_ref, tb_ref, c_ref):
    b = pl.program_id(0)
    nj, kb, ns = sf_ref.shape
    half = ns // 2

    @pl.when(b == 0)
    def _():
        c_ref[...] = h0_ref[...]

    for j in range(nj):
        tf_ref[:, j, :] = sf_ref[j]
        tb_ref[:, j, :] = sb_ref[j]

    def advance(d, c, s):
        lr, li = lam_ref[d, :, :half], lam_ref[d, :, half:]
        pr, pi = c[:, :half], c[:, half:]
        return jnp.concatenate([lr * pr - li * pi + s[:, :half], lr * pi + li * pr + s[:, half:]], axis=1)

    def step(i, carry):
        cf, cb = carry
        kr = kb - 1 - i
        sf = tf_ref[i]
        sb = tb_ref[kr]
        tf_ref[i] = cf
        tb_ref[kr] = cb
        return advance(0, cf, sf), advance(1, cb, sb)

    cf, cb = lax.fori_loop(0, kb, step, (c_ref[0], c_ref[1]), unroll=4)
    c_ref[0] = cf
    c_ref[1] = cb
    for j in range(nj):
        pf_ref[j] = tf_ref[:, j, :]
        pb_ref[j] = tb_ref[:, j, :]

    @pl.when(b == pl.num_programs(0) - 1)
    def _():
        hf_ref[...] = c_ref[...]


def _s5_carry(s_f, s_b, lam, h0):
    nj, n, ns = s_f.shape
    kb = min(n, S5_CARRY_SUBCHUNKS)
    nblk = n // kb
    fwd = pl.BlockSpec((nj, kb, ns), lambda b: (0, b, 0))
    bwd = pl.BlockSpec((nj, kb, ns), lambda b: (0, nblk - 1 - b, 0))
    whole = pl.BlockSpec(lam.shape, lambda b: (0, 0, 0))
    return pl.pallas_call(
        _s5_carry_kernel,
        grid=(nblk,),
        in_specs=[fwd, bwd, whole, whole],
        out_specs=[fwd, bwd, whole],
        out_shape=[jax.ShapeDtypeStruct(s_f.shape, F32), jax.ShapeDtypeStruct(s_b.shape, F32),
                   jax.ShapeDtypeStruct(lam.shape, F32)],
        scratch_shapes=[pltpu.VMEM((kb, nj, ns), F32), pltpu.VMEM((kb, nj, ns), F32), pltpu.VMEM(lam.shape, F32)],
        compiler_params=_cparams(("arbitrary",), VMEM_MATRICES_MIB),
        name="s5_carry",
    )(s_f, s_b, lam, h0)


def _s5_readout_kernel(u_ref, p_ref, wout_ref, tt_ref, add_ref, y_ref, *, reverse):
    nb = p_ref.shape[0]
    us = _sub_chunk_rows(u_ref, nb)
    xb = jnp.concatenate(us, axis=1).astype(BF16)
    pb = p_ref[...].astype(BF16)
    width = xb.shape[1]
    tile = 2 * LANES
    for m in range(width // tile):
        cols = slice(m * tile, (m + 1) * tile)
        rows = slice(m * tile, width) if reverse else slice(0, (m + 1) * tile)
        y = jnp.dot(pb, wout_ref[:, cols], preferred_element_type=F32)
        y = y + jnp.dot(xb[:, rows], tt_ref[rows, cols], preferred_element_type=F32)
        for q in range(tile // LANES):
            s = m * (tile // LANES) + q
            other = add_ref[pl.ds(s, nb, stride=S5_SUB), :] if reverse else add_ref[...] * us[s]
            y_ref[pl.ds(s, nb, stride=S5_SUB), :] = y[:, q * LANES:(q + 1) * LANES] + other


def _s5_readout(u, p, wout, tt, add, reverse):
    nj, t, _ = u.shape
    n = t // S5_SUB
    nb = min(n, S5_READOUT_SUBCHUNKS)
    ns, width = wout.shape[1:]
    tokens = pl.BlockSpec((None, nb * S5_SUB, LANES), lambda j, b: (j, b, 0))
    add_spec = tokens if reverse else pl.BlockSpec((None, 1, LANES), lambda j, b: (j, 0, 0))
    return pl.pallas_call(
        functools.partial(_s5_readout_kernel, reverse=reverse),
        grid=(nj, n // nb),
        in_specs=[tokens,
                  pl.BlockSpec((None, nb, ns), lambda j, b: (j, b, 0)),
                  pl.BlockSpec((None, ns, width), lambda j, b: (j, 0, 0)),
                  pl.BlockSpec((None, width, width), lambda j, b: (j, 0, 0)),
                  add_spec],
        out_specs=tokens,
        out_shape=jax.ShapeDtypeStruct(u.shape, F32),
        compiler_params=_cparams(("arbitrary", "arbitrary"), VMEM_RESIDENT_MIB),
        name="s5_readout_bwd" if reverse else "s5_readout_fwd",
    )(u, p, wout, tt, add)


def _odd_out_kernel(h_ref, y_ref, wv_ref, wgl_ref, vec_ref, mod_ref, wg_ref, wu_ref, wd_ref, o_ref, a_ref):
    nj, grid_w, rb, _ = y_ref.shape
    for wi in range(grid_w):
        for j in range(nj):
            a_ref[j, pl.ds(wi, rb, stride=grid_w), :] = y_ref[j, wi]
    a = jax.nn.gelu(jnp.concatenate([a_ref[j] for j in range(nj)], axis=1)).astype(BF16)
    val = jnp.dot(a, wv_ref[...], preferred_element_type=F32)
    gate = jnp.dot(a, wgl_ref[...], preferred_element_type=F32)
    h3 = h_ref[...] + vec_ref[0:1, :] * (val * jax.nn.sigmoid(gate))
    o_ref[...] = _ffn_block(h3, mod_ref, wg_ref, wu_ref, wd_ref, final=True)


def _odd_out(h, y, w_val, w_gate, vec, mod, wg, wu, wd, grid_w):
    t, d = h.shape
    rows = t // grid_w
    rb = min(rows, ROWS_ODD_OUT // GRID_W)
    nj = y.shape[0]
    tok = pl.BlockSpec((rb * grid_w, d), lambda i: (i, 0))
    return pl.pallas_call(
        _odd_out_kernel,
        grid=(rows // rb,),
        in_specs=[tok, pl.BlockSpec((nj, grid_w, rb, LANES), lambda i: (0, 0, i, 0)),
                  _resident(w_val), _resident(w_gate), _resident(vec), _resident(mod),
                  _resident(wg), _resident(wu), _resident(wd)],
        out_specs=tok,
        out_shape=jax.ShapeDtypeStruct((t, d), F32),
        scratch_shapes=[pltpu.VMEM((nj, rb * grid_w, LANES), F32)],
        compiler_params=_cparams(("arbitrary",), VMEM_RESIDENT_MIB),
        name="odd_out_ffn",
    )(h, y.reshape(nj, grid_w, rows, LANES), w_val, w_gate, vec, mod, wg, wu, wd)


def _s5_prep_kernel(prm_ref, bre_ref, bim_ref, colp_ref, cre_ref, cim_ref, win_ref, wout_ref, tt_ref, lam_ref, *,
                    reverse):
    T = S5_SUB
    ns = prm_ref.shape[1]
    a_re, a_im, dt = prm_ref[0:1, :], prm_ref[1:2, :], jnp.exp(prm_ref[2:3, :])
    tau = lax.broadcasted_iota(jnp.int32, (3 * SUBLANES, ns), 0).astype(F32)
    mag = jnp.exp(a_re * dt * tau)
    pr = mag * jnp.cos(a_im * dt * tau)
    pi = mag * jnp.sin(a_im * dt * tau)
    lr, li = pr[1:2], pi[1:2]
    den = a_re * a_re + a_im * a_im
    cr = ((lr - 1.0) * a_re + li * a_im) / den
    ci = (li * a_re - (lr - 1.0) * a_im) / den
    row_g = lax.broadcasted_iota(jnp.int32, (LANES, ns), 0) // S5_GROUP
    col_g = lax.broadcasted_iota(jnp.int32, (LANES, ns), 1) // S5_STATE
    same = row_g == col_g
    bbr = jnp.where(same, cr * bre_ref[...] - ci * bim_ref[...], 0.0)
    bbi = jnp.where(same, cr * bim_ref[...] + ci * bre_ref[...], 0.0)
    for s in range(T):
        e = s if reverse else T - 1 - s
        rows = slice(s * LANES, (s + 1) * LANES)
        win_ref[rows, :ns] = (pr[e:e + 1] * bbr - pi[e:e + 1] * bbi).astype(BF16)
        win_ref[rows, ns:] = (pr[e:e + 1] * bbi + pi[e:e + 1] * bbr).astype(BF16)
    lam_ref[...] = jnp.broadcast_to(jnp.concatenate([pr[T:T + 1], pi[T:T + 1]], axis=1), lam_ref.shape)
    dt_c = jnp.exp(colp_ref[2])
    mag_c = jnp.exp(colp_ref[0] * dt_c)
    lr_c = mag_c * jnp.cos(colp_ref[1] * dt_c)
    li_c = mag_c * jnp.sin(colp_ref[1] * dt_c)
    row_gc = lax.broadcasted_iota(jnp.int32, (ns, LANES), 0) // S5_STATE
    col_gc = lax.broadcasted_iota(jnp.int32, (ns, LANES), 1) // S5_GROUP
    same_c = row_gc == col_gc
    c_re = jnp.where(same_c, cre_ref[...], 0.0)
    c_im = jnp.where(same_c, cim_ref[...], 0.0)
    bb = jnp.concatenate([bbr, bbi], axis=1)
    bb_hi = bb.astype(BF16)
    bb_lo = (bb - bb_hi.astype(F32)).astype(BF16)
    cur_r = jnp.ones((ns, LANES), F32)
    cur_i = jnp.zeros((ns, LANES), F32)
    kblk = []
    for e in range(T + 1):
        wo_r = c_re * cur_r - c_im * cur_i
        wo_i = -(c_re * cur_i + c_im * cur_r)
        if e >= 1:
            t = (T - e) if reverse else e - 1
            wout_ref[:ns, t * LANES:(t + 1) * LANES] = wo_r.astype(BF16)
            wout_ref[ns:, t * LANES:(t + 1) * LANES] = wo_i.astype(BF16)
        if e < T:
            wo = jnp.concatenate([wo_r, wo_i], axis=0)
            wo_hi = wo.astype(BF16)
            kblk.append((wo_hi, (wo - wo_hi.astype(F32)).astype(BF16)))
        cur_r, cur_i = cur_r * lr_c - cur_i * li_c, cur_r * li_c + cur_i * lr_c
    wo_hi = jnp.concatenate([hi for hi, _ in kblk], axis=1)
    wo_lo = jnp.concatenate([lo for _, lo in kblk], axis=1)
    k_all = jnp.dot(bb_hi, wo_hi, preferred_element_type=F32)
    k_all = k_all + jnp.dot(bb_hi, wo_lo, preferred_element_type=F32) + jnp.dot(bb_lo, wo_hi, preferred_element_type=F32)
    kblk = [k_all[:, e * LANES:(e + 1) * LANES].astype(BF16) for e in range(T)]
    zeros = jnp.zeros((LANES, LANES), BF16)
    for s in range(T):
        for t in range(T):
            lag = (s - t) if reverse else (t - s)
            tt_ref[s * LANES:(s + 1) * LANES, t * LANES:(t + 1) * LANES] = kblk[lag] if lag >= 0 else zeros


def _s5_prep(a_re, a_im, log_dt, b_re, b_im, c_re, c_im, reverse):
    g, p = a_re.shape
    cdim = b_re.shape[2]
    nj = g // S5_GPB
    ns = S5_GPB * p
    width = S5_SUB * LANES
    prm = jnp.zeros((nj, SUBLANES, ns), F32)
    prm = prm.at[:, 0].set(a_re.reshape(nj, ns)).at[:, 1].set(a_im.reshape(nj, ns))
    prm = prm.at[:, 2].set(jnp.repeat(log_dt, p).reshape(nj, ns))
    colp = jnp.broadcast_to(prm[:, :3, :, None], (nj, 3, ns, LANES))
    row_tile = lambda b: jnp.tile(b.reshape(nj, S5_GPB, p, cdim).transpose(0, 3, 1, 2).reshape(nj, 1, cdim, ns),
                                  (1, S5_GPB, 1, 1)).reshape(nj, LANES, ns)
    col_tile = lambda c_: jnp.tile(c_.reshape(nj, S5_GPB, cdim, p).transpose(0, 1, 3, 2).reshape(nj, ns, 1, cdim),
                                   (1, 1, S5_GPB, 1)).reshape(nj, ns, LANES)
    blk = lambda *shape: pl.BlockSpec((None,) + shape, lambda j: (j,) + (0,) * len(shape))
    return pl.pallas_call(
        functools.partial(_s5_prep_kernel, reverse=reverse),
        grid=(nj,),
        in_specs=[blk(SUBLANES, ns), blk(LANES, ns), blk(LANES, ns), blk(3, ns, LANES), blk(ns, LANES),
                  blk(ns, LANES)],
        out_specs=[blk(width, 2 * ns), blk(2 * ns, width), blk(width, width), blk(SUBLANES, 2 * ns)],
        out_shape=[jax.ShapeDtypeStruct((nj, width, 2 * ns), BF16), jax.ShapeDtypeStruct((nj, 2 * ns, width), BF16),
                   jax.ShapeDtypeStruct((nj, width, width), BF16), jax.ShapeDtypeStruct((nj, SUBLANES, 2 * ns), F32)],
        compiler_params=_cparams(("arbitrary",), VMEM_RESIDENT_MIB),
        name="s5_prep_bwd" if reverse else "s5_prep_fwd",
    )(prm, row_tile(b_re), row_tile(b_im), colp, col_tile(c_re), col_tile(c_im))


def _regroup_kernel(w_ref, o_ref, *, pieces):
    col = 0
    for start, width in pieces:
        if start is None:
            o_ref[:, col:col + width] = jnp.zeros((o_ref.shape[0], width), o_ref.dtype)
        else:
            o_ref[:, col:col + width] = w_ref[:, start:start + width].astype(o_ref.dtype)
        col += width


def _regroup_bf16(w, layer, pieces):
    _, rows, cols = w.shape
    n_out = sum(width for _, width in pieces)
    block_bytes = 8 * 1024 * 1024
    tr = max(k for k in (512, 256, 128, 64, 32, 16) if rows % k == 0 and k * cols * 4 <= block_bytes)
    return pl.pallas_call(
        functools.partial(_regroup_kernel, pieces=pieces),
        grid=(rows // tr,),
        in_specs=[pl.BlockSpec((None, tr, cols), lambda i: (layer, i, 0))],
        out_specs=pl.BlockSpec((tr, n_out), lambda i: (i, 0)),
        out_shape=jax.ShapeDtypeStruct((rows, n_out), BF16),
        compiler_params=_cparams(("arbitrary",), VMEM_MATRICES_MIB),
        name="weights_bf16",
    )(w)


def _bf16(w, layer):
    return _regroup_bf16(w, layer, ((0, w.shape[2]),))


def _rows(*vs, width):
    out = jnp.zeros((SUBLANES, width), F32)
    for i, v in enumerate(vs):
        out = out.at[i].set(v.astype(F32))
    return out


def kernel(x, c, ctx, c_ctx, ada_w, ada_b, norm_mix_g, norm_ffn_g, ffn_w_gate, ffn_w_up, ffn_w_down, final_norm_g, ev_w_in, ev_w_out, ssd_conv_w, ssd_conv_b, ssd_dt_bias, ssd_a_log, ssd_d, ssd_norm_g, hg_lb_logits, hg_norm_g, od_w_in, s5_a_re, s5_a_im, s5_log_dt, s5_b_re, s5_b_im, s5_c_re, s5_c_im, s5_d, od_w_val, od_w_gate):
    d = x.shape[-1]
    lat = x[0].astype(F32)
    hc = ctx[0].astype(F32)

    m = _ada(_rows(c[0], c_ctx, width=d), ada_w, ada_b)

    def ada_vecs(layer, stream):
        return [m[layer, stream, i * d:(i + 1) * d] for i in range(6)]

    n_x = SSD_HEADS * SSD_HEAD_DIM
    n_xbc = n_x + 2 * SSD_GROUPS * SSD_STATE
    n_hg = HG_HEADS * HG_HEAD_DIM
    o_z, o_xbc, o_dt = 0, n_x, n_x + n_xbc
    o_q = o_dt + 2 * SSD_HEADS
    o_f = o_q + n_hg
    o_v = o_f + 2 * n_hg
    o_g = o_v + n_hg
    w_in = _regroup_bf16(ev_w_in, 0, (
        (o_xbc, n_xbc),
        (o_dt, SSD_HEADS), (None, LANES - SSD_HEADS), (o_dt + SSD_HEADS, SSD_HEADS), (None, LANES - SSD_HEADS),
        (None, ZS_Q - ZS_XD),
        (o_q, n_hg), (o_v, n_hg), (o_f, 2 * n_hg), (o_z, n_x), (o_g, n_hg)))
    cw = jnp.zeros((SUBLANES, n_xbc), F32).at[:SSD_CONV].set(ssd_conv_w[0])
    cb = ssd_conv_b[0].reshape(1, n_xbc)
    dtc = jnp.zeros((2, SUBLANES, LANES), F32)
    dtc = dtc.at[:, 0, :SSD_HEADS].set(ssd_dt_bias[0]).at[:, 1, :SSD_HEADS].set(ssd_a_log[0])
    dsk = jnp.repeat(ssd_d[0], SSD_HEAD_DIM).reshape(1, n_x)
    lower = jnp.cumsum(jax.nn.softmax(hg_lb_logits.astype(F32), axis=0), axis=0)[0].reshape(1, n_hg)
    head_of_lane = jnp.arange(n_x) // SSD_HEAD_DIM
    e16 = (jnp.arange(LANES)[:, None] == head_of_lane[None, :]).astype(BF16)
    consts = (dtc, dsk, e16)
    w_out = _bf16(ev_w_out, 0)
    out_vec = lambda gate: _rows(ssd_norm_g[0], jnp.tile(hg_norm_g[0], HG_HEADS), gate, width=d)
    wg0, wu0, wd0 = (_bf16(t_, 0) for t_ in (ffn_w_gate, ffn_w_up, ffn_w_down))

    s_state = jnp.zeros((2, SSD_GROUPS, SSD_STATE, n_x // SSD_GROUPS), F32)
    g_state = jnp.zeros((2, HG_HEADS, HG_HEAD_DIM, HG_HEAD_DIM), F32)
    layer0 = {}
    for stream, h in ((1, hc), (0, lat)):
        sm, scm, gm, sf, scf, gf = ada_vecs(0, stream)
        za, zb = _even_in(h, _rows(norm_mix_g[0], scm, sm, width=d), w_in, cw, cb, dtc, lower)
        yo_f, yo_b, s_state, g_state = _even_scan(za, zb, consts, s_state, g_state)
        layer0[stream] = _even_out(h, zb, yo_f, yo_b, w_out, out_vec(gm),
                                   _rows(norm_ffn_g[0], scf, sf, gf, width=d), wg0, wu0, wd0)

    lat, hc = layer0[0], layer0[1]
    w_s5 = _bf16(od_w_in, 0)
    u = {}
    for stream, h, gw in ((1, hc, 1), (0, lat, GRID_W)):
        sm, scm = ada_vecs(1, stream)[:2]
        u[stream] = _odd_in(h, _rows(norm_mix_g[1], scm, sm, width=d), w_s5, gw)
    mats = [_s5_prep(s5_a_re[0, di], s5_a_im[0, di], s5_log_dt[0, di], s5_b_re[0, di], s5_b_im[0, di],
                     s5_c_re[0, di], s5_c_im[0, di], reverse=bool(di)) for di in range(2)]
    lam = jnp.stack([mats[0][3][:, 0], mats[1][3][:, 0]])
    carry = jnp.zeros_like(lam)
    for stream in (1, 0):
        s_f, s_b = (_s5_inject(u[stream], mats[di][0]) for di in range(2))
        p_f, p_b, carry = _s5_carry(s_f, s_b, lam, carry)
    nj = u[0].shape[0]
    y = _s5_readout(u[0], p_f, mats[0][1], mats[0][2], s5_d[0].reshape(nj, 1, LANES), reverse=False)
    y = _s5_readout(u[0], p_b, mats[1][1], mats[1][2], y, reverse=True)
    _, _, gm, sf, scf, gf = ada_vecs(1, 0)
    wg1, wu1, wd1 = (_bf16(t_, 1) for t_ in (ffn_w_gate, ffn_w_up, ffn_w_down))
    out = _odd_out(lat, y, _bf16(od_w_val, 0), _bf16(od_w_gate, 0), _rows(gm, width=d),
                   _rows(norm_ffn_g[1], scf, sf, gf, final_norm_g, width=d), wg1, wu1, wd1, GRID_W)
    return out[None].astype(x.dtype)
```
